```python
import math
import jax, jax.numpy as jnp
from jax import lax
import numpy as np

D_MODEL = 1024
BATCH = 4
SEQ = 8192
DEPTH = 1
DEC_BATCH = 8
DEC_SEQ = 4096
PAST_LEN = 128

N_META = 16
D_MIX = D_MODEL
W_CONV = D_MIX // 2
N_CONV_HEADS = 8
CONV_WIDTH = 3
W_SSM = D_MIX - W_CONV
SSM_GROUP = 16
N_SSM_GROUPS = W_SSM // SSM_GROUP
SSM_STATE = 64
N_EXPERTS = 256
TOP_K = 8
N_EXPERT_GROUPS = 8
TOPK_GROUPS = 4
D_EXPERT = D_MODEL // 4
D_SHARED = D_EXPERT
ROUTED_SCALE = 2.5
EXPERT_BLOCK = 128
DEEPNORM_ALPHA = (2.0 * DEPTH) ** 0.25
DEEPNORM_BETA = (8.0 * DEPTH) ** -0.25
LN_EPS = 1e-5
RMS_EPS = 1e-6

kernel_name = "hymba_conv_s5_moe_deepnorm_encoder"

F32 = jnp.float32


def _layer_norm(x, g, b):
    xf = x.astype(F32)
    xc = xf - xf.mean(-1, keepdims=True)
    var = (xc * xc).mean(-1, keepdims=True)
    return (xc * lax.rsqrt(var + LN_EPS) * g.astype(F32) + b.astype(F32)).astype(x.dtype)


def _rms_norm(x, g):
    xf = x.astype(F32)
    return (xf * lax.rsqrt((xf * xf).mean(-1, keepdims=True) + RMS_EPS) * g.astype(F32)).astype(x.dtype)


def _s5_combine(e1, e2):
    a1r, a1i, b1r, b1i = e1
    a2r, a2i, b2r, b2i = e2
    return (a2r * a1r - a2i * a1i,
            a2r * a1i + a2i * a1r,
            a2r * b1r - a2i * b1i + b2r,
            a2r * b1i + a2i * b1r + b2i)


def _s5_direction(u, lam_re, lam_im, log_step, b_re, b_im, c_re, c_im, reverse):
    L = u.shape[0]
    lam_re = lam_re.astype(F32); lam_im = lam_im.astype(F32)
    dt = jnp.exp(log_step.astype(F32))[:, None]
    mag = jnp.exp(lam_re * dt)
    ab_re = mag * jnp.cos(lam_im * dt)
    ab_im = mag * jnp.sin(lam_im * dt)
    den = lam_re * lam_re + lam_im * lam_im
    nr, ni = ab_re - 1.0, ab_im
    f_re = (nr * lam_re + ni * lam_im) / den
    f_im = (ni * lam_re - nr * lam_im) / den
    b_re = b_re.astype(F32); b_im = b_im.astype(F32)
    bb_re = f_re[..., None] * b_re - f_im[..., None] * b_im
    bb_im = f_re[..., None] * b_im + f_im[..., None] * b_re
    bu_re = jnp.einsum('lbgh,gph->lbgp', u, bb_re)
    bu_im = jnp.einsum('lbgh,gph->lbgp', u, bb_im)
    a_re = jnp.broadcast_to(ab_re[None, None], (L, 1) + ab_re.shape)
    a_im = jnp.broadcast_to(ab_im[None, None], (L, 1) + ab_im.shape)
    _, _, s_re, s_im = lax.associative_scan(_s5_combine, (a_re, a_im, bu_re, bu_im),
                                            reverse=reverse, axis=0)
    return (jnp.einsum('lbgp,ghp->blgh', s_re, c_re.astype(F32))
            - jnp.einsum('lbgp,ghp->blgh', s_im, c_im.astype(F32)))


def _mixer(h, w_in, conv_w, conv_b, lam_re, lam_im, log_step, b_re, b_im, c_re, c_im,
           ssm_d, w_glu, b_glu, norm_a_g, norm_b_g, w_out):
    bsz, L, _ = h.shape
    proj = jnp.einsum('bld,de->ble', h, w_in)
    g_b, g_c, v, s_u = jnp.split(proj, [W_CONV, 2 * W_CONV, 3 * W_CONV], axis=-1)
    u = g_c * v
    up = jnp.pad(u, ((0, 0), (1, 1), (0, 0)))
    conv = up[:, :-2] * conv_w[0] + up[:, 1:-1] * conv_w[1] + up[:, 2:] * conv_w[2] + conv_b
    y_a = g_b * conv
    us = s_u.astype(F32)
    ul = us.reshape(bsz, L, N_SSM_GROUPS, SSM_GROUP).transpose(1, 0, 2, 3)
    y_f = _s5_direction(ul, lam_re[0], lam_im[0], log_step[0], b_re[0], b_im[0], c_re[0], c_im[0], False)
    y_r = _s5_direction(ul, lam_re[1], lam_im[1], log_step[1], b_re[1], b_im[1], c_re[1], c_im[1], True)
    y_s = (y_f + y_r).reshape(bsz, L, W_SSM) + ssm_d.astype(F32) * us
    z = jax.nn.gelu(y_s)
    y_b = (z * jax.nn.sigmoid(z @ w_glu.astype(F32) + b_glu.astype(F32))).astype(h.dtype)
    merged = jnp.concatenate([_rms_norm(y_a, norm_a_g), _rms_norm(y_b, norm_b_g)], axis=-1)
    return jnp.einsum('ble,ed->bld', merged, w_out)


def _moe(h, w_router, router_bias, w_gate, w_up, w_down, w_sh_gate, w_sh_up, w_sh_down):
    T = h.shape[0]
    TK = T * TOP_K
    n_blocks = (TK + N_EXPERTS * (EXPERT_BLOCK - 1) + EXPERT_BLOCK - 1) // EXPERT_BLOCK
    n_slots = n_blocks * EXPERT_BLOCK
    epg = N_EXPERTS // N_EXPERT_GROUPS
    scores = jax.nn.sigmoid(jnp.einsum('td,de->te', h.astype(F32), w_router.astype(F32)))
    sel = scores + router_bias.astype(F32)
    gscore = lax.top_k(sel.reshape(T, N_EXPERT_GROUPS, epg), 2)[0].sum(-1)
    _, gidx = lax.top_k(gscore, TOPK_GROUPS)
    gmask = jax.nn.one_hot(gidx, N_EXPERT_GROUPS, dtype=jnp.bool_).any(axis=1)
    sel = jnp.where(jnp.repeat(gmask, epg, axis=1), sel, -jnp.inf)
    _, eidx = lax.top_k(sel, TOP_K)
    gate = jnp.take_along_axis(scores, eidx, axis=1)
    gate = gate / gate.sum(-1, keepdims=True) * ROUTED_SCALE
    flat_e = eidx.reshape(-1).astype(jnp.int32)
    order = jnp.argsort(flat_e, stable=True)
    sorted_e = flat_e[order]
    counts = jnp.bincount(flat_e, length=N_EXPERTS)
    padded = (counts + EXPERT_BLOCK - 1) // EXPERT_BLOCK * EXPERT_BLOCK
    start = jnp.cumsum(counts) - counts
    pad_end = jnp.cumsum(padded)
    pad_start = pad_end - padded
    dest = pad_start[sorted_e] + jnp.arange(TK, dtype=jnp.int32) - start[sorted_e]
    slot_tok = jnp.full((n_slots,), T, jnp.int32).at[dest].set((order // TOP_K).astype(jnp.int32))
    slot_w = jnp.zeros((n_slots,), h.dtype).at[dest].set(gate.reshape(-1)[order].astype(h.dtype))
    block_e = jnp.clip(jnp.searchsorted(pad_end, jnp.arange(n_blocks, dtype=jnp.int32) * EXPERT_BLOCK,
                                        side='right'), 0, N_EXPERTS - 1)
    h_pad = jnp.concatenate([h, jnp.zeros((1, D_MODEL), h.dtype)], axis=0)

    def block_fn(args):
        tok, wt, e = args
        xb = h_pad[tok]
        act = jax.nn.silu(xb @ w_gate[e]) * (xb @ w_up[e])
        return ((act @ w_down[e]) * wt[:, None]).astype(h.dtype)

    yb = lax.map(block_fn, (slot_tok.reshape(n_blocks, EXPERT_BLOCK),
                            slot_w.reshape(n_blocks, EXPERT_BLOCK), block_e))
    routed = jnp.zeros((T + 1, D_MODEL), h.dtype).at[slot_tok].add(yb.reshape(n_slots, D_MODEL))[:T]
    shared = (jax.nn.silu(h @ w_sh_gate) * (h @ w_sh_up)) @ w_sh_down
    return (routed + shared).astype(h.dtype)


def _encode(x, meta_tokens, ln_emb_g, ln_emb_b, w_in, conv_w, conv_b, ssm_lambda_re, ssm_lambda_im,
            ssm_log_step, ssm_b_re, ssm_b_im, ssm_c_re, ssm_c_im, ssm_d, w_glu, b_glu, norm_a_g,
            norm_b_g, w_out, ln1_g, ln1_b, w_router, router_bias, w_exp_gate, w_exp_up, w_exp_down,
            w_sh_gate, w_sh_up, w_sh_down, ln2_g, ln2_b):
    bsz = x.shape[0]
    meta = jnp.broadcast_to(meta_tokens.astype(x.dtype)[None], (bsz, N_META, D_MODEL))
    h = _layer_norm(jnp.concatenate([meta, x], axis=1), ln_emb_g, ln_emb_b)
    for l in range(DEPTH):
        m = _mixer(h, w_in[l], conv_w[l], conv_b[l], ssm_lambda_re[l], ssm_lambda_im[l],
                   ssm_log_step[l], ssm_b_re[l], ssm_b_im[l], ssm_c_re[l], ssm_c_im[l], ssm_d[l],
                   w_glu[l], b_glu[l], norm_a_g[l], norm_b_g[l], w_out[l])
        h = _layer_norm(DEEPNORM_ALPHA * h + m, ln1_g[l], ln1_b[l])
        f = _moe(h.reshape(-1, D_MODEL), w_router[l], router_bias[l], w_exp_gate[l], w_exp_up[l],
                 w_exp_down[l], w_sh_gate[l], w_sh_up[l], w_sh_down[l]).reshape(h.shape)
        h = _layer_norm(DEEPNORM_ALPHA * h + f, ln2_g[l], ln2_b[l])
    return h[:, N_META:]


def setup_inputs(seed: int = 0) -> dict:
    key = jax.random.key(seed)
    ks = jax.random.split(key, 40)

    def nrm(k, shape, scale):
        return jax.random.normal(k, shape, F32) * scale

    G, P, H = N_SSM_GROUPS, SSM_STATE, SSM_GROUP
    lam_shape = (DEPTH, 2, G, P)
    n_idx = jnp.arange(P, dtype=F32)
    return {
        "x_prompt": nrm(ks[0], (BATCH, SEQ, D_MODEL), 1.0),
        "x_sample": nrm(ks[1], (DEC_BATCH, DEC_SEQ, D_MODEL), 1.0),
        "meta_tokens": nrm(ks[2], (N_META, D_MODEL), 1.0),
        "ln_emb_g": 1.0 + nrm(ks[3], (D_MODEL,), 0.02),
        "ln_emb_b": nrm(ks[4], (D_MODEL,), 0.02),
        "w_in": nrm(ks[5], (DEPTH, D_MODEL, 3 * W_CONV + W_SSM), D_MODEL ** -0.5),
        "conv_w": nrm(ks[6], (DEPTH, CONV_WIDTH, W_CONV), CONV_WIDTH ** -0.5),
        "conv_b": nrm(ks[7], (DEPTH, W_CONV), 0.02),
        "ssm_lambda_re": -0.5 + nrm(ks[8], lam_shape, 0.01),
        "ssm_lambda_im": math.pi * n_idx + nrm(ks[9], lam_shape, 0.01),
        "ssm_log_step": jax.random.uniform(ks[10], (DEPTH, 2, G), F32, math.log(1e-3), math.log(1e-1)),
        "ssm_b_re": nrm(ks[11], (DEPTH, 2, G, P, H), (2.0 * H) ** -0.5),
        "ssm_b_im": nrm(ks[12], (DEPTH, 2, G, P, H), (2.0 * H) ** -0.5),
        "ssm_c_re": nrm(ks[13], (DEPTH, 2, G, H, P), P ** -0.5),
        "ssm_c_im": nrm(ks[14], (DEPTH, 2, G, H, P), P ** -0.5),
        "ssm_d": nrm(ks[15], (DEPTH, W_SSM), 1.0),
        "w_glu": nrm(ks[16], (DEPTH, W_SSM, W_SSM), W_SSM ** -0.5),
        "b_glu": nrm(ks[17], (DEPTH, W_SSM), 0.02),
        "norm_a_g": 1.0 + nrm(ks[18], (DEPTH, W_CONV), 0.02),
        "norm_b_g": 1.0 + nrm(ks[19], (DEPTH, W_SSM), 0.02),
        "w_out": nrm(ks[20], (DEPTH, D_MIX, D_MODEL), D_MIX ** -0.5 * DEEPNORM_BETA),
        "ln1_g": 1.0 + nrm(ks[21], (DEPTH, D_MODEL), 0.02),
        "ln1_b": nrm(ks[22], (DEPTH, D_MODEL), 0.02),
        "w_router": nrm(ks[23], (DEPTH, D_MODEL, N_EXPERTS), D_MODEL ** -0.5),
        "router_bias": nrm(ks[24], (DEPTH, N_EXPERTS), 0.01),
        "w_exp_gate": nrm(ks[25], (DEPTH, N_EXPERTS, D_MODEL, D_EXPERT), D_MODEL ** -0.5),
        "w_exp_up": nrm(ks[26], (DEPTH, N_EXPERTS, D_MODEL, D_EXPERT), D_MODEL ** -0.5),
        "w_exp_down": nrm(ks[27], (DEPTH, N_EXPERTS, D_EXPERT, D_MODEL), D_EXPERT ** -0.5 * DEEPNORM_BETA),
        "w_sh_gate": nrm(ks[28], (DEPTH, D_MODEL, D_SHARED), D_MODEL ** -0.5),
        "w_sh_up": nrm(ks[29], (DEPTH, D_MODEL, D_SHARED), D_MODEL ** -0.5),
        "w_sh_down": nrm(ks[30], (DEPTH, D_SHARED, D_MODEL), D_SHARED ** -0.5 * DEEPNORM_BETA),
        "ln2_g": 1.0 + nrm(ks[31], (DEPTH, D_MODEL), 0.02),
        "ln2_b": nrm(ks[32], (DEPTH, D_MODEL), 0.02),
    }


def reference(x_prompt, x_sample, meta_tokens, ln_emb_g, ln_emb_b, w_in, conv_w, conv_b,
              ssm_lambda_re, ssm_lambda_im, ssm_log_step, ssm_b_re, ssm_b_im, ssm_c_re, ssm_c_im,
              ssm_d, w_glu, b_glu, norm_a_g, norm_b_g, w_out, ln1_g, ln1_b, w_router, router_bias,
              w_exp_gate, w_exp_up, w_exp_down, w_sh_gate, w_sh_up, w_sh_down, ln2_g, ln2_b):
    params = (meta_tokens, ln_emb_g, ln_emb_b, w_in, conv_w, conv_b, ssm_lambda_re, ssm_lambda_im,
              ssm_log_step, ssm_b_re, ssm_b_im, ssm_c_re, ssm_c_im, ssm_d, w_glu, b_glu, norm_a_g,
              norm_b_g, w_out, ln1_g, ln1_b, w_router, router_bias, w_exp_gate, w_exp_up, w_exp_down,
              w_sh_gate, w_sh_up, w_sh_down, ln2_g, ln2_b)
    y_prompt = _encode(x_prompt, *params)
    y_sample = _encode(x_sample, *params)
    return (y_prompt, y_sample)
```

```python
import functools
import math

import jax
import jax.numpy as jnp
from jax import lax
from jax.experimental import pallas as pl
from jax.experimental.pallas import tpu as pltpu

F32 = jnp.float32
BF16 = jnp.bfloat16

N_META = 16
CONV_WIDTH = 3
SSM_GROUP = 16
SSM_STATE = 64
N_EXPERTS = 256
TOP_K = 8
N_EXPERT_GROUPS = 8
TOPK_GROUPS = 4
ROUTED_SCALE = 2.5
DEPTH = 1
DEEPNORM_ALPHA = (2.0 * DEPTH) ** 0.25
LN_EPS = 1e-5
RMS_EPS = 1e-6

LANES = 128
SUBLANES = 8
N_SLABS = 4
GROUPS_PER_SLAB = LANES // SSM_GROUP
SLAB_STATE = GROUPS_PER_SLAB * SSM_STATE
EXPERT_BLOCK = 128
VMEM_LIMIT = 48 * 1024 * 1024


def _cparams(*sem):
    return pltpu.CompilerParams(dimension_semantics=sem, vmem_limit_bytes=VMEM_LIMIT)


def _layer_norm(x, g, b):
    mu = jnp.mean(x, axis=-1, keepdims=True)
    xc = x - mu
    var = jnp.mean(xc * xc, axis=-1, keepdims=True)
    return xc * lax.rsqrt(var + LN_EPS) * g + b


def _rms_norm(x, g):
    return x * lax.rsqrt(jnp.mean(x * x, axis=-1, keepdims=True) + RMS_EPS) * g


def _embed_inproj_kernel(x_ref, g_ref, b_ref, w_ref, h0_ref, gb_ref, u_ref, su_ref, *, bsz, tt):
    d = x_ref.shape[-1]
    x = x_ref[...].reshape(bsz * tt, d)
    h0 = _layer_norm(x, g_ref[...], b_ref[...])
    h0_ref[...] = h0.reshape(bsz, tt, d)
    proj = jnp.dot(h0.astype(BF16), w_ref[...], preferred_element_type=F32)
    wc = gb_ref.shape[-1]
    gb_ref[...] = proj[:, :wc].reshape(bsz, tt, wc)
    u_ref[...] = (proj[:, wc:2 * wc] * proj[:, 2 * wc:3 * wc]).reshape(bsz, tt, wc)
    s_u = proj[:, 3 * wc:]
    for b in range(bsz):
        for j in range(N_SLABS):
            val = s_u[b * tt:(b + 1) * tt, j * LANES:(j + 1) * LANES]
            if bsz == 1:
                su_ref[j] = val
            else:
                su_ref[j, pl.ds(b, tt, stride=bsz), :] = val


def _embed_inproj(x, ln_g, ln_b, w_in_bf, tt):
    bsz, seq, d = x.shape
    e = w_in_bf.shape[1]
    wc = (e - N_SLABS * LANES) // 3
    nt = seq // tt
    kern = functools.partial(_embed_inproj_kernel, bsz=bsz, tt=tt)
    return pl.pallas_call(
        kern,
        grid=(nt,),
        in_specs=[
            pl.BlockSpec((bsz, tt, d), lambda i: (0, i, 0)),
            pl.BlockSpec((1, d), lambda i: (0, 0)),
            pl.BlockSpec((1, d), lambda i: (0, 0)),
            pl.BlockSpec((d, e), lambda i: (0, 0)),
        ],
        out_specs=[
            pl.BlockSpec((bsz, tt, d), lambda i: (0, i, 0)),
            pl.BlockSpec((bsz, tt, wc), lambda i: (0, i, 0)),
            pl.BlockSpec((bsz, tt, wc), lambda i: (0, i, 0)),
            pl.BlockSpec((N_SLABS, tt * bsz, LANES), lambda i: (0, i, 0)),
        ],
        out_shape=[
            jax.ShapeDtypeStruct((bsz, seq, d), F32),
            jax.ShapeDtypeStruct((bsz, seq, wc), F32),
            jax.ShapeDtypeStruct((bsz, seq, wc), F32),
            jax.ShapeDtypeStruct((N_SLABS, seq * bsz, LANES), F32),
        ],
        compiler_params=_cparams("parallel"),
        name="embed_inproj",
    )(x, ln_g.reshape(1, d), ln_b.reshape(1, d), w_in_bf)


def _s5_kernel(*refs, bsz, tc, dirs, emit_y, emit_state):
    nd = len(dirs)
    it = iter(refs)
    u_refs = [next(it) for _ in range(nd)]
    s0_ref, wb_ref, wc_ref, a_ref = next(it), next(it), next(it), next(it)
    y_refs = [next(it) for _ in range(nd)] if emit_y else []
    sf_ref = next(it) if emit_state else None
    bu_ref, st_ref = next(it), next(it)
    half = SLAB_STATE

    @pl.when(pl.program_id(0) == 0)
    def _():
        st_ref[...] = s0_ref[...]

    for k in range(nd):
        for j in range(N_SLABS):
            bu_ref[k, j] = jnp.dot(u_refs[k][j].astype(BF16), wb_ref[k, j],
                                   preferred_element_type=F32)

    per_tile = SUBLANES // bsz
    n_tiles = tc // per_tile
    first_half = lax.broadcasted_iota(jnp.int32, (SUBLANES, half), 0) < bsz

    for j in range(N_SLABS):
        a_re = [a_ref[k, j, :, :half] for k in range(nd)]
        a_im = [a_ref[k, j, :, half:] for k in range(nd)]
        init = tuple((st_ref[k, j, :, :half], st_ref[k, j, :, half:]) for k in range(nd))

        def step(i, carry, j=j, a_re=a_re, a_im=a_im):
            out = []
            for k in range(nd):
                ti = (n_tiles - 1 - i) if dirs[k] else i
                row = pl.multiple_of(ti * SUBLANES, SUBLANES)
                x_re = bu_ref[k, j, pl.ds(row, SUBLANES), :half]
                x_im = bu_ref[k, j, pl.ds(row, SUBLANES), half:]

                def advance(s_re, s_im, k=k, x_re=x_re, x_im=x_im):
                    return (a_re[k] * s_re - a_im[k] * s_im + x_re,
                            a_re[k] * s_im + a_im[k] * s_re + x_im)

                c_re, c_im = carry[k]
                if per_tile == 1:
                    n_re, n_im = advance(c_re, c_im)
                    o_re, o_im = n_re, n_im
                else:
                    t1_re, t1_im = advance(pltpu.roll(c_re, bsz, axis=0),
                                           pltpu.roll(c_im, bsz, axis=0))
                    n_re, n_im = advance(pltpu.roll(t1_re, bsz, axis=0),
                                         pltpu.roll(t1_im, bsz, axis=0))
                    if dirs[k]:
                        o_re = jnp.where(first_half, n_re, t1_re)
                        o_im = jnp.where(first_half, n_im, t1_im)
                    else:
                        o_re = jnp.where(first_half, t1_re, n_re)
                        o_im = jnp.where(first_half, t1_im, n_im)
                bu_ref[k, j, pl.ds(row, SUBLANES), :half] = o_re
                bu_ref[k, j, pl.ds(row, SUBLANES), half:] = o_im
                out.append((n_re, n_im))
            return tuple(out)

        fin = lax.fori_loop(0, n_tiles, step, init)
        for k in range(nd):
            st_ref[k, j, :, :half] = fin[k][0]
            st_ref[k, j, :, half:] = fin[k][1]

    if emit_y:
        for k in range(nd):
            for j in range(N_SLABS):
                y_refs[k][j] = jnp.dot(bu_ref[k, j].astype(BF16), wc_ref[k, j],
                                       preferred_element_type=F32)
    if emit_state:
        sf_ref[...] = st_ref[...]


def _s5_scan(su, s0, wb, wc, a_b, *, bsz, tc, dirs, emit_y, emit_state):
    rows = su.shape[1]
    seq = rows // bsz
    nc = seq // tc
    r = tc * bsz
    nd = len(dirs)
    sw = 2 * SLAB_STATE

    def u_map(rev):
        return (lambda c: (0, nc - 1 - c, 0)) if rev else (lambda c: (0, c, 0))

    in_specs = [pl.BlockSpec((N_SLABS, r, LANES), u_map(rev)) for rev in dirs]
    in_specs += [
        pl.BlockSpec((nd, N_SLABS, SUBLANES, sw), lambda c: (0, 0, 0, 0)),
        pl.BlockSpec((nd, N_SLABS, LANES, sw), lambda c: (0, 0, 0, 0)),
        pl.BlockSpec((nd, N_SLABS, sw, LANES), lambda c: (0, 0, 0, 0)),
        pl.BlockSpec((nd, N_SLABS, SUBLANES, sw), lambda c: (0, 0, 0, 0)),
    ]
    out_specs, out_shape = [], []
    if emit_y:
        for rev in dirs:
            out_specs.append(pl.BlockSpec((N_SLABS, r, LANES), u_map(rev)))
            out_shape.append(jax.ShapeDtypeStruct((N_SLABS, rows, LANES), F32))
    if emit_state:
        out_specs.append(pl.BlockSpec((nd, N_SLABS, SUBLANES, sw), lambda c: (0, 0, 0, 0)))
        out_shape.append(jax.ShapeDtypeStruct((nd, N_SLABS, SUBLANES, sw), F32))
    kern = functools.partial(_s5_kernel, bsz=bsz, tc=tc, dirs=dirs, emit_y=emit_y,
                             emit_state=emit_state)
    return pl.pallas_call(
        kern,
        grid=(nc,),
        in_specs=in_specs,
        out_specs=out_specs,
        out_shape=out_shape,
        scratch_shapes=[
            pltpu.VMEM((nd, N_SLABS, r, sw), F32),
            pltpu.VMEM((nd, N_SLABS, SUBLANES, sw), F32),
        ],
        compiler_params=_cparams("arbitrary"),
        name="s5_scan",
    )(*([su] * nd), s0, wb, wc, a_b)


def _s5_params(lam_re, lam_im, log_step, b_re, b_im, c_re, c_im):
    g, p = lam_re.shape
    h = b_re.shape[-1]
    dt = jnp.exp(log_step.astype(F32))[:, None]
    mag = jnp.exp(lam_re * dt)
    ab_re = mag * jnp.cos(lam_im * dt)
    ab_im = mag * jnp.sin(lam_im * dt)
    den = lam_re * lam_re + lam_im * lam_im
    nr, ni = ab_re - 1.0, ab_im
    f_re = (nr * lam_re + ni * lam_im) / den
    f_im = (ni * lam_re - nr * lam_im) / den
    bb_re = f_re[..., None] * b_re - f_im[..., None] * b_im
    bb_im = f_re[..., None] * b_im + f_im[..., None] * b_re
    ns, gl = N_SLABS, GROUPS_PER_SLAB
    eye = jnp.eye(gl, dtype=F32)

    def in_block(bb):
        bb = bb.reshape(ns, gl, p, h)
        return jnp.einsum('sgph,gk->sghkp', bb, eye).reshape(ns, gl * h, gl * p)

    def out_block(cc):
        cc = cc.reshape(ns, gl, h, p)
        return jnp.einsum('sghp,gk->sgpkh', cc, eye).reshape(ns, gl * p, gl * h)

    wb = jnp.concatenate([in_block(bb_re), in_block(bb_im)], axis=-1)
    wc = jnp.concatenate([out_block(c_re.astype(F32)), -out_block(c_im.astype(F32))], axis=1)
    a = jnp.concatenate([ab_re.reshape(ns, gl * p), ab_im.reshape(ns, gl * p)], axis=-1)
    return wb.astype(BF16), wc.astype(BF16), a


def _mixer_tail_kernel(h0_ref, gb_ref, u_ref, up_ref, un_ref, um_ref, su_ref, yf_ref, yr_ref,
                       cw_ref, cb_ref, sd_ref, wg_ref, bg_ref, na_ref, nb_ref, wo_ref,
                       g1_ref, b1_ref, h1_ref, h1b_ref, *, bsz, tt):
    i = pl.program_id(0)
    nt = pl.num_programs(0)
    d = h0_ref.shape[-1]
    wcv = gb_ref.shape[-1]
    row_id = lax.broadcasted_iota(jnp.int32, (tt, wcv), 0)
    ya, ys = [], []
    for b in range(bsz):
        u = u_ref[b]
        prev_edge = jnp.where(i == 0, um_ref[...], up_ref[b, SUBLANES - 1:SUBLANES, :])
        next_edge = jnp.where(i == nt - 1, jnp.zeros((1, wcv), F32), un_ref[b, 0:1, :])
        u_prev = jnp.where(row_id == 0, prev_edge, pltpu.roll(u, 1, axis=0))
        u_next = jnp.where(row_id == tt - 1, next_edge, pltpu.roll(u, tt - 1, axis=0))
        conv = u_prev * cw_ref[0:1, :] + u * cw_ref[1:2, :] + u_next * cw_ref[2:3, :] + cb_ref[...]
        ya.append(gb_ref[b] * conv)

        def slab(ref, b=b):
            parts = []
            for j in range(N_SLABS):
                if bsz == 1:
                    parts.append(ref[j])
                else:
                    parts.append(ref[j, pl.ds(b, tt, stride=bsz), :])
            return jnp.concatenate(parts, axis=-1)

        ys.append(slab(yf_ref) + slab(yr_ref) + sd_ref[...] * slab(su_ref))
    y_a = jnp.concatenate(ya, axis=0)
    y_s = jnp.concatenate(ys, axis=0)
    z = jax.nn.gelu(y_s)
    glu = jnp.dot(z.astype(BF16), wg_ref[...], preferred_element_type=F32) + bg_ref[...]
    y_b = z * jax.nn.sigmoid(glu)
    merged = jnp.concatenate([_rms_norm(y_a, na_ref[...]), _rms_norm(y_b, nb_ref[...])], axis=-1)
    m = jnp.dot(merged.astype(BF16), wo_ref[...], preferred_element_type=F32)
    h0 = h0_ref[...].reshape(bsz * tt, d)
    h1 = _layer_norm(DEEPNORM_ALPHA * h0 + m, g1_ref[...], b1_ref[...])
    h1_ref[...] = h1.reshape(bsz, tt, d)
    h1b_ref[...] = h1.astype(BF16).reshape(bsz, tt, d)


def _mixer_tail(h0, gb, u, u_meta_last, su, yf, yr, conv_w, conv_b, ssm_d, w_glu_bf, b_glu,
                norm_a_g, norm_b_g, w_out_bf, ln1_g, ln1_b, tt):
    bsz, seq, d = h0.shape
    wcv = gb.shape[-1]
    ws = N_SLABS * LANES
    nt = seq // tt
    tb = tt // SUBLANES
    nb8 = seq // SUBLANES
    kern = functools.partial(_mixer_tail_kernel, bsz=bsz, tt=tt)
    row = lambda n: pl.BlockSpec((1, n), lambda i: (0, 0))
    slab_spec = pl.BlockSpec((N_SLABS, tt * bsz, LANES), lambda i: (0, i, 0))
    return pl.pallas_call(
        kern,
        grid=(nt,),
        in_specs=[
            pl.BlockSpec((bsz, tt, d), lambda i: (0, i, 0)),
            pl.BlockSpec((bsz, tt, wcv), lambda i: (0, i, 0)),
            pl.BlockSpec((bsz, tt, wcv), lambda i: (0, i, 0)),
            pl.BlockSpec((bsz, SUBLANES, wcv), lambda i: (0, jnp.maximum(i * tb - 1, 0), 0)),
            pl.BlockSpec((bsz, SUBLANES, wcv), lambda i: (0, jnp.minimum((i + 1) * tb, nb8 - 1), 0)),
            row(wcv),
            slab_spec, slab_spec, slab_spec,
            pl.BlockSpec((CONV_WIDTH, wcv), lambda i: (0, 0)),
            row(wcv), row(ws),
            pl.BlockSpec((ws, ws), lambda i: (0, 0)),
            row(ws), row(wcv), row(ws),
            pl.BlockSpec((wcv + ws, d), lambda i: (0, 0)),
            row(d), row(d),
        ],
        out_specs=[
            pl.BlockSpec((bsz, tt, d), lambda i: (0, i, 0)),
            pl.BlockSpec((bsz, tt, d), lambda i: (0, i, 0)),
        ],
        out_shape=[
            jax.ShapeDtypeStruct((bsz, seq, d), F32),
            jax.ShapeDtypeStruct((bsz, seq, d), BF16),
        ],
        compiler_params=_cparams("parallel"),
        name="mixer_tail",
    )(h0, gb, u, u, u, u_meta_last.reshape(1, wcv), su, yf, yr, conv_w, conv_b.reshape(1, wcv),
      ssm_d.reshape(1, ws), w_glu_bf, b_glu.reshape(1, ws), norm_a_g.reshape(1, wcv),
      norm_b_g.reshape(1, ws), w_out_bf, ln1_g.reshape(1, d), ln1_b.reshape(1, d))


def _router_kernel(h_ref, wh_ref, wl_ref, bias_ref, eidx_ref, rank_ref, gate_ref, cnt_ref,
                   cnt_scr, *, tt):
    ne = wh_ref.shape[0]
    epg = ne // N_EXPERT_GROUPS
    neg = jnp.float32(-jnp.inf)

    @pl.when(pl.program_id(0) == 0)
    def _():
        cnt_scr[...] = jnp.zeros_like(cnt_scr)

    h = h_ref[...]
    hh = h.astype(BF16)
    hl = (h - hh.astype(F32)).astype(BF16)
    dn = (((1,), (1,)), ((), ()))
    wh, wl = wh_ref[...], wl_ref[...]
    logits = (lax.dot_general(wh, hh, dn, preferred_element_type=F32)
              + lax.dot_general(wh, hl, dn, preferred_element_type=F32)
              + lax.dot_general(wl, hh, dn, preferred_element_type=F32))
    scores = jax.nn.sigmoid(logits)
    sel = scores + bias_ref[...]

    gi = lax.broadcasted_iota(jnp.int32, (epg, tt), 0)
    gs = []
    for g in range(N_EXPERT_GROUPS):
        x = sel[g * epg:(g + 1) * epg, :]
        m1 = jnp.max(x, axis=0, keepdims=True)
        i1 = jnp.min(jnp.where(x == m1, gi, epg), axis=0, keepdims=True)
        m2 = jnp.max(jnp.where(gi == i1, neg, x), axis=0, keepdims=True)
        gs.append(m1 + m2)
    chosen = [jnp.zeros((1, tt), F32) for _ in range(N_EXPERT_GROUPS)]
    for _ in range(TOPK_GROUPS):
        m = gs[0]
        for g in range(1, N_EXPERT_GROUPS):
            m = jnp.maximum(m, gs[g])
        found = jnp.zeros((1, tt), F32)
        for g in range(N_EXPERT_GROUPS):
            hit = jnp.where((gs[g] == m) & (found == 0.0), 1.0, 0.0)
            chosen[g] = chosen[g] + hit
            found = found + hit
            gs[g] = jnp.where(hit > 0.0, neg, gs[g])
    selm = jnp.concatenate(
        [jnp.where(chosen[g] > 0.0, sel[g * epg:(g + 1) * epg, :], neg)
         for g in range(N_EXPERT_GROUPS)], axis=0)

    ei = lax.broadcasted_iota(jnp.int32, (ne, tt), 0)
    msel = jnp.zeros((ne, tt), F32)
    idxs, gvals = [], []
    for _ in range(TOP_K):
        m = jnp.max(selm, axis=0, keepdims=True)
        idx = jnp.min(jnp.where(selm == m, ei, ne), axis=0, keepdims=True)
        hit = ei == idx
        gvals.append(jnp.sum(jnp.where(hit, scores, 0.0), axis=0, keepdims=True))
        selm = jnp.where(hit, neg, selm)
        msel = jnp.where(hit, 1.0, msel)
        idxs.append(idx)
    gsum = gvals[0]
    for k in range(1, TOP_K):
        gsum = gsum + gvals[k]
    gate_ref[...] = jnp.concatenate([gv / gsum * ROUTED_SCALE for gv in gvals], axis=0)
    eidx_ref[...] = jnp.concatenate(idxs, axis=0)

    r_i = lax.broadcasted_iota(jnp.int32, (tt, tt), 0)
    c_i = lax.broadcasted_iota(jnp.int32, (tt, tt), 1)
    upper = jnp.where(r_i < c_i, 1.0, 0.0).astype(BF16)
    rank_full = jnp.dot(msel.astype(BF16), upper, preferred_element_type=F32) + cnt_scr[...]
    ranks = [jnp.sum(jnp.where(ei == idxs[k], rank_full, 0.0), axis=0, keepdims=True)
             for k in range(TOP_K)]
    rank_ref[...] = jnp.concatenate(ranks, axis=0).astype(jnp.int32)
    cnt_scr[...] = cnt_scr[...] + jnp.sum(msel, axis=1, keepdims=True)
    cnt_ref[...] = cnt_scr[...]


def _router(h, w_router, router_bias, tt):
    n, d = h.shape
    ne = w_router.shape[1]
    wt = w_router.astype(F32).T
    wh = wt.astype(BF16)
    wl = (wt - wh.astype(F32)).astype(BF16)
    kern = functools.partial(_router_kernel, tt=tt)
    return pl.pallas_call(
        kern,
        grid=(n // tt,),
        in_specs=[
            pl.BlockSpec((tt, d), lambda i: (i, 0)),
            pl.BlockSpec((ne, d), lambda i: (0, 0)),
            pl.BlockSpec((ne, d), lambda i: (0, 0)),
            pl.BlockSpec((ne, 1), lambda i: (0, 0)),
        ],
        out_specs=[
            pl.BlockSpec((TOP_K, tt), lambda i: (0, i)),
            pl.BlockSpec((TOP_K, tt), lambda i: (0, i)),
            pl.BlockSpec((TOP_K, tt), lambda i: (0, i)),
            pl.BlockSpec((ne, 1), lambda i: (0, 0)),
        ],
        out_shape=[
            jax.ShapeDtypeStruct((TOP_K, n), jnp.int32),
            jax.ShapeDtypeStruct((TOP_K, n), jnp.int32),
            jax.ShapeDtypeStruct((TOP_K, n), F32),
            jax.ShapeDtypeStruct((ne, 1), F32),
        ],
        scratch_shapes=[pltpu.VMEM((ne, 1), F32)],
        compiler_params=_cparams("arbitrary"),
        name="router",
    )(h, wh, wl, router_bias.astype(F32).reshape(ne, 1))


def _dispatch_kernel(pos_ref, h_ref, xs_in_ref, xs_ref, sem, *, tt):
    del xs_in_ref

    def issue(t, carry):
        for k in range(TOP_K):
            pltpu.make_async_copy(h_ref.at[t], xs_ref.at[pos_ref[k, t]], sem).start()
        return carry

    lax.fori_loop(0, tt, issue, 0)
    for _ in range(TOP_K):
        pltpu.make_async_copy(h_ref, xs_ref.at[pl.ds(0, tt)], sem).wait()


def _dispatch(h, pos, xs_init, tt):
    n, d = h.shape
    kern = functools.partial(_dispatch_kernel, tt=tt)
    return pl.pallas_call(
        kern,
        grid=(n // tt,),
        in_specs=[
            pl.BlockSpec((TOP_K, tt), lambda i: (0, i), memory_space=pltpu.SMEM),
            pl.BlockSpec((tt, d), lambda i: (i, 0)),
            pl.BlockSpec(memory_space=pl.ANY),
        ],
        out_specs=pl.BlockSpec(memory_space=pl.ANY),
        out_shape=jax.ShapeDtypeStruct(xs_init.shape, xs_init.dtype),
        scratch_shapes=[pltpu.SemaphoreType.DMA],
        input_output_aliases={2: 0},
        compiler_params=_cparams("arbitrary"),
        name="dispatch",
    )(pos, h, xs_init)


def _experts_kernel(be_ref, nu_ref, xs_ref, wg_ref, wu_ref, wd_ref, ys_ref, wgb, wub, wdb):
    i = pl.program_id(0)
    e = be_ref[i]
    prev = be_ref[jnp.maximum(i - 1, 0)]

    @pl.when((i == 0) | (e != prev))
    def _():
        wgb[...] = wg_ref[0].astype(BF16)
        wub[...] = wu_ref[0].astype(BF16)
        wdb[...] = wd_ref[0].astype(BF16)

    @pl.when(i < nu_ref[0])
    def _():
        x = xs_ref[...].astype(BF16)
        g = jnp.dot(x, wgb[...], preferred_element_type=F32)
        u = jnp.dot(x, wub[...], preferred_element_type=F32)
        act = (g * jax.nn.sigmoid(g)) * u
        ys_ref[...] = jnp.dot(act.astype(BF16), wdb[...], preferred_element_type=F32)

    @pl.when(i >= nu_ref[0])
    def _():
        ys_ref[...] = jnp.zeros_like(ys_ref)


def _experts(xs, block_e, n_used, w_gate, w_up, w_down):
    n_slots, d = xs.shape
    ne, _, de = w_gate.shape
    nb = n_slots // EXPERT_BLOCK

    def row_map(i, be, nu):
        return (jnp.minimum(i, nu[0] - 1), 0)

    def w_map(i, be, nu):
        return (be[jnp.minimum(i, nu[0] - 1)], 0, 0)

    grid_spec = pltpu.PrefetchScalarGridSpec(
        num_scalar_prefetch=2,
        grid=(nb,),
        in_specs=[
            pl.BlockSpec((EXPERT_BLOCK, d), row_map),
            pl.BlockSpec((1, d, de), w_map),
            pl.BlockSpec((1, d, de), w_map),
            pl.BlockSpec((1, de, d), w_map),
        ],
        out_specs=pl.BlockSpec((EXPERT_BLOCK, d), lambda i, be, nu: (i, 0)),
        scratch_shapes=[
            pltpu.VMEM((d, de), BF16),
            pltpu.VMEM((d, de), BF16),
            pltpu.VMEM((de, d), BF16),
        ],
    )
    return pl.pallas_call(
        _experts_kernel,
        grid_spec=grid_spec,
        out_shape=jax.ShapeDtypeStruct((n_slots, d), F32),
        compiler_params=_cparams("arbitrary"),
        name="experts",
    )(block_e, n_used, xs, w_gate, w_up, w_down)


def _combine_kernel(pos_ref, h_ref, gate_ref, ys_ref, wsg_ref, wsu_ref, wsd_ref, g2_ref, b2_ref,
                    out_ref, buf, sem, *, tt):
    def issue(t, carry):
        for k in range(TOP_K):
            pltpu.make_async_copy(ys_ref.at[pos_ref[k, t]], buf.at[k, t], sem).start()
        return carry

    lax.fori_loop(0, tt, issue, 0)
    h = h_ref[...]
    hb = h.astype(BF16)
    g = jnp.dot(hb, wsg_ref[...], preferred_element_type=F32)
    u = jnp.dot(hb, wsu_ref[...], preferred_element_type=F32)
    act = (g * jax.nn.sigmoid(g)) * u
    f = jnp.dot(act.astype(BF16), wsd_ref[...], preferred_element_type=F32)
    for k in range(TOP_K):
        pltpu.make_async_copy(ys_ref.at[pl.ds(0, tt)], buf.at[k], sem).wait()
    gate = gate_ref[...]
    for k in range(TOP_K):
        f = f + gate[:, k:k + 1] * buf[k]
    out_ref[...] = _layer_norm(DEEPNORM_ALPHA * h + f, g2_ref[...], b2_ref[...])


def _combine(h, pos, gate_t, ys, wsg_bf, wsu_bf, wsd_bf, ln2_g, ln2_b, tt):
    n, d = h.shape
    ds_ = wsg_bf.shape[1]
    kern = functools.partial(_combine_kernel, tt=tt)
    return pl.pallas_call(
        kern,
        grid=(n // tt,),
        in_specs=[
            pl.BlockSpec((TOP_K, tt), lambda i: (0, i), memory_space=pltpu.SMEM),
            pl.BlockSpec((tt, d), lambda i: (i, 0)),
            pl.BlockSpec((tt, TOP_K), lambda i: (i, 0)),
            pl.BlockSpec(memory_space=pl.ANY),
            pl.BlockSpec((d, ds_), lambda i: (0, 0)),
            pl.BlockSpec((d, ds_), lambda i: (0, 0)),
            pl.BlockSpec((ds_, d), lambda i: (0, 0)),
            pl.BlockSpec((1, d), lambda i: (0, 0)),
            pl.BlockSpec((1, d), lambda i: (0, 0)),
        ],
        out_specs=pl.BlockSpec((tt, d), lambda i: (i, 0)),
        out_shape=jax.ShapeDtypeStruct((n, d), F32),
        scratch_shapes=[pltpu.VMEM((TOP_K, tt, d), F32), pltpu.SemaphoreType.DMA],
        compiler_params=_cparams("arbitrary"),
        name="combine",
    )(pos, h, gate_t, ys, wsg_bf, wsu_bf, wsd_bf, ln2_g.reshape(1, d), ln2_b.reshape(1, d))


def _pick_tile(seq, bsz, rows):
    return max(SUBLANES, min(seq, rows // bsz))


def _mixer(x, meta_state, u_meta_last, p):
    bsz, seq, _ = x.shape
    tt = _pick_tile(seq, bsz, 1024)
    h0, gb, u, su = _embed_inproj(x, p["ln_emb_g"], p["ln_emb_b"], p["w_in_bf"], tt)
    tc = _pick_tile(seq, bsz, 256)
    s0 = jnp.stack([jnp.broadcast_to(meta_state, (N_SLABS, SUBLANES, 2 * SLAB_STATE)),
                    jnp.zeros((N_SLABS, SUBLANES, 2 * SLAB_STATE), F32)])
    a_b = jnp.broadcast_to(p["s5_a"][:, :, None, :], (2, N_SLABS, SUBLANES, 2 * SLAB_STATE))
    yf, yr = _s5_scan(su, s0, p["s5_wb"], p["s5_wc"], a_b, bsz=bsz, tc=tc, dirs=(False, True),
                      emit_y=True, emit_state=False)
    tt3 = _pick_tile(seq, bsz, 512)
    h1, _ = _mixer_tail(h0, gb, u, u_meta_last, su, yf, yr, p["conv_w"], p["conv_b"], p["ssm_d"],
                        p["w_glu_bf"], p["b_glu"], p["norm_a_g"], p["norm_b_g"], p["w_out_bf"],
                        p["ln1_g"], p["ln1_b"], tt3)
    return h1


def kernel(x_prompt, x_sample, meta_tokens, ln_emb_g, ln_emb_b, w_in, conv_w, conv_b, ssm_lambda_re, ssm_lambda_im, ssm_log_step, ssm_b_re, ssm_b_im, ssm_c_re, ssm_c_im, ssm_d, w_glu, b_glu, norm_a_g, norm_b_g, w_out, ln1_g, ln1_b, w_router, router_bias, w_exp_gate, w_exp_up, w_exp_down, w_sh_gate, w_sh_up, w_sh_down, ln2_g, ln2_b):
    l = 0
    d = x_prompt.shape[-1]
    dirs = [_s5_params(ssm_lambda_re[l, k].astype(F32), ssm_lambda_im[l, k].astype(F32),
                       ssm_log_step[l, k], ssm_b_re[l, k].astype(F32), ssm_b_im[l, k].astype(F32),
                       ssm_c_re[l, k], ssm_c_im[l, k]) for k in range(2)]
    p = dict(
        ln_emb_g=ln_emb_g, ln_emb_b=ln_emb_b, w_in_bf=w_in[l].astype(BF16),
        conv_w=conv_w[l], conv_b=conv_b[l], ssm_d=ssm_d[l],
        w_glu_bf=w_glu[l].astype(BF16), b_glu=b_glu[l], norm_a_g=norm_a_g[l],
        norm_b_g=norm_b_g[l], w_out_bf=w_out[l].astype(BF16), ln1_g=ln1_g[l], ln1_b=ln1_b[l],
        s5_wb=jnp.stack([dirs[0][0], dirs[1][0]]), s5_wc=jnp.stack([dirs[0][1], dirs[1][1]]),
        s5_a=jnp.stack([dirs[0][2], dirs[1][2]]),
    )
    mb = SUBLANES
    xm = jnp.broadcast_to(meta_tokens.astype(F32)[None], (mb, N_META, d))
    _, _, u_m, su_m = _embed_inproj(xm, ln_emb_g, ln_emb_b, p["w_in_bf"], N_META)
    a_m = jnp.broadcast_to(p["s5_a"][:1, :, None, :], (1, N_SLABS, mb, 2 * SLAB_STATE))
    (st_m,) = _s5_scan(su_m, jnp.zeros((1, N_SLABS, mb, 2 * SLAB_STATE), F32), p["s5_wb"][:1],
                       p["s5_wc"][:1], a_m, bsz=mb, tc=N_META, dirs=(False,), emit_y=False,
                       emit_state=True)
    meta_state = st_m[0, :, :1, :]
    u_meta_last = u_m[0, N_META - 1]

    h1_p = _mixer(x_prompt, meta_state, u_meta_last, p)
    h1_s = _mixer(x_sample, meta_state, u_meta_last, p)
    n_p = h1_p.shape[0] * h1_p.shape[1]
    h1 = jnp.concatenate([h1_p.reshape(n_p, d), h1_s.reshape(-1, d)], axis=0)
    n = h1.shape[0]

    tr = min(n, 256)
    eidx, rank, gate, cnt = _router(h1, w_router[l], router_bias[l], tr)
    counts = cnt[:, 0].astype(jnp.int32)
    padded = (counts + EXPERT_BLOCK - 1) // EXPERT_BLOCK * EXPERT_BLOCK
    pad_end = jnp.cumsum(padded)
    pad_start = pad_end - padded
    pos = pad_start[eidx] + rank
    n_blocks = n * TOP_K // EXPERT_BLOCK + N_EXPERTS
    n_slots = n_blocks * EXPERT_BLOCK
    block_e = jnp.clip(jnp.searchsorted(pad_end, jnp.arange(n_blocks, dtype=jnp.int32) * EXPERT_BLOCK,
                                        side='right'), 0, N_EXPERTS - 1).astype(jnp.int32)
    n_used = (pad_end[-1:] // EXPERT_BLOCK).astype(jnp.int32)

    td = min(n, 128)
    xs = _dispatch(h1, pos, jnp.zeros((n_slots, d), F32), td)
    ys = _experts(xs, block_e, n_used, w_exp_gate[l], w_exp_up[l], w_exp_down[l])
    out = _combine(h1, pos, gate.T, ys, w_sh_gate[l].astype(BF16), w_sh_up[l].astype(BF16),
                   w_sh_down[l].astype(BF16), ln2_g[l], ln2_b[l], td)
    return (out[:n_p].reshape(x_prompt.shape), out[n_p:].reshape(x_sample.shape))
```

```python
import functools
import math

import jax
import jax.numpy as jnp
from jax import lax
from jax.experimental import pallas as pl
from jax.experimental.pallas import tpu as pltpu

F32 = jnp.float32
BF16 = jnp.bfloat16

N_META = 16
CONV_WIDTH = 3
SSM_GROUP = 16
SSM_STATE = 64
N_EXPERTS = 256
TOP_K = 8
N_EXPERT_GROUPS = 8
TOPK_GROUPS = 4
ROUTED_SCALE = 2.5
DEPTH = 1
DEEPNORM_ALPHA = (2.0 * DEPTH) ** 0.25
LN_EPS = 1e-5
RMS_EPS = 1e-6

LANES = 128
SUBLANES = 8
N_SLABS = 4
GROUPS_PER_SLAB = LANES // SSM_GROUP
SLAB_STATE = GROUPS_PER_SLAB * SSM_STATE
EXPERT_TILE = 256
ROUTER_TILE = 256
GATHER_TILE = 128
VMEM_LIMIT = 48 * 1024 * 1024


def _cparams(*sem):
    return pltpu.CompilerParams(dimension_semantics=sem, vmem_limit_bytes=VMEM_LIMIT)


def _layer_norm(x, g, b):
    mu = jnp.mean(x, axis=-1, keepdims=True)
    xc = x - mu
    var = jnp.mean(xc * xc, axis=-1, keepdims=True)
    return xc * lax.rsqrt(var + LN_EPS) * g + b


def _rms_norm(x, g):
    return x * lax.rsqrt(jnp.mean(x * x, axis=-1, keepdims=True) + RMS_EPS) * g


def _pack_halves(x):
    half = x.shape[-1] // 2
    bits = pltpu.bitcast(x.astype(BF16).astype(F32), jnp.uint32)
    return (bits[:, :half] >> 16) | (bits[:, half:] & jnp.uint32(0xFFFF0000))


def _unpack_halves(p):
    lo = pltpu.bitcast(p << 16, F32).astype(BF16)
    hi = pltpu.bitcast(p & jnp.uint32(0xFFFF0000), F32).astype(BF16)
    return lo, hi


def _embed_inproj_kernel(x_ref, g_ref, b_ref, w_ref, h0_ref, gb_ref, u_ref, su_ref, *, bsz, tt):
    d = x_ref.shape[-1]
    x = x_ref[...].reshape(bsz * tt, d)
    h0 = _layer_norm(x, g_ref[...], b_ref[...])
    h0_ref[...] = h0.reshape(bsz, tt, d)
    proj = jnp.dot(h0.astype(BF16), w_ref[...], preferred_element_type=F32)
    wc = gb_ref.shape[-1]
    gb_ref[...] = proj[:, :wc].reshape(bsz, tt, wc)
    u_ref[...] = (proj[:, wc:2 * wc] * proj[:, 2 * wc:3 * wc]).reshape(bsz, tt, wc)
    s_u = proj[:, 3 * wc:]
    for b in range(bsz):
        for j in range(N_SLABS):
            val = s_u[b * tt:(b + 1) * tt, j * LANES:(j + 1) * LANES]
            if bsz == 1:
                su_ref[j] = val
            else:
                su_ref[j, pl.ds(b, tt, stride=bsz), :] = val


def _embed_inproj(x, ln_g, ln_b, w_in_bf, tt):
    bsz, seq, d = x.shape
    e = w_in_bf.shape[1]
    wc = (e - N_SLABS * LANES) // 3
    nt = seq // tt
    kern = functools.partial(_embed_inproj_kernel, bsz=bsz, tt=tt)
    return pl.pallas_call(
        kern,
        grid=(nt,),
        in_specs=[
            pl.BlockSpec((bsz, tt, d), lambda i: (0, i, 0)),
            pl.BlockSpec((1, d), lambda i: (0, 0)),
            pl.BlockSpec((1, d), lambda i: (0, 0)),
            pl.BlockSpec((d, e), lambda i: (0, 0)),
        ],
        out_specs=[
            pl.BlockSpec((bsz, tt, d), lambda i: (0, i, 0)),
            pl.BlockSpec((bsz, tt, wc), lambda i: (0, i, 0)),
            pl.BlockSpec((bsz, tt, wc), lambda i: (0, i, 0)),
            pl.BlockSpec((N_SLABS, tt * bsz, LANES), lambda i: (0, i, 0)),
        ],
        out_shape=[
            jax.ShapeDtypeStruct((bsz, seq, d), F32),
            jax.ShapeDtypeStruct((bsz, seq, wc), F32),
            jax.ShapeDtypeStruct((bsz, seq, wc), F32),
            jax.ShapeDtypeStruct((N_SLABS, seq * bsz, LANES), F32),
        ],
        compiler_params=_cparams("parallel"),
        name="embed_inproj",
    )(x, ln_g.reshape(1, d), ln_b.reshape(1, d), w_in_bf)


def _s5_kernel(*refs, bsz, tc, dirs, emit_y, emit_state):
    nd = len(dirs)
    it = iter(refs)
    u_refs = [next(it) for _ in range(nd)]
    s0_ref, wb_ref, wc_ref, a_ref = next(it), next(it), next(it), next(it)
    y_refs = [next(it) for _ in range(nd)] if emit_y else []
    sf_ref = next(it) if emit_state else None
    bu_ref, st_ref = next(it), next(it)
    half = SLAB_STATE

    @pl.when(pl.program_id(0) == 0)
    def _():
        st_ref[...] = s0_ref[...]

    for k in range(nd):
        for j in range(N_SLABS):
            bu_ref[k, j] = jnp.dot(u_refs[k][j].astype(BF16), wb_ref[k, j],
                                   preferred_element_type=F32)

    per_tile = SUBLANES // bsz
    n_tiles = tc // per_tile
    first_half = lax.broadcasted_iota(jnp.int32, (SUBLANES, half), 0) < bsz

    for j in range(N_SLABS):
        a_re = [a_ref[k, j, :, :half] for k in range(nd)]
        a_im = [a_ref[k, j, :, half:] for k in range(nd)]
        init = tuple((st_ref[k, j, :, :half], st_ref[k, j, :, half:]) for k in range(nd))

        def step(i, carry, j=j, a_re=a_re, a_im=a_im):
            out = []
            for k in range(nd):
                ti = (n_tiles - 1 - i) if dirs[k] else i
                row = pl.multiple_of(ti * SUBLANES, SUBLANES)
                x_re = bu_ref[k, j, pl.ds(row, SUBLANES), :half]
                x_im = bu_ref[k, j, pl.ds(row, SUBLANES), half:]

                def advance(s_re, s_im, k=k, x_re=x_re, x_im=x_im):
                    return (a_re[k] * s_re - a_im[k] * s_im + x_re,
                            a_re[k] * s_im + a_im[k] * s_re + x_im)

                c_re, c_im = carry[k]
                if per_tile == 1:
                    n_re, n_im = advance(c_re, c_im)
                    o_re, o_im = n_re, n_im
                else:
                    t1_re, t1_im = advance(pltpu.roll(c_re, bsz, axis=0),
                                           pltpu.roll(c_im, bsz, axis=0))
                    n_re, n_im = advance(pltpu.roll(t1_re, bsz, axis=0),
                                         pltpu.roll(t1_im, bsz, axis=0))
                    if dirs[k]:
                        o_re = jnp.where(first_half, n_re, t1_re)
                        o_im = jnp.where(first_half, n_im, t1_im)
                    else:
                        o_re = jnp.where(first_half, t1_re, n_re)
                        o_im = jnp.where(first_half, t1_im, n_im)
                bu_ref[k, j, pl.ds(row, SUBLANES), :half] = o_re
                bu_ref[k, j, pl.ds(row, SUBLANES), half:] = o_im
                out.append((n_re, n_im))
            return tuple(out)

        fin = lax.fori_loop(0, n_tiles, step, init)
        for k in range(nd):
            st_ref[k, j, :, :half] = fin[k][0]
            st_ref[k, j, :, half:] = fin[k][1]

    if emit_y:
        for k in range(nd):
            for j in range(N_SLABS):
                y_refs[k][j] = jnp.dot(bu_ref[k, j].astype(BF16), wc_ref[k, j],
                                       preferred_element_type=F32)
    if emit_state:
        sf_ref[...] = st_ref[...]


def _s5_scan(su, s0, wb, wc, a_b, *, bsz, tc, dirs, emit_y, emit_state):
    rows = su.shape[1]
    seq = rows // bsz
    nc = seq // tc
    r = tc * bsz
    nd = len(dirs)
    sw = 2 * SLAB_STATE

    def u_map(rev):
        return (lambda c: (0, nc - 1 - c, 0)) if rev else (lambda c: (0, c, 0))

    in_specs = [pl.BlockSpec((N_SLABS, r, LANES), u_map(rev)) for rev in dirs]
    in_specs += [
        pl.BlockSpec((nd, N_SLABS, SUBLANES, sw), lambda c: (0, 0, 0, 0)),
        pl.BlockSpec((nd, N_SLABS, LANES, sw), lambda c: (0, 0, 0, 0)),
        pl.BlockSpec((nd, N_SLABS, sw, LANES), lambda c: (0, 0, 0, 0)),
        pl.BlockSpec((nd, N_SLABS, SUBLANES, sw), lambda c: (0, 0, 0, 0)),
    ]
    out_specs, out_shape = [], []
    if emit_y:
        for rev in dirs:
            out_specs.append(pl.BlockSpec((N_SLABS, r, LANES), u_map(rev)))
            out_shape.append(jax.ShapeDtypeStruct((N_SLABS, rows, LANES), F32))
    if emit_state:
        out_specs.append(pl.BlockSpec((nd, N_SLABS, SUBLANES, sw), lambda c: (0, 0, 0, 0)))
        out_shape.append(jax.ShapeDtypeStruct((nd, N_SLABS, SUBLANES, sw), F32))
    kern = functools.partial(_s5_kernel, bsz=bsz, tc=tc, dirs=dirs, emit_y=emit_y,
                             emit_state=emit_state)
    return pl.pallas_call(
        kern,
        grid=(nc,),
        in_specs=in_specs,
        out_specs=out_specs,
        out_shape=out_shape,
        scratch_shapes=[
            pltpu.VMEM((nd, N_SLABS, r, sw), F32),
            pltpu.VMEM((nd, N_SLABS, SUBLANES, sw), F32),
        ],
        compiler_params=_cparams("arbitrary"),
        name="s5_scan",
    )(*([su] * nd), s0, wb, wc, a_b)


def _s5_params(lam_re, lam_im, log_step, b_re, b_im, c_re, c_im):
    g, p = lam_re.shape
    h = b_re.shape[-1]
    dt = jnp.exp(log_step.astype(F32))[:, None]
    mag = jnp.exp(lam_re * dt)
    ab_re = mag * jnp.cos(lam_im * dt)
    ab_im = mag * jnp.sin(lam_im * dt)
    den = lam_re * lam_re + lam_im * lam_im
    nr, ni = ab_re - 1.0, ab_im
    f_re = (nr * lam_re + ni * lam_im) / den
    f_im = (ni * lam_re - nr * lam_im) / den
    bb_re = f_re[..., None] * b_re - f_im[..., None] * b_im
    bb_im = f_re[..., None] * b_im + f_im[..., None] * b_re
    ns, gl = N_SLABS, GROUPS_PER_SLAB
    eye = jnp.eye(gl, dtype=F32)

    def in_block(bb):
        bb = bb.reshape(ns, gl, p, h)
        return jnp.einsum('sgph,gk->sghkp', bb, eye).reshape(ns, gl * h, gl * p)

    def out_block(cc):
        cc = cc.reshape(ns, gl, h, p)
        return jnp.einsum('sghp,gk->sgpkh', cc, eye).reshape(ns, gl * p, gl * h)

    wb = jnp.concatenate([in_block(bb_re), in_block(bb_im)], axis=-1)
    wc = jnp.concatenate([out_block(c_re.astype(F32)), -out_block(c_im.astype(F32))], axis=1)
    a = jnp.concatenate([ab_re.reshape(ns, gl * p), ab_im.reshape(ns, gl * p)], axis=-1)
    return wb.astype(BF16), wc.astype(BF16), a


def _mixer_tail_kernel(h0_ref, gb_ref, u_ref, up_ref, un_ref, um_ref, su_ref, yf_ref, yr_ref,
                       cw_ref, cb_ref, sd_ref, wg_ref, bg_ref, na_ref, nb_ref, wo_ref,
                       g1_ref, b1_ref, h1_ref, h1p_ref, *, bsz, tt):
    i = pl.program_id(0)
    nt = pl.num_programs(0)
    d = h0_ref.shape[-1]
    wcv = gb_ref.shape[-1]
    row_id = lax.broadcasted_iota(jnp.int32, (tt, wcv), 0)
    ya, ys = [], []
    for b in range(bsz):
        u = u_ref[b]
        prev_edge = jnp.where(i == 0, um_ref[...], up_ref[b, SUBLANES - 1:SUBLANES, :])
        next_edge = jnp.where(i == nt - 1, jnp.zeros((1, wcv), F32), un_ref[b, 0:1, :])
        u_prev = jnp.where(row_id == 0, prev_edge, pltpu.roll(u, 1, axis=0))
        u_next = jnp.where(row_id == tt - 1, next_edge, pltpu.roll(u, tt - 1, axis=0))
        conv = u_prev * cw_ref[0:1, :] + u * cw_ref[1:2, :] + u_next * cw_ref[2:3, :] + cb_ref[...]
        ya.append(gb_ref[b] * conv)

        def slab(ref, b=b):
            parts = []
            for j in range(N_SLABS):
                if bsz == 1:
                    parts.append(ref[j])
                else:
                    parts.append(ref[j, pl.ds(b, tt, stride=bsz), :])
            return jnp.concatenate(parts, axis=-1)

        ys.append(slab(yf_ref) + slab(yr_ref) + sd_ref[...] * slab(su_ref))
    y_a = jnp.concatenate(ya, axis=0)
    y_s = jnp.concatenate(ys, axis=0)
    z = jax.nn.gelu(y_s)
    glu = jnp.dot(z.astype(BF16), wg_ref[...], preferred_element_type=F32) + bg_ref[...]
    y_b = z * jax.nn.sigmoid(glu)
    merged = jnp.concatenate([_rms_norm(y_a, na_ref[...]), _rms_norm(y_b, nb_ref[...])], axis=-1)
    m = jnp.dot(merged.astype(BF16), wo_ref[...], preferred_element_type=F32)
    h0 = h0_ref[...].reshape(bsz * tt, d)
    h1 = _layer_norm(DEEPNORM_ALPHA * h0 + m, g1_ref[...], b1_ref[...])
    h1_ref[...] = h1.reshape(bsz, tt, d)
    h1p_ref[...] = _pack_halves(h1).reshape(bsz, tt, d // 2)


def _mixer_tail(h0, gb, u, u_meta_last, su, yf, yr, conv_w, conv_b, ssm_d, w_glu_bf, b_glu,
                norm_a_g, norm_b_g, w_out_bf, ln1_g, ln1_b, tt):
    bsz, seq, d = h0.shape
    wcv = gb.shape[-1]
    ws = N_SLABS * LANES
    nt = seq // tt
    tb = tt // SUBLANES
    nb8 = seq // SUBLANES
    kern = functools.partial(_mixer_tail_kernel, bsz=bsz, tt=tt)
    row = lambda n: pl.BlockSpec((1, n), lambda i: (0, 0))
    slab_spec = pl.BlockSpec((N_SLABS, tt * bsz, LANES), lambda i: (0, i, 0))
    return pl.pallas_call(
        kern,
        grid=(nt,),
        in_specs=[
            pl.BlockSpec((bsz, tt, d), lambda i: (0, i, 0)),
            pl.BlockSpec((bsz, tt, wcv), lambda i: (0, i, 0)),
            pl.BlockSpec((bsz, tt, wcv), lambda i: (0, i, 0)),
            pl.BlockSpec((bsz, SUBLANES, wcv), lambda i: (0, jnp.maximum(i * tb - 1, 0), 0)),
            pl.BlockSpec((bsz, SUBLANES, wcv), lambda i: (0, jnp.minimum((i + 1) * tb, nb8 - 1), 0)),
            row(wcv),
            slab_spec, slab_spec, slab_spec,
            pl.BlockSpec((CONV_WIDTH, wcv), lambda i: (0, 0)),
            row(wcv), row(ws),
            pl.BlockSpec((ws, ws), lambda i: (0, 0)),
            row(ws), row(wcv), row(ws),
            pl.BlockSpec((wcv + ws, d), lambda i: (0, 0)),
            row(d), row(d),
        ],
        out_specs=[
            pl.BlockSpec((bsz, tt, d), lambda i: (0, i, 0)),
            pl.BlockSpec((bsz, tt, d // 2), lambda i: (0, i, 0)),
        ],
        out_shape=[
            jax.ShapeDtypeStruct((bsz, seq, d), F32),
            jax.ShapeDtypeStruct((bsz, seq, d // 2), jnp.uint32),
        ],
        compiler_params=_cparams("parallel"),
        name="mixer_tail",
    )(h0, gb, u, u, u, u_meta_last.reshape(1, wcv), su, yf, yr, conv_w, conv_b.reshape(1, wcv),
      ssm_d.reshape(1, ws), w_glu_bf, b_glu.reshape(1, ws), norm_a_g.reshape(1, wcv),
      norm_b_g.reshape(1, ws), w_out_bf, ln1_g.reshape(1, d), ln1_b.reshape(1, d))


def _dual_row_specs(rows, width, nq_p):
    return [pl.BlockSpec((rows, width), lambda q, *_: (jnp.minimum(q, nq_p - 1), 0)),
            pl.BlockSpec((rows, width), lambda q, *_: (jnp.maximum(q - nq_p, 0), 0))]


def _router_kernel(hp_ref, hs_ref, wh_ref, wl_ref, bias_ref, eidx_ref, rank_ref, gate_ref, cnt_ref,
                   cnt_scr, *, tt, nq_p):
    ne = wh_ref.shape[0]
    epg = ne // N_EXPERT_GROUPS
    neg = jnp.float32(-jnp.inf)

    @pl.when(pl.program_id(0) == 0)
    def _():
        cnt_scr[...] = jnp.zeros_like(cnt_scr)

    h = jnp.where(pl.program_id(0) < nq_p, hp_ref[...], hs_ref[...])
    hh = h.astype(BF16)
    hl = (h - hh.astype(F32)).astype(BF16)
    dn = (((1,), (1,)), ((), ()))
    wh, wl = wh_ref[...], wl_ref[...]
    logits = (lax.dot_general(wh, hh, dn, preferred_element_type=F32)
              + lax.dot_general(wh, hl, dn, preferred_element_type=F32)
              + lax.dot_general(wl, hh, dn, preferred_element_type=F32))
    scores = jax.nn.sigmoid(logits)
    sel = scores + bias_ref[...]

    gi = lax.broadcasted_iota(jnp.int32, (epg, tt), 0)
    gs = []
    for g in range(N_EXPERT_GROUPS):
        x = sel[g * epg:(g + 1) * epg, :]
        m1 = jnp.max(x, axis=0, keepdims=True)
        i1 = jnp.min(jnp.where(x == m1, gi, epg), axis=0, keepdims=True)
        m2 = jnp.max(jnp.where(gi == i1, neg, x), axis=0, keepdims=True)
        gs.append(m1 + m2)
    chosen = [jnp.zeros((1, tt), F32) for _ in range(N_EXPERT_GROUPS)]
    for _ in range(TOPK_GROUPS):
        m = gs[0]
        for g in range(1, N_EXPERT_GROUPS):
            m = jnp.maximum(m, gs[g])
        found = jnp.zeros((1, tt), F32)
        for g in range(N_EXPERT_GROUPS):
            hit = jnp.where((gs[g] == m) & (found == 0.0), 1.0, 0.0)
            chosen[g] = chosen[g] + hit
            found = found + hit
            gs[g] = jnp.where(hit > 0.0, neg, gs[g])
    selm = jnp.concatenate(
        [jnp.where(chosen[g] > 0.0, sel[g * epg:(g + 1) * epg, :], neg)
         for g in range(N_EXPERT_GROUPS)], axis=0)

    ei = lax.broadcasted_iota(jnp.int32, (ne, tt), 0)
    msel = jnp.zeros((ne, tt), F32)
    idxs, gvals = [], []
    for _ in range(TOP_K):
        m = jnp.max(selm, axis=0, keepdims=True)
        idx = jnp.min(jnp.where(selm == m, ei, ne), axis=0, keepdims=True)
        hit = ei == idx
        gvals.append(jnp.sum(jnp.where(hit, scores, 0.0), axis=0, keepdims=True))
        selm = jnp.where(hit, neg, selm)
        msel = jnp.where(hit, 1.0, msel)
        idxs.append(idx)
    gsum = gvals[0]
    for k in range(1, TOP_K):
        gsum = gsum + gvals[k]
    gate_ref[...] = jnp.concatenate([gv / gsum * ROUTED_SCALE for gv in gvals], axis=0)
    eidx_ref[...] = jnp.concatenate(idxs, axis=0)

    r_i = lax.broadcasted_iota(jnp.int32, (tt, tt), 0)
    c_i = lax.broadcasted_iota(jnp.int32, (tt, tt), 1)
    upper = jnp.where(r_i < c_i, 1.0, 0.0).astype(BF16)
    rank_full = jnp.dot(msel.astype(BF16), upper, preferred_element_type=F32) + cnt_scr[...]
    ranks = [jnp.sum(jnp.where(ei == idxs[k], rank_full, 0.0), axis=0, keepdims=True)
             for k in range(TOP_K)]
    rank_ref[...] = jnp.concatenate(ranks, axis=0).astype(jnp.int32)
    cnt_scr[...] = cnt_scr[...] + jnp.sum(msel, axis=1, keepdims=True)
    cnt_ref[...] = cnt_scr[...]


def _router(h_p, h_s, w_router, router_bias, tt):
    d = h_p.shape[1]
    n = h_p.shape[0] + h_s.shape[0]
    nq_p = h_p.shape[0] // tt
    ne = w_router.shape[1]
    wt = w_router.astype(F32).T
    wh = wt.astype(BF16)
    wl = (wt - wh.astype(F32)).astype(BF16)
    kern = functools.partial(_router_kernel, tt=tt, nq_p=nq_p)
    return pl.pallas_call(
        kern,
        grid=(n // tt,),
        in_specs=_dual_row_specs(tt, d, nq_p) + [
            pl.BlockSpec((ne, d), lambda i: (0, 0)),
            pl.BlockSpec((ne, d), lambda i: (0, 0)),
            pl.BlockSpec((ne, 1), lambda i: (0, 0)),
        ],
        out_specs=[
            pl.BlockSpec((TOP_K, tt), lambda i: (0, i)),
            pl.BlockSpec((TOP_K, tt), lambda i: (0, i)),
            pl.BlockSpec((TOP_K, tt), lambda i: (0, i)),
            pl.BlockSpec((ne, 1), lambda i: (0, 0)),
        ],
        out_shape=[
            jax.ShapeDtypeStruct((TOP_K, n), jnp.int32),
            jax.ShapeDtypeStruct((TOP_K, n), jnp.int32),
            jax.ShapeDtypeStruct((TOP_K, n), F32),
            jax.ShapeDtypeStruct((ne, 1), F32),
        ],
        scratch_shapes=[pltpu.VMEM((ne, 1), F32)],
        compiler_params=_cparams("arbitrary"),
        name="router",
    )(h_p, h_s, wh, wl, router_bias.astype(F32).reshape(ne, 1))


def _positions_kernel(eidx_ref, rank_ref, start_ref, pos_ref):
    ne = start_ref.shape[0]
    tt = eidx_ref.shape[1]
    ei = lax.broadcasted_iota(jnp.int32, (ne, tt), 0)
    start = start_ref[...]
    rows = [jnp.sum(jnp.where(ei == eidx_ref[k:k + 1, :], start, 0.0), axis=0, keepdims=True)
            for k in range(TOP_K)]
    pos_ref[...] = jnp.concatenate(rows, axis=0).astype(jnp.int32) + rank_ref[...]


def _positions(eidx, rank, start, tt):
    n = eidx.shape[1]
    ne = start.shape[0]
    return pl.pallas_call(
        _positions_kernel,
        grid=(n // tt,),
        in_specs=[
            pl.BlockSpec((TOP_K, tt), lambda i: (0, i)),
            pl.BlockSpec((TOP_K, tt), lambda i: (0, i)),
            pl.BlockSpec((ne, 1), lambda i: (0, 0)),
        ],
        out_specs=pl.BlockSpec((TOP_K, tt), lambda i: (0, i)),
        out_shape=jax.ShapeDtypeStruct((TOP_K, n), jnp.int32),
        compiler_params=_cparams("parallel"),
        name="positions",
    )(eidx, rank, start.astype(F32).reshape(ne, 1))


def _dispatch_kernel(pos_ref, hp_ref, hs_ref, xs_ref, sem, *, tt, nq_p):
    def issue_from(h_ref):
        def issue(t, carry):
            for k in range(TOP_K):
                pltpu.make_async_copy(h_ref.at[t], xs_ref.at[pos_ref[k, t]], sem).start()
            return carry

        lax.fori_loop(0, tt, issue, 0)

    is_p = pl.program_id(0) < nq_p
    pl.when(is_p)(lambda: issue_from(hp_ref))
    pl.when(jnp.logical_not(is_p))(lambda: issue_from(hs_ref))
    for _ in range(TOP_K):
        pltpu.make_async_copy(hp_ref, xs_ref.at[pl.ds(0, tt)], sem).wait()


def _dispatch(hp_p, hp_s, pos, tt):
    w = hp_p.shape[1]
    n = hp_p.shape[0] + hp_s.shape[0]
    nq_p = hp_p.shape[0] // tt
    kern = functools.partial(_dispatch_kernel, tt=tt, nq_p=nq_p)
    return pl.pallas_call(
        kern,
        grid=(n // tt,),
        in_specs=[pl.BlockSpec((TOP_K, tt), lambda i: (0, i), memory_space=pltpu.SMEM)]
        + _dual_row_specs(tt, w, nq_p),
        out_specs=pl.BlockSpec(memory_space=pl.ANY),
        out_shape=jax.ShapeDtypeStruct((n * TOP_K, w), hp_p.dtype),
        scratch_shapes=[pltpu.SemaphoreType.DMA],
        compiler_params=_cparams("arbitrary"),
        name="dispatch",
    )(pos, hp_p, hp_s)


def _experts_kernel(tile_ref, exp_ref, ns_ref, off_ref, xs_ref, wg_ref, wu_ref, wd_ref, ys_ref,
                    wgb, wub, wdb, acc, *, tm):
    s = pl.program_id(0)
    prev = jnp.maximum(s - 1, 0)
    e = exp_ref[s]
    t = tile_ref[s]
    half = xs_ref.shape[1]

    @pl.when((s == 0) | (e != exp_ref[prev]))
    def _():
        wgb[...] = wg_ref[0].astype(BF16)
        wub[...] = wu_ref[0].astype(BF16)
        wdb[...] = wd_ref[0].astype(BF16)

    @pl.when(s < ns_ref[0])
    def _():
        lo, hi = _unpack_halves(xs_ref[...])
        g = (jnp.dot(lo, wgb[:half], preferred_element_type=F32)
             + jnp.dot(hi, wgb[half:], preferred_element_type=F32))
        u = (jnp.dot(lo, wub[:half], preferred_element_type=F32)
             + jnp.dot(hi, wub[half:], preferred_element_type=F32))
        row = t * tm + lax.broadcasted_iota(jnp.int32, (tm, 1), 0)
        valid = (row >= off_ref[e]) & (row < off_ref[e + 1])
        act = jnp.where(valid, (g * jax.nn.sigmoid(g)) * u, 0.0)
        y = jnp.dot(act.astype(BF16), wdb[...], preferred_element_type=F32)
        first = (s == 0) | (t != tile_ref[prev])

        @pl.when(first)
        def _():
            acc[...] = y

        @pl.when(jnp.logical_not(first))
        def _():
            acc[...] = acc[...] + y

        ys_ref[...] = _pack_halves(acc[...])


def _experts(xs, tile_of, exp_of, n_steps, offsets, w_gate, w_up, w_down, tm):
    n_rows, half = xs.shape
    ne, d, de = w_gate.shape
    s_max = tile_of.shape[0]

    def row_map(s, tile, ex, ns, off):
        return (tile[s], 0)

    def w_map(s, tile, ex, ns, off):
        return (ex[s], 0, 0)

    grid_spec = pltpu.PrefetchScalarGridSpec(
        num_scalar_prefetch=4,
        grid=(s_max,),
        in_specs=[
            pl.BlockSpec((tm, half), row_map),
            pl.BlockSpec((1, d, de), w_map),
            pl.BlockSpec((1, d, de), w_map),
            pl.BlockSpec((1, de, d), w_map),
        ],
        out_specs=pl.BlockSpec((tm, half), row_map),
        scratch_shapes=[
            pltpu.VMEM((d, de), BF16),
            pltpu.VMEM((d, de), BF16),
            pltpu.VMEM((de, d), BF16),
            pltpu.VMEM((tm, d), F32),
        ],
    )
    return pl.pallas_call(
        functools.partial(_experts_kernel, tm=tm),
        grid_spec=grid_spec,
        out_shape=jax.ShapeDtypeStruct((n_rows, half), jnp.uint32),
        compiler_params=_cparams("arbitrary"),
        name="experts",
    )(tile_of, exp_of, n_steps, offsets, xs, w_gate, w_up, w_down)


def _expert_schedule(counts, n_rows, tm):
    ne = counts.shape[0]
    s_max = n_rows // tm + ne
    off = jnp.concatenate([jnp.zeros((1,), jnp.int32), jnp.cumsum(counts)]).astype(jnp.int32)
    first_tile = off[:-1] // tm
    last_tile = (off[1:] - 1) // tm
    visits = jnp.where(counts > 0, last_tile - first_tile + 1, 0)
    cum = jnp.cumsum(visits)
    n_steps = cum[-1]
    step = jnp.minimum(jnp.arange(s_max, dtype=jnp.int32), n_steps - 1)
    exp_of = jnp.sum((cum[None, :] <= step[:, None]).astype(jnp.int32), axis=1)
    onehot = exp_of[:, None] == jnp.arange(ne, dtype=jnp.int32)[None, :]
    pick = lambda v: jnp.sum(jnp.where(onehot, v[None, :], 0), axis=1)
    tile_of = pick(first_tile) + step - pick(cum - visits)
    return (tile_of.astype(jnp.int32), exp_of.astype(jnp.int32),
            n_steps.reshape(1).astype(jnp.int32), off)


def _combine_kernel(pos_ref, hp_ref, hs_ref, gate_ref, ys_ref, wsg_ref, wsu_ref, wsd_ref, g2_ref,
                    b2_ref, outp_ref, outs_ref, buf, sem, *, tt, nq_p):
    def issue(t, carry):
        for k in range(TOP_K):
            pltpu.make_async_copy(ys_ref.at[pos_ref[k, t]], buf.at[k, t], sem).start()
        return carry

    lax.fori_loop(0, tt, issue, 0)
    is_p = pl.program_id(0) < nq_p
    h = jnp.where(is_p, hp_ref[...], hs_ref[...])
    hb = h.astype(BF16)
    g = jnp.dot(hb, wsg_ref[...], preferred_element_type=F32)
    u = jnp.dot(hb, wsu_ref[...], preferred_element_type=F32)
    act = (g * jax.nn.sigmoid(g)) * u
    f = jnp.dot(act.astype(BF16), wsd_ref[...], preferred_element_type=F32)
    for k in range(TOP_K):
        pltpu.make_async_copy(ys_ref.at[pl.ds(0, tt)], buf.at[k], sem).wait()
    gate = gate_ref[...]
    r_lo = jnp.zeros((tt, buf.shape[-1]), F32)
    r_hi = jnp.zeros((tt, buf.shape[-1]), F32)
    for k in range(TOP_K):
        p = buf[k]
        gk = gate[:, k:k + 1]
        r_lo = r_lo + gk * pltpu.bitcast(p << 16, F32)
        r_hi = r_hi + gk * pltpu.bitcast(p & jnp.uint32(0xFFFF0000), F32)
    f = f + jnp.concatenate([r_lo, r_hi], axis=-1)
    out = _layer_norm(DEEPNORM_ALPHA * h + f, g2_ref[...], b2_ref[...])

    @pl.when(is_p)
    def _():
        outp_ref[...] = out

    @pl.when(jnp.logical_not(is_p))
    def _():
        outs_ref[...] = out


def _combine(h_p, h_s, pos, gate_t, ys, wsg_bf, wsu_bf, wsd_bf, ln2_g, ln2_b, tt):
    d = h_p.shape[1]
    n_p, n_s = h_p.shape[0], h_s.shape[0]
    nq_p = n_p // tt
    ds_ = wsg_bf.shape[1]
    kern = functools.partial(_combine_kernel, tt=tt, nq_p=nq_p)
    return pl.pallas_call(
        kern,
        grid=((n_p + n_s) // tt,),
        in_specs=[pl.BlockSpec((TOP_K, tt), lambda i: (0, i), memory_space=pltpu.SMEM)]
        + _dual_row_specs(tt, d, nq_p) + [
            pl.BlockSpec((tt, TOP_K), lambda i: (i, 0)),
            pl.BlockSpec(memory_space=pl.ANY),
            pl.BlockSpec((d, ds_), lambda i: (0, 0)),
            pl.BlockSpec((d, ds_), lambda i: (0, 0)),
            pl.BlockSpec((ds_, d), lambda i: (0, 0)),
            pl.BlockSpec((1, d), lambda i: (0, 0)),
            pl.BlockSpec((1, d), lambda i: (0, 0)),
        ],
        out_specs=_dual_row_specs(tt, d, nq_p),
        out_shape=[jax.ShapeDtypeStruct((n_p, d), F32), jax.ShapeDtypeStruct((n_s, d), F32)],
        scratch_shapes=[pltpu.VMEM((TOP_K, tt, ys.shape[1]), ys.dtype), pltpu.SemaphoreType.DMA],
        compiler_params=_cparams("arbitrary"),
        name="combine",
    )(pos, h_p, h_s, gate_t, ys, wsg_bf, wsu_bf, wsd_bf, ln2_g.reshape(1, d), ln2_b.reshape(1, d))


def _pick_tile(seq, bsz, rows):
    return max(SUBLANES, min(seq, rows // bsz))


def _mixer(x, meta_state, u_meta_last, p):
    bsz, seq, _ = x.shape
    tt = _pick_tile(seq, bsz, 1024)
    h0, gb, u, su = _embed_inproj(x, p["ln_emb_g"], p["ln_emb_b"], p["w_in_bf"], tt)
    tc = _pick_tile(seq, bsz, 256)
    s0 = jnp.stack([jnp.broadcast_to(meta_state, (N_SLABS, SUBLANES, 2 * SLAB_STATE)),
                    jnp.zeros((N_SLABS, SUBLANES, 2 * SLAB_STATE), F32)])
    a_b = jnp.broadcast_to(p["s5_a"][:, :, None, :], (2, N_SLABS, SUBLANES, 2 * SLAB_STATE))
    yf, yr = _s5_scan(su, s0, p["s5_wb"], p["s5_wc"], a_b, bsz=bsz, tc=tc, dirs=(False, True),
                      emit_y=True, emit_state=False)
    tt3 = _pick_tile(seq, bsz, 512)
    h1, h1p = _mixer_tail(h0, gb, u, u_meta_last, su, yf, yr, p["conv_w"], p["conv_b"],
                          p["ssm_d"], p["w_glu_bf"], p["b_glu"], p["norm_a_g"], p["norm_b_g"],
                          p["w_out_bf"], p["ln1_g"], p["ln1_b"], tt3)
    d = h1.shape[-1]
    return h1.reshape(bsz * seq, d), h1p.reshape(bsz * seq, d // 2)


def kernel(x_prompt, x_sample, meta_tokens, ln_emb_g, ln_emb_b, w_in, conv_w, conv_b, ssm_lambda_re, ssm_lambda_im, ssm_log_step, ssm_b_re, ssm_b_im, ssm_c_re, ssm_c_im, ssm_d, w_glu, b_glu, norm_a_g, norm_b_g, w_out, ln1_g, ln1_b, w_router, router_bias, w_exp_gate, w_exp_up, w_exp_down, w_sh_gate, w_sh_up, w_sh_down, ln2_g, ln2_b):
    l = 0
    d = x_prompt.shape[-1]
    dirs = [_s5_params(ssm_lambda_re[l, k].astype(F32), ssm_lambda_im[l, k].astype(F32),
                       ssm_log_step[l, k], ssm_b_re[l, k].astype(F32), ssm_b_im[l, k].astype(F32),
                       ssm_c_re[l, k], ssm_c_im[l, k]) for k in range(2)]
    p = dict(
        ln_emb_g=ln_emb_g, ln_emb_b=ln_emb_b, w_in_bf=w_in[l].astype(BF16),
        conv_w=conv_w[l], conv_b=conv_b[l], ssm_d=ssm_d[l],
        w_glu_bf=w_glu[l].astype(BF16), b_glu=b_glu[l], norm_a_g=norm_a_g[l],
        norm_b_g=norm_b_g[l], w_out_bf=w_out[l].astype(BF16), ln1_g=ln1_g[l], ln1_b=ln1_b[l],
        s5_wb=jnp.stack([dirs[0][0], dirs[1][0]]), s5_wc=jnp.stack([dirs[0][1], dirs[1][1]]),
        s5_a=jnp.stack([dirs[0][2], dirs[1][2]]),
    )
    mb = SUBLANES
    xm = jnp.broadcast_to(meta_tokens.astype(F32)[None], (mb, N_META, d))
    _, _, u_m, su_m = _embed_inproj(xm, ln_emb_g, ln_emb_b, p["w_in_bf"], N_META)
    a_m = jnp.broadcast_to(p["s5_a"][:1, :, None, :], (1, N_SLABS, mb, 2 * SLAB_STATE))
    (st_m,) = _s5_scan(su_m, jnp.zeros((1, N_SLABS, mb, 2 * SLAB_STATE), F32), p["s5_wb"][:1],
                       p["s5_wc"][:1], a_m, bsz=mb, tc=N_META, dirs=(False,), emit_y=False,
                       emit_state=True)
    meta_state = st_m[0, :, :1, :]
    u_meta_last = u_m[0, N_META - 1]

    h1_p, h1p_p = _mixer(x_prompt, meta_state, u_meta_last, p)
    h1_s, h1p_s = _mixer(x_sample, meta_state, u_meta_last, p)
    n = h1_p.shape[0] + h1_s.shape[0]

    tr = min(h1_p.shape[0], h1_s.shape[0], ROUTER_TILE)
    eidx, rank, gate, cnt = _router(h1_p, h1_s, w_router[l], router_bias[l], tr)
    counts = cnt[:, 0].astype(jnp.int32)
    tile_of, exp_of, n_steps, offsets = _expert_schedule(counts, n * TOP_K, EXPERT_TILE)
    pos = _positions(eidx, rank, offsets[:-1], tr)
    td = min(h1_p.shape[0], h1_s.shape[0], GATHER_TILE)
    xs = _dispatch(h1p_p, h1p_s, pos, td)
    ys = _experts(xs, tile_of, exp_of, n_steps, offsets, w_exp_gate[l], w_exp_up[l],
                  w_exp_down[l], EXPERT_TILE)
    out_p, out_s = _combine(h1_p, h1_s, pos, gate.T, ys, w_sh_gate[l].astype(BF16),
                            w_sh_up[l].astype(BF16), w_sh_down[l].astype(BF16), ln2_g[l],
                            ln2_b[l], td)
    return (out_p.reshape(x_prompt.shape), out_s.reshape(x_sample.shape))
```

```python
import functools
import math

import jax
import jax.numpy as jnp
from jax import lax
from jax.experimental import pallas as pl
from jax.experimental.pallas import tpu as pltpu
from jax.experimental.pallas import tpu_sc as plsc

F32 = jnp.float32
BF16 = jnp.bfloat16

N_META = 16
CONV_WIDTH = 3
SSM_GROUP = 16
SSM_STATE = 64
N_EXPERTS = 256
TOP_K = 8
N_EXPERT_GROUPS = 8
TOPK_GROUPS = 4
ROUTED_SCALE = 2.5
DEPTH = 1
DEEPNORM_ALPHA = (2.0 * DEPTH) ** 0.25
LN_EPS = 1e-5
RMS_EPS = 1e-6

LANES = 128
SUBLANES = 8
N_SLABS = 4
GROUPS_PER_SLAB = LANES // SSM_GROUP
SLAB_STATE = GROUPS_PER_SLAB * SSM_STATE
EXPERT_TILE = 256
ROUTER_TILE = 256
COMBINE_TILE = 128
SC_CHUNK = 64
VMEM_LIMIT = 48 * 1024 * 1024


def _cparams(*sem):
    return pltpu.CompilerParams(dimension_semantics=sem, vmem_limit_bytes=VMEM_LIMIT)


def _layer_norm(x, g, b):
    mu = jnp.mean(x, axis=-1, keepdims=True)
    xc = x - mu
    var = jnp.mean(xc * xc, axis=-1, keepdims=True)
    return xc * lax.rsqrt(var + LN_EPS) * g + b


def _rms_norm(x, g):
    return x * lax.rsqrt(jnp.mean(x * x, axis=-1, keepdims=True) + RMS_EPS) * g


def _pack_halves(x):
    half = x.shape[-1] // 2
    bits = pltpu.bitcast(x.astype(BF16).astype(F32), jnp.uint32)
    return (bits[:, :half] >> 16) | (bits[:, half:] & jnp.uint32(0xFFFF0000))


def _unpack_halves(p):
    lo = pltpu.bitcast(p << 16, F32).astype(BF16)
    hi = pltpu.bitcast(p & jnp.uint32(0xFFFF0000), F32).astype(BF16)
    return lo, hi


def _embed_inproj_kernel(x_ref, g_ref, b_ref, w_ref, h0_ref, gb_ref, u_ref, su_ref, *, bsz, tt):
    d = x_ref.shape[-1]
    x = x_ref[...].reshape(bsz * tt, d)
    h0 = _layer_norm(x, g_ref[...], b_ref[...])
    h0_ref[...] = h0.reshape(bsz, tt, d)
    proj = jnp.dot(h0.astype(BF16), w_ref[...], preferred_element_type=F32)
    wc = gb_ref.shape[-1]
    gb_ref[...] = proj[:, :wc].reshape(bsz, tt, wc)
    u_ref[...] = (proj[:, wc:2 * wc] * proj[:, 2 * wc:3 * wc]).reshape(bsz, tt, wc)
    s_u = proj[:, 3 * wc:]
    for b in range(bsz):
        for j in range(N_SLABS):
            val = s_u[b * tt:(b + 1) * tt, j * LANES:(j + 1) * LANES]
            if bsz == 1:
                su_ref[j] = val
            else:
                su_ref[j, pl.ds(b, tt, stride=bsz), :] = val


def _embed_inproj(x, ln_g, ln_b, w_in_bf, tt):
    bsz, seq, d = x.shape
    e = w_in_bf.shape[1]
    wc = (e - N_SLABS * LANES) // 3
    nt = seq // tt
    kern = functools.partial(_embed_inproj_kernel, bsz=bsz, tt=tt)
    return pl.pallas_call(
        kern,
        grid=(nt,),
        in_specs=[
            pl.BlockSpec((bsz, tt, d), lambda i: (0, i, 0)),
            pl.BlockSpec((1, d), lambda i: (0, 0)),
            pl.BlockSpec((1, d), lambda i: (0, 0)),
            pl.BlockSpec((d, e), lambda i: (0, 0)),
        ],
        out_specs=[
            pl.BlockSpec((bsz, tt, d), lambda i: (0, i, 0)),
            pl.BlockSpec((bsz, tt, wc), lambda i: (0, i, 0)),
            pl.BlockSpec((bsz, tt, wc), lambda i: (0, i, 0)),
            pl.BlockSpec((N_SLABS, tt * bsz, LANES), lambda i: (0, i, 0)),
        ],
        out_shape=[
            jax.ShapeDtypeStruct((bsz, seq, d), F32),
            jax.ShapeDtypeStruct((bsz, seq, wc), F32),
            jax.ShapeDtypeStruct((bsz, seq, wc), F32),
            jax.ShapeDtypeStruct((N_SLABS, seq * bsz, LANES), F32),
        ],
        compiler_params=_cparams("parallel"),
        name="embed_inproj",
    )(x, ln_g.reshape(1, d), ln_b.reshape(1, d), w_in_bf)


def _s5_kernel(*refs, bsz, tc, dirs, emit_y, emit_state):
    nd = len(dirs)
    it = iter(refs)
    u_refs = [next(it) for _ in range(nd)]
    s0_ref, wb_ref, wc_ref, a_ref = next(it), next(it), next(it), next(it)
    y_refs = [next(it) for _ in range(nd)] if emit_y else []
    sf_ref = next(it) if emit_state else None
    bu_ref, st_ref = next(it), next(it)
    half = SLAB_STATE

    @pl.when(pl.program_id(0) == 0)
    def _():
        st_ref[...] = s0_ref[...]

    for k in range(nd):
        for j in range(N_SLABS):
            bu_ref[k, j] = jnp.dot(u_refs[k][j].astype(BF16), wb_ref[k, j],
                                   preferred_element_type=F32)

    per_tile = SUBLANES // bsz
    n_tiles = tc // per_tile
    first_half = lax.broadcasted_iota(jnp.int32, (SUBLANES, half), 0) < bsz

    for j in range(N_SLABS):
        a_re = [a_ref[k, j, :, :half] for k in range(nd)]
        a_im = [a_ref[k, j, :, half:] for k in range(nd)]
        init = tuple((st_ref[k, j, :, :half], st_ref[k, j, :, half:]) for k in range(nd))

        def step(i, carry, j=j, a_re=a_re, a_im=a_im):
            out = []
            for k in range(nd):
                ti = (n_tiles - 1 - i) if dirs[k] else i
                row = pl.multiple_of(ti * SUBLANES, SUBLANES)
                x_re = bu_ref[k, j, pl.ds(row, SUBLANES), :half]
                x_im = bu_ref[k, j, pl.ds(row, SUBLANES), half:]

                def advance(s_re, s_im, k=k, x_re=x_re, x_im=x_im):
                    return (a_re[k] * s_re - a_im[k] * s_im + x_re,
                            a_re[k] * s_im + a_im[k] * s_re + x_im)

                c_re, c_im = carry[k]
                if per_tile == 1:
                    n_re, n_im = advance(c_re, c_im)
                    o_re, o_im = n_re, n_im
                else:
                    t1_re, t1_im = advance(pltpu.roll(c_re, bsz, axis=0),
                                           pltpu.roll(c_im, bsz, axis=0))
                    n_re, n_im = advance(pltpu.roll(t1_re, bsz, axis=0),
                                         pltpu.roll(t1_im, bsz, axis=0))
                    if dirs[k]:
                        o_re = jnp.where(first_half, n_re, t1_re)
                        o_im = jnp.where(first_half, n_im, t1_im)
                    else:
                        o_re = jnp.where(first_half, t1_re, n_re)
                        o_im = jnp.where(first_half, t1_im, n_im)
                bu_ref[k, j, pl.ds(row, SUBLANES), :half] = o_re
                bu_ref[k, j, pl.ds(row, SUBLANES), half:] = o_im
                out.append((n_re, n_im))
            return tuple(out)

        fin = lax.fori_loop(0, n_tiles, step, init)
        for k in range(nd):
            st_ref[k, j, :, :half] = fin[k][0]
            st_ref[k, j, :, half:] = fin[k][1]

    if emit_y:
        for k in range(nd):
            for j in range(N_SLABS):
                y_refs[k][j] = jnp.dot(bu_ref[k, j].astype(BF16), wc_ref[k, j],
                                       preferred_element_type=F32)
    if emit_state:
        sf_ref[...] = st_ref[...]


def _s5_scan(su, s0, wb, wc, a_b, *, bsz, tc, dirs, emit_y, emit_state):
    rows = su.shape[1]
    seq = rows // bsz
    nc = seq // tc
    r = tc * bsz
    nd = len(dirs)
    sw = 2 * SLAB_STATE

    def u_map(rev):
        return (lambda c: (0, nc - 1 - c, 0)) if rev else (lambda c: (0, c, 0))

    in_specs = [pl.BlockSpec((N_SLABS, r, LANES), u_map(rev)) for rev in dirs]
    in_specs += [
        pl.BlockSpec((nd, N_SLABS, SUBLANES, sw), lambda c: (0, 0, 0, 0)),
        pl.BlockSpec((nd, N_SLABS, LANES, sw), lambda c: (0, 0, 0, 0)),
        pl.BlockSpec((nd, N_SLABS, sw, LANES), lambda c: (0, 0, 0, 0)),
        pl.BlockSpec((nd, N_SLABS, SUBLANES, sw), lambda c: (0, 0, 0, 0)),
    ]
    out_specs, out_shape = [], []
    if emit_y:
        for rev in dirs:
            out_specs.append(pl.BlockSpec((N_SLABS, r, LANES), u_map(rev)))
            out_shape.append(jax.ShapeDtypeStruct((N_SLABS, rows, LANES), F32))
    if emit_state:
        out_specs.append(pl.BlockSpec((nd, N_SLABS, SUBLANES, sw), lambda c: (0, 0, 0, 0)))
        out_shape.append(jax.ShapeDtypeStruct((nd, N_SLABS, SUBLANES, sw), F32))
    kern = functools.partial(_s5_kernel, bsz=bsz, tc=tc, dirs=dirs, emit_y=emit_y,
                             emit_state=emit_state)
    return pl.pallas_call(
        kern,
        grid=(nc,),
        in_specs=in_specs,
        out_specs=out_specs,
        out_shape=out_shape,
        scratch_shapes=[
            pltpu.VMEM((nd, N_SLABS, r, sw), F32),
            pltpu.VMEM((nd, N_SLABS, SUBLANES, sw), F32),
        ],
        compiler_params=_cparams("arbitrary"),
        name="s5_scan",
    )(*([su] * nd), s0, wb, wc, a_b)


def _s5_params(lam_re, lam_im, log_step, b_re, b_im, c_re, c_im):
    g, p = lam_re.shape
    h = b_re.shape[-1]
    dt = jnp.exp(log_step.astype(F32))[:, None]
    mag = jnp.exp(lam_re * dt)
    ab_re = mag * jnp.cos(lam_im * dt)
    ab_im = mag * jnp.sin(lam_im * dt)
    den = lam_re * lam_re + lam_im * lam_im
    nr, ni = ab_re - 1.0, ab_im
    f_re = (nr * lam_re + ni * lam_im) / den
    f_im = (ni * lam_re - nr * lam_im) / den
    bb_re = f_re[..., None] * b_re - f_im[..., None] * b_im
    bb_im = f_re[..., None] * b_im + f_im[..., None] * b_re
    ns, gl = N_SLABS, GROUPS_PER_SLAB
    eye = jnp.eye(gl, dtype=F32)

    def in_block(bb):
        bb = bb.reshape(ns, gl, p, h)
        return jnp.einsum('sgph,gk->sghkp', bb, eye).reshape(ns, gl * h, gl * p)

    def out_block(cc):
        cc = cc.reshape(ns, gl, h, p)
        return jnp.einsum('sghp,gk->sgpkh', cc, eye).reshape(ns, gl * p, gl * h)

    wb = jnp.concatenate([in_block(bb_re), in_block(bb_im)], axis=-1)
    wc = jnp.concatenate([out_block(c_re.astype(F32)), -out_block(c_im.astype(F32))], axis=1)
    a = jnp.concatenate([ab_re.reshape(ns, gl * p), ab_im.reshape(ns, gl * p)], axis=-1)
    return wb.astype(BF16), wc.astype(BF16), a


def _mixer_tail_kernel(h0_ref, gb_ref, u_ref, up_ref, un_ref, um_ref, su_ref, yf_ref, yr_ref,
                       cw_ref, cb_ref, sd_ref, wg_ref, bg_ref, na_ref, nb_ref, wo_ref,
                       g1_ref, b1_ref, h1_ref, h1p_ref, *, bsz, tt):
    i = pl.program_id(0)
    nt = pl.num_programs(0)
    d = h0_ref.shape[-1]
    wcv = gb_ref.shape[-1]
    row_id = lax.broadcasted_iota(jnp.int32, (tt, wcv), 0)
    ya, ys = [], []
    for b in range(bsz):
        u = u_ref[b]
        prev_edge = jnp.where(i == 0, um_ref[...], up_ref[b, SUBLANES - 1:SUBLANES, :])
        next_edge = jnp.where(i == nt - 1, jnp.zeros((1, wcv), F32), un_ref[b, 0:1, :])
        u_prev = jnp.where(row_id == 0, prev_edge, pltpu.roll(u, 1, axis=0))
        u_next = jnp.where(row_id == tt - 1, next_edge, pltpu.roll(u, tt - 1, axis=0))
        conv = u_prev * cw_ref[0:1, :] + u * cw_ref[1:2, :] + u_next * cw_ref[2:3, :] + cb_ref[...]
        ya.append(gb_ref[b] * conv)

        def slab(ref, b=b):
            parts = []
            for j in range(N_SLABS):
                if bsz == 1:
                    parts.append(ref[j])
                else:
                    parts.append(ref[j, pl.ds(b, tt, stride=bsz), :])
            return jnp.concatenate(parts, axis=-1)

        ys.append(slab(yf_ref) + slab(yr_ref) + sd_ref[...] * slab(su_ref))
    y_a = jnp.concatenate(ya, axis=0)
    y_s = jnp.concatenate(ys, axis=0)
    z = jax.nn.gelu(y_s)
    glu = jnp.dot(z.astype(BF16), wg_ref[...], preferred_element_type=F32) + bg_ref[...]
    y_b = z * jax.nn.sigmoid(glu)
    merged = jnp.concatenate([_rms_norm(y_a, na_ref[...]), _rms_norm(y_b, nb_ref[...])], axis=-1)
    m = jnp.dot(merged.astype(BF16), wo_ref[...], preferred_element_type=F32)
    h0 = h0_ref[...].reshape(bsz * tt, d)
    h1 = _layer_norm(DEEPNORM_ALPHA * h0 + m, g1_ref[...], b1_ref[...])
    h1_ref[...] = h1.reshape(bsz, tt, d)
    h1p_ref[...] = _pack_halves(h1).reshape(bsz, tt, d // 2)


def _mixer_tail(h0, gb, u, u_meta_last, su, yf, yr, conv_w, conv_b, ssm_d, w_glu_bf, b_glu,
                norm_a_g, norm_b_g, w_out_bf, ln1_g, ln1_b, tt):
    bsz, seq, d = h0.shape
    wcv = gb.shape[-1]
    ws = N_SLABS * LANES
    nt = seq // tt
    tb = tt // SUBLANES
    nb8 = seq // SUBLANES
    kern = functools.partial(_mixer_tail_kernel, bsz=bsz, tt=tt)
    row = lambda n: pl.BlockSpec((1, n), lambda i: (0, 0))
    slab_spec = pl.BlockSpec((N_SLABS, tt * bsz, LANES), lambda i: (0, i, 0))
    return pl.pallas_call(
        kern,
        grid=(nt,),
        in_specs=[
            pl.BlockSpec((bsz, tt, d), lambda i: (0, i, 0)),
            pl.BlockSpec((bsz, tt, wcv), lambda i: (0, i, 0)),
            pl.BlockSpec((bsz, tt, wcv), lambda i: (0, i, 0)),
            pl.BlockSpec((bsz, SUBLANES, wcv), lambda i: (0, jnp.maximum(i * tb - 1, 0), 0)),
            pl.BlockSpec((bsz, SUBLANES, wcv), lambda i: (0, jnp.minimum((i + 1) * tb, nb8 - 1), 0)),
            row(wcv),
            slab_spec, slab_spec, slab_spec,
            pl.BlockSpec((CONV_WIDTH, wcv), lambda i: (0, 0)),
            row(wcv), row(ws),
            pl.BlockSpec((ws, ws), lambda i: (0, 0)),
            row(ws), row(wcv), row(ws),
            pl.BlockSpec((wcv + ws, d), lambda i: (0, 0)),
            row(d), row(d),
        ],
        out_specs=[
            pl.BlockSpec((bsz, tt, d), lambda i: (0, i, 0)),
            pl.BlockSpec((bsz, tt, d // 2), lambda i: (0, i, 0)),
        ],
        out_shape=[
            jax.ShapeDtypeStruct((bsz, seq, d), F32),
            jax.ShapeDtypeStruct((bsz, seq, d // 2), jnp.uint32),
        ],
        compiler_params=_cparams("parallel"),
        name="mixer_tail",
    )(h0, gb, u, u, u, u_meta_last.reshape(1, wcv), su, yf, yr, conv_w, conv_b.reshape(1, wcv),
      ssm_d.reshape(1, ws), w_glu_bf, b_glu.reshape(1, ws), norm_a_g.reshape(1, wcv),
      norm_b_g.reshape(1, ws), w_out_bf, ln1_g.reshape(1, d), ln1_b.reshape(1, d))


def _dual_row_specs(rows, width, nq_p):
    return [pl.BlockSpec((rows, width), lambda q, *_: (jnp.minimum(q, nq_p - 1), 0)),
            pl.BlockSpec((rows, width), lambda q, *_: (jnp.maximum(q - nq_p, 0), 0))]


def _router_kernel(hp_ref, hs_ref, wh_ref, bias_ref, eidx_ref, rank_ref, gate_ref, cnt_ref,
                   cnt_scr, *, tt, nq_p):
    ne = wh_ref.shape[0]
    epg = ne // N_EXPERT_GROUPS
    neg = jnp.float32(-jnp.inf)

    @pl.when(pl.program_id(0) == 0)
    def _():
        cnt_scr[...] = jnp.zeros_like(cnt_scr)

    h = jnp.where(pl.program_id(0) < nq_p, hp_ref[...], hs_ref[...])
    dn = (((1,), (1,)), ((), ()))
    logits = lax.dot_general(wh_ref[...], h.astype(BF16), dn,
                             preferred_element_type=F32)
    scores = jax.nn.sigmoid(logits)
    sel = scores + bias_ref[...]

    gi = lax.broadcasted_iota(jnp.int32, (epg, tt), 0)
    gs = []
    for g in range(N_EXPERT_GROUPS):
        x = sel[g * epg:(g + 1) * epg, :]
        m1 = jnp.max(x, axis=0, keepdims=True)
        i1 = jnp.min(jnp.where(x == m1, gi, epg), axis=0, keepdims=True)
        m2 = jnp.max(jnp.where(gi == i1, neg, x), axis=0, keepdims=True)
        gs.append(m1 + m2)
    chosen = [jnp.zeros((1, tt), F32) for _ in range(N_EXPERT_GROUPS)]
    for _ in range(TOPK_GROUPS):
        m = gs[0]
        for g in range(1, N_EXPERT_GROUPS):
            m = jnp.maximum(m, gs[g])
        found = jnp.zeros((1, tt), F32)
        for g in range(N_EXPERT_GROUPS):
            hit = jnp.where((gs[g] == m) & (found == 0.0), 1.0, 0.0)
            chosen[g] = chosen[g] + hit
            found = found + hit
            gs[g] = jnp.where(hit > 0.0, neg, gs[g])
    selm = jnp.concatenate(
        [jnp.where(chosen[g] > 0.0, sel[g * epg:(g + 1) * epg, :], neg)
         for g in range(N_EXPERT_GROUPS)], axis=0)

    ei = lax.broadcasted_iota(jnp.int32, (ne, tt), 0)
    msel = jnp.zeros((ne, tt), F32)
    idxs, gvals = [], []
    for _ in range(TOP_K):
        m = jnp.max(selm, axis=0, keepdims=True)
        idx = jnp.min(jnp.where(selm == m, ei, ne), axis=0, keepdims=True)
        hit = ei == idx
        gvals.append(jnp.sum(jnp.where(hit, scores, 0.0), axis=0, keepdims=True))
        selm = jnp.where(hit, neg, selm)
        msel = jnp.where(hit, 1.0, msel)
        idxs.append(idx)
    gsum = gvals[0]
    for k in range(1, TOP_K):
        gsum = gsum + gvals[k]
    gate_ref[...] = jnp.concatenate([gv / gsum * ROUTED_SCALE for gv in gvals], axis=0)
    eidx_ref[...] = jnp.concatenate(idxs, axis=0)

    r_i = lax.broadcasted_iota(jnp.int32, (tt, tt), 0)
    c_i = lax.broadcasted_iota(jnp.int32, (tt, tt), 1)
    upper = jnp.where(r_i < c_i, 1.0, 0.0).astype(BF16)
    rank_full = jnp.dot(msel.astype(BF16), upper, preferred_element_type=F32) + cnt_scr[...]
    ranks = [jnp.sum(jnp.where(ei == idxs[k], rank_full, 0.0), axis=0, keepdims=True)
             for k in range(TOP_K)]
    rank_ref[...] = jnp.concatenate(ranks, axis=0).astype(jnp.int32)
    cnt_scr[...] = cnt_scr[...] + jnp.sum(msel, axis=1, keepdims=True)
    cnt_ref[...] = cnt_scr[...]


def _router(h_p, h_s, w_router, router_bias, tt):
    d = h_p.shape[1]
    n = h_p.shape[0] + h_s.shape[0]
    nq_p = h_p.shape[0] // tt
    ne = w_router.shape[1]
    wh = w_router.T.astype(BF16)
    kern = functools.partial(_router_kernel, tt=tt, nq_p=nq_p)
    return pl.pallas_call(
        kern,
        grid=(n // tt,),
        in_specs=_dual_row_specs(tt, d, nq_p) + [
            pl.BlockSpec((ne, d), lambda i: (0, 0)),
            pl.BlockSpec((ne, 1), lambda i: (0, 0)),
        ],
        out_specs=[
            pl.BlockSpec((TOP_K, tt), lambda i: (0, i)),
            pl.BlockSpec((TOP_K, tt), lambda i: (0, i)),
            pl.BlockSpec((TOP_K, tt), lambda i: (0, i)),
            pl.BlockSpec((ne, 1), lambda i: (0, 0)),
        ],
        out_shape=[
            jax.ShapeDtypeStruct((TOP_K, n), jnp.int32),
            jax.ShapeDtypeStruct((TOP_K, n), jnp.int32),
            jax.ShapeDtypeStruct((TOP_K, n), F32),
            jax.ShapeDtypeStruct((ne, 1), F32),
        ],
        scratch_shapes=[pltpu.VMEM((ne, 1), F32)],
        compiler_params=_cparams("arbitrary"),
        name="router",
    )(h_p, h_s, wh, router_bias.astype(F32).reshape(ne, 1))


def _positions_kernel(eidx_ref, rank_ref, start_ref, pos_ref):
    ne = start_ref.shape[0]
    tt = eidx_ref.shape[1]
    ei = lax.broadcasted_iota(jnp.int32, (ne, tt), 0)
    start = start_ref[...]
    rows = [jnp.sum(jnp.where(ei == eidx_ref[k:k + 1, :], start, 0.0), axis=0, keepdims=True)
            for k in range(TOP_K)]
    pos_ref[...] = jnp.concatenate(rows, axis=0).astype(jnp.int32) + rank_ref[...]


def _positions(eidx, rank, start, tt):
    n = eidx.shape[1]
    ne = start.shape[0]
    return pl.pallas_call(
        _positions_kernel,
        grid=(n // tt,),
        in_specs=[
            pl.BlockSpec((TOP_K, tt), lambda i: (0, i)),
            pl.BlockSpec((TOP_K, tt), lambda i: (0, i)),
            pl.BlockSpec((ne, 1), lambda i: (0, 0)),
        ],
        out_specs=pl.BlockSpec((TOP_K, tt), lambda i: (0, i)),
        out_shape=jax.ShapeDtypeStruct((TOP_K, n), jnp.int32),
        compiler_params=_cparams("parallel"),
        name="positions",
    )(eidx, rank, start.astype(F32).reshape(ne, 1))


def _sc_workers():
    info = plsc.get_sparse_core_info()
    return info.num_cores, info.num_subcores


def _sc_dispatch(hp_p, hp_s, pos_c, chunk):
    w = hp_p.shape[1]
    n = hp_p.shape[0] + hp_s.shape[0]
    nch_p = hp_p.shape[0] // chunk
    nc, ns = _sc_workers()
    per_worker = (n // chunk) // (nc * ns)
    mesh = plsc.VectorSubcoreMesh(core_axis_name="c", subcore_axis_name="s")

    @functools.partial(
        pl.kernel, mesh=mesh,
        out_type=jax.ShapeDtypeStruct((n * TOP_K, w), hp_p.dtype),
        scratch_types=[pltpu.VMEM((TOP_K, chunk), jnp.int32),
                       pltpu.VMEM((chunk, w), hp_p.dtype),
                       pltpu.SemaphoreType.DMA],
    )
    def dispatch(hp_hbm, hs_hbm, pos_hbm, xs_hbm, idx_v, rows_v, sem):
        wid = lax.axis_index("s") * nc + lax.axis_index("c")

        @pl.loop(0, per_worker)
        def _(ci):
            c = wid * per_worker + ci

            @pl.when(c < nch_p)
            def _():
                pltpu.sync_copy(hp_hbm.at[pl.ds(pl.multiple_of(c * chunk, chunk), chunk)], rows_v)

            @pl.when(c >= nch_p)
            def _():
                pltpu.sync_copy(
                    hs_hbm.at[pl.ds(pl.multiple_of((c - nch_p) * chunk, chunk), chunk)], rows_v)

            pltpu.sync_copy(pos_hbm.at[c], idx_v)
            copies = [pltpu.async_copy(rows_v, xs_hbm.at[idx_v.at[k]], sem) for k in range(TOP_K)]
            for cp in copies:
                cp.wait()

    return dispatch(hp_p, hp_s, pos_c)


def _sc_gather(ys, pos_c, chunk):
    w = ys.shape[1]
    n = pos_c.shape[0] * chunk
    nc, ns = _sc_workers()
    per_worker = pos_c.shape[0] // (nc * ns)
    mesh = plsc.VectorSubcoreMesh(core_axis_name="c", subcore_axis_name="s")
    nbuf = 3

    @functools.partial(
        pl.kernel, mesh=mesh,
        out_type=jax.ShapeDtypeStruct((TOP_K, n, w), ys.dtype),
        scratch_types=[pltpu.VMEM((TOP_K, chunk), jnp.int32),
                       pltpu.VMEM((nbuf, chunk, w), ys.dtype),
                       pltpu.SemaphoreType.DMA((nbuf,)),
                       pltpu.SemaphoreType.DMA((nbuf,))],
    )
    def gather(ys_hbm, pos_hbm, out_hbm, idx_v, rows_v, gsem, wsem):
        wid = lax.axis_index("s") * nc + lax.axis_index("c")

        @pl.loop(0, per_worker)
        def _(ci):
            c = wid * per_worker + ci
            off = pl.multiple_of(c * chunk, chunk)
            pltpu.sync_copy(pos_hbm.at[c], idx_v)

            def start_gather(k):
                b = k % nbuf
                return pltpu.async_copy(ys_hbm.at[idx_v.at[k]], rows_v.at[b], gsem.at[b])

            gathers = {0: start_gather(0)}
            writes = {}
            for k in range(TOP_K):
                if k + 1 < TOP_K:
                    if k + 1 - nbuf >= 0:
                        writes.pop(k + 1 - nbuf).wait()
                    gathers[k + 1] = start_gather(k + 1)
                gathers.pop(k).wait()
                b = k % nbuf
                writes[k] = pltpu.async_copy(rows_v.at[b], out_hbm.at[k, pl.ds(off, chunk)],
                                             wsem.at[b])
            for k in sorted(writes):
                writes[k].wait()

    return gather(ys, pos_c)


def _experts_kernel(tile_ref, exp_ref, ns_ref, off_ref, xs_ref, wg_ref, wu_ref, wd_ref, ys_ref,
                    wgb, wub, wdb, acc, *, tm):
    s = pl.program_id(0)
    prev = jnp.maximum(s - 1, 0)
    e = exp_ref[s]
    t = tile_ref[s]
    half = xs_ref.shape[1]

    @pl.when((s == 0) | (e != exp_ref[prev]))
    def _():
        wgb[...] = wg_ref[0].astype(BF16)
        wub[...] = wu_ref[0].astype(BF16)
        wdb[...] = wd_ref[0].astype(BF16)

    @pl.when(s < ns_ref[0])
    def _():
        lo, hi = _unpack_halves(xs_ref[...])
        g = (jnp.dot(lo, wgb[:half], preferred_element_type=F32)
             + jnp.dot(hi, wgb[half:], preferred_element_type=F32))
        u = (jnp.dot(lo, wub[:half], preferred_element_type=F32)
             + jnp.dot(hi, wub[half:], preferred_element_type=F32))
        row = t * tm + lax.broadcasted_iota(jnp.int32, (tm, 1), 0)
        valid = (row >= off_ref[e]) & (row < off_ref[e + 1])
        act = jnp.where(valid, (g * jax.nn.sigmoid(g)) * u, 0.0)
        y = jnp.dot(act.astype(BF16), wdb[...], preferred_element_type=F32)
        first = (s == 0) | (t != tile_ref[prev])

        @pl.when(first)
        def _():
            acc[...] = y

        @pl.when(jnp.logical_not(first))
        def _():
            acc[...] = acc[...] + y

        ys_ref[...] = _pack_halves(acc[...])


def _experts(xs, tile_of, exp_of, n_steps, offsets, w_gate, w_up, w_down, tm):
    n_rows, half = xs.shape
    ne, d, de = w_gate.shape
    s_max = tile_of.shape[0]

    def row_map(s, tile, ex, ns, off):
        return (tile[s], 0)

    def w_map(s, tile, ex, ns, off):
        return (ex[s], 0, 0)

    grid_spec = pltpu.PrefetchScalarGridSpec(
        num_scalar_prefetch=4,
        grid=(s_max,),
        in_specs=[
            pl.BlockSpec((tm, half), row_map),
            pl.BlockSpec((1, d, de), w_map),
            pl.BlockSpec((1, d, de), w_map),
            pl.BlockSpec((1, de, d), w_map),
        ],
        out_specs=pl.BlockSpec((tm, half), row_map),
        scratch_shapes=[
            pltpu.VMEM((d, de), BF16),
            pltpu.VMEM((d, de), BF16),
            pltpu.VMEM((de, d), BF16),
            pltpu.VMEM((tm, d), F32),
        ],
    )
    return pl.pallas_call(
        functools.partial(_experts_kernel, tm=tm),
        grid_spec=grid_spec,
        out_shape=jax.ShapeDtypeStruct((n_rows, half), jnp.uint32),
        compiler_params=_cparams("arbitrary"),
        name="experts",
    )(tile_of, exp_of, n_steps, offsets, xs, w_gate, w_up, w_down)


def _expert_schedule(counts, n_rows, tm):
    ne = counts.shape[0]
    s_max = n_rows // tm + ne
    off = jnp.concatenate([jnp.zeros((1,), jnp.int32), jnp.cumsum(counts)]).astype(jnp.int32)
    first_tile = off[:-1] // tm
    last_tile = (off[1:] - 1) // tm
    visits = jnp.where(counts > 0, last_tile - first_tile + 1, 0)
    cum = jnp.cumsum(visits)
    n_steps = cum[-1]
    step = jnp.minimum(jnp.arange(s_max, dtype=jnp.int32), n_steps - 1)
    exp_of = jnp.sum((cum[None, :] <= step[:, None]).astype(jnp.int32), axis=1)
    onehot = exp_of[:, None] == jnp.arange(ne, dtype=jnp.int32)[None, :]
    pick = lambda v: jnp.sum(jnp.where(onehot, v[None, :], 0), axis=1)
    tile_of = pick(first_tile) + step - pick(cum - visits)
    return (tile_of.astype(jnp.int32), exp_of.astype(jnp.int32),
            n_steps.reshape(1).astype(jnp.int32), off)


def _combine_kernel(hp_ref, hs_ref, gate_ref, yk_ref, wsg_ref, wsu_ref, wsd_ref, g2_ref,
                    b2_ref, outp_ref, outs_ref, *, tt, nq_p):
    is_p = pl.program_id(0) < nq_p
    h = jnp.where(is_p, hp_ref[...], hs_ref[...])
    hb = h.astype(BF16)
    g = jnp.dot(hb, wsg_ref[...], preferred_element_type=F32)
    u = jnp.dot(hb, wsu_ref[...], preferred_element_type=F32)
    act = (g * jax.nn.sigmoid(g)) * u
    f = jnp.dot(act.astype(BF16), wsd_ref[...], preferred_element_type=F32)
    gate = gate_ref[...]
    r_lo = jnp.zeros((tt, yk_ref.shape[-1]), F32)
    r_hi = jnp.zeros((tt, yk_ref.shape[-1]), F32)
    for k in range(TOP_K):
        p = yk_ref[k]
        gk = gate[:, k:k + 1]
        r_lo = r_lo + gk * pltpu.bitcast(p << 16, F32)
        r_hi = r_hi + gk * pltpu.bitcast(p & jnp.uint32(0xFFFF0000), F32)
    f = f + jnp.concatenate([r_lo, r_hi], axis=-1)
    out = _layer_norm(DEEPNORM_ALPHA * h + f, g2_ref[...], b2_ref[...])

    @pl.when(is_p)
    def _():
        outp_ref[...] = out

    @pl.when(jnp.logical_not(is_p))
    def _():
        outs_ref[...] = out


def _combine(h_p, h_s, gate_t, yk, wsg_bf, wsu_bf, wsd_bf, ln2_g, ln2_b, tt):
    d = h_p.shape[1]
    n_p, n_s = h_p.shape[0], h_s.shape[0]
    nq_p = n_p // tt
    ds_ = wsg_bf.shape[1]
    kern = functools.partial(_combine_kernel, tt=tt, nq_p=nq_p)
    return pl.pallas_call(
        kern,
        grid=((n_p + n_s) // tt,),
        in_specs=_dual_row_specs(tt, d, nq_p) + [
            pl.BlockSpec((tt, TOP_K), lambda i: (i, 0)),
            pl.BlockSpec((TOP_K, tt, yk.shape[2]), lambda i: (0, i, 0)),
            pl.BlockSpec((d, ds_), lambda i: (0, 0)),
            pl.BlockSpec((d, ds_), lambda i: (0, 0)),
            pl.BlockSpec((ds_, d), lambda i: (0, 0)),
            pl.BlockSpec((1, d), lambda i: (0, 0)),
            pl.BlockSpec((1, d), lambda i: (0, 0)),
        ],
        out_specs=_dual_row_specs(tt, d, nq_p),
        out_shape=[jax.ShapeDtypeStruct((n_p, d), F32), jax.ShapeDtypeStruct((n_s, d), F32)],
        compiler_params=_cparams("arbitrary"),
        name="combine",
    )(h_p, h_s, gate_t, yk, wsg_bf, wsu_bf, wsd_bf, ln2_g.reshape(1, d), ln2_b.reshape(1, d))


def _pick_tile(seq, bsz, rows):
    return max(SUBLANES, min(seq, rows // bsz))


def _mixer(x, meta_state, u_meta_last, p):
    bsz, seq, _ = x.shape
    tt = _pick_tile(seq, bsz, 1024)
    h0, gb, u, su = _embed_inproj(x, p["ln_emb_g"], p["ln_emb_b"], p["w_in_bf"], tt)
    tc = _pick_tile(seq, bsz, 256)
    s0 = jnp.stack([jnp.broadcast_to(meta_state, (N_SLABS, SUBLANES, 2 * SLAB_STATE)),
                    jnp.zeros((N_SLABS, SUBLANES, 2 * SLAB_STATE), F32)])
    a_b = jnp.broadcast_to(p["s5_a"][:, :, None, :], (2, N_SLABS, SUBLANES, 2 * SLAB_STATE))
    yf, yr = _s5_scan(su, s0, p["s5_wb"], p["s5_wc"], a_b, bsz=bsz, tc=tc, dirs=(False, True),
                      emit_y=True, emit_state=False)
    tt3 = _pick_tile(seq, bsz, 512)
    h1, h1p = _mixer_tail(h0, gb, u, u_meta_last, su, yf, yr, p["conv_w"], p["conv_b"],
                          p["ssm_d"], p["w_glu_bf"], p["b_glu"], p["norm_a_g"], p["norm_b_g"],
                          p["w_out_bf"], p["ln1_g"], p["ln1_b"], tt3)
    d = h1.shape[-1]
    return h1.reshape(bsz * seq, d), h1p.reshape(bsz * seq, d // 2)


def kernel(x_prompt, x_sample, meta_tokens, ln_emb_g, ln_emb_b, w_in, conv_w, conv_b, ssm_lambda_re, ssm_lambda_im, ssm_log_step, ssm_b_re, ssm_b_im, ssm_c_re, ssm_c_im, ssm_d, w_glu, b_glu, norm_a_g, norm_b_g, w_out, ln1_g, ln1_b, w_router, router_bias, w_exp_gate, w_exp_up, w_exp_down, w_sh_gate, w_sh_up, w_sh_down, ln2_g, ln2_b):
    l = 0
    d = x_prompt.shape[-1]
    dirs = [_s5_params(ssm_lambda_re[l, k].astype(F32), ssm_lambda_im[l, k].astype(F32),
                       ssm_log_step[l, k], ssm_b_re[l, k].astype(F32), ssm_b_im[l, k].astype(F32),
                       ssm_c_re[l, k], ssm_c_im[l, k]) for k in range(2)]
    p = dict(
        ln_emb_g=ln_emb_g, ln_emb_b=ln_emb_b, w_in_bf=w_in[l].astype(BF16),
        conv_w=conv_w[l], conv_b=conv_b[l], ssm_d=ssm_d[l],
        w_glu_bf=w_glu[l].astype(BF16), b_glu=b_glu[l], norm_a_g=norm_a_g[l],
        norm_b_g=norm_b_g[l], w_out_bf=w_out[l].astype(BF16), ln1_g=ln1_g[l], ln1_b=ln1_b[l],
        s5_wb=jnp.stack([dirs[0][0], dirs[1][0]]), s5_wc=jnp.stack([dirs[0][1], dirs[1][1]]),
        s5_a=jnp.stack([dirs[0][2], dirs[1][2]]),
    )
    mb = SUBLANES
    xm = jnp.broadcast_to(meta_tokens.astype(F32)[None], (mb, N_META, d))
    _, _, u_m, su_m = _embed_inproj(xm, ln_emb_g, ln_emb_b, p["w_in_bf"], N_META)
    a_m = jnp.broadcast_to(p["s5_a"][:1, :, None, :], (1, N_SLABS, mb, 2 * SLAB_STATE))
    (st_m,) = _s5_scan(su_m, jnp.zeros((1, N_SLABS, mb, 2 * SLAB_STATE), F32), p["s5_wb"][:1],
                       p["s5_wc"][:1], a_m, bsz=mb, tc=N_META, dirs=(False,), emit_y=False,
                       emit_state=True)
    meta_state = st_m[0, :, :1, :]
    u_meta_last = u_m[0, N_META - 1]

    h1_p, h1p_p = _mixer(x_prompt, meta_state, u_meta_last, p)
    h1_s, h1p_s = _mixer(x_sample, meta_state, u_meta_last, p)
    n = h1_p.shape[0] + h1_s.shape[0]

    tr = min(h1_p.shape[0], h1_s.shape[0], ROUTER_TILE)
    eidx, rank, gate, cnt = _router(h1_p, h1_s, w_router[l], router_bias[l], tr)
    counts = cnt[:, 0].astype(jnp.int32)
    tile_of, exp_of, n_steps, offsets = _expert_schedule(counts, n * TOP_K, EXPERT_TILE)
    pos = _positions(eidx, rank, offsets[:-1], tr)
    pos_c = pos.reshape(TOP_K, n // SC_CHUNK, SC_CHUNK).transpose(1, 0, 2)
    xs = _sc_dispatch(h1p_p, h1p_s, pos_c, SC_CHUNK)
    ys = _experts(xs, tile_of, exp_of, n_steps, offsets, w_exp_gate[l], w_exp_up[l],
                  w_exp_down[l], EXPERT_TILE)
    yk = _sc_gather(ys, pos_c, SC_CHUNK)
    td = min(h1_p.shape[0], h1_s.shape[0], COMBINE_TILE)
    out_p, out_s = _combine(h1_p, h1_s, gate.T, yk, w_sh_gate[l].astype(BF16),
                            w_sh_up[l].astype(BF16), w_sh_down[l].astype(BF16), ln2_g[l],
                            ln2_b[l], td)
    return (out_p.reshape(x_prompt.shape), out_s.reshape(x_sample.shape))
```

```python
import functools
import math

import jax
import jax.numpy as jnp
from jax import lax
from jax.experimental import pallas as pl
from jax.experimental.pallas import tpu as pltpu
from jax.experimental.pallas import tpu_sc as plsc

F32 = jnp.float32
BF16 = jnp.bfloat16

N_META = 16
CONV_WIDTH = 3
SSM_GROUP = 16
SSM_STATE = 64
N_EXPERTS = 256
TOP_K = 8
N_EXPERT_GROUPS = 8
TOPK_GROUPS = 4
ROUTED_SCALE = 2.5
DEPTH = 1
DEEPNORM_ALPHA = (2.0 * DEPTH) ** 0.25
LN_EPS = 1e-5
RMS_EPS = 1e-6

LANES = 128
SUBLANES = 8
N_SLABS = 4
GROUPS_PER_SLAB = LANES // SSM_GROUP
SLAB_STATE = GROUPS_PER_SLAB * SSM_STATE
EXPERT_TILE = 512
ROUTER_TILE = 256
COMBINE_TILE = 128
SC_CHUNK = 64
VMEM_LIMIT = 48 * 1024 * 1024


def _cparams(*sem):
    return pltpu.CompilerParams(dimension_semantics=sem, vmem_limit_bytes=VMEM_LIMIT)


def _layer_norm(x, g, b):
    mu = jnp.mean(x, axis=-1, keepdims=True)
    xc = x - mu
    var = jnp.mean(xc * xc, axis=-1, keepdims=True)
    return xc * lax.rsqrt(var + LN_EPS) * g + b


def _rms_norm(x, g):
    return x * lax.rsqrt(jnp.mean(x * x, axis=-1, keepdims=True) + RMS_EPS) * g


def _pack_halves(x):
    half = x.shape[-1] // 2
    bits = pltpu.bitcast(x.astype(BF16).astype(F32), jnp.uint32)
    return (bits[:, :half] >> 16) | (bits[:, half:] & jnp.uint32(0xFFFF0000))


def _unpack_halves(p):
    lo = pltpu.bitcast(p << 16, F32).astype(BF16)
    hi = pltpu.bitcast(p & jnp.uint32(0xFFFF0000), F32).astype(BF16)
    return lo, hi


def _embed_inproj_kernel(x_ref, g_ref, b_ref, w_ref, h0_ref, gb_ref, u_ref, su_ref, *, bsz, tt):
    d = x_ref.shape[-1]
    x = x_ref[...].reshape(bsz * tt, d)
    h0 = _layer_norm(x, g_ref[...], b_ref[...])
    h0_ref[...] = h0.reshape(bsz, tt, d)
    proj = jnp.dot(h0.astype(BF16), w_ref[...], preferred_element_type=F32)
    wc = gb_ref.shape[-1]
    gb_ref[...] = proj[:, :wc].reshape(bsz, tt, wc)
    u_ref[...] = (proj[:, wc:2 * wc] * proj[:, 2 * wc:3 * wc]).reshape(bsz, tt, wc)
    s_u = proj[:, 3 * wc:]
    for b in range(bsz):
        for j in range(N_SLABS):
            val = s_u[b * tt:(b + 1) * tt, j * LANES:(j + 1) * LANES]
            if bsz == 1:
                su_ref[j] = val
            else:
                su_ref[j, pl.ds(b, tt, stride=bsz), :] = val


def _embed_inproj(x, ln_g, ln_b, w_in_bf, tt):
    bsz, seq, d = x.shape
    e = w_in_bf.shape[1]
    wc = (e - N_SLABS * LANES) // 3
    nt = seq // tt
    kern = functools.partial(_embed_inproj_kernel, bsz=bsz, tt=tt)
    return pl.pallas_call(
        kern,
        grid=(nt,),
        in_specs=[
            pl.BlockSpec((bsz, tt, d), lambda i: (0, i, 0)),
            pl.BlockSpec((1, d), lambda i: (0, 0)),
            pl.BlockSpec((1, d), lambda i: (0, 0)),
            pl.BlockSpec((d, e), lambda i: (0, 0)),
        ],
        out_specs=[
            pl.BlockSpec((bsz, tt, d), lambda i: (0, i, 0)),
            pl.BlockSpec((bsz, tt, wc), lambda i: (0, i, 0)),
            pl.BlockSpec((bsz, tt, wc), lambda i: (0, i, 0)),
            pl.BlockSpec((N_SLABS, tt * bsz, LANES), lambda i: (0, i, 0)),
        ],
        out_shape=[
            jax.ShapeDtypeStruct((bsz, seq, d), F32),
            jax.ShapeDtypeStruct((bsz, seq, wc), F32),
            jax.ShapeDtypeStruct((bsz, seq, wc), F32),
            jax.ShapeDtypeStruct((N_SLABS, seq * bsz, LANES), F32),
        ],
        compiler_params=_cparams("parallel"),
        name="embed_inproj",
    )(x, ln_g.reshape(1, d), ln_b.reshape(1, d), w_in_bf)


def _s5_kernel(*refs, bsz, tc, dirs, emit_y, emit_state):
    nd = len(dirs)
    it = iter(refs)
    u_refs = [next(it) for _ in range(nd)]
    s0_ref, wb_ref, wc_ref, a_ref = next(it), next(it), next(it), next(it)
    y_refs = [next(it) for _ in range(nd)] if emit_y else []
    sf_ref = next(it) if emit_state else None
    bu_ref, st_ref = next(it), next(it)
    half = SLAB_STATE

    @pl.when(pl.program_id(0) == 0)
    def _():
        st_ref[...] = s0_ref[...]

    for k in range(nd):
        for j in range(N_SLABS):
            bu_ref[k, j] = jnp.dot(u_refs[k][j].astype(BF16), wb_ref[k, j],
                                   preferred_element_type=F32)

    per_tile = SUBLANES // bsz
    n_tiles = tc // per_tile
    first_half = lax.broadcasted_iota(jnp.int32, (SUBLANES, half), 0) < bsz

    for j in range(N_SLABS):
        a_re = [a_ref[k, j, :, :half] for k in range(nd)]
        a_im = [a_ref[k, j, :, half:] for k in range(nd)]
        init = tuple((st_ref[k, j, :, :half], st_ref[k, j, :, half:]) for k in range(nd))

        def step(i, carry, j=j, a_re=a_re, a_im=a_im):
            out = []
            for k in range(nd):
                ti = (n_tiles - 1 - i) if dirs[k] else i
                row = pl.multiple_of(ti * SUBLANES, SUBLANES)
                x_re = bu_ref[k, j, pl.ds(row, SUBLANES), :half]
                x_im = bu_ref[k, j, pl.ds(row, SUBLANES), half:]

                def advance(s_re, s_im, k=k, x_re=x_re, x_im=x_im):
                    return (a_re[k] * s_re - a_im[k] * s_im + x_re,
                            a_re[k] * s_im + a_im[k] * s_re + x_im)

                c_re, c_im = carry[k]
                if per_tile == 1:
                    n_re, n_im = advance(c_re, c_im)
                    o_re, o_im = n_re, n_im
                else:
                    t1_re, t1_im = advance(pltpu.roll(c_re, bsz, axis=0),
                                           pltpu.roll(c_im, bsz, axis=0))
                    n_re, n_im = advance(pltpu.roll(t1_re, bsz, axis=0),
                                         pltpu.roll(t1_im, bsz, axis=0))
                    if dirs[k]:
                        o_re = jnp.where(first_half, n_re, t1_re)
                        o_im = jnp.where(first_half, n_im, t1_im)
                    else:
                        o_re = jnp.where(first_half, t1_re, n_re)
                        o_im = jnp.where(first_half, t1_im, n_im)
                bu_ref[k, j, pl.ds(row, SUBLANES), :half] = o_re
                bu_ref[k, j, pl.ds(row, SUBLANES), half:] = o_im
                out.append((n_re, n_im))
            return tuple(out)

        fin = lax.fori_loop(0, n_tiles, step, init)
        for k in range(nd):
            st_ref[k, j, :, :half] = fin[k][0]
            st_ref[k, j, :, half:] = fin[k][1]

    if emit_y:
        for k in range(nd):
            for j in range(N_SLABS):
                y_refs[k][j] = jnp.dot(bu_ref[k, j].astype(BF16), wc_ref[k, j],
                                       preferred_element_type=F32)
    if emit_state:
        sf_ref[...] = st_ref[...]


def _s5_scan(su, s0, wb, wc, a_b, *, bsz, tc, dirs, emit_y, emit_state):
    rows = su.shape[1]
    seq = rows // bsz
    nc = seq // tc
    r = tc * bsz
    nd = len(dirs)
    sw = 2 * SLAB_STATE

    def u_map(rev):
        return (lambda c: (0, nc - 1 - c, 0)) if rev else (lambda c: (0, c, 0))

    in_specs = [pl.BlockSpec((N_SLABS, r, LANES), u_map(rev)) for rev in dirs]
    in_specs += [
        pl.BlockSpec((nd, N_SLABS, SUBLANES, sw), lambda c: (0, 0, 0, 0)),
        pl.BlockSpec((nd, N_SLABS, LANES, sw), lambda c: (0, 0, 0, 0)),
        pl.BlockSpec((nd, N_SLABS, sw, LANES), lambda c: (0, 0, 0, 0)),
        pl.BlockSpec((nd, N_SLABS, SUBLANES, sw), lambda c: (0, 0, 0, 0)),
    ]
    out_specs, out_shape = [], []
    if emit_y:
        for rev in dirs:
            out_specs.append(pl.BlockSpec((N_SLABS, r, LANES), u_map(rev)))
            out_shape.append(jax.ShapeDtypeStruct((N_SLABS, rows, LANES), F32))
    if emit_state:
        out_specs.append(pl.BlockSpec((nd, N_SLABS, SUBLANES, sw), lambda c: (0, 0, 0, 0)))
        out_shape.append(jax.ShapeDtypeStruct((nd, N_SLABS, SUBLANES, sw), F32))
    kern = functools.partial(_s5_kernel, bsz=bsz, tc=tc, dirs=dirs, emit_y=emit_y,
                             emit_state=emit_state)
    return pl.pallas_call(
        kern,
        grid=(nc,),
        in_specs=in_specs,
        out_specs=out_specs,
        out_shape=out_shape,
        scratch_shapes=[
            pltpu.VMEM((nd, N_SLABS, r, sw), F32),
            pltpu.VMEM((nd, N_SLABS, SUBLANES, sw), F32),
        ],
        compiler_params=_cparams("arbitrary"),
        name="s5_scan",
    )(*([su] * nd), s0, wb, wc, a_b)


def _s5_params(lam_re, lam_im, log_step, b_re, b_im, c_re, c_im):
    g, p = lam_re.shape
    h = b_re.shape[-1]
    dt = jnp.exp(log_step.astype(F32))[:, None]
    mag = jnp.exp(lam_re * dt)
    ab_re = mag * jnp.cos(lam_im * dt)
    ab_im = mag * jnp.sin(lam_im * dt)
    den = lam_re * lam_re + lam_im * lam_im
    nr, ni = ab_re - 1.0, ab_im
    f_re = (nr * lam_re + ni * lam_im) / den
    f_im = (ni * lam_re - nr * lam_im) / den
    bb_re = f_re[..., None] * b_re - f_im[..., None] * b_im
    bb_im = f_re[..., None] * b_im + f_im[..., None] * b_re
    ns, gl = N_SLABS, GROUPS_PER_SLAB
    eye = jnp.eye(gl, dtype=F32)

    def in_block(bb):
        bb = bb.reshape(ns, gl, p, h)
        return jnp.einsum('sgph,gk->sghkp', bb, eye).reshape(ns, gl * h, gl * p)

    def out_block(cc):
        cc = cc.reshape(ns, gl, h, p)
        return jnp.einsum('sghp,gk->sgpkh', cc, eye).reshape(ns, gl * p, gl * h)

    wb = jnp.concatenate([in_block(bb_re), in_block(bb_im)], axis=-1)
    wc = jnp.concatenate([out_block(c_re.astype(F32)), -out_block(c_im.astype(F32))], axis=1)
    a = jnp.concatenate([ab_re.reshape(ns, gl * p), ab_im.reshape(ns, gl * p)], axis=-1)
    return wb.astype(BF16), wc.astype(BF16), a


def _mixer_tail_kernel(h0_ref, gb_ref, u_ref, up_ref, un_ref, um_ref, su_ref, yf_ref, yr_ref,
                       cw_ref, cb_ref, sd_ref, wg_ref, bg_ref, na_ref, nb_ref, wo_ref,
                       g1_ref, b1_ref, h1_ref, h1p_ref, *, bsz, tt):
    i = pl.program_id(0)
    nt = pl.num_programs(0)
    d = h0_ref.shape[-1]
    wcv = gb_ref.shape[-1]
    row_id = lax.broadcasted_iota(jnp.int32, (tt, wcv), 0)
    ya, ys = [], []
    for b in range(bsz):
        u = u_ref[b]
        prev_edge = jnp.where(i == 0, um_ref[...], up_ref[b, SUBLANES - 1:SUBLANES, :])
        next_edge = jnp.where(i == nt - 1, jnp.zeros((1, wcv), F32), un_ref[b, 0:1, :])
        u_prev = jnp.where(row_id == 0, prev_edge, pltpu.roll(u, 1, axis=0))
        u_next = jnp.where(row_id == tt - 1, next_edge, pltpu.roll(u, tt - 1, axis=0))
        conv = u_prev * cw_ref[0:1, :] + u * cw_ref[1:2, :] + u_next * cw_ref[2:3, :] + cb_ref[...]
        ya.append(gb_ref[b] * conv)

        def slab(ref, b=b):
            parts = []
            for j in range(N_SLABS):
                if bsz == 1:
                    parts.append(ref[j])
                else:
                    parts.append(ref[j, pl.ds(b, tt, stride=bsz), :])
            return jnp.concatenate(parts, axis=-1)

        ys.append(slab(yf_ref) + slab(yr_ref) + sd_ref[...] * slab(su_ref))
    y_a = jnp.concatenate(ya, axis=0)
    y_s = jnp.concatenate(ys, axis=0)
    z = jax.nn.gelu(y_s)
    glu = jnp.dot(z.astype(BF16), wg_ref[...], preferred_element_type=F32) + bg_ref[...]
    y_b = z * jax.nn.sigmoid(glu)
    merged = jnp.concatenate([_rms_norm(y_a, na_ref[...]), _rms_norm(y_b, nb_ref[...])], axis=-1)
    m = jnp.dot(merged.astype(BF16), wo_ref[...], preferred_element_type=F32)
    h0 = h0_ref[...].reshape(bsz * tt, d)
    h1 = _layer_norm(DEEPNORM_ALPHA * h0 + m, g1_ref[...], b1_ref[...])
    h1_ref[...] = h1.reshape(bsz, tt, d)
    h1p_ref[...] = _pack_halves(h1).reshape(bsz, tt, d // 2)


def _mixer_tail(h0, gb, u, u_meta_last, su, yf, yr, conv_w, conv_b, ssm_d, w_glu_bf, b_glu,
                norm_a_g, norm_b_g, w_out_bf, ln1_g, ln1_b, tt):
    bsz, seq, d = h0.shape
    wcv = gb.shape[-1]
    ws = N_SLABS * LANES
    nt = seq // tt
    tb = tt // SUBLANES
    nb8 = seq // SUBLANES
    kern = functools.partial(_mixer_tail_kernel, bsz=bsz, tt=tt)
    row = lambda n: pl.BlockSpec((1, n), lambda i: (0, 0))
    slab_spec = pl.BlockSpec((N_SLABS, tt * bsz, LANES), lambda i: (0, i, 0))
    return pl.pallas_call(
        kern,
        grid=(nt,),
        in_specs=[
            pl.BlockSpec((bsz, tt, d), lambda i: (0, i, 0)),
            pl.BlockSpec((bsz, tt, wcv), lambda i: (0, i, 0)),
            pl.BlockSpec((bsz, tt, wcv), lambda i: (0, i, 0)),
            pl.BlockSpec((bsz, SUBLANES, wcv), lambda i: (0, jnp.maximum(i * tb - 1, 0), 0)),
            pl.BlockSpec((bsz, SUBLANES, wcv), lambda i: (0, jnp.minimum((i + 1) * tb, nb8 - 1), 0)),
            row(wcv),
            slab_spec, slab_spec, slab_spec,
            pl.BlockSpec((CONV_WIDTH, wcv), lambda i: (0, 0)),
            row(wcv), row(ws),
            pl.BlockSpec((ws, ws), lambda i: (0, 0)),
            row(ws), row(wcv), row(ws),
            pl.BlockSpec((wcv + ws, d), lambda i: (0, 0)),
            row(d), row(d),
        ],
        out_specs=[
            pl.BlockSpec((bsz, tt, d), lambda i: (0, i, 0)),
            pl.BlockSpec((bsz, tt, d // 2), lambda i: (0, i, 0)),
        ],
        out_shape=[
            jax.ShapeDtypeStruct((bsz, seq, d), F32),
            jax.ShapeDtypeStruct((bsz, seq, d // 2), jnp.uint32),
        ],
        compiler_params=_cparams("parallel"),
        name="mixer_tail",
    )(h0, gb, u, u, u, u_meta_last.reshape(1, wcv), su, yf, yr, conv_w, conv_b.reshape(1, wcv),
      ssm_d.reshape(1, ws), w_glu_bf, b_glu.reshape(1, ws), norm_a_g.reshape(1, wcv),
      norm_b_g.reshape(1, ws), w_out_bf, ln1_g.reshape(1, d), ln1_b.reshape(1, d))


def _dual_row_specs(rows, width, nq_p):
    return [pl.BlockSpec((rows, width), lambda q, *_: (jnp.minimum(q, nq_p - 1), 0)),
            pl.BlockSpec((rows, width), lambda q, *_: (jnp.maximum(q - nq_p, 0), 0))]


def _router_kernel(hp_ref, hs_ref, wh_ref, bias_ref, eidx_ref, rank_ref, gate_ref, cnt_ref,
                   cnt_scr, *, tt, nq_p):
    ne = wh_ref.shape[0]
    epg = ne // N_EXPERT_GROUPS
    neg = jnp.float32(-jnp.inf)

    @pl.when(pl.program_id(0) == 0)
    def _():
        cnt_scr[...] = jnp.zeros_like(cnt_scr)

    h = jnp.where(pl.program_id(0) < nq_p, hp_ref[...], hs_ref[...])
    dn = (((1,), (1,)), ((), ()))
    logits = lax.dot_general(wh_ref[...], h.astype(BF16), dn,
                             preferred_element_type=F32)
    scores = jax.nn.sigmoid(logits)
    sel = scores + bias_ref[...]

    gi = lax.broadcasted_iota(jnp.int32, (epg, tt), 0)
    gs = []
    for g in range(N_EXPERT_GROUPS):
        x = sel[g * epg:(g + 1) * epg, :]
        m1 = jnp.max(x, axis=0, keepdims=True)
        i1 = jnp.min(jnp.where(x == m1, gi, epg), axis=0, keepdims=True)
        m2 = jnp.max(jnp.where(gi == i1, neg, x), axis=0, keepdims=True)
        gs.append(m1 + m2)
    chosen = [jnp.zeros((1, tt), F32) for _ in range(N_EXPERT_GROUPS)]
    for _ in range(TOPK_GROUPS):
        m = gs[0]
        for g in range(1, N_EXPERT_GROUPS):
            m = jnp.maximum(m, gs[g])
        found = jnp.zeros((1, tt), F32)
        for g in range(N_EXPERT_GROUPS):
            hit = jnp.where((gs[g] == m) & (found == 0.0), 1.0, 0.0)
            chosen[g] = chosen[g] + hit
            found = found + hit
            gs[g] = jnp.where(hit > 0.0, neg, gs[g])
    selm = jnp.concatenate(
        [jnp.where(chosen[g] > 0.0, sel[g * epg:(g + 1) * epg, :], neg)
         for g in range(N_EXPERT_GROUPS)], axis=0)

    ei = lax.broadcasted_iota(jnp.int32, (ne, tt), 0)
    msel = jnp.zeros((ne, tt), F32)
    idxs, gvals = [], []
    for _ in range(TOP_K):
        m = jnp.max(selm, axis=0, keepdims=True)
        idx = jnp.min(jnp.where(selm == m, ei, ne), axis=0, keepdims=True)
        hit = ei == idx
        gvals.append(jnp.sum(jnp.where(hit, scores, 0.0), axis=0, keepdims=True))
        selm = jnp.where(hit, neg, selm)
        msel = jnp.where(hit, 1.0, msel)
        idxs.append(idx)
    gsum = gvals[0]
    for k in range(1, TOP_K):
        gsum = gsum + gvals[k]
    gate_ref[...] = jnp.concatenate([gv / gsum * ROUTED_SCALE for gv in gvals], axis=0)
    eidx_ref[...] = jnp.concatenate(idxs, axis=0)

    r_i = lax.broadcasted_iota(jnp.int32, (tt, tt), 0)
    c_i = lax.broadcasted_iota(jnp.int32, (tt, tt), 1)
    upper = jnp.where(r_i < c_i, 1.0, 0.0).astype(BF16)
    rank_full = jnp.dot(msel.astype(BF16), upper, preferred_element_type=F32) + cnt_scr[...]
    ranks = [jnp.sum(jnp.where(ei == idxs[k], rank_full, 0.0), axis=0, keepdims=True)
             for k in range(TOP_K)]
    rank_ref[...] = jnp.concatenate(ranks, axis=0).astype(jnp.int32)
    cnt_scr[...] = cnt_scr[...] + jnp.sum(msel, axis=1, keepdims=True)
    cnt_ref[...] = cnt_scr[...]


def _router(h_p, h_s, w_router, router_bias, tt):
    d = h_p.shape[1]
    n = h_p.shape[0] + h_s.shape[0]
    nq_p = h_p.shape[0] // tt
    ne = w_router.shape[1]
    wh = w_router.T.astype(BF16)
    kern = functools.partial(_router_kernel, tt=tt, nq_p=nq_p)
    return pl.pallas_call(
        kern,
        grid=(n // tt,),
        in_specs=_dual_row_specs(tt, d, nq_p) + [
            pl.BlockSpec((ne, d), lambda i: (0, 0)),
            pl.BlockSpec((ne, 1), lambda i: (0, 0)),
        ],
        out_specs=[
            pl.BlockSpec((TOP_K, tt), lambda i: (0, i)),
            pl.BlockSpec((TOP_K, tt), lambda i: (0, i)),
            pl.BlockSpec((TOP_K, tt), lambda i: (0, i)),
            pl.BlockSpec((ne, 1), lambda i: (0, 0)),
        ],
        out_shape=[
            jax.ShapeDtypeStruct((TOP_K, n), jnp.int32),
            jax.ShapeDtypeStruct((TOP_K, n), jnp.int32),
            jax.ShapeDtypeStruct((TOP_K, n), F32),
            jax.ShapeDtypeStruct((ne, 1), F32),
        ],
        scratch_shapes=[pltpu.VMEM((ne, 1), F32)],
        compiler_params=_cparams("arbitrary"),
        name="router",
    )(h_p, h_s, wh, router_bias.astype(F32).reshape(ne, 1))


def _positions_kernel(eidx_ref, rank_ref, start_ref, pos_ref):
    ne = start_ref.shape[0]
    tt = eidx_ref.shape[1]
    ei = lax.broadcasted_iota(jnp.int32, (ne, tt), 0)
    start = start_ref[...]
    rows = [jnp.sum(jnp.where(ei == eidx_ref[k:k + 1, :], start, 0.0), axis=0, keepdims=True)
            for k in range(TOP_K)]
    pos_ref[...] = jnp.concatenate(rows, axis=0).astype(jnp.int32) + rank_ref[...]


def _positions(eidx, rank, start, tt):
    n = eidx.shape[1]
    ne = start.shape[0]
    return pl.pallas_call(
        _positions_kernel,
        grid=(n // tt,),
        in_specs=[
            pl.BlockSpec((TOP_K, tt), lambda i: (0, i)),
            pl.BlockSpec((TOP_K, tt), lambda i: (0, i)),
            pl.BlockSpec((ne, 1), lambda i: (0, 0)),
        ],
        out_specs=pl.BlockSpec((TOP_K, tt), lambda i: (0, i)),
        out_shape=jax.ShapeDtypeStruct((TOP_K, n), jnp.int32),
        compiler_params=_cparams("parallel"),
        name="positions",
    )(eidx, rank, start.astype(F32).reshape(ne, 1))


def _sc_workers():
    info = plsc.get_sparse_core_info()
    return info.num_cores, info.num_subcores


def _sc_dispatch(hp_p, hp_s, pos_c, chunk):
    w = hp_p.shape[1]
    n = hp_p.shape[0] + hp_s.shape[0]
    nch_p = hp_p.shape[0] // chunk
    nc, ns = _sc_workers()
    per_worker = (n // chunk) // (nc * ns)
    mesh = plsc.VectorSubcoreMesh(core_axis_name="c", subcore_axis_name="s")

    @functools.partial(
        pl.kernel, mesh=mesh,
        out_type=jax.ShapeDtypeStruct((n * TOP_K, w), hp_p.dtype),
        scratch_types=[pltpu.VMEM((TOP_K, chunk), jnp.int32),
                       pltpu.VMEM((chunk, w), hp_p.dtype),
                       pltpu.SemaphoreType.DMA],
    )
    def dispatch(hp_hbm, hs_hbm, pos_hbm, xs_hbm, idx_v, rows_v, sem):
        wid = lax.axis_index("s") * nc + lax.axis_index("c")

        @pl.loop(0, per_worker)
        def _(ci):
            c = wid * per_worker + ci

            @pl.when(c < nch_p)
            def _():
                pltpu.sync_copy(hp_hbm.at[pl.ds(pl.multiple_of(c * chunk, chunk), chunk)], rows_v)

            @pl.when(c >= nch_p)
            def _():
                pltpu.sync_copy(
                    hs_hbm.at[pl.ds(pl.multiple_of((c - nch_p) * chunk, chunk), chunk)], rows_v)

            pltpu.sync_copy(pos_hbm.at[c], idx_v)
            copies = [pltpu.async_copy(rows_v, xs_hbm.at[idx_v.at[k]], sem) for k in range(TOP_K)]
            for cp in copies:
                cp.wait()

    return dispatch(hp_p, hp_s, pos_c)


def _sc_gather(ys, pos_c, chunk):
    w = ys.shape[1]
    n = pos_c.shape[0] * chunk
    nc, ns = _sc_workers()
    per_worker = pos_c.shape[0] // (nc * ns)
    mesh = plsc.VectorSubcoreMesh(core_axis_name="c", subcore_axis_name="s")
    nbuf = 3

    @functools.partial(
        pl.kernel, mesh=mesh,
        out_type=jax.ShapeDtypeStruct((TOP_K, n, w), ys.dtype),
        scratch_types=[pltpu.VMEM((TOP_K, chunk), jnp.int32),
                       pltpu.VMEM((nbuf, chunk, w), ys.dtype),
                       pltpu.SemaphoreType.DMA((nbuf,)),
                       pltpu.SemaphoreType.DMA((nbuf,))],
    )
    def gather(ys_hbm, pos_hbm, out_hbm, idx_v, rows_v, gsem, wsem):
        wid = lax.axis_index("s") * nc + lax.axis_index("c")

        @pl.loop(0, per_worker)
        def _(ci):
            c = wid * per_worker + ci
            off = pl.multiple_of(c * chunk, chunk)
            pltpu.sync_copy(pos_hbm.at[c], idx_v)

            def start_gather(k):
                b = k % nbuf
                return pltpu.async_copy(ys_hbm.at[idx_v.at[k]], rows_v.at[b], gsem.at[b])

            gathers = {0: start_gather(0)}
            writes = {}
            for k in range(TOP_K):
                if k + 1 < TOP_K:
                    if k + 1 - nbuf >= 0:
                        writes.pop(k + 1 - nbuf).wait()
                    gathers[k + 1] = start_gather(k + 1)
                gathers.pop(k).wait()
                b = k % nbuf
                writes[k] = pltpu.async_copy(rows_v.at[b], out_hbm.at[k, pl.ds(off, chunk)],
                                             wsem.at[b])
            for k in sorted(writes):
                writes[k].wait()

    return gather(ys, pos_c)


def _experts_kernel(tile_ref, exp_ref, ns_ref, off_ref, xs_ref, wg_ref, wu_ref, wd_ref, ys_ref,
                    wgb, wub, wdb, acc, *, tm):
    s = pl.program_id(0)
    prev = jnp.maximum(s - 1, 0)
    e = exp_ref[s]
    t = tile_ref[s]
    half = xs_ref.shape[1]
    lo_row, hi_row = off_ref[e], off_ref[e + 1]
    base = t * tm

    @pl.when((s == 0) | (e != exp_ref[prev]))
    def _():
        wgb[...] = wg_ref[0].astype(BF16)
        wub[...] = wu_ref[0].astype(BF16)
        wdb[...] = wd_ref[0].astype(BF16)

    def ffn(masked):
        lo, hi = _unpack_halves(xs_ref[...])
        g = (jnp.dot(lo, wgb[:half], preferred_element_type=F32)
             + jnp.dot(hi, wgb[half:], preferred_element_type=F32))
        u = (jnp.dot(lo, wub[:half], preferred_element_type=F32)
             + jnp.dot(hi, wub[half:], preferred_element_type=F32))
        act = (g * jax.nn.sigmoid(g)) * u
        if masked:
            row = base + lax.broadcasted_iota(jnp.int32, (tm, 1), 0)
            act = jnp.where((row >= lo_row) & (row < hi_row), act, 0.0)
        return jnp.dot(act.astype(BF16), wdb[...], preferred_element_type=F32)

    live = s < ns_ref[0]
    whole = (lo_row <= base) & (hi_row >= base + tm)

    @pl.when(live & whole)
    def _():
        ys_ref[...] = _pack_halves(ffn(False))

    @pl.when(live & jnp.logical_not(whole))
    def _():
        y = ffn(True)
        opens = lo_row <= base

        @pl.when(opens)
        def _():
            acc[...] = y

        @pl.when(jnp.logical_not(opens))
        def _():
            acc[...] = acc[...] + y

        ys_ref[...] = _pack_halves(acc[...])


def _experts(xs, tile_of, exp_of, n_steps, offsets, w_gate, w_up, w_down, tm):
    n_rows, half = xs.shape
    ne, d, de = w_gate.shape
    s_max = tile_of.shape[0]

    def row_map(s, tile, ex, ns, off):
        return (tile[s], 0)

    def w_map(s, tile, ex, ns, off):
        return (ex[s], 0, 0)

    grid_spec = pltpu.PrefetchScalarGridSpec(
        num_scalar_prefetch=4,
        grid=(s_max,),
        in_specs=[
            pl.BlockSpec((tm, half), row_map),
            pl.BlockSpec((1, d, de), w_map),
            pl.BlockSpec((1, d, de), w_map),
            pl.BlockSpec((1, de, d), w_map),
        ],
        out_specs=pl.BlockSpec((tm, half), row_map),
        scratch_shapes=[
            pltpu.VMEM((d, de), BF16),
            pltpu.VMEM((d, de), BF16),
            pltpu.VMEM((de, d), BF16),
            pltpu.VMEM((tm, d), F32),
        ],
    )
    return pl.pallas_call(
        functools.partial(_experts_kernel, tm=tm),
        grid_spec=grid_spec,
        out_shape=jax.ShapeDtypeStruct((n_rows, half), jnp.uint32),
        compiler_params=_cparams("arbitrary"),
        name="experts",
    )(tile_of, exp_of, n_steps, offsets, xs, w_gate, w_up, w_down)


def _expert_schedule(counts, n_rows, tm):
    ne = counts.shape[0]
    s_max = n_rows // tm + ne
    off = jnp.concatenate([jnp.zeros((1,), jnp.int32), jnp.cumsum(counts)]).astype(jnp.int32)
    first_tile = off[:-1] // tm
    last_tile = (off[1:] - 1) // tm
    visits = jnp.where(counts > 0, last_tile - first_tile + 1, 0)
    cum = jnp.cumsum(visits)
    n_steps = cum[-1]
    step = jnp.minimum(jnp.arange(s_max, dtype=jnp.int32), n_steps - 1)
    exp_of = jnp.sum((cum[None, :] <= step[:, None]).astype(jnp.int32), axis=1)
    onehot = exp_of[:, None] == jnp.arange(ne, dtype=jnp.int32)[None, :]
    pick = lambda v: jnp.sum(jnp.where(onehot, v[None, :], 0), axis=1)
    tile_of = pick(first_tile) + step - pick(cum - visits)
    return (tile_of.astype(jnp.int32), exp_of.astype(jnp.int32),
            n_steps.reshape(1).astype(jnp.int32), off)


def _combine_kernel(hp_ref, hs_ref, gate_ref, yk_ref, wsg_ref, wsu_ref, wsd_ref, g2_ref,
                    b2_ref, outp_ref, outs_ref, *, tt, nq_p):
    is_p = pl.program_id(0) < nq_p
    h = jnp.where(is_p, hp_ref[...], hs_ref[...])
    hb = h.astype(BF16)
    g = jnp.dot(hb, wsg_ref[...], preferred_element_type=F32)
    u = jnp.dot(hb, wsu_ref[...], preferred_element_type=F32)
    act = (g * jax.nn.sigmoid(g)) * u
    f = jnp.dot(act.astype(BF16), wsd_ref[...], preferred_element_type=F32)
    gate = gate_ref[...]
    r_lo = jnp.zeros((tt, yk_ref.shape[-1]), F32)
    r_hi = jnp.zeros((tt, yk_ref.shape[-1]), F32)
    for k in range(TOP_K):
        p = yk_ref[k]
        gk = gate[:, k:k + 1]
        r_lo = r_lo + gk * pltpu.bitcast(p << 16, F32)
        r_hi = r_hi + gk * pltpu.bitcast(p & jnp.uint32(0xFFFF0000), F32)
    f = f + jnp.concatenate([r_lo, r_hi], axis=-1)
    out = _layer_norm(DEEPNORM_ALPHA * h + f, g2_ref[...], b2_ref[...])

    @pl.when(is_p)
    def _():
        outp_ref[...] = out

    @pl.when(jnp.logical_not(is_p))
    def _():
        outs_ref[...] = out


def _combine(h_p, h_s, gate_t, yk, wsg_bf, wsu_bf, wsd_bf, ln2_g, ln2_b, tt):
    d = h_p.shape[1]
    n_p, n_s = h_p.shape[0], h_s.shape[0]
    nq_p = n_p // tt
    ds_ = wsg_bf.shape[1]
    kern = functools.partial(_combine_kernel, tt=tt, nq_p=nq_p)
    return pl.pallas_call(
        kern,
        grid=((n_p + n_s) // tt,),
        in_specs=_dual_row_specs(tt, d, nq_p) + [
            pl.BlockSpec((tt, TOP_K), lambda i: (i, 0)),
            pl.BlockSpec((TOP_K, tt, yk.shape[2]), lambda i: (0, i, 0)),
            pl.BlockSpec((d, ds_), lambda i: (0, 0)),
            pl.BlockSpec((d, ds_), lambda i: (0, 0)),
            pl.BlockSpec((ds_, d), lambda i: (0, 0)),
            pl.BlockSpec((1, d), lambda i: (0, 0)),
            pl.BlockSpec((1, d), lambda i: (0, 0)),
        ],
        out_specs=_dual_row_specs(tt, d, nq_p),
        out_shape=[jax.ShapeDtypeStruct((n_p, d), F32), jax.ShapeDtypeStruct((n_s, d), F32)],
        compiler_params=_cparams("arbitrary"),
        name="combine",
    )(h_p, h_s, gate_t, yk, wsg_bf, wsu_bf, wsd_bf, ln2_g.reshape(1, d), ln2_b.reshape(1, d))


def _pick_tile(seq, bsz, rows):
    return max(SUBLANES, min(seq, rows // bsz))


def _mixer(x, meta_state, u_meta_last, p):
    bsz, seq, _ = x.shape
    tt = _pick_tile(seq, bsz, 1024)
    h0, gb, u, su = _embed_inproj(x, p["ln_emb_g"], p["ln_emb_b"], p["w_in_bf"], tt)
    tc = _pick_tile(seq, bsz, 256)
    s0 = jnp.stack([jnp.broadcast_to(meta_state, (N_SLABS, SUBLANES, 2 * SLAB_STATE)),
                    jnp.zeros((N_SLABS, SUBLANES, 2 * SLAB_STATE), F32)])
    a_b = jnp.broadcast_to(p["s5_a"][:, :, None, :], (2, N_SLABS, SUBLANES, 2 * SLAB_STATE))
    yf, yr = _s5_scan(su, s0, p["s5_wb"], p["s5_wc"], a_b, bsz=bsz, tc=tc, dirs=(False, True),
                      emit_y=True, emit_state=False)
    tt3 = _pick_tile(seq, bsz, 512)
    h1, h1p = _mixer_tail(h0, gb, u, u_meta_last, su, yf, yr, p["conv_w"], p["conv_b"],
                          p["ssm_d"], p["w_glu_bf"], p["b_glu"], p["norm_a_g"], p["norm_b_g"],
                          p["w_out_bf"], p["ln1_g"], p["ln1_b"], tt3)
    d = h1.shape[-1]
    return h1.reshape(bsz * seq, d), h1p.reshape(bsz * seq, d // 2)


def kernel(x_prompt, x_sample, meta_tokens, ln_emb_g, ln_emb_b, w_in, conv_w, conv_b, ssm_lambda_re, ssm_lambda_im, ssm_log_step, ssm_b_re, ssm_b_im, ssm_c_re, ssm_c_im, ssm_d, w_glu, b_glu, norm_a_g, norm_b_g, w_out, ln1_g, ln1_b, w_router, router_bias, w_exp_gate, w_exp_up, w_exp_down, w_sh_gate, w_sh_up, w_sh_down, ln2_g, ln2_b):
    l = 0
    d = x_prompt.shape[-1]
    dirs = [_s5_params(ssm_lambda_re[l, k].astype(F32), ssm_lambda_im[l, k].astype(F32),
                       ssm_log_step[l, k], ssm_b_re[l, k].astype(F32), ssm_b_im[l, k].astype(F32),
                       ssm_c_re[l, k], ssm_c_im[l, k]) for k in range(2)]
    p = dict(
        ln_emb_g=ln_emb_g, ln_emb_b=ln_emb_b, w_in_bf=w_in[l].astype(BF16),
        conv_w=conv_w[l], conv_b=conv_b[l], ssm_d=ssm_d[l],
        w_glu_bf=w_glu[l].astype(BF16), b_glu=b_glu[l], norm_a_g=norm_a_g[l],
        norm_b_g=norm_b_g[l], w_out_bf=w_out[l].astype(BF16), ln1_g=ln1_g[l], ln1_b=ln1_b[l],
        s5_wb=jnp.stack([dirs[0][0], dirs[1][0]]), s5_wc=jnp.stack([dirs[0][1], dirs[1][1]]),
        s5_a=jnp.stack([dirs[0][2], dirs[1][2]]),
    )
    mb = SUBLANES
    xm = jnp.broadcast_to(meta_tokens.astype(F32)[None], (mb, N_META, d))
    _, _, u_m, su_m = _embed_inproj(xm, ln_emb_g, ln_emb_b, p["w_in_bf"], N_META)
    a_m = jnp.broadcast_to(p["s5_a"][:1, :, None, :], (1, N_SLABS, mb, 2 * SLAB_STATE))
    (st_m,) = _s5_scan(su_m, jnp.zeros((1, N_SLABS, mb, 2 * SLAB_STATE), F32), p["s5_wb"][:1],
                       p["s5_wc"][:1], a_m, bsz=mb, tc=N_META, dirs=(False,), emit_y=False,
                       emit_state=True)
    meta_state = st_m[0, :, :1, :]
    u_meta_last = u_m[0, N_META - 1]

    h1_p, h1p_p = _mixer(x_prompt, meta_state, u_meta_last, p)
    h1_s, h1p_s = _mixer(x_sample, meta_state, u_meta_last, p)
    n = h1_p.shape[0] + h1_s.shape[0]

    tr = min(h1_p.shape[0], h1_s.shape[0], ROUTER_TILE)
    eidx, rank, gate, cnt = _router(h1_p, h1_s, w_router[l], router_bias[l], tr)
    counts = cnt[:, 0].astype(jnp.int32)
    tile_of, exp_of, n_steps, offsets = _expert_schedule(counts, n * TOP_K, EXPERT_TILE)
    pos = _positions(eidx, rank, offsets[:-1], tr)
    pos_c = pos.reshape(TOP_K, n // SC_CHUNK, SC_CHUNK).transpose(1, 0, 2)
    xs = _sc_dispatch(h1p_p, h1p_s, pos_c, SC_CHUNK)
    ys = _experts(xs, tile_of, exp_of, n_steps, offsets, w_exp_gate[l], w_exp_up[l],
                  w_exp_down[l], EXPERT_TILE)
    yk = _sc_gather(ys, pos_c, SC_CHUNK)
    td = min(h1_p.shape[0], h1_s.shape[0], COMBINE_TILE)
    out_p, out_s = _combine(h1_p, h1_s, gate.T, yk, w_sh_gate[l].astype(BF16),
                            w_sh_up[l].astype(BF16), w_sh_down[l].astype(BF16), ln2_g[l],
                            ln2_b[l], td)
    return (out_p.reshape(x_prompt.shape), out_s.reshape(x_sample.shape))
```

```python
import functools
import math

import jax
import jax.numpy as jnp
from jax import lax
from jax.experimental import pallas as pl
from jax.experimental.pallas import tpu as pltpu
from jax.experimental.pallas import tpu_sc as plsc

F32 = jnp.float32
BF16 = jnp.bfloat16

N_META = 16
CONV_WIDTH = 3
SSM_GROUP = 16
SSM_STATE = 64
N_EXPERTS = 256
TOP_K = 8
N_EXPERT_GROUPS = 8
TOPK_GROUPS = 4
ROUTED_SCALE = 2.5
DEPTH = 1
DEEPNORM_ALPHA = (2.0 * DEPTH) ** 0.25
LN_EPS = 1e-5
RMS_EPS = 1e-6

LANES = 128
SUBLANES = 8
N_SLABS = 4
GROUPS_PER_SLAB = LANES // SSM_GROUP
SLAB_STATE = GROUPS_PER_SLAB * SSM_STATE
EXPERT_TILE = 512
ROUTER_TILE = 256
COMBINE_TILE = 256
SC_CHUNK = 64
VMEM_LIMIT = 48 * 1024 * 1024


def _cparams(*sem):
    return pltpu.CompilerParams(dimension_semantics=sem, vmem_limit_bytes=VMEM_LIMIT)


def _layer_norm(x, g, b):
    mu = jnp.mean(x, axis=-1, keepdims=True)
    xc = x - mu
    var = jnp.mean(xc * xc, axis=-1, keepdims=True)
    return xc * lax.rsqrt(var + LN_EPS) * g + b


def _rms_norm(x, g):
    return x * lax.rsqrt(jnp.mean(x * x, axis=-1, keepdims=True) + RMS_EPS) * g


def _pack_halves(x):
    half = x.shape[-1] // 2
    bits = pltpu.bitcast(x.astype(BF16).astype(F32), jnp.uint32)
    return (bits[:, :half] >> 16) | (bits[:, half:] & jnp.uint32(0xFFFF0000))


def _unpack_halves(p):
    lo = pltpu.bitcast(p << 16, F32).astype(BF16)
    hi = pltpu.bitcast(p & jnp.uint32(0xFFFF0000), F32).astype(BF16)
    return lo, hi


def _embed_inproj_kernel(x_ref, g_ref, b_ref, w_ref, h0_ref, gb_ref, u_ref, su_ref, *, bsz, tt):
    d = x_ref.shape[-1]
    x = x_ref[...].reshape(bsz * tt, d)
    h0 = _layer_norm(x, g_ref[...], b_ref[...])
    h0_ref[...] = h0.reshape(bsz, tt, d)
    proj = jnp.dot(h0.astype(BF16), w_ref[...], preferred_element_type=F32)
    wc = gb_ref.shape[-1]
    gb_ref[...] = proj[:, :wc].reshape(bsz, tt, wc)
    u_ref[...] = (proj[:, wc:2 * wc] * proj[:, 2 * wc:3 * wc]).reshape(bsz, tt, wc)
    s_u = proj[:, 3 * wc:]
    for b in range(bsz):
        for j in range(N_SLABS):
            val = s_u[b * tt:(b + 1) * tt, j * LANES:(j + 1) * LANES]
            if bsz == 1:
                su_ref[j] = val
            else:
                su_ref[j, pl.ds(b, tt, stride=bsz), :] = val


def _embed_inproj(x, ln_g, ln_b, w_in_bf, tt):
    bsz, seq, d = x.shape
    e = w_in_bf.shape[1]
    wc = (e - N_SLABS * LANES) // 3
    nt = seq // tt
    kern = functools.partial(_embed_inproj_kernel, bsz=bsz, tt=tt)
    return pl.pallas_call(
        kern,
        grid=(nt,),
        in_specs=[
            pl.BlockSpec((bsz, tt, d), lambda i: (0, i, 0)),
            pl.BlockSpec((1, d), lambda i: (0, 0)),
            pl.BlockSpec((1, d), lambda i: (0, 0)),
            pl.BlockSpec((d, e), lambda i: (0, 0)),
        ],
        out_specs=[
            pl.BlockSpec((bsz, tt, d), lambda i: (0, i, 0)),
            pl.BlockSpec((bsz, tt, wc), lambda i: (0, i, 0)),
            pl.BlockSpec((bsz, tt, wc), lambda i: (0, i, 0)),
            pl.BlockSpec((N_SLABS, tt * bsz, LANES), lambda i: (0, i, 0)),
        ],
        out_shape=[
            jax.ShapeDtypeStruct((bsz, seq, d), F32),
            jax.ShapeDtypeStruct((bsz, seq, wc), F32),
            jax.ShapeDtypeStruct((bsz, seq, wc), F32),
            jax.ShapeDtypeStruct((N_SLABS, seq * bsz, LANES), F32),
        ],
        compiler_params=_cparams("parallel"),
        name="embed_inproj",
    )(x, ln_g.reshape(1, d), ln_b.reshape(1, d), w_in_bf)


def _s5_kernel(*refs, bsz, tc, dirs, emit_y, emit_state):
    nd = len(dirs)
    it = iter(refs)
    u_refs = [next(it) for _ in range(nd)]
    s0_ref, wb_ref, wc_ref, a_ref = next(it), next(it), next(it), next(it)
    y_refs = [next(it) for _ in range(nd)] if emit_y else []
    sf_ref = next(it) if emit_state else None
    bu_ref, st_ref = next(it), next(it)
    half = SLAB_STATE

    @pl.when(pl.program_id(0) == 0)
    def _():
        st_ref[...] = s0_ref[...]

    for k in range(nd):
        for j in range(N_SLABS):
            bu_ref[k, j] = jnp.dot(u_refs[k][j].astype(BF16), wb_ref[k, j],
                                   preferred_element_type=F32)

    per_tile = SUBLANES // bsz
    n_tiles = tc // per_tile

    def cmul_add(a, s, x):
        return (a[0] * s[0] - a[1] * s[1] + x[0], a[0] * s[1] + a[1] * s[0] + x[1])

    def load(k, j, ti):
        rows = pl.ds(ti * SUBLANES, SUBLANES)
        return bu_ref[k, j, rows, :half], bu_ref[k, j, rows, half:]

    def store(k, j, ti, v):
        rows = pl.ds(ti * SUBLANES, SUBLANES)
        bu_ref[k, j, rows, :half] = v[0]
        bu_ref[k, j, rows, half:] = v[1]

    if per_tile == 1:
        for j in range(N_SLABS):
            for k in range(nd):
                a = (a_ref[k, j, :, :half], a_ref[k, j, :, half:])
                s = (st_ref[k, j, :, :half], st_ref[k, j, :, half:])
                for i in range(n_tiles):
                    ti = (n_tiles - 1 - i) if dirs[k] else i
                    s = cmul_add(a, s, load(k, j, ti))
                    store(k, j, ti, s)
                st_ref[k, j, :, :half] = s[0]
                st_ref[k, j, :, half:] = s[1]
    else:
        assert per_tile == 2 and tuple(dirs) == (False, True)
        low = lax.broadcasted_iota(jnp.int32, (SUBLANES, half), 0) < bsz

        def pick(p, q):
            return (jnp.where(low, p[0], q[0]), jnp.where(low, p[1], q[1]))

        def swap(v):
            return (pltpu.roll(v[0], bsz, axis=0), pltpu.roll(v[1], bsz, axis=0))

        for j in range(N_SLABS):
            af = (a_ref[0, j, :, :half], a_ref[0, j, :, half:])
            ar = (a_ref[1, j, :, :half], a_ref[1, j, :, half:])
            a_fr, a_rf = pick(af, ar), pick(ar, af)
            s = pick((st_ref[0, j, :, :half], st_ref[0, j, :, half:]),
                     (st_ref[1, j, :, :half], st_ref[1, j, :, half:]))
            for i in range(n_tiles):
                tf, tr = i, n_tiles - 1 - i
                xf, xr = load(0, j, tf), load(1, j, tr)
                s1 = cmul_add(a_fr, s, pick(xf, xr))
                s2 = cmul_add(a_rf, swap(s1), pick(xr, xf))
                store(0, j, tf, pick(s1, s2))
                store(1, j, tr, pick(s2, s1))
                s = swap(s2)
            for k in range(nd):
                st_ref[k, j, :, :half] = s[0]
                st_ref[k, j, :, half:] = s[1]

    if emit_y:
        for k in range(nd):
            for j in range(N_SLABS):
                y_refs[k][j] = jnp.dot(bu_ref[k, j].astype(BF16), wc_ref[k, j],
                                       preferred_element_type=F32)
    if emit_state:
        sf_ref[...] = st_ref[...]


def _s5_scan(su, s0, wb, wc, a_b, *, bsz, tc, dirs, emit_y, emit_state):
    rows = su.shape[1]
    seq = rows // bsz
    nc = seq // tc
    r = tc * bsz
    nd = len(dirs)
    sw = 2 * SLAB_STATE

    def u_map(rev):
        return (lambda c: (0, nc - 1 - c, 0)) if rev else (lambda c: (0, c, 0))

    in_specs = [pl.BlockSpec((N_SLABS, r, LANES), u_map(rev)) for rev in dirs]
    in_specs += [
        pl.BlockSpec((nd, N_SLABS, SUBLANES, sw), lambda c: (0, 0, 0, 0)),
        pl.BlockSpec((nd, N_SLABS, LANES, sw), lambda c: (0, 0, 0, 0)),
        pl.BlockSpec((nd, N_SLABS, sw, LANES), lambda c: (0, 0, 0, 0)),
        pl.BlockSpec((nd, N_SLABS, SUBLANES, sw), lambda c: (0, 0, 0, 0)),
    ]
    out_specs, out_shape = [], []
    if emit_y:
        for rev in dirs:
            out_specs.append(pl.BlockSpec((N_SLABS, r, LANES), u_map(rev)))
            out_shape.append(jax.ShapeDtypeStruct((N_SLABS, rows, LANES), F32))
    if emit_state:
        out_specs.append(pl.BlockSpec((nd, N_SLABS, SUBLANES, sw), lambda c: (0, 0, 0, 0)))
        out_shape.append(jax.ShapeDtypeStruct((nd, N_SLABS, SUBLANES, sw), F32))
    kern = functools.partial(_s5_kernel, bsz=bsz, tc=tc, dirs=dirs, emit_y=emit_y,
                             emit_state=emit_state)
    return pl.pallas_call(
        kern,
        grid=(nc,),
        in_specs=in_specs,
        out_specs=out_specs,
        out_shape=out_shape,
        scratch_shapes=[
            pltpu.VMEM((nd, N_SLABS, r, sw), F32),
            pltpu.VMEM((nd, N_SLABS, SUBLANES, sw), F32),
        ],
        compiler_params=_cparams("arbitrary"),
        name="s5_scan",
    )(*([su] * nd), s0, wb, wc, a_b)


def _s5_params(lam_re, lam_im, log_step, b_re, b_im, c_re, c_im):
    g, p = lam_re.shape
    h = b_re.shape[-1]
    dt = jnp.exp(log_step.astype(F32))[:, None]
    mag = jnp.exp(lam_re * dt)
    ab_re = mag * jnp.cos(lam_im * dt)
    ab_im = mag * jnp.sin(lam_im * dt)
    den = lam_re * lam_re + lam_im * lam_im
    nr, ni = ab_re - 1.0, ab_im
    f_re = (nr * lam_re + ni * lam_im) / den
    f_im = (ni * lam_re - nr * lam_im) / den
    bb_re = f_re[..., None] * b_re - f_im[..., None] * b_im
    bb_im = f_re[..., None] * b_im + f_im[..., None] * b_re
    ns, gl = N_SLABS, GROUPS_PER_SLAB
    eye = jnp.eye(gl, dtype=F32)

    def in_block(bb):
        bb = bb.reshape(ns, gl, p, h)
        return jnp.einsum('sgph,gk->sghkp', bb, eye).reshape(ns, gl * h, gl * p)

    def out_block(cc):
        cc = cc.reshape(ns, gl, h, p)
        return jnp.einsum('sghp,gk->sgpkh', cc, eye).reshape(ns, gl * p, gl * h)

    wb = jnp.concatenate([in_block(bb_re), in_block(bb_im)], axis=-1)
    wc = jnp.concatenate([out_block(c_re.astype(F32)), -out_block(c_im.astype(F32))], axis=1)
    a = jnp.concatenate([ab_re.reshape(ns, gl * p), ab_im.reshape(ns, gl * p)], axis=-1)
    return wb.astype(BF16), wc.astype(BF16), a


def _mixer_tail_kernel(h0_ref, gb_ref, u_ref, up_ref, un_ref, um_ref, su_ref, yf_ref, yr_ref,
                       cw_ref, cb_ref, sd_ref, wg_ref, bg_ref, na_ref, nb_ref, wo_ref,
                       g1_ref, b1_ref, h1_ref, h1p_ref, *, bsz, tt):
    i = pl.program_id(0)
    nt = pl.num_programs(0)
    d = h0_ref.shape[-1]
    wcv = gb_ref.shape[-1]
    row_id = lax.broadcasted_iota(jnp.int32, (tt, wcv), 0)
    ya, ys = [], []
    for b in range(bsz):
        u = u_ref[b]
        prev_edge = jnp.where(i == 0, um_ref[...], up_ref[b, SUBLANES - 1:SUBLANES, :])
        next_edge = jnp.where(i == nt - 1, jnp.zeros((1, wcv), F32), un_ref[b, 0:1, :])
        u_prev = jnp.where(row_id == 0, prev_edge, pltpu.roll(u, 1, axis=0))
        u_next = jnp.where(row_id == tt - 1, next_edge, pltpu.roll(u, tt - 1, axis=0))
        conv = u_prev * cw_ref[0:1, :] + u * cw_ref[1:2, :] + u_next * cw_ref[2:3, :] + cb_ref[...]
        ya.append(gb_ref[b] * conv)

        def slab(ref, b=b):
            parts = []
            for j in range(N_SLABS):
                if bsz == 1:
                    parts.append(ref[j])
                else:
                    parts.append(ref[j, pl.ds(b, tt, stride=bsz), :])
            return jnp.concatenate(parts, axis=-1)

        ys.append(slab(yf_ref) + slab(yr_ref) + sd_ref[...] * slab(su_ref))
    y_a = jnp.concatenate(ya, axis=0)
    y_s = jnp.concatenate(ys, axis=0)
    z = jax.nn.gelu(y_s)
    glu = jnp.dot(z.astype(BF16), wg_ref[...], preferred_element_type=F32) + bg_ref[...]
    y_b = z * jax.nn.sigmoid(glu)
    merged = jnp.concatenate([_rms_norm(y_a, na_ref[...]), _rms_norm(y_b, nb_ref[...])], axis=-1)
    m = jnp.dot(merged.astype(BF16), wo_ref[...], preferred_element_type=F32)
    h0 = h0_ref[...].reshape(bsz * tt, d)
    h1 = _layer_norm(DEEPNORM_ALPHA * h0 + m, g1_ref[...], b1_ref[...])
    h1_ref[...] = h1.reshape(bsz, tt, d)
    h1p_ref[...] = _pack_halves(h1).reshape(bsz, tt, d // 2)


def _mixer_tail(h0, gb, u, u_meta_last, su, yf, yr, conv_w, conv_b, ssm_d, w_glu_bf, b_glu,
                norm_a_g, norm_b_g, w_out_bf, ln1_g, ln1_b, tt):
    bsz, seq, d = h0.shape
    wcv = gb.shape[-1]
    ws = N_SLABS * LANES
    nt = seq // tt
    tb = tt // SUBLANES
    nb8 = seq // SUBLANES
    kern = functools.partial(_mixer_tail_kernel, bsz=bsz, tt=tt)
    row = lambda n: pl.BlockSpec((1, n), lambda i: (0, 0))
    slab_spec = pl.BlockSpec((N_SLABS, tt * bsz, LANES), lambda i: (0, i, 0))
    return pl.pallas_call(
        kern,
        grid=(nt,),
        in_specs=[
            pl.BlockSpec((bsz, tt, d), lambda i: (0, i, 0)),
            pl.BlockSpec((bsz, tt, wcv), lambda i: (0, i, 0)),
            pl.BlockSpec((bsz, tt, wcv), lambda i: (0, i, 0)),
            pl.BlockSpec((bsz, SUBLANES, wcv), lambda i: (0, jnp.maximum(i * tb - 1, 0), 0)),
            pl.BlockSpec((bsz, SUBLANES, wcv), lambda i: (0, jnp.minimum((i + 1) * tb, nb8 - 1), 0)),
            row(wcv),
            slab_spec, slab_spec, slab_spec,
            pl.BlockSpec((CONV_WIDTH, wcv), lambda i: (0, 0)),
            row(wcv), row(ws),
            pl.BlockSpec((ws, ws), lambda i: (0, 0)),
            row(ws), row(wcv), row(ws),
            pl.BlockSpec((wcv + ws, d), lambda i: (0, 0)),
            row(d), row(d),
        ],
        out_specs=[
            pl.BlockSpec((bsz, tt, d), lambda i: (0, i, 0)),
            pl.BlockSpec((bsz, tt, d // 2), lambda i: (0, i, 0)),
        ],
        out_shape=[
            jax.ShapeDtypeStruct((bsz, seq, d), F32),
            jax.ShapeDtypeStruct((bsz, seq, d // 2), jnp.uint32),
        ],
        compiler_params=_cparams("parallel"),
        name="mixer_tail",
    )(h0, gb, u, u, u, u_meta_last.reshape(1, wcv), su, yf, yr, conv_w, conv_b.reshape(1, wcv),
      ssm_d.reshape(1, ws), w_glu_bf, b_glu.reshape(1, ws), norm_a_g.reshape(1, wcv),
      norm_b_g.reshape(1, ws), w_out_bf, ln1_g.reshape(1, d), ln1_b.reshape(1, d))


def _dual_row_specs(rows, width, nq_p):
    return [pl.BlockSpec((rows, width), lambda q, *_: (jnp.minimum(q, nq_p - 1), 0)),
            pl.BlockSpec((rows, width), lambda q, *_: (jnp.maximum(q - nq_p, 0), 0))]


def _router_kernel(hp_ref, hs_ref, wh_ref, bias_ref, eidx_ref, rank_ref, gate_ref, cnt_ref,
                   cnt_scr, *, tt, nq_p):
    ne = wh_ref.shape[0]
    epg = ne // N_EXPERT_GROUPS
    neg = jnp.float32(-jnp.inf)

    @pl.when(pl.program_id(0) == 0)
    def _():
        cnt_scr[...] = jnp.zeros_like(cnt_scr)

    h = jnp.where(pl.program_id(0) < nq_p, hp_ref[...], hs_ref[...])
    dn = (((1,), (1,)), ((), ()))
    logits = lax.dot_general(wh_ref[...], h.astype(BF16), dn,
                             preferred_element_type=F32)
    scores = jax.nn.sigmoid(logits)
    sel = scores + bias_ref[...]

    gi = lax.broadcasted_iota(jnp.int32, (epg, tt), 0)
    gs = []
    for g in range(N_EXPERT_GROUPS):
        x = sel[g * epg:(g + 1) * epg, :]
        m1 = jnp.max(x, axis=0, keepdims=True)
        i1 = jnp.min(jnp.where(x == m1, gi, epg), axis=0, keepdims=True)
        m2 = jnp.max(jnp.where(gi == i1, neg, x), axis=0, keepdims=True)
        gs.append(m1 + m2)
    chosen = [jnp.zeros((1, tt), F32) for _ in range(N_EXPERT_GROUPS)]
    for _ in range(TOPK_GROUPS):
        m = gs[0]
        for g in range(1, N_EXPERT_GROUPS):
            m = jnp.maximum(m, gs[g])
        found = jnp.zeros((1, tt), F32)
        for g in range(N_EXPERT_GROUPS):
            hit = jnp.where((gs[g] == m) & (found == 0.0), 1.0, 0.0)
            chosen[g] = chosen[g] + hit
            found = found + hit
            gs[g] = jnp.where(hit > 0.0, neg, gs[g])
    selm = jnp.concatenate(
        [jnp.where(chosen[g] > 0.0, sel[g * epg:(g + 1) * epg, :], neg)
         for g in range(N_EXPERT_GROUPS)], axis=0)

    ei = lax.broadcasted_iota(jnp.int32, (ne, tt), 0)
    msel = jnp.zeros((ne, tt), F32)
    idxs, gvals = [], []
    for _ in range(TOP_K):
        m = jnp.max(selm, axis=0, keepdims=True)
        idx = jnp.min(jnp.where(selm == m, ei, ne), axis=0, keepdims=True)
        hit = ei == idx
        gvals.append(jnp.sum(jnp.where(hit, scores, 0.0), axis=0, keepdims=True))
        selm = jnp.where(hit, neg, selm)
        msel = jnp.where(hit, 1.0, msel)
        idxs.append(idx)
    gsum = gvals[0]
    for k in range(1, TOP_K):
        gsum = gsum + gvals[k]
    gate_ref[...] = jnp.concatenate([gv / gsum * ROUTED_SCALE for gv in gvals], axis=0)
    eidx_ref[...] = jnp.concatenate(idxs, axis=0)

    r_i = lax.broadcasted_iota(jnp.int32, (tt, tt), 0)
    c_i = lax.broadcasted_iota(jnp.int32, (tt, tt), 1)
    upper = jnp.where(r_i < c_i, 1.0, 0.0).astype(BF16)
    rank_full = jnp.dot(msel.astype(BF16), upper, preferred_element_type=F32) + cnt_scr[...]
    ranks = [jnp.sum(jnp.where(ei == idxs[k], rank_full, 0.0), axis=0, keepdims=True)
             for k in range(TOP_K)]
    rank_ref[...] = jnp.concatenate(ranks, axis=0).astype(jnp.int32)
    cnt_scr[...] = cnt_scr[...] + jnp.sum(msel, axis=1, keepdims=True)
    cnt_ref[...] = cnt_scr[...]


def _router(h_p, h_s, w_router, router_bias, tt):
    d = h_p.shape[1]
    n = h_p.shape[0] + h_s.shape[0]
    nq_p = h_p.shape[0] // tt
    ne = w_router.shape[1]
    wh = w_router.T.astype(BF16)
    kern = functools.partial(_router_kernel, tt=tt, nq_p=nq_p)
    return pl.pallas_call(
        kern,
        grid=(n // tt,),
        in_specs=_dual_row_specs(tt, d, nq_p) + [
            pl.BlockSpec((ne, d), lambda i: (0, 0)),
            pl.BlockSpec((ne, 1), lambda i: (0, 0)),
        ],
        out_specs=[
            pl.BlockSpec((TOP_K, tt), lambda i: (0, i)),
            pl.BlockSpec((TOP_K, tt), lambda i: (0, i)),
            pl.BlockSpec((TOP_K, tt), lambda i: (0, i)),
            pl.BlockSpec((ne, 1), lambda i: (0, 0)),
        ],
        out_shape=[
            jax.ShapeDtypeStruct((TOP_K, n), jnp.int32),
            jax.ShapeDtypeStruct((TOP_K, n), jnp.int32),
            jax.ShapeDtypeStruct((TOP_K, n), F32),
            jax.ShapeDtypeStruct((ne, 1), F32),
        ],
        scratch_shapes=[pltpu.VMEM((ne, 1), F32)],
        compiler_params=_cparams("arbitrary"),
        name="router",
    )(h_p, h_s, wh, router_bias.astype(F32).reshape(ne, 1))


def _positions_kernel(eidx_ref, rank_ref, start_ref, pos_ref):
    ne = start_ref.shape[0]
    tt = eidx_ref.shape[1]
    ei = lax.broadcasted_iota(jnp.int32, (ne, tt), 0)
    start = start_ref[...]
    rows = [jnp.sum(jnp.where(ei == eidx_ref[k:k + 1, :], start, 0.0), axis=0, keepdims=True)
            for k in range(TOP_K)]
    pos_ref[...] = jnp.concatenate(rows, axis=0).astype(jnp.int32) + rank_ref[...]


def _positions(eidx, rank, start, tt):
    n = eidx.shape[1]
    ne = start.shape[0]
    return pl.pallas_call(
        _positions_kernel,
        grid=(n // tt,),
        in_specs=[
            pl.BlockSpec((TOP_K, tt), lambda i: (0, i)),
            pl.BlockSpec((TOP_K, tt), lambda i: (0, i)),
            pl.BlockSpec((ne, 1), lambda i: (0, 0)),
        ],
        out_specs=pl.BlockSpec((TOP_K, tt), lambda i: (0, i)),
        out_shape=jax.ShapeDtypeStruct((TOP_K, n), jnp.int32),
        compiler_params=_cparams("parallel"),
        name="positions",
    )(eidx, rank, start.astype(F32).reshape(ne, 1))


def _sc_workers():
    info = plsc.get_sparse_core_info()
    return info.num_cores, info.num_subcores


def _sc_dispatch(hp_p, hp_s, pos_c, chunk):
    w = hp_p.shape[1]
    n = hp_p.shape[0] + hp_s.shape[0]
    nch_p = hp_p.shape[0] // chunk
    nc, ns = _sc_workers()
    per_worker = (n // chunk) // (nc * ns)
    mesh = plsc.VectorSubcoreMesh(core_axis_name="c", subcore_axis_name="s")

    @functools.partial(
        pl.kernel, mesh=mesh,
        out_type=jax.ShapeDtypeStruct((n * TOP_K, w), hp_p.dtype),
        scratch_types=[pltpu.VMEM((TOP_K, chunk), jnp.int32),
                       pltpu.VMEM((chunk, w), hp_p.dtype),
                       pltpu.SemaphoreType.DMA],
    )
    def dispatch(hp_hbm, hs_hbm, pos_hbm, xs_hbm, idx_v, rows_v, sem):
        wid = lax.axis_index("s") * nc + lax.axis_index("c")

        @pl.loop(0, per_worker)
        def _(ci):
            c = wid * per_worker + ci

            @pl.when(c < nch_p)
            def _():
                pltpu.sync_copy(hp_hbm.at[pl.ds(pl.multiple_of(c * chunk, chunk), chunk)], rows_v)

            @pl.when(c >= nch_p)
            def _():
                pltpu.sync_copy(
                    hs_hbm.at[pl.ds(pl.multiple_of((c - nch_p) * chunk, chunk), chunk)], rows_v)

            pltpu.sync_copy(pos_hbm.at[c], idx_v)
            copies = [pltpu.async_copy(rows_v, xs_hbm.at[idx_v.at[k]], sem) for k in range(TOP_K)]
            for cp in copies:
                cp.wait()

    return dispatch(hp_p, hp_s, pos_c)


def _sc_gather(ys, pos_c, chunk):
    w = ys.shape[1]
    n = pos_c.shape[0] * chunk
    nc, ns = _sc_workers()
    per_worker = pos_c.shape[0] // (nc * ns)
    mesh = plsc.VectorSubcoreMesh(core_axis_name="c", subcore_axis_name="s")
    nbuf = 3

    @functools.partial(
        pl.kernel, mesh=mesh,
        out_type=jax.ShapeDtypeStruct((TOP_K, n, w), ys.dtype),
        scratch_types=[pltpu.VMEM((TOP_K, chunk), jnp.int32),
                       pltpu.VMEM((nbuf, chunk, w), ys.dtype),
                       pltpu.SemaphoreType.DMA((nbuf,)),
                       pltpu.SemaphoreType.DMA((nbuf,))],
    )
    def gather(ys_hbm, pos_hbm, out_hbm, idx_v, rows_v, gsem, wsem):
        wid = lax.axis_index("s") * nc + lax.axis_index("c")

        @pl.loop(0, per_worker)
        def _(ci):
            c = wid * per_worker + ci
            off = pl.multiple_of(c * chunk, chunk)
            pltpu.sync_copy(pos_hbm.at[c], idx_v)

            def start_gather(k):
                b = k % nbuf
                return pltpu.async_copy(ys_hbm.at[idx_v.at[k]], rows_v.at[b], gsem.at[b])

            gathers = {0: start_gather(0)}
            writes = {}
            for k in range(TOP_K):
                if k + 1 < TOP_K:
                    if k + 1 - nbuf >= 0:
                        writes.pop(k + 1 - nbuf).wait()
                    gathers[k + 1] = start_gather(k + 1)
                gathers.pop(k).wait()
                b = k % nbuf
                writes[k] = pltpu.async_copy(rows_v.at[b], out_hbm.at[k, pl.ds(off, chunk)],
                                             wsem.at[b])
            for k in sorted(writes):
                writes[k].wait()

    return gather(ys, pos_c)


def _experts_kernel(tile_ref, exp_ref, ns_ref, off_ref, xs_ref, wg_ref, wu_ref, wd_ref, ys_ref,
                    wgb, wub, wdb, acc, *, tm):
    s = pl.program_id(0)
    prev = jnp.maximum(s - 1, 0)
    e = exp_ref[s]
    t = tile_ref[s]
    half = xs_ref.shape[1]
    lo_row, hi_row = off_ref[e], off_ref[e + 1]
    base = t * tm

    @pl.when((s == 0) | (e != exp_ref[prev]))
    def _():
        wgb[...] = wg_ref[0].astype(BF16)
        wub[...] = wu_ref[0].astype(BF16)
        wdb[...] = wd_ref[0].astype(BF16)

    def ffn(masked):
        lo, hi = _unpack_halves(xs_ref[...])
        g = (jnp.dot(lo, wgb[:half], preferred_element_type=F32)
             + jnp.dot(hi, wgb[half:], preferred_element_type=F32))
        u = (jnp.dot(lo, wub[:half], preferred_element_type=F32)
             + jnp.dot(hi, wub[half:], preferred_element_type=F32))
        act = (g * jax.nn.sigmoid(g)) * u
        if masked:
            row = base + lax.broadcasted_iota(jnp.int32, (tm, 1), 0)
            act = jnp.where((row >= lo_row) & (row < hi_row), act, 0.0)
        return jnp.dot(act.astype(BF16), wdb[...], preferred_element_type=F32)

    live = s < ns_ref[0]
    whole = (lo_row <= base) & (hi_row >= base + tm)

    @pl.when(live & whole)
    def _():
        ys_ref[...] = _pack_halves(ffn(False))

    @pl.when(live & jnp.logical_not(whole))
    def _():
        y = ffn(True)
        opens = lo_row <= base

        @pl.when(opens)
        def _():
            acc[...] = y

        @pl.when(jnp.logical_not(opens))
        def _():
            acc[...] = acc[...] + y

        ys_ref[...] = _pack_halves(acc[...])


def _experts(xs, tile_of, exp_of, n_steps, offsets, w_gate, w_up, w_down, tm):
    n_rows, half = xs.shape
    ne, d, de = w_gate.shape
    s_max = tile_of.shape[0]

    def row_map(s, tile, ex, ns, off):
        return (tile[s], 0)

    def w_map(s, tile, ex, ns, off):
        return (ex[s], 0, 0)

    grid_spec = pltpu.PrefetchScalarGridSpec(
        num_scalar_prefetch=4,
        grid=(s_max,),
        in_specs=[
            pl.BlockSpec((tm, half), row_map),
            pl.BlockSpec((1, d, de), w_map),
            pl.BlockSpec((1, d, de), w_map),
            pl.BlockSpec((1, de, d), w_map),
        ],
        out_specs=pl.BlockSpec((tm, half), row_map),
        scratch_shapes=[
            pltpu.VMEM((d, de), BF16),
            pltpu.VMEM((d, de), BF16),
            pltpu.VMEM((de, d), BF16),
            pltpu.VMEM((tm, d), F32),
        ],
    )
    return pl.pallas_call(
        functools.partial(_experts_kernel, tm=tm),
        grid_spec=grid_spec,
        out_shape=jax.ShapeDtypeStruct((n_rows, half), jnp.uint32),
        compiler_params=_cparams("arbitrary"),
        name="experts",
    )(tile_of, exp_of, n_steps, offsets, xs, w_gate, w_up, w_down)


def _expert_schedule(counts, n_rows, tm):
    ne = counts.shape[0]
    s_max = n_rows // tm + ne
    off = jnp.concatenate([jnp.zeros((1,), jnp.int32), jnp.cumsum(counts)]).astype(jnp.int32)
    first_tile = off[:-1] // tm
    last_tile = (off[1:] - 1) // tm
    visits = jnp.where(counts > 0, last_tile - first_tile + 1, 0)
    cum = jnp.cumsum(visits)
    n_steps = cum[-1]
    step = jnp.minimum(jnp.arange(s_max, dtype=jnp.int32), n_steps - 1)
    exp_of = jnp.sum((cum[None, :] <= step[:, None]).astype(jnp.int32), axis=1)
    onehot = exp_of[:, None] == jnp.arange(ne, dtype=jnp.int32)[None, :]
    pick = lambda v: jnp.sum(jnp.where(onehot, v[None, :], 0), axis=1)
    tile_of = pick(first_tile) + step - pick(cum - visits)
    return (tile_of.astype(jnp.int32), exp_of.astype(jnp.int32),
            n_steps.reshape(1).astype(jnp.int32), off)


def _combine_kernel(hp_ref, hs_ref, gate_ref, yk_ref, wsg_ref, wsu_ref, wsd_ref, g2_ref,
                    b2_ref, outp_ref, outs_ref, *, tt, nq_p):
    is_p = pl.program_id(0) < nq_p
    h = jnp.where(is_p, hp_ref[...], hs_ref[...])
    hb = h.astype(BF16)
    g = jnp.dot(hb, wsg_ref[...], preferred_element_type=F32)
    u = jnp.dot(hb, wsu_ref[...], preferred_element_type=F32)
    act = (g * jax.nn.sigmoid(g)) * u
    f = jnp.dot(act.astype(BF16), wsd_ref[...], preferred_element_type=F32)
    gate = gate_ref[...]
    r_lo = jnp.zeros((tt, yk_ref.shape[-1]), F32)
    r_hi = jnp.zeros((tt, yk_ref.shape[-1]), F32)
    for k in range(TOP_K):
        p = yk_ref[k]
        gk = gate[:, k:k + 1]
        r_lo = r_lo + gk * pltpu.bitcast(p << 16, F32)
        r_hi = r_hi + gk * pltpu.bitcast(p & jnp.uint32(0xFFFF0000), F32)
    f = f + jnp.concatenate([r_lo, r_hi], axis=-1)
    out = _layer_norm(DEEPNORM_ALPHA * h + f, g2_ref[...], b2_ref[...])

    @pl.when(is_p)
    def _():
        outp_ref[...] = out

    @pl.when(jnp.logical_not(is_p))
    def _():
        outs_ref[...] = out


def _combine(h_p, h_s, gate_t, yk, wsg_bf, wsu_bf, wsd_bf, ln2_g, ln2_b, tt):
    d = h_p.shape[1]
    n_p, n_s = h_p.shape[0], h_s.shape[0]
    nq_p = n_p // tt
    ds_ = wsg_bf.shape[1]
    kern = functools.partial(_combine_kernel, tt=tt, nq_p=nq_p)
    return pl.pallas_call(
        kern,
        grid=((n_p + n_s) // tt,),
        in_specs=_dual_row_specs(tt, d, nq_p) + [
            pl.BlockSpec((tt, TOP_K), lambda i: (i, 0)),
            pl.BlockSpec((TOP_K, tt, yk.shape[2]), lambda i: (0, i, 0)),
            pl.BlockSpec((d, ds_), lambda i: (0, 0)),
            pl.BlockSpec((d, ds_), lambda i: (0, 0)),
            pl.BlockSpec((ds_, d), lambda i: (0, 0)),
            pl.BlockSpec((1, d), lambda i: (0, 0)),
            pl.BlockSpec((1, d), lambda i: (0, 0)),
        ],
        out_specs=_dual_row_specs(tt, d, nq_p),
        out_shape=[jax.ShapeDtypeStruct((n_p, d), F32), jax.ShapeDtypeStruct((n_s, d), F32)],
        compiler_params=_cparams("arbitrary"),
        name="combine",
    )(h_p, h_s, gate_t, yk, wsg_bf, wsu_bf, wsd_bf, ln2_g.reshape(1, d), ln2_b.reshape(1, d))


def _pick_tile(seq, bsz, rows):
    return max(SUBLANES, min(seq, rows // bsz))


def _mixer(x, meta_state, u_meta_last, p):
    bsz, seq, _ = x.shape
    tt = _pick_tile(seq, bsz, 1024)
    h0, gb, u, su = _embed_inproj(x, p["ln_emb_g"], p["ln_emb_b"], p["w_in_bf"], tt)
    tc = _pick_tile(seq, bsz, 256)
    s0 = jnp.stack([jnp.broadcast_to(meta_state, (N_SLABS, SUBLANES, 2 * SLAB_STATE)),
                    jnp.zeros((N_SLABS, SUBLANES, 2 * SLAB_STATE), F32)])
    a_b = jnp.broadcast_to(p["s5_a"][:, :, None, :], (2, N_SLABS, SUBLANES, 2 * SLAB_STATE))
    yf, yr = _s5_scan(su, s0, p["s5_wb"], p["s5_wc"], a_b, bsz=bsz, tc=tc, dirs=(False, True),
                      emit_y=True, emit_state=False)
    tt3 = _pick_tile(seq, bsz, 512)
    h1, h1p = _mixer_tail(h0, gb, u, u_meta_last, su, yf, yr, p["conv_w"], p["conv_b"],
                          p["ssm_d"], p["w_glu_bf"], p["b_glu"], p["norm_a_g"], p["norm_b_g"],
                          p["w_out_bf"], p["ln1_g"], p["ln1_b"], tt3)
    d = h1.shape[-1]
    return h1.reshape(bsz * seq, d), h1p.reshape(bsz * seq, d // 2)


def kernel(x_prompt, x_sample, meta_tokens, ln_emb_g, ln_emb_b, w_in, conv_w, conv_b, ssm_lambda_re, ssm_lambda_im, ssm_log_step, ssm_b_re, ssm_b_im, ssm_c_re, ssm_c_im, ssm_d, w_glu, b_glu, norm_a_g, norm_b_g, w_out, ln1_g, ln1_b, w_router, router_bias, w_exp_gate, w_exp_up, w_exp_down, w_sh_gate, w_sh_up, w_sh_down, ln2_g, ln2_b):
    l = 0
    d = x_prompt.shape[-1]
    dirs = [_s5_params(ssm_lambda_re[l, k].astype(F32), ssm_lambda_im[l, k].astype(F32),
                       ssm_log_step[l, k], ssm_b_re[l, k].astype(F32), ssm_b_im[l, k].astype(F32),
                       ssm_c_re[l, k], ssm_c_im[l, k]) for k in range(2)]
    p = dict(
        ln_emb_g=ln_emb_g, ln_emb_b=ln_emb_b, w_in_bf=w_in[l].astype(BF16),
        conv_w=conv_w[l], conv_b=conv_b[l], ssm_d=ssm_d[l],
        w_glu_bf=w_glu[l].astype(BF16), b_glu=b_glu[l], norm_a_g=norm_a_g[l],
        norm_b_g=norm_b_g[l], w_out_bf=w_out[l].astype(BF16), ln1_g=ln1_g[l], ln1_b=ln1_b[l],
        s5_wb=jnp.stack([dirs[0][0], dirs[1][0]]), s5_wc=jnp.stack([dirs[0][1], dirs[1][1]]),
        s5_a=jnp.stack([dirs[0][2], dirs[1][2]]),
    )
    mb = SUBLANES
    xm = jnp.broadcast_to(meta_tokens.astype(F32)[None], (mb, N_META, d))
    _, _, u_m, su_m = _embed_inproj(xm, ln_emb_g, ln_emb_b, p["w_in_bf"], N_META)
    a_m = jnp.broadcast_to(p["s5_a"][:1, :, None, :], (1, N_SLABS, mb, 2 * SLAB_STATE))
    (st_m,) = _s5_scan(su_m, jnp.zeros((1, N_SLABS, mb, 2 * SLAB_STATE), F32), p["s5_wb"][:1],
                       p["s5_wc"][:1], a_m, bsz=mb, tc=N_META, dirs=(False,), emit_y=False,
                       emit_state=True)
    meta_state = st_m[0, :, :1, :]
    u_meta_last = u_m[0, N_META - 1]

    h1_p, h1p_p = _mixer(x_prompt, meta_state, u_meta_last, p)
    h1_s, h1p_s = _mixer(x_sample, meta_state, u_meta_last, p)
    n = h1_p.shape[0] + h1_s.shape[0]

    tr = min(h1_p.shape[0], h1_s.shape[0], ROUTER_TILE)
    eidx, rank, gate, cnt = _router(h1_p, h1_s, w_router[l], router_bias[l], tr)
    counts = cnt[:, 0].astype(jnp.int32)
    tile_of, exp_of, n_steps, offsets = _expert_schedule(counts, n * TOP_K, EXPERT_TILE)
    pos = _positions(eidx, rank, offsets[:-1], tr)
    pos_c = pos.reshape(TOP_K, n // SC_CHUNK, SC_CHUNK).transpose(1, 0, 2)
    xs = _sc_dispatch(h1p_p, h1p_s, pos_c, SC_CHUNK)
    ys = _experts(xs, tile_of, exp_of, n_steps, offsets, w_exp_gate[l], w_exp_up[l],
                  w_exp_down[l], EXPERT_TILE)
    yk = _sc_gather(ys, pos_c, SC_CHUNK)
    td = min(h1_p.shape[0], h1_s.shape[0], COMBINE_TILE)
    out_p, out_s = _combine(h1_p, h1_s, gate.T, yk, w_sh_gate[l].astype(BF16),
                            w_sh_up[l].astype(BF16), w_sh_down[l].astype(BF16), ln2_g[l],
                            ln2_b[l], td)
    return (out_p.reshape(x_prompt.shape), out_s.reshape(x_sample.shape))
```

```python
import functools
import math

import jax
import jax.numpy as jnp
from jax import lax
from jax.experimental import pallas as pl
from jax.experimental.pallas import tpu as pltpu
from jax.experimental.pallas import tpu_sc as plsc

F32 = jnp.float32
BF16 = jnp.bfloat16

N_META = 16
CONV_WIDTH = 3
SSM_GROUP = 16
SSM_STATE = 64
N_EXPERTS = 256
TOP_K = 8
N_EXPERT_GROUPS = 8
TOPK_GROUPS = 4
ROUTED_SCALE = 2.5
DEPTH = 1
DEEPNORM_ALPHA = (2.0 * DEPTH) ** 0.25
LN_EPS = 1e-5
RMS_EPS = 1e-6

LANES = 128
SUBLANES = 8
N_SLABS = 4
GROUPS_PER_SLAB = LANES // SSM_GROUP
SLAB_STATE = GROUPS_PER_SLAB * SSM_STATE
EXPERT_TILE = 2048
EXPERT_SUB = 512
ROUTER_TILE = 256
COMBINE_TILE = 256
SC_CHUNK = 64
VMEM_LIMIT = 48 * 1024 * 1024


def _cparams(*sem):
    return pltpu.CompilerParams(dimension_semantics=sem, vmem_limit_bytes=VMEM_LIMIT)


def _layer_norm(x, g, b):
    mu = jnp.mean(x, axis=-1, keepdims=True)
    xc = x - mu
    var = jnp.mean(xc * xc, axis=-1, keepdims=True)
    return xc * lax.rsqrt(var + LN_EPS) * g + b


def _rms_norm(x, g):
    return x * lax.rsqrt(jnp.mean(x * x, axis=-1, keepdims=True) + RMS_EPS) * g


def _pack_halves(x):
    half = x.shape[-1] // 2
    bits = pltpu.bitcast(x.astype(BF16).astype(F32), jnp.uint32)
    return (bits[:, :half] >> 16) | (bits[:, half:] & jnp.uint32(0xFFFF0000))


def _unpack_halves(p):
    lo = pltpu.bitcast(p << 16, F32).astype(BF16)
    hi = pltpu.bitcast(p & jnp.uint32(0xFFFF0000), F32).astype(BF16)
    return lo, hi


def _embed_inproj_kernel(x_ref, g_ref, b_ref, w_ref, h0_ref, gb_ref, u_ref, su_ref, *, bsz, tt):
    d = x_ref.shape[-1]
    x = x_ref[...].reshape(bsz * tt, d)
    h0 = _layer_norm(x, g_ref[...], b_ref[...])
    h0_ref[...] = h0.reshape(bsz, tt, d)
    proj = jnp.dot(h0.astype(BF16), w_ref[...], preferred_element_type=F32)
    wc = gb_ref.shape[-1]
    gb_ref[...] = proj[:, :wc].reshape(bsz, tt, wc)
    u_ref[...] = (proj[:, wc:2 * wc] * proj[:, 2 * wc:3 * wc]).reshape(bsz, tt, wc)
    s_u = proj[:, 3 * wc:]
    for b in range(bsz):
        for j in range(N_SLABS):
            val = s_u[b * tt:(b + 1) * tt, j * LANES:(j + 1) * LANES]
            if bsz == 1:
                su_ref[j] = val
            else:
                su_ref[j, pl.ds(b, tt, stride=bsz), :] = val


def _embed_inproj(x, ln_g, ln_b, w_in_bf, tt):
    bsz, seq, d = x.shape
    e = w_in_bf.shape[1]
    wc = (e - N_SLABS * LANES) // 3
    nt = seq // tt
    kern = functools.partial(_embed_inproj_kernel, bsz=bsz, tt=tt)
    return pl.pallas_call(
        kern,
        grid=(nt,),
        in_specs=[
            pl.BlockSpec((bsz, tt, d), lambda i: (0, i, 0)),
            pl.BlockSpec((1, d), lambda i: (0, 0)),
            pl.BlockSpec((1, d), lambda i: (0, 0)),
            pl.BlockSpec((d, e), lambda i: (0, 0)),
        ],
        out_specs=[
            pl.BlockSpec((bsz, tt, d), lambda i: (0, i, 0)),
            pl.BlockSpec((bsz, tt, wc), lambda i: (0, i, 0)),
            pl.BlockSpec((bsz, tt, wc), lambda i: (0, i, 0)),
            pl.BlockSpec((N_SLABS, tt * bsz, LANES), lambda i: (0, i, 0)),
        ],
        out_shape=[
            jax.ShapeDtypeStruct((bsz, seq, d), F32),
            jax.ShapeDtypeStruct((bsz, seq, wc), F32),
            jax.ShapeDtypeStruct((bsz, seq, wc), F32),
            jax.ShapeDtypeStruct((N_SLABS, seq * bsz, LANES), F32),
        ],
        compiler_params=_cparams("parallel"),
        name="embed_inproj",
    )(x, ln_g.reshape(1, d), ln_b.reshape(1, d), w_in_bf)


def _s5_kernel(*refs, bsz, tc, dirs, emit_y, emit_state):
    nd = len(dirs)
    it = iter(refs)
    u_refs = [next(it) for _ in range(nd)]
    s0_ref, wb_ref, wc_ref, a_ref = next(it), next(it), next(it), next(it)
    y_refs = [next(it) for _ in range(nd)] if emit_y else []
    sf_ref = next(it) if emit_state else None
    bu_ref, st_ref = next(it), next(it)
    half = SLAB_STATE

    @pl.when(pl.program_id(0) == 0)
    def _():
        st_ref[...] = s0_ref[...]

    for k in range(nd):
        for j in range(N_SLABS):
            bu_ref[k, j] = jnp.dot(u_refs[k][j].astype(BF16), wb_ref[k, j],
                                   preferred_element_type=F32)

    per_tile = SUBLANES // bsz
    n_tiles = tc // per_tile

    def cmul_add(a, s, x):
        return (a[0] * s[0] - a[1] * s[1] + x[0], a[0] * s[1] + a[1] * s[0] + x[1])

    def load(k, j, ti):
        rows = pl.ds(ti * SUBLANES, SUBLANES)
        return bu_ref[k, j, rows, :half], bu_ref[k, j, rows, half:]

    def store(k, j, ti, v):
        rows = pl.ds(ti * SUBLANES, SUBLANES)
        bu_ref[k, j, rows, :half] = v[0]
        bu_ref[k, j, rows, half:] = v[1]

    if per_tile == 1:
        for j in range(N_SLABS):
            for k in range(nd):
                a = (a_ref[k, j, :, :half], a_ref[k, j, :, half:])
                s = (st_ref[k, j, :, :half], st_ref[k, j, :, half:])
                for i in range(n_tiles):
                    ti = (n_tiles - 1 - i) if dirs[k] else i
                    s = cmul_add(a, s, load(k, j, ti))
                    store(k, j, ti, s)
                st_ref[k, j, :, :half] = s[0]
                st_ref[k, j, :, half:] = s[1]
    else:
        assert per_tile == 2 and tuple(dirs) == (False, True)
        low = lax.broadcasted_iota(jnp.int32, (SUBLANES, half), 0) < bsz

        def pick(p, q):
            return (jnp.where(low, p[0], q[0]), jnp.where(low, p[1], q[1]))

        def swap(v):
            return (pltpu.roll(v[0], bsz, axis=0), pltpu.roll(v[1], bsz, axis=0))

        for j in range(N_SLABS):
            af = (a_ref[0, j, :, :half], a_ref[0, j, :, half:])
            ar = (a_ref[1, j, :, :half], a_ref[1, j, :, half:])
            a_fr, a_rf = pick(af, ar), pick(ar, af)
            s = pick((st_ref[0, j, :, :half], st_ref[0, j, :, half:]),
                     (st_ref[1, j, :, :half], st_ref[1, j, :, half:]))
            for i in range(n_tiles):
                tf, tr = i, n_tiles - 1 - i
                xf, xr = load(0, j, tf), load(1, j, tr)
                s1 = cmul_add(a_fr, s, pick(xf, xr))
                s2 = cmul_add(a_rf, swap(s1), pick(xr, xf))
                store(0, j, tf, pick(s1, s2))
                store(1, j, tr, pick(s2, s1))
                s = swap(s2)
            for k in range(nd):
                st_ref[k, j, :, :half] = s[0]
                st_ref[k, j, :, half:] = s[1]

    if emit_y:
        for k in range(nd):
            for j in range(N_SLABS):
                y_refs[k][j] = jnp.dot(bu_ref[k, j].astype(BF16), wc_ref[k, j],
                                       preferred_element_type=F32)
    if emit_state:
        sf_ref[...] = st_ref[...]


def _s5_scan(su, s0, wb, wc, a_b, *, bsz, tc, dirs, emit_y, emit_state):
    rows = su.shape[1]
    seq = rows // bsz
    nc = seq // tc
    r = tc * bsz
    nd = len(dirs)
    sw = 2 * SLAB_STATE

    def u_map(rev):
        return (lambda c: (0, nc - 1 - c, 0)) if rev else (lambda c: (0, c, 0))

    in_specs = [pl.BlockSpec((N_SLABS, r, LANES), u_map(rev)) for rev in dirs]
    in_specs += [
        pl.BlockSpec((nd, N_SLABS, SUBLANES, sw), lambda c: (0, 0, 0, 0)),
        pl.BlockSpec((nd, N_SLABS, LANES, sw), lambda c: (0, 0, 0, 0)),
        pl.BlockSpec((nd, N_SLABS, sw, LANES), lambda c: (0, 0, 0, 0)),
        pl.BlockSpec((nd, N_SLABS, SUBLANES, sw), lambda c: (0, 0, 0, 0)),
    ]
    out_specs, out_shape = [], []
    if emit_y:
        for rev in dirs:
            out_specs.append(pl.BlockSpec((N_SLABS, r, LANES), u_map(rev)))
            out_shape.append(jax.ShapeDtypeStruct((N_SLABS, rows, LANES), F32))
    if emit_state:
        out_specs.append(pl.BlockSpec((nd, N_SLABS, SUBLANES, sw), lambda c: (0, 0, 0, 0)))
        out_shape.append(jax.ShapeDtypeStruct((nd, N_SLABS, SUBLANES, sw), F32))
    kern = functools.partial(_s5_kernel, bsz=bsz, tc=tc, dirs=dirs, emit_y=emit_y,
                             emit_state=emit_state)
    return pl.pallas_call(
        kern,
        grid=(nc,),
        in_specs=in_specs,
        out_specs=out_specs,
        out_shape=out_shape,
        scratch_shapes=[
            pltpu.VMEM((nd, N_SLABS, r, sw), F32),
            pltpu.VMEM((nd, N_SLABS, SUBLANES, sw), F32),
        ],
        compiler_params=_cparams("arbitrary"),
        name="s5_scan",
    )(*([su] * nd), s0, wb, wc, a_b)


def _s5_params(lam_re, lam_im, log_step, b_re, b_im, c_re, c_im):
    g, p = lam_re.shape
    h = b_re.shape[-1]
    dt = jnp.exp(log_step.astype(F32))[:, None]
    mag = jnp.exp(lam_re * dt)
    ab_re = mag * jnp.cos(lam_im * dt)
    ab_im = mag * jnp.sin(lam_im * dt)
    den = lam_re * lam_re + lam_im * lam_im
    nr, ni = ab_re - 1.0, ab_im
    f_re = (nr * lam_re + ni * lam_im) / den
    f_im = (ni * lam_re - nr * lam_im) / den
    bb_re = f_re[..., None] * b_re - f_im[..., None] * b_im
    bb_im = f_re[..., None] * b_im + f_im[..., None] * b_re
    ns, gl = N_SLABS, GROUPS_PER_SLAB
    eye = jnp.eye(gl, dtype=F32)

    def in_block(bb):
        bb = bb.reshape(ns, gl, p, h)
        return jnp.einsum('sgph,gk->sghkp', bb, eye).reshape(ns, gl * h, gl * p)

    def out_block(cc):
        cc = cc.reshape(ns, gl, h, p)
        return jnp.einsum('sghp,gk->sgpkh', cc, eye).reshape(ns, gl * p, gl * h)

    wb = jnp.concatenate([in_block(bb_re), in_block(bb_im)], axis=-1)
    wc = jnp.concatenate([out_block(c_re.astype(F32)), -out_block(c_im.astype(F32))], axis=1)
    a = jnp.concatenate([ab_re.reshape(ns, gl * p), ab_im.reshape(ns, gl * p)], axis=-1)
    return wb.astype(BF16), wc.astype(BF16), a


def _mixer_tail_kernel(h0_ref, gb_ref, u_ref, up_ref, un_ref, um_ref, su_ref, yf_ref, yr_ref,
                       cw_ref, cb_ref, sd_ref, wg_ref, bg_ref, na_ref, nb_ref, wo_ref,
                       g1_ref, b1_ref, h1_ref, h1p_ref, *, bsz, tt):
    i = pl.program_id(0)
    nt = pl.num_programs(0)
    d = h0_ref.shape[-1]
    wcv = gb_ref.shape[-1]
    row_id = lax.broadcasted_iota(jnp.int32, (tt, wcv), 0)
    ya, ys = [], []
    for b in range(bsz):
        u = u_ref[b]
        prev_edge = jnp.where(i == 0, um_ref[...], up_ref[b, SUBLANES - 1:SUBLANES, :])
        next_edge = jnp.where(i == nt - 1, jnp.zeros((1, wcv), F32), un_ref[b, 0:1, :])
        u_prev = jnp.where(row_id == 0, prev_edge, pltpu.roll(u, 1, axis=0))
        u_next = jnp.where(row_id == tt - 1, next_edge, pltpu.roll(u, tt - 1, axis=0))
        conv = u_prev * cw_ref[0:1, :] + u * cw_ref[1:2, :] + u_next * cw_ref[2:3, :] + cb_ref[...]
        ya.append(gb_ref[b] * conv)

        def slab(ref, b=b):
            parts = []
            for j in range(N_SLABS):
                if bsz == 1:
                    parts.append(ref[j])
                else:
                    parts.append(ref[j, pl.ds(b, tt, stride=bsz), :])
            return jnp.concatenate(parts, axis=-1)

        ys.append(slab(yf_ref) + slab(yr_ref) + sd_ref[...] * slab(su_ref))
    y_a = jnp.concatenate(ya, axis=0)
    y_s = jnp.concatenate(ys, axis=0)
    z = jax.nn.gelu(y_s)
    glu = jnp.dot(z.astype(BF16), wg_ref[...], preferred_element_type=F32) + bg_ref[...]
    y_b = z * jax.nn.sigmoid(glu)
    merged = jnp.concatenate([_rms_norm(y_a, na_ref[...]), _rms_norm(y_b, nb_ref[...])], axis=-1)
    m = jnp.dot(merged.astype(BF16), wo_ref[...], preferred_element_type=F32)
    h0 = h0_ref[...].reshape(bsz * tt, d)
    h1 = _layer_norm(DEEPNORM_ALPHA * h0 + m, g1_ref[...], b1_ref[...])
    h1_ref[...] = h1.reshape(bsz, tt, d)
    h1p_ref[...] = _pack_halves(h1).reshape(bsz, tt, d // 2)


def _mixer_tail(h0, gb, u, u_meta_last, su, yf, yr, conv_w, conv_b, ssm_d, w_glu_bf, b_glu,
                norm_a_g, norm_b_g, w_out_bf, ln1_g, ln1_b, tt):
    bsz, seq, d = h0.shape
    wcv = gb.shape[-1]
    ws = N_SLABS * LANES
    nt = seq // tt
    tb = tt // SUBLANES
    nb8 = seq // SUBLANES
    kern = functools.partial(_mixer_tail_kernel, bsz=bsz, tt=tt)
    row = lambda n: pl.BlockSpec((1, n), lambda i: (0, 0))
    slab_spec = pl.BlockSpec((N_SLABS, tt * bsz, LANES), lambda i: (0, i, 0))
    return pl.pallas_call(
        kern,
        grid=(nt,),
        in_specs=[
            pl.BlockSpec((bsz, tt, d), lambda i: (0, i, 0)),
            pl.BlockSpec((bsz, tt, wcv), lambda i: (0, i, 0)),
            pl.BlockSpec((bsz, tt, wcv), lambda i: (0, i, 0)),
            pl.BlockSpec((bsz, SUBLANES, wcv), lambda i: (0, jnp.maximum(i * tb - 1, 0), 0)),
            pl.BlockSpec((bsz, SUBLANES, wcv), lambda i: (0, jnp.minimum((i + 1) * tb, nb8 - 1), 0)),
            row(wcv),
            slab_spec, slab_spec, slab_spec,
            pl.BlockSpec((CONV_WIDTH, wcv), lambda i: (0, 0)),
            row(wcv), row(ws),
            pl.BlockSpec((ws, ws), lambda i: (0, 0)),
            row(ws), row(wcv), row(ws),
            pl.BlockSpec((wcv + ws, d), lambda i: (0, 0)),
            row(d), row(d),
        ],
        out_specs=[
            pl.BlockSpec((bsz, tt, d), lambda i: (0, i, 0)),
            pl.BlockSpec((bsz, tt, d // 2), lambda i: (0, i, 0)),
        ],
        out_shape=[
            jax.ShapeDtypeStruct((bsz, seq, d), F32),
            jax.ShapeDtypeStruct((bsz, seq, d // 2), jnp.uint32),
        ],
        compiler_params=_cparams("parallel"),
        name="mixer_tail",
    )(h0, gb, u, u, u, u_meta_last.reshape(1, wcv), su, yf, yr, conv_w, conv_b.reshape(1, wcv),
      ssm_d.reshape(1, ws), w_glu_bf, b_glu.reshape(1, ws), norm_a_g.reshape(1, wcv),
      norm_b_g.reshape(1, ws), w_out_bf, ln1_g.reshape(1, d), ln1_b.reshape(1, d))


def _dual_row_specs(rows, width, nq_p):
    return [pl.BlockSpec((rows, width), lambda q, *_: (jnp.minimum(q, nq_p - 1), 0)),
            pl.BlockSpec((rows, width), lambda q, *_: (jnp.maximum(q - nq_p, 0), 0))]


def _router_kernel(hp_ref, hs_ref, wh_ref, bias_ref, eidx_ref, rank_ref, gate_ref, cnt_ref,
                   cnt_scr, *, tt, nq_p):
    ne = wh_ref.shape[0]
    epg = ne // N_EXPERT_GROUPS
    neg = jnp.float32(-jnp.inf)

    @pl.when(pl.program_id(0) == 0)
    def _():
        cnt_scr[...] = jnp.zeros_like(cnt_scr)

    h = jnp.where(pl.program_id(0) < nq_p, hp_ref[...], hs_ref[...])
    dn = (((1,), (1,)), ((), ()))
    logits = lax.dot_general(wh_ref[...], h.astype(BF16), dn,
                             preferred_element_type=F32)
    scores = jax.nn.sigmoid(logits)
    sel = scores + bias_ref[...]

    gi = lax.broadcasted_iota(jnp.int32, (epg, tt), 0)
    gs = []
    for g in range(N_EXPERT_GROUPS):
        x = sel[g * epg:(g + 1) * epg, :]
        m1 = jnp.max(x, axis=0, keepdims=True)
        i1 = jnp.min(jnp.where(x == m1, gi, epg), axis=0, keepdims=True)
        m2 = jnp.max(jnp.where(gi == i1, neg, x), axis=0, keepdims=True)
        gs.append(m1 + m2)
    chosen = [jnp.zeros((1, tt), F32) for _ in range(N_EXPERT_GROUPS)]
    for _ in range(TOPK_GROUPS):
        m = gs[0]
        for g in range(1, N_EXPERT_GROUPS):
            m = jnp.maximum(m, gs[g])
        found = jnp.zeros((1, tt), F32)
        for g in range(N_EXPERT_GROUPS):
            hit = jnp.where((gs[g] == m) & (found == 0.0), 1.0, 0.0)
            chosen[g] = chosen[g] + hit
            found = found + hit
            gs[g] = jnp.where(hit > 0.0, neg, gs[g])
    selm = jnp.concatenate(
        [jnp.where(chosen[g] > 0.0, sel[g * epg:(g + 1) * epg, :], neg)
         for g in range(N_EXPERT_GROUPS)], axis=0)

    ei = lax.broadcasted_iota(jnp.int32, (ne, tt), 0)
    msel = jnp.zeros((ne, tt), F32)
    idxs, gvals = [], []
    for _ in range(TOP_K):
        m = jnp.max(selm, axis=0, keepdims=True)
        idx = jnp.min(jnp.where(selm == m, ei, ne), axis=0, keepdims=True)
        hit = ei == idx
        gvals.append(jnp.sum(jnp.where(hit, scores, 0.0), axis=0, keepdims=True))
        selm = jnp.where(hit, neg, selm)
        msel = jnp.where(hit, 1.0, msel)
        idxs.append(idx)
    gsum = gvals[0]
    for k in range(1, TOP_K):
        gsum = gsum + gvals[k]
    gate_ref[...] = jnp.concatenate([gv / gsum * ROUTED_SCALE for gv in gvals], axis=0)
    eidx_ref[...] = jnp.concatenate(idxs, axis=0)

    r_i = lax.broadcasted_iota(jnp.int32, (tt, tt), 0)
    c_i = lax.broadcasted_iota(jnp.int32, (tt, tt), 1)
    upper = jnp.where(r_i < c_i, 1.0, 0.0).astype(BF16)
    rank_full = jnp.dot(msel.astype(BF16), upper, preferred_element_type=F32) + cnt_scr[...]
    ranks = [jnp.sum(jnp.where(ei == idxs[k], rank_full, 0.0), axis=0, keepdims=True)
             for k in range(TOP_K)]
    rank_ref[...] = jnp.concatenate(ranks, axis=0).astype(jnp.int32)
    cnt_scr[...] = cnt_scr[...] + jnp.sum(msel, axis=1, keepdims=True)
    cnt_ref[...] = cnt_scr[...]


def _router(h_p, h_s, w_router, router_bias, tt):
    d = h_p.shape[1]
    n = h_p.shape[0] + h_s.shape[0]
    nq_p = h_p.shape[0] // tt
    ne = w_router.shape[1]
    wh = w_router.T.astype(BF16)
    kern = functools.partial(_router_kernel, tt=tt, nq_p=nq_p)
    return pl.pallas_call(
        kern,
        grid=(n // tt,),
        in_specs=_dual_row_specs(tt, d, nq_p) + [
            pl.BlockSpec((ne, d), lambda i: (0, 0)),
            pl.BlockSpec((ne, 1), lambda i: (0, 0)),
        ],
        out_specs=[
            pl.BlockSpec((TOP_K, tt), lambda i: (0, i)),
            pl.BlockSpec((TOP_K, tt), lambda i: (0, i)),
            pl.BlockSpec((TOP_K, tt), lambda i: (0, i)),
            pl.BlockSpec((ne, 1), lambda i: (0, 0)),
        ],
        out_shape=[
            jax.ShapeDtypeStruct((TOP_K, n), jnp.int32),
            jax.ShapeDtypeStruct((TOP_K, n), jnp.int32),
            jax.ShapeDtypeStruct((TOP_K, n), F32),
            jax.ShapeDtypeStruct((ne, 1), F32),
        ],
        scratch_shapes=[pltpu.VMEM((ne, 1), F32)],
        compiler_params=_cparams("arbitrary"),
        name="router",
    )(h_p, h_s, wh, router_bias.astype(F32).reshape(ne, 1))


def _positions_kernel(eidx_ref, rank_ref, start_ref, pos_ref):
    ne = start_ref.shape[0]
    tt = eidx_ref.shape[1]
    ei = lax.broadcasted_iota(jnp.int32, (ne, tt), 0)
    start = start_ref[...]
    rows = [jnp.sum(jnp.where(ei == eidx_ref[k:k + 1, :], start, 0.0), axis=0, keepdims=True)
            for k in range(TOP_K)]
    pos_ref[...] = jnp.concatenate(rows, axis=0).astype(jnp.int32) + rank_ref[...]


def _positions(eidx, rank, start, tt):
    n = eidx.shape[1]
    ne = start.shape[0]
    return pl.pallas_call(
        _positions_kernel,
        grid=(n // tt,),
        in_specs=[
            pl.BlockSpec((TOP_K, tt), lambda i: (0, i)),
            pl.BlockSpec((TOP_K, tt), lambda i: (0, i)),
            pl.BlockSpec((ne, 1), lambda i: (0, 0)),
        ],
        out_specs=pl.BlockSpec((TOP_K, tt), lambda i: (0, i)),
        out_shape=jax.ShapeDtypeStruct((TOP_K, n), jnp.int32),
        compiler_params=_cparams("parallel"),
        name="positions",
    )(eidx, rank, start.astype(F32).reshape(ne, 1))


def _sc_workers():
    info = plsc.get_sparse_core_info()
    return info.num_cores, info.num_subcores


def _sc_dispatch(hp_p, hp_s, pos_c, chunk):
    w = hp_p.shape[1]
    n = hp_p.shape[0] + hp_s.shape[0]
    nch_p = hp_p.shape[0] // chunk
    nc, ns = _sc_workers()
    per_worker = (n // chunk) // (nc * ns)
    mesh = plsc.VectorSubcoreMesh(core_axis_name="c", subcore_axis_name="s")

    @functools.partial(
        pl.kernel, mesh=mesh,
        out_type=jax.ShapeDtypeStruct((n * TOP_K, w), hp_p.dtype),
        scratch_types=[pltpu.VMEM((TOP_K, chunk), jnp.int32),
                       pltpu.VMEM((chunk, w), hp_p.dtype),
                       pltpu.SemaphoreType.DMA],
    )
    def dispatch(hp_hbm, hs_hbm, pos_hbm, xs_hbm, idx_v, rows_v, sem):
        wid = lax.axis_index("s") * nc + lax.axis_index("c")

        @pl.loop(0, per_worker)
        def _(ci):
            c = wid * per_worker + ci

            @pl.when(c < nch_p)
            def _():
                pltpu.sync_copy(hp_hbm.at[pl.ds(pl.multiple_of(c * chunk, chunk), chunk)], rows_v)

            @pl.when(c >= nch_p)
            def _():
                pltpu.sync_copy(
                    hs_hbm.at[pl.ds(pl.multiple_of((c - nch_p) * chunk, chunk), chunk)], rows_v)

            pltpu.sync_copy(pos_hbm.at[c], idx_v)
            copies = [pltpu.async_copy(rows_v, xs_hbm.at[idx_v.at[k]], sem) for k in range(TOP_K)]
            for cp in copies:
                cp.wait()

    return dispatch(hp_p, hp_s, pos_c)


def _sc_gather(ys, pos_c, chunk):
    w = ys.shape[1]
    n = pos_c.shape[0] * chunk
    nc, ns = _sc_workers()
    per_worker = pos_c.shape[0] // (nc * ns)
    mesh = plsc.VectorSubcoreMesh(core_axis_name="c", subcore_axis_name="s")
    nbuf = 3

    @functools.partial(
        pl.kernel, mesh=mesh,
        out_type=jax.ShapeDtypeStruct((TOP_K, n, w), ys.dtype),
        scratch_types=[pltpu.VMEM((TOP_K, chunk), jnp.int32),
                       pltpu.VMEM((nbuf, chunk, w), ys.dtype),
                       pltpu.SemaphoreType.DMA((nbuf,)),
                       pltpu.SemaphoreType.DMA((nbuf,))],
    )
    def gather(ys_hbm, pos_hbm, out_hbm, idx_v, rows_v, gsem, wsem):
        wid = lax.axis_index("s") * nc + lax.axis_index("c")

        @pl.loop(0, per_worker)
        def _(ci):
            c = wid * per_worker + ci
            off = pl.multiple_of(c * chunk, chunk)
            pltpu.sync_copy(pos_hbm.at[c], idx_v)

            def start_gather(k):
                b = k % nbuf
                return pltpu.async_copy(ys_hbm.at[idx_v.at[k]], rows_v.at[b], gsem.at[b])

            gathers = {0: start_gather(0)}
            writes = {}
            for k in range(TOP_K):
                if k + 1 < TOP_K:
                    if k + 1 - nbuf >= 0:
                        writes.pop(k + 1 - nbuf).wait()
                    gathers[k + 1] = start_gather(k + 1)
                gathers.pop(k).wait()
                b = k % nbuf
                writes[k] = pltpu.async_copy(rows_v.at[b], out_hbm.at[k, pl.ds(off, chunk)],
                                             wsem.at[b])
            for k in sorted(writes):
                writes[k].wait()

    return gather(ys, pos_c)


def _experts_kernel(tile_ref, exp_ref, ns_ref, off_ref, xs_ref, wg_ref, wu_ref, wd_ref, ys_ref,
                    wgb, wub, wdb, acc, *, tm):
    s = pl.program_id(0)
    prev = jnp.maximum(s - 1, 0)
    e = exp_ref[s]
    t = tile_ref[s]
    half = xs_ref.shape[1]
    lo_row, hi_row = off_ref[e], off_ref[e + 1]
    base = t * tm

    @pl.when((s == 0) | (e != exp_ref[prev]))
    def _():
        wgb[...] = wg_ref[0].astype(BF16)
        wub[...] = wu_ref[0].astype(BF16)
        wdb[...] = wd_ref[0].astype(BF16)

    def ffn(rows, r0, masked):
        lo, hi = _unpack_halves(xs_ref[rows, :])
        g = (jnp.dot(lo, wgb[:half], preferred_element_type=F32)
             + jnp.dot(hi, wgb[half:], preferred_element_type=F32))
        u = (jnp.dot(lo, wub[:half], preferred_element_type=F32)
             + jnp.dot(hi, wub[half:], preferred_element_type=F32))
        act = (g * jax.nn.sigmoid(g)) * u
        if masked:
            row = r0 + lax.broadcasted_iota(jnp.int32, (EXPERT_SUB, 1), 0)
            act = jnp.where((row >= lo_row) & (row < hi_row), act, 0.0)
        return jnp.dot(act.astype(BF16), wdb[...], preferred_element_type=F32)

    live = s < ns_ref[0]
    for b in range(tm // EXPERT_SUB):
        rows = pl.ds(b * EXPERT_SUB, EXPERT_SUB)
        r0 = base + b * EXPERT_SUB
        touched = live & (lo_row < r0 + EXPERT_SUB) & (hi_row > r0)
        whole = (lo_row <= r0) & (hi_row >= r0 + EXPERT_SUB)

        @pl.when(touched & whole)
        def _(rows=rows, r0=r0):
            ys_ref[rows, :] = _pack_halves(ffn(rows, r0, False))

        @pl.when(touched & jnp.logical_not(whole))
        def _(rows=rows, r0=r0):
            y = ffn(rows, r0, True)
            opens = lo_row <= r0

            @pl.when(opens)
            def _():
                acc[rows, :] = y

            @pl.when(jnp.logical_not(opens))
            def _():
                acc[rows, :] = acc[rows, :] + y

            ys_ref[rows, :] = _pack_halves(acc[rows, :])


def _experts(xs, tile_of, exp_of, n_steps, offsets, w_gate, w_up, w_down, tm):
    n_rows, half = xs.shape
    ne, d, de = w_gate.shape
    s_max = tile_of.shape[0]

    def row_map(s, tile, ex, ns, off):
        return (tile[s], 0)

    def w_map(s, tile, ex, ns, off):
        return (ex[s], 0, 0)

    grid_spec = pltpu.PrefetchScalarGridSpec(
        num_scalar_prefetch=4,
        grid=(s_max,),
        in_specs=[
            pl.BlockSpec((tm, half), row_map),
            pl.BlockSpec((1, d, de), w_map),
            pl.BlockSpec((1, d, de), w_map),
            pl.BlockSpec((1, de, d), w_map),
        ],
        out_specs=pl.BlockSpec((tm, half), row_map),
        scratch_shapes=[
            pltpu.VMEM((d, de), BF16),
            pltpu.VMEM((d, de), BF16),
            pltpu.VMEM((de, d), BF16),
            pltpu.VMEM((tm, d), F32),
        ],
    )
    return pl.pallas_call(
        functools.partial(_experts_kernel, tm=tm),
        grid_spec=grid_spec,
        out_shape=jax.ShapeDtypeStruct((n_rows, half), jnp.uint32),
        compiler_params=_cparams("arbitrary"),
        name="experts",
    )(tile_of, exp_of, n_steps, offsets, xs, w_gate, w_up, w_down)


def _expert_schedule(counts, n_rows, tm):
    ne = counts.shape[0]
    s_max = n_rows // tm + ne
    off = jnp.concatenate([jnp.zeros((1,), jnp.int32), jnp.cumsum(counts)]).astype(jnp.int32)
    first_tile = off[:-1] // tm
    last_tile = (off[1:] - 1) // tm
    visits = jnp.where(counts > 0, last_tile - first_tile + 1, 0)
    cum = jnp.cumsum(visits)
    n_steps = cum[-1]
    step = jnp.minimum(jnp.arange(s_max, dtype=jnp.int32), n_steps - 1)
    exp_of = jnp.sum((cum[None, :] <= step[:, None]).astype(jnp.int32), axis=1)
    onehot = exp_of[:, None] == jnp.arange(ne, dtype=jnp.int32)[None, :]
    pick = lambda v: jnp.sum(jnp.where(onehot, v[None, :], 0), axis=1)
    tile_of = pick(first_tile) + step - pick(cum - visits)
    return (tile_of.astype(jnp.int32), exp_of.astype(jnp.int32),
            n_steps.reshape(1).astype(jnp.int32), off)


def _combine_kernel(hp_ref, hs_ref, gate_ref, yk_ref, wsg_ref, wsu_ref, wsd_ref, g2_ref,
                    b2_ref, outp_ref, outs_ref, *, tt, nq_p):
    is_p = pl.program_id(0) < nq_p
    h = jnp.where(is_p, hp_ref[...], hs_ref[...])
    hb = h.astype(BF16)
    g = jnp.dot(hb, wsg_ref[...], preferred_element_type=F32)
    u = jnp.dot(hb, wsu_ref[...], preferred_element_type=F32)
    act = (g * jax.nn.sigmoid(g)) * u
    f = jnp.dot(act.astype(BF16), wsd_ref[...], preferred_element_type=F32)
    gate = gate_ref[...]
    r_lo = jnp.zeros((tt, yk_ref.shape[-1]), F32)
    r_hi = jnp.zeros((tt, yk_ref.shape[-1]), F32)
    for k in range(TOP_K):
        p = yk_ref[k]
        gk = gate[:, k:k + 1]
        r_lo = r_lo + gk * pltpu.bitcast(p << 16, F32)
        r_hi = r_hi + gk * pltpu.bitcast(p & jnp.uint32(0xFFFF0000), F32)
    f = f + jnp.concatenate([r_lo, r_hi], axis=-1)
    out = _layer_norm(DEEPNORM_ALPHA * h + f, g2_ref[...], b2_ref[...])

    @pl.when(is_p)
    def _():
        outp_ref[...] = out

    @pl.when(jnp.logical_not(is_p))
    def _():
        outs_ref[...] = out


def _combine(h_p, h_s, gate_t, yk, wsg_bf, wsu_bf, wsd_bf, ln2_g, ln2_b, tt):
    d = h_p.shape[1]
    n_p, n_s = h_p.shape[0], h_s.shape[0]
    nq_p = n_p // tt
    ds_ = wsg_bf.shape[1]
    kern = functools.partial(_combine_kernel, tt=tt, nq_p=nq_p)
    return pl.pallas_call(
        kern,
        grid=((n_p + n_s) // tt,),
        in_specs=_dual_row_specs(tt, d, nq_p) + [
            pl.BlockSpec((tt, TOP_K), lambda i: (i, 0)),
            pl.BlockSpec((TOP_K, tt, yk.shape[2]), lambda i: (0, i, 0)),
            pl.BlockSpec((d, ds_), lambda i: (0, 0)),
            pl.BlockSpec((d, ds_), lambda i: (0, 0)),
            pl.BlockSpec((ds_, d), lambda i: (0, 0)),
            pl.BlockSpec((1, d), lambda i: (0, 0)),
            pl.BlockSpec((1, d), lambda i: (0, 0)),
        ],
        out_specs=_dual_row_specs(tt, d, nq_p),
        out_shape=[jax.ShapeDtypeStruct((n_p, d), F32), jax.ShapeDtypeStruct((n_s, d), F32)],
        compiler_params=_cparams("arbitrary"),
        name="combine",
    )(h_p, h_s, gate_t, yk, wsg_bf, wsu_bf, wsd_bf, ln2_g.reshape(1, d), ln2_b.reshape(1, d))


def _pick_tile(seq, bsz, rows):
    return max(SUBLANES, min(seq, rows // bsz))


def _mixer(x, meta_state, u_meta_last, p):
    bsz, seq, _ = x.shape
    tt = _pick_tile(seq, bsz, 1024)
    h0, gb, u, su = _embed_inproj(x, p["ln_emb_g"], p["ln_emb_b"], p["w_in_bf"], tt)
    tc = _pick_tile(seq, bsz, 256)
    s0 = jnp.stack([jnp.broadcast_to(meta_state, (N_SLABS, SUBLANES, 2 * SLAB_STATE)),
                    jnp.zeros((N_SLABS, SUBLANES, 2 * SLAB_STATE), F32)])
    a_b = jnp.broadcast_to(p["s5_a"][:, :, None, :], (2, N_SLABS, SUBLANES, 2 * SLAB_STATE))
    yf, yr = _s5_scan(su, s0, p["s5_wb"], p["s5_wc"], a_b, bsz=bsz, tc=tc, dirs=(False, True),
                      emit_y=True, emit_state=False)
    tt3 = _pick_tile(seq, bsz, 512)
    h1, h1p = _mixer_tail(h0, gb, u, u_meta_last, su, yf, yr, p["conv_w"], p["conv_b"],
                          p["ssm_d"], p["w_glu_bf"], p["b_glu"], p["norm_a_g"], p["norm_b_g"],
                          p["w_out_bf"], p["ln1_g"], p["ln1_b"], tt3)
    d = h1.shape[-1]
    return h1.reshape(bsz * seq, d), h1p.reshape(bsz * seq, d // 2)


def kernel(x_prompt, x_sample, meta_tokens, ln_emb_g, ln_emb_b, w_in, conv_w, conv_b, ssm_lambda_re, ssm_lambda_im, ssm_log_step, ssm_b_re, ssm_b_im, ssm_c_re, ssm_c_im, ssm_d, w_glu, b_glu, norm_a_g, norm_b_g, w_out, ln1_g, ln1_b, w_router, router_bias, w_exp_gate, w_exp_up, w_exp_down, w_sh_gate, w_sh_up, w_sh_down, ln2_g, ln2_b):
    l = 0
    d = x_prompt.shape[-1]
    dirs = [_s5_params(ssm_lambda_re[l, k].astype(F32), ssm_lambda_im[l, k].astype(F32),
                       ssm_log_step[l, k], ssm_b_re[l, k].astype(F32), ssm_b_im[l, k].astype(F32),
                       ssm_c_re[l, k], ssm_c_im[l, k]) for k in range(2)]
    p = dict(
        ln_emb_g=ln_emb_g, ln_emb_b=ln_emb_b, w_in_bf=w_in[l].astype(BF16),
        conv_w=conv_w[l], conv_b=conv_b[l], ssm_d=ssm_d[l],
        w_glu_bf=w_glu[l].astype(BF16), b_glu=b_glu[l], norm_a_g=norm_a_g[l],
        norm_b_g=norm_b_g[l], w_out_bf=w_out[l].astype(BF16), ln1_g=ln1_g[l], ln1_b=ln1_b[l],
        s5_wb=jnp.stack([dirs[0][0], dirs[1][0]]), s5_wc=jnp.stack([dirs[0][1], dirs[1][1]]),
        s5_a=jnp.stack([dirs[0][2], dirs[1][2]]),
    )
    mb = SUBLANES
    xm = jnp.broadcast_to(meta_tokens.astype(F32)[None], (mb, N_META, d))
    _, _, u_m, su_m = _embed_inproj(xm, ln_emb_g, ln_emb_b, p["w_in_bf"], N_META)
    a_m = jnp.broadcast_to(p["s5_a"][:1, :, None, :], (1, N_SLABS, mb, 2 * SLAB_STATE))
    (st_m,) = _s5_scan(su_m, jnp.zeros((1, N_SLABS, mb, 2 * SLAB_STATE), F32), p["s5_wb"][:1],
                       p["s5_wc"][:1], a_m, bsz=mb, tc=N_META, dirs=(False,), emit_y=False,
                       emit_state=True)
    meta_state = st_m[0, :, :1, :]
    u_meta_last = u_m[0, N_META - 1]

    h1_p, h1p_p = _mixer(x_prompt, meta_state, u_meta_last, p)
    h1_s, h1p_s = _mixer(x_sample, meta_state, u_meta_last, p)
    n = h1_p.shape[0] + h1_s.shape[0]

    tr = min(h1_p.shape[0], h1_s.shape[0], ROUTER_TILE)
    eidx, rank, gate, cnt = _router(h1_p, h1_s, w_router[l], router_bias[l], tr)
    counts = cnt[:, 0].astype(jnp.int32)
    tile_of, exp_of, n_steps, offsets = _expert_schedule(counts, n * TOP_K, EXPERT_TILE)
    pos = _positions(eidx, rank, offsets[:-1], tr)
    pos_c = pos.reshape(TOP_K, n // SC_CHUNK, SC_CHUNK).transpose(1, 0, 2)
    xs = _sc_dispatch(h1p_p, h1p_s, pos_c, SC_CHUNK)
    ys = _experts(xs, tile_of, exp_of, n_steps, offsets, w_exp_gate[l], w_exp_up[l],
                  w_exp_down[l], EXPERT_TILE)
    yk = _sc_gather(ys, pos_c, SC_CHUNK)
    td = min(h1_p.shape[0], h1_s.shape[0], COMBINE_TILE)
    out_p, out_s = _combine(h1_p, h1_s, gate.T, yk, w_sh_gate[l].astype(BF16),
                            w_sh_up[l].astype(BF16), w_sh_down[l].astype(BF16), ln2_g[l],
                            ln2_b[l], td)
    return (out_p.reshape(x_prompt.shape), out_s.reshape(x_sample.shape))
```

```python
import functools
import math

import jax
import jax.numpy as jnp
from jax import lax
from jax.experimental import pallas as pl
from jax.experimental.pallas import tpu as pltpu
from jax.experimental.pallas import tpu_sc as plsc

F32 = jnp.float32
BF16 = jnp.bfloat16

N_META = 16
CONV_WIDTH = 3
SSM_GROUP = 16
SSM_STATE = 64
N_EXPERTS = 256
TOP_K = 8
N_EXPERT_GROUPS = 8
TOPK_GROUPS = 4
ROUTED_SCALE = 2.5
DEPTH = 1
DEEPNORM_ALPHA = (2.0 * DEPTH) ** 0.25
LN_EPS = 1e-5
RMS_EPS = 1e-6

LANES = 128
SUBLANES = 8
N_SLABS = 4
GROUPS_PER_SLAB = LANES // SSM_GROUP
SLAB_STATE = GROUPS_PER_SLAB * SSM_STATE
EXPERT_TILE = 2048
EXPERT_SUB = 512
ROUTER_TILE = 256
COMBINE_TILE = 256
SC_CHUNK = 64
VMEM_LIMIT = 48 * 1024 * 1024


def _cparams(*sem):
    return pltpu.CompilerParams(dimension_semantics=sem, vmem_limit_bytes=VMEM_LIMIT)


def _layer_norm(x, g, b):
    mu = jnp.mean(x, axis=-1, keepdims=True)
    xc = x - mu
    var = jnp.mean(xc * xc, axis=-1, keepdims=True)
    return xc * lax.rsqrt(var + LN_EPS) * g + b


def _rms_norm(x, g):
    return x * lax.rsqrt(jnp.mean(x * x, axis=-1, keepdims=True) + RMS_EPS) * g


def _pack_halves(x):
    half = x.shape[-1] // 2
    bits = pltpu.bitcast(x.astype(BF16).astype(F32), jnp.uint32)
    return (bits[:, :half] >> 16) | (bits[:, half:] & jnp.uint32(0xFFFF0000))


def _unpack_halves(p):
    lo = pltpu.bitcast(p << 16, F32).astype(BF16)
    hi = pltpu.bitcast(p & jnp.uint32(0xFFFF0000), F32).astype(BF16)
    return lo, hi


def _embed_inproj_kernel(x_ref, g_ref, b_ref, w_ref, h0_ref, gb_ref, u_ref, su_ref, *, bsz, tt):
    d = x_ref.shape[-1]
    x = x_ref[...].reshape(bsz * tt, d)
    h0 = _layer_norm(x, g_ref[...], b_ref[...])
    h0_ref[...] = h0.reshape(bsz, tt, d)
    proj = jnp.dot(h0.astype(BF16), w_ref[...], preferred_element_type=F32)
    wc = gb_ref.shape[-1]
    gb_ref[...] = proj[:, :wc].reshape(bsz, tt, wc)
    u_ref[...] = (proj[:, wc:2 * wc] * proj[:, 2 * wc:3 * wc]).reshape(bsz, tt, wc)
    s_u = proj[:, 3 * wc:]
    for b in range(bsz):
        for j in range(N_SLABS):
            val = s_u[b * tt:(b + 1) * tt, j * LANES:(j + 1) * LANES]
            if bsz == 1:
                su_ref[j] = val
            else:
                su_ref[j, pl.ds(b, tt, stride=bsz), :] = val


def _embed_inproj(x, ln_g, ln_b, w_in_bf, tt):
    bsz, seq, d = x.shape
    e = w_in_bf.shape[1]
    wc = (e - N_SLABS * LANES) // 3
    nt = seq // tt
    kern = functools.partial(_embed_inproj_kernel, bsz=bsz, tt=tt)
    return pl.pallas_call(
        kern,
        grid=(nt,),
        in_specs=[
            pl.BlockSpec((bsz, tt, d), lambda i: (0, i, 0)),
            pl.BlockSpec((1, d), lambda i: (0, 0)),
            pl.BlockSpec((1, d), lambda i: (0, 0)),
            pl.BlockSpec((d, e), lambda i: (0, 0)),
        ],
        out_specs=[
            pl.BlockSpec((bsz, tt, d), lambda i: (0, i, 0)),
            pl.BlockSpec((bsz, tt, wc), lambda i: (0, i, 0)),
            pl.BlockSpec((bsz, tt, wc), lambda i: (0, i, 0)),
            pl.BlockSpec((N_SLABS, tt * bsz, LANES), lambda i: (0, i, 0)),
        ],
        out_shape=[
            jax.ShapeDtypeStruct((bsz, seq, d), F32),
            jax.ShapeDtypeStruct((bsz, seq, wc), F32),
            jax.ShapeDtypeStruct((bsz, seq, wc), F32),
            jax.ShapeDtypeStruct((N_SLABS, seq * bsz, LANES), F32),
        ],
        compiler_params=_cparams("parallel"),
        name="embed_inproj",
    )(x, ln_g.reshape(1, d), ln_b.reshape(1, d), w_in_bf)


def _s5_kernel(*refs, bsz, tc, dirs, emit_y, emit_state):
    nd = len(dirs)
    it = iter(refs)
    u_refs = [next(it) for _ in range(nd)]
    s0_ref, wb_ref, wc_ref, a_ref = next(it), next(it), next(it), next(it)
    y_refs = [next(it) for _ in range(nd)] if emit_y else []
    sf_ref = next(it) if emit_state else None
    bu_ref, st_ref = next(it), next(it)
    half = SLAB_STATE

    @pl.when(pl.program_id(0) == 0)
    def _():
        st_ref[...] = s0_ref[...]

    def project_in(j):
        for k in range(nd):
            bu_ref[k, j] = jnp.dot(u_refs[k][j].astype(BF16), wb_ref[k, j],
                                   preferred_element_type=F32)

    def project_out(j):
        for k in range(nd):
            y_refs[k][j] = jnp.dot(bu_ref[k, j].astype(BF16), wc_ref[k, j],
                                   preferred_element_type=F32)

    per_tile = SUBLANES // bsz
    n_tiles = tc // per_tile

    def cmul_add(a, s, x):
        return (a[0] * s[0] - a[1] * s[1] + x[0], a[0] * s[1] + a[1] * s[0] + x[1])

    def load(k, j, ti):
        rows = pl.ds(ti * SUBLANES, SUBLANES)
        return bu_ref[k, j, rows, :half], bu_ref[k, j, rows, half:]

    def store(k, j, ti, v):
        rows = pl.ds(ti * SUBLANES, SUBLANES)
        bu_ref[k, j, rows, :half] = v[0]
        bu_ref[k, j, rows, half:] = v[1]

    if per_tile == 1:
        def recur(j):
            for k in range(nd):
                a = (a_ref[k, j, :, :half], a_ref[k, j, :, half:])
                s = (st_ref[k, j, :, :half], st_ref[k, j, :, half:])
                for i in range(n_tiles):
                    ti = (n_tiles - 1 - i) if dirs[k] else i
                    s = cmul_add(a, s, load(k, j, ti))
                    store(k, j, ti, s)
                st_ref[k, j, :, :half] = s[0]
                st_ref[k, j, :, half:] = s[1]
    else:
        assert per_tile == 2 and tuple(dirs) == (False, True)
        low = lax.broadcasted_iota(jnp.int32, (SUBLANES, half), 0) < bsz

        def pick(p, q):
            return (jnp.where(low, p[0], q[0]), jnp.where(low, p[1], q[1]))

        def swap(v):
            return (pltpu.roll(v[0], bsz, axis=0), pltpu.roll(v[1], bsz, axis=0))

        def recur(j):
            af = (a_ref[0, j, :, :half], a_ref[0, j, :, half:])
            ar = (a_ref[1, j, :, :half], a_ref[1, j, :, half:])
            a_fr, a_rf = pick(af, ar), pick(ar, af)
            s = pick((st_ref[0, j, :, :half], st_ref[0, j, :, half:]),
                     (st_ref[1, j, :, :half], st_ref[1, j, :, half:]))
            for i in range(n_tiles):
                tf, tr = i, n_tiles - 1 - i
                xf, xr = load(0, j, tf), load(1, j, tr)
                s1 = cmul_add(a_fr, s, pick(xf, xr))
                s2 = cmul_add(a_rf, swap(s1), pick(xr, xf))
                store(0, j, tf, pick(s1, s2))
                store(1, j, tr, pick(s2, s1))
                s = swap(s2)
            for k in range(nd):
                st_ref[k, j, :, :half] = s[0]
                st_ref[k, j, :, half:] = s[1]

    project_in(0)
    for j in range(N_SLABS):
        if j + 1 < N_SLABS:
            project_in(j + 1)
        recur(j)
        if emit_y:
            project_out(j)
    if emit_state:
        sf_ref[...] = st_ref[...]


def _s5_scan(su, s0, wb, wc, a_b, *, bsz, tc, dirs, emit_y, emit_state):
    rows = su.shape[1]
    seq = rows // bsz
    nc = seq // tc
    r = tc * bsz
    nd = len(dirs)
    sw = 2 * SLAB_STATE

    def u_map(rev):
        return (lambda c: (0, nc - 1 - c, 0)) if rev else (lambda c: (0, c, 0))

    in_specs = [pl.BlockSpec((N_SLABS, r, LANES), u_map(rev)) for rev in dirs]
    in_specs += [
        pl.BlockSpec((nd, N_SLABS, SUBLANES, sw), lambda c: (0, 0, 0, 0)),
        pl.BlockSpec((nd, N_SLABS, LANES, sw), lambda c: (0, 0, 0, 0)),
        pl.BlockSpec((nd, N_SLABS, sw, LANES), lambda c: (0, 0, 0, 0)),
        pl.BlockSpec((nd, N_SLABS, SUBLANES, sw), lambda c: (0, 0, 0, 0)),
    ]
    out_specs, out_shape = [], []
    if emit_y:
        for rev in dirs:
            out_specs.append(pl.BlockSpec((N_SLABS, r, LANES), u_map(rev)))
            out_shape.append(jax.ShapeDtypeStruct((N_SLABS, rows, LANES), F32))
    if emit_state:
        out_specs.append(pl.BlockSpec((nd, N_SLABS, SUBLANES, sw), lambda c: (0, 0, 0, 0)))
        out_shape.append(jax.ShapeDtypeStruct((nd, N_SLABS, SUBLANES, sw), F32))
    kern = functools.partial(_s5_kernel, bsz=bsz, tc=tc, dirs=dirs, emit_y=emit_y,
                             emit_state=emit_state)
    return pl.pallas_call(
        kern,
        grid=(nc,),
        in_specs=in_specs,
        out_specs=out_specs,
        out_shape=out_shape,
        scratch_shapes=[
            pltpu.VMEM((nd, N_SLABS, r, sw), F32),
            pltpu.VMEM((nd, N_SLABS, SUBLANES, sw), F32),
        ],
        compiler_params=_cparams("arbitrary"),
        name="s5_scan",
    )(*([su] * nd), s0, wb, wc, a_b)


def _s5_params(lam_re, lam_im, log_step, b_re, b_im, c_re, c_im):
    g, p = lam_re.shape
    h = b_re.shape[-1]
    dt = jnp.exp(log_step.astype(F32))[:, None]
    mag = jnp.exp(lam_re * dt)
    ab_re = mag * jnp.cos(lam_im * dt)
    ab_im = mag * jnp.sin(lam_im * dt)
    den = lam_re * lam_re + lam_im * lam_im
    nr, ni = ab_re - 1.0, ab_im
    f_re = (nr * lam_re + ni * lam_im) / den
    f_im = (ni * lam_re - nr * lam_im) / den
    bb_re = f_re[..., None] * b_re - f_im[..., None] * b_im
    bb_im = f_re[..., None] * b_im + f_im[..., None] * b_re
    ns, gl = N_SLABS, GROUPS_PER_SLAB
    eye = jnp.eye(gl, dtype=F32)

    def in_block(bb):
        bb = bb.reshape(ns, gl, p, h)
        return jnp.einsum('sgph,gk->sghkp', bb, eye).reshape(ns, gl * h, gl * p)

    def out_block(cc):
        cc = cc.reshape(ns, gl, h, p)
        return jnp.einsum('sghp,gk->sgpkh', cc, eye).reshape(ns, gl * p, gl * h)

    wb = jnp.concatenate([in_block(bb_re), in_block(bb_im)], axis=-1)
    wc = jnp.concatenate([out_block(c_re.astype(F32)), -out_block(c_im.astype(F32))], axis=1)
    a = jnp.concatenate([ab_re.reshape(ns, gl * p), ab_im.reshape(ns, gl * p)], axis=-1)
    return wb.astype(BF16), wc.astype(BF16), a


def _mixer_tail_kernel(h0_ref, gb_ref, u_ref, up_ref, un_ref, um_ref, su_ref, yf_ref, yr_ref,
                       cw_ref, cb_ref, sd_ref, wg_ref, bg_ref, na_ref, nb_ref, wo_ref,
                       g1_ref, b1_ref, h1_ref, h1p_ref, *, bsz, tt):
    i = pl.program_id(0)
    nt = pl.num_programs(0)
    d = h0_ref.shape[-1]
    wcv = gb_ref.shape[-1]
    row_id = lax.broadcasted_iota(jnp.int32, (tt, wcv), 0)
    ya, ys = [], []
    for b in range(bsz):
        u = u_ref[b]
        prev_edge = jnp.where(i == 0, um_ref[...], up_ref[b, SUBLANES - 1:SUBLANES, :])
        next_edge = jnp.where(i == nt - 1, jnp.zeros((1, wcv), F32), un_ref[b, 0:1, :])
        u_prev = jnp.where(row_id == 0, prev_edge, pltpu.roll(u, 1, axis=0))
        u_next = jnp.where(row_id == tt - 1, next_edge, pltpu.roll(u, tt - 1, axis=0))
        conv = u_prev * cw_ref[0:1, :] + u * cw_ref[1:2, :] + u_next * cw_ref[2:3, :] + cb_ref[...]
        ya.append(gb_ref[b] * conv)

        def slab(ref, b=b):
            parts = []
            for j in range(N_SLABS):
                if bsz == 1:
                    parts.append(ref[j])
                else:
                    parts.append(ref[j, pl.ds(b, tt, stride=bsz), :])
            return jnp.concatenate(parts, axis=-1)

        ys.append(slab(yf_ref) + slab(yr_ref) + sd_ref[...] * slab(su_ref))
    y_a = jnp.concatenate(ya, axis=0)
    y_s = jnp.concatenate(ys, axis=0)
    z = jax.nn.gelu(y_s)
    glu = jnp.dot(z.astype(BF16), wg_ref[...], preferred_element_type=F32) + bg_ref[...]
    y_b = z * jax.nn.sigmoid(glu)
    merged = jnp.concatenate([_rms_norm(y_a, na_ref[...]), _rms_norm(y_b, nb_ref[...])], axis=-1)
    m = jnp.dot(merged.astype(BF16), wo_ref[...], preferred_element_type=F32)
    h0 = h0_ref[...].reshape(bsz * tt, d)
    h1 = _layer_norm(DEEPNORM_ALPHA * h0 + m, g1_ref[...], b1_ref[...])
    h1_ref[...] = h1.reshape(bsz, tt, d)
    h1p_ref[...] = _pack_halves(h1).reshape(bsz, tt, d // 2)


def _mixer_tail(h0, gb, u, u_meta_last, su, yf, yr, conv_w, conv_b, ssm_d, w_glu_bf, b_glu,
                norm_a_g, norm_b_g, w_out_bf, ln1_g, ln1_b, tt):
    bsz, seq, d = h0.shape
    wcv = gb.shape[-1]
    ws = N_SLABS * LANES
    nt = seq // tt
    tb = tt // SUBLANES
    nb8 = seq // SUBLANES
    kern = functools.partial(_mixer_tail_kernel, bsz=bsz, tt=tt)
    row = lambda n: pl.BlockSpec((1, n), lambda i: (0, 0))
    slab_spec = pl.BlockSpec((N_SLABS, tt * bsz, LANES), lambda i: (0, i, 0))
    return pl.pallas_call(
        kern,
        grid=(nt,),
        in_specs=[
            pl.BlockSpec((bsz, tt, d), lambda i: (0, i, 0)),
            pl.BlockSpec((bsz, tt, wcv), lambda i: (0, i, 0)),
            pl.BlockSpec((bsz, tt, wcv), lambda i: (0, i, 0)),
            pl.BlockSpec((bsz, SUBLANES, wcv), lambda i: (0, jnp.maximum(i * tb - 1, 0), 0)),
            pl.BlockSpec((bsz, SUBLANES, wcv), lambda i: (0, jnp.minimum((i + 1) * tb, nb8 - 1), 0)),
            row(wcv),
            slab_spec, slab_spec, slab_spec,
            pl.BlockSpec((CONV_WIDTH, wcv), lambda i: (0, 0)),
            row(wcv), row(ws),
            pl.BlockSpec((ws, ws), lambda i: (0, 0)),
            row(ws), row(wcv), row(ws),
            pl.BlockSpec((wcv + ws, d), lambda i: (0, 0)),
            row(d), row(d),
        ],
        out_specs=[
            pl.BlockSpec((bsz, tt, d), lambda i: (0, i, 0)),
            pl.BlockSpec((bsz, tt, d // 2), lambda i: (0, i, 0)),
        ],
        out_shape=[
            jax.ShapeDtypeStruct((bsz, seq, d), F32),
            jax.ShapeDtypeStruct((bsz, seq, d // 2), jnp.uint32),
        ],
        compiler_params=_cparams("parallel"),
        name="mixer_tail",
    )(h0, gb, u, u, u, u_meta_last.reshape(1, wcv), su, yf, yr, conv_w, conv_b.reshape(1, wcv),
      ssm_d.reshape(1, ws), w_glu_bf, b_glu.reshape(1, ws), norm_a_g.reshape(1, wcv),
      norm_b_g.reshape(1, ws), w_out_bf, ln1_g.reshape(1, d), ln1_b.reshape(1, d))


def _dual_row_specs(rows, width, nq_p):
    return [pl.BlockSpec((rows, width), lambda q, *_: (jnp.minimum(q, nq_p - 1), 0)),
            pl.BlockSpec((rows, width), lambda q, *_: (jnp.maximum(q - nq_p, 0), 0))]


def _router_kernel(hp_ref, hs_ref, wh_ref, bias_ref, eidx_ref, rank_ref, gate_ref, cnt_ref,
                   cnt_scr, *, tt, nq_p):
    ne = wh_ref.shape[0]
    epg = ne // N_EXPERT_GROUPS
    neg = jnp.float32(-jnp.inf)

    @pl.when(pl.program_id(0) == 0)
    def _():
        cnt_scr[...] = jnp.zeros_like(cnt_scr)

    h = jnp.where(pl.program_id(0) < nq_p, hp_ref[...], hs_ref[...])
    dn = (((1,), (1,)), ((), ()))
    logits = lax.dot_general(wh_ref[...], h.astype(BF16), dn,
                             preferred_element_type=F32)
    scores = jax.nn.sigmoid(logits)
    sel = scores + bias_ref[...]

    gi = lax.broadcasted_iota(jnp.int32, (epg, tt), 0)
    gs = []
    for g in range(N_EXPERT_GROUPS):
        x = sel[g * epg:(g + 1) * epg, :]
        m1 = jnp.max(x, axis=0, keepdims=True)
        i1 = jnp.min(jnp.where(x == m1, gi, epg), axis=0, keepdims=True)
        m2 = jnp.max(jnp.where(gi == i1, neg, x), axis=0, keepdims=True)
        gs.append(m1 + m2)
    chosen = [jnp.zeros((1, tt), F32) for _ in range(N_EXPERT_GROUPS)]
    for _ in range(TOPK_GROUPS):
        m = gs[0]
        for g in range(1, N_EXPERT_GROUPS):
            m = jnp.maximum(m, gs[g])
        found = jnp.zeros((1, tt), F32)
        for g in range(N_EXPERT_GROUPS):
            hit = jnp.where((gs[g] == m) & (found == 0.0), 1.0, 0.0)
            chosen[g] = chosen[g] + hit
            found = found + hit
            gs[g] = jnp.where(hit > 0.0, neg, gs[g])
    selm = jnp.concatenate(
        [jnp.where(chosen[g] > 0.0, sel[g * epg:(g + 1) * epg, :], neg)
         for g in range(N_EXPERT_GROUPS)], axis=0)

    ei = lax.broadcasted_iota(jnp.int32, (ne, tt), 0)
    msel = jnp.zeros((ne, tt), F32)
    idxs, gvals = [], []
    for _ in range(TOP_K):
        m = jnp.max(selm, axis=0, keepdims=True)
        idx = jnp.min(jnp.where(selm == m, ei, ne), axis=0, keepdims=True)
        hit = ei == idx
        gvals.append(jnp.sum(jnp.where(hit, scores, 0.0), axis=0, keepdims=True))
        selm = jnp.where(hit, neg, selm)
        msel = jnp.where(hit, 1.0, msel)
        idxs.append(idx)
    gsum = gvals[0]
    for k in range(1, TOP_K):
        gsum = gsum + gvals[k]
    gate_ref[...] = jnp.concatenate([gv / gsum * ROUTED_SCALE for gv in gvals], axis=0)
    eidx_ref[...] = jnp.concatenate(idxs, axis=0)

    r_i = lax.broadcasted_iota(jnp.int32, (tt, tt), 0)
    c_i = lax.broadcasted_iota(jnp.int32, (tt, tt), 1)
    upper = jnp.where(r_i < c_i, 1.0, 0.0).astype(BF16)
    rank_full = jnp.dot(msel.astype(BF16), upper, preferred_element_type=F32) + cnt_scr[...]
    ranks = [jnp.sum(jnp.where(ei == idxs[k], rank_full, 0.0), axis=0, keepdims=True)
             for k in range(TOP_K)]
    rank_ref[...] = jnp.concatenate(ranks, axis=0).astype(jnp.int32)
    cnt_scr[...] = cnt_scr[...] + jnp.sum(msel, axis=1, keepdims=True)
    cnt_ref[...] = cnt_scr[...]


def _router(h_p, h_s, w_router, router_bias, tt):
    d = h_p.shape[1]
    n = h_p.shape[0] + h_s.shape[0]
    nq_p = h_p.shape[0] // tt
    ne = w_router.shape[1]
    wh = w_router.T.astype(BF16)
    kern = functools.partial(_router_kernel, tt=tt, nq_p=nq_p)
    return pl.pallas_call(
        kern,
        grid=(n // tt,),
        in_specs=_dual_row_specs(tt, d, nq_p) + [
            pl.BlockSpec((ne, d), lambda i: (0, 0)),
            pl.BlockSpec((ne, 1), lambda i: (0, 0)),
        ],
        out_specs=[
            pl.BlockSpec((TOP_K, tt), lambda i: (0, i)),
            pl.BlockSpec((TOP_K, tt), lambda i: (0, i)),
            pl.BlockSpec((TOP_K, tt), lambda i: (0, i)),
            pl.BlockSpec((ne, 1), lambda i: (0, 0)),
        ],
        out_shape=[
            jax.ShapeDtypeStruct((TOP_K, n), jnp.int32),
            jax.ShapeDtypeStruct((TOP_K, n), jnp.int32),
            jax.ShapeDtypeStruct((TOP_K, n), F32),
            jax.ShapeDtypeStruct((ne, 1), F32),
        ],
        scratch_shapes=[pltpu.VMEM((ne, 1), F32)],
        compiler_params=_cparams("arbitrary"),
        name="router",
    )(h_p, h_s, wh, router_bias.astype(F32).reshape(ne, 1))


def _positions_kernel(eidx_ref, rank_ref, start_ref, pos_ref):
    ne = start_ref.shape[0]
    tt = eidx_ref.shape[1]
    ei = lax.broadcasted_iota(jnp.int32, (ne, tt), 0)
    start = start_ref[...]
    rows = [jnp.sum(jnp.where(ei == eidx_ref[k:k + 1, :], start, 0.0), axis=0, keepdims=True)
            for k in range(TOP_K)]
    pos_ref[...] = jnp.concatenate(rows, axis=0).astype(jnp.int32) + rank_ref[...]


def _positions(eidx, rank, start, tt):
    n = eidx.shape[1]
    ne = start.shape[0]
    return pl.pallas_call(
        _positions_kernel,
        grid=(n // tt,),
        in_specs=[
            pl.BlockSpec((TOP_K, tt), lambda i: (0, i)),
            pl.BlockSpec((TOP_K, tt), lambda i: (0, i)),
            pl.BlockSpec((ne, 1), lambda i: (0, 0)),
        ],
        out_specs=pl.BlockSpec((TOP_K, tt), lambda i: (0, i)),
        out_shape=jax.ShapeDtypeStruct((TOP_K, n), jnp.int32),
        compiler_params=_cparams("parallel"),
        name="positions",
    )(eidx, rank, start.astype(F32).reshape(ne, 1))


def _sc_workers():
    info = plsc.get_sparse_core_info()
    return info.num_cores, info.num_subcores


def _sc_dispatch(hp_p, hp_s, pos_c, chunk):
    w = hp_p.shape[1]
    n = hp_p.shape[0] + hp_s.shape[0]
    nch_p = hp_p.shape[0] // chunk
    nc, ns = _sc_workers()
    per_worker = (n // chunk) // (nc * ns)
    mesh = plsc.VectorSubcoreMesh(core_axis_name="c", subcore_axis_name="s")

    @functools.partial(
        pl.kernel, mesh=mesh,
        out_type=jax.ShapeDtypeStruct((n * TOP_K, w), hp_p.dtype),
        scratch_types=[pltpu.VMEM((TOP_K, chunk), jnp.int32),
                       pltpu.VMEM((chunk, w), hp_p.dtype),
                       pltpu.SemaphoreType.DMA],
    )
    def dispatch(hp_hbm, hs_hbm, pos_hbm, xs_hbm, idx_v, rows_v, sem):
        wid = lax.axis_index("s") * nc + lax.axis_index("c")

        @pl.loop(0, per_worker)
        def _(ci):
            c = wid * per_worker + ci

            @pl.when(c < nch_p)
            def _():
                pltpu.sync_copy(hp_hbm.at[pl.ds(pl.multiple_of(c * chunk, chunk), chunk)], rows_v)

            @pl.when(c >= nch_p)
            def _():
                pltpu.sync_copy(
                    hs_hbm.at[pl.ds(pl.multiple_of((c - nch_p) * chunk, chunk), chunk)], rows_v)

            pltpu.sync_copy(pos_hbm.at[c], idx_v)
            copies = [pltpu.async_copy(rows_v, xs_hbm.at[idx_v.at[k]], sem) for k in range(TOP_K)]
            for cp in copies:
                cp.wait()

    return dispatch(hp_p, hp_s, pos_c)


def _sc_gather(ys, pos_c, chunk):
    w = ys.shape[1]
    n = pos_c.shape[0] * chunk
    nc, ns = _sc_workers()
    per_worker = pos_c.shape[0] // (nc * ns)
    mesh = plsc.VectorSubcoreMesh(core_axis_name="c", subcore_axis_name="s")
    nbuf = 3

    @functools.partial(
        pl.kernel, mesh=mesh,
        out_type=jax.ShapeDtypeStruct((TOP_K, n, w), ys.dtype),
        scratch_types=[pltpu.VMEM((TOP_K, chunk), jnp.int32),
                       pltpu.VMEM((nbuf, chunk, w), ys.dtype),
                       pltpu.SemaphoreType.DMA((nbuf,)),
                       pltpu.SemaphoreType.DMA((nbuf,))],
    )
    def gather(ys_hbm, pos_hbm, out_hbm, idx_v, rows_v, gsem, wsem):
        wid = lax.axis_index("s") * nc + lax.axis_index("c")

        @pl.loop(0, per_worker)
        def _(ci):
            c = wid * per_worker + ci
            off = pl.multiple_of(c * chunk, chunk)
            pltpu.sync_copy(pos_hbm.at[c], idx_v)

            def start_gather(k):
                b = k % nbuf
                return pltpu.async_copy(ys_hbm.at[idx_v.at[k]], rows_v.at[b], gsem.at[b])

            gathers = {0: start_gather(0)}
            writes = {}
            for k in range(TOP_K):
                if k + 1 < TOP_K:
                    if k + 1 - nbuf >= 0:
                        writes.pop(k + 1 - nbuf).wait()
                    gathers[k + 1] = start_gather(k + 1)
                gathers.pop(k).wait()
                b = k % nbuf
                writes[k] = pltpu.async_copy(rows_v.at[b], out_hbm.at[k, pl.ds(off, chunk)],
                                             wsem.at[b])
            for k in sorted(writes):
                writes[k].wait()

    return gather(ys, pos_c)


def _experts_kernel(tile_ref, exp_ref, ns_ref, off_ref, xs_ref, wg_ref, wu_ref, wd_ref, ys_ref,
                    wgb, wub, wdb, acc, *, tm):
    s = pl.program_id(0)
    prev = jnp.maximum(s - 1, 0)
    e = exp_ref[s]
    t = tile_ref[s]
    half = xs_ref.shape[1]
    lo_row, hi_row = off_ref[e], off_ref[e + 1]
    base = t * tm

    @pl.when((s == 0) | (e != exp_ref[prev]))
    def _():
        wgb[...] = wg_ref[0].astype(BF16)
        wub[...] = wu_ref[0].astype(BF16)
        wdb[...] = wd_ref[0].astype(BF16)

    def ffn(rows, r0, masked):
        lo, hi = _unpack_halves(xs_ref[rows, :])
        g = (jnp.dot(lo, wgb[:half], preferred_element_type=F32)
             + jnp.dot(hi, wgb[half:], preferred_element_type=F32))
        u = (jnp.dot(lo, wub[:half], preferred_element_type=F32)
             + jnp.dot(hi, wub[half:], preferred_element_type=F32))
        act = (g * jax.nn.sigmoid(g)) * u
        if masked:
            row = r0 + lax.broadcasted_iota(jnp.int32, (EXPERT_SUB, 1), 0)
            act = jnp.where((row >= lo_row) & (row < hi_row), act, 0.0)
        return jnp.dot(act.astype(BF16), wdb[...], preferred_element_type=F32)

    live = s < ns_ref[0]
    for b in range(tm // EXPERT_SUB):
        rows = pl.ds(b * EXPERT_SUB, EXPERT_SUB)
        r0 = base + b * EXPERT_SUB
        touched = live & (lo_row < r0 + EXPERT_SUB) & (hi_row > r0)
        whole = (lo_row <= r0) & (hi_row >= r0 + EXPERT_SUB)

        @pl.when(touched & whole)
        def _(rows=rows, r0=r0):
            ys_ref[rows, :] = _pack_halves(ffn(rows, r0, False))

        @pl.when(touched & jnp.logical_not(whole))
        def _(rows=rows, r0=r0):
            y = ffn(rows, r0, True)
            opens = lo_row <= r0

            @pl.when(opens)
            def _():
                acc[rows, :] = y

            @pl.when(jnp.logical_not(opens))
            def _():
                acc[rows, :] = acc[rows, :] + y

            ys_ref[rows, :] = _pack_halves(acc[rows, :])


def _experts(xs, tile_of, exp_of, n_steps, offsets, w_gate, w_up, w_down, tm):
    n_rows, half = xs.shape
    ne, d, de = w_gate.shape
    s_max = tile_of.shape[0]

    def row_map(s, tile, ex, ns, off):
        return (tile[s], 0)

    def w_map(s, tile, ex, ns, off):
        return (ex[s], 0, 0)

    grid_spec = pltpu.PrefetchScalarGridSpec(
        num_scalar_prefetch=4,
        grid=(s_max,),
        in_specs=[
            pl.BlockSpec((tm, half), row_map),
            pl.BlockSpec((1, d, de), w_map),
            pl.BlockSpec((1, d, de), w_map),
            pl.BlockSpec((1, de, d), w_map),
        ],
        out_specs=pl.BlockSpec((tm, half), row_map),
        scratch_shapes=[
            pltpu.VMEM((d, de), BF16),
            pltpu.VMEM((d, de), BF16),
            pltpu.VMEM((de, d), BF16),
            pltpu.VMEM((tm, d), F32),
        ],
    )
    return pl.pallas_call(
        functools.partial(_experts_kernel, tm=tm),
        grid_spec=grid_spec,
        out_shape=jax.ShapeDtypeStruct((n_rows, half), jnp.uint32),
        compiler_params=_cparams("arbitrary"),
        name="experts",
    )(tile_of, exp_of, n_steps, offsets, xs, w_gate, w_up, w_down)


def _expert_schedule(counts, n_rows, tm):
    ne = counts.shape[0]
    s_max = n_rows // tm + ne
    off = jnp.concatenate([jnp.zeros((1,), jnp.int32), jnp.cumsum(counts)]).astype(jnp.int32)
    first_tile = off[:-1] // tm
    last_tile = (off[1:] - 1) // tm
    visits = jnp.where(counts > 0, last_tile - first_tile + 1, 0)
    cum = jnp.cumsum(visits)
    n_steps = cum[-1]
    step = jnp.minimum(jnp.arange(s_max, dtype=jnp.int32), n_steps - 1)
    exp_of = jnp.sum((cum[None, :] <= step[:, None]).astype(jnp.int32), axis=1)
    onehot = exp_of[:, None] == jnp.arange(ne, dtype=jnp.int32)[None, :]
    pick = lambda v: jnp.sum(jnp.where(onehot, v[None, :], 0), axis=1)
    tile_of = pick(first_tile) + step - pick(cum - visits)
    return (tile_of.astype(jnp.int32), exp_of.astype(jnp.int32),
            n_steps.reshape(1).astype(jnp.int32), off)


def _combine_kernel(h_ref, gate_ref, yk_ref, wsg_ref, wsu_ref, wsd_ref, g2_ref, b2_ref, out_ref,
                    *, tt):
    h = h_ref[...]
    hb = h.astype(BF16)
    g = jnp.dot(hb, wsg_ref[...], preferred_element_type=F32)
    u = jnp.dot(hb, wsu_ref[...], preferred_element_type=F32)
    act = (g * jax.nn.sigmoid(g)) * u
    f = jnp.dot(act.astype(BF16), wsd_ref[...], preferred_element_type=F32)
    gate = gate_ref[...]
    r_lo = jnp.zeros((tt, yk_ref.shape[-1]), F32)
    r_hi = jnp.zeros((tt, yk_ref.shape[-1]), F32)
    for k in range(TOP_K):
        p = yk_ref[k]
        gk = gate[:, k:k + 1]
        r_lo = r_lo + gk * pltpu.bitcast(p << 16, F32)
        r_hi = r_hi + gk * pltpu.bitcast(p & jnp.uint32(0xFFFF0000), F32)
    f = f + jnp.concatenate([r_lo, r_hi], axis=-1)
    out_ref[...] = _layer_norm(DEEPNORM_ALPHA * h + f, g2_ref[...], b2_ref[...])


def _combine(h, gate_t, gate_row0, yk, wsg_bf, wsu_bf, wsd_bf, ln2_g, ln2_b, tt):
    n, d = h.shape
    q0 = gate_row0 // tt
    ds_ = wsg_bf.shape[1]
    kern = functools.partial(_combine_kernel, tt=tt)
    return pl.pallas_call(
        kern,
        grid=(n // tt,),
        in_specs=[
            pl.BlockSpec((tt, d), lambda i: (i, 0)),
            pl.BlockSpec((tt, TOP_K), lambda i: (i + q0, 0)),
            pl.BlockSpec((TOP_K, tt, yk.shape[2]), lambda i: (0, i, 0)),
            pl.BlockSpec((d, ds_), lambda i: (0, 0)),
            pl.BlockSpec((d, ds_), lambda i: (0, 0)),
            pl.BlockSpec((ds_, d), lambda i: (0, 0)),
            pl.BlockSpec((1, d), lambda i: (0, 0)),
            pl.BlockSpec((1, d), lambda i: (0, 0)),
        ],
        out_specs=pl.BlockSpec((tt, d), lambda i: (i, 0)),
        out_shape=jax.ShapeDtypeStruct((n, d), F32),
        compiler_params=_cparams("parallel"),
        name="combine",
    )(h, gate_t, yk, wsg_bf, wsu_bf, wsd_bf, ln2_g.reshape(1, d), ln2_b.reshape(1, d))


def _pick_tile(seq, bsz, rows):
    return max(SUBLANES, min(seq, rows // bsz))


def _mixer(x, meta_state, u_meta_last, p):
    bsz, seq, _ = x.shape
    tt = _pick_tile(seq, bsz, 1024)
    h0, gb, u, su = _embed_inproj(x, p["ln_emb_g"], p["ln_emb_b"], p["w_in_bf"], tt)
    tc = _pick_tile(seq, bsz, 256)
    s0 = jnp.stack([jnp.broadcast_to(meta_state, (N_SLABS, SUBLANES, 2 * SLAB_STATE)),
                    jnp.zeros((N_SLABS, SUBLANES, 2 * SLAB_STATE), F32)])
    a_b = jnp.broadcast_to(p["s5_a"][:, :, None, :], (2, N_SLABS, SUBLANES, 2 * SLAB_STATE))
    yf, yr = _s5_scan(su, s0, p["s5_wb"], p["s5_wc"], a_b, bsz=bsz, tc=tc, dirs=(False, True),
                      emit_y=True, emit_state=False)
    tt3 = _pick_tile(seq, bsz, 512)
    h1, h1p = _mixer_tail(h0, gb, u, u_meta_last, su, yf, yr, p["conv_w"], p["conv_b"],
                          p["ssm_d"], p["w_glu_bf"], p["b_glu"], p["norm_a_g"], p["norm_b_g"],
                          p["w_out_bf"], p["ln1_g"], p["ln1_b"], tt3)
    d = h1.shape[-1]
    return h1.reshape(bsz * seq, d), h1p.reshape(bsz * seq, d // 2)


def kernel(x_prompt, x_sample, meta_tokens, ln_emb_g, ln_emb_b, w_in, conv_w, conv_b, ssm_lambda_re, ssm_lambda_im, ssm_log_step, ssm_b_re, ssm_b_im, ssm_c_re, ssm_c_im, ssm_d, w_glu, b_glu, norm_a_g, norm_b_g, w_out, ln1_g, ln1_b, w_router, router_bias, w_exp_gate, w_exp_up, w_exp_down, w_sh_gate, w_sh_up, w_sh_down, ln2_g, ln2_b):
    l = 0
    d = x_prompt.shape[-1]
    dirs = [_s5_params(ssm_lambda_re[l, k].astype(F32), ssm_lambda_im[l, k].astype(F32),
                       ssm_log_step[l, k], ssm_b_re[l, k].astype(F32), ssm_b_im[l, k].astype(F32),
                       ssm_c_re[l, k], ssm_c_im[l, k]) for k in range(2)]
    p = dict(
        ln_emb_g=ln_emb_g, ln_emb_b=ln_emb_b, w_in_bf=w_in[l].astype(BF16),
        conv_w=conv_w[l], conv_b=conv_b[l], ssm_d=ssm_d[l],
        w_glu_bf=w_glu[l].astype(BF16), b_glu=b_glu[l], norm_a_g=norm_a_g[l],
        norm_b_g=norm_b_g[l], w_out_bf=w_out[l].astype(BF16), ln1_g=ln1_g[l], ln1_b=ln1_b[l],
        s5_wb=jnp.stack([dirs[0][0], dirs[1][0]]), s5_wc=jnp.stack([dirs[0][1], dirs[1][1]]),
        s5_a=jnp.stack([dirs[0][2], dirs[1][2]]),
    )
    mb = SUBLANES
    xm = jnp.broadcast_to(meta_tokens.astype(F32)[None], (mb, N_META, d))
    _, _, u_m, su_m = _embed_inproj(xm, ln_emb_g, ln_emb_b, p["w_in_bf"], N_META)
    a_m = jnp.broadcast_to(p["s5_a"][:1, :, None, :], (1, N_SLABS, mb, 2 * SLAB_STATE))
    (st_m,) = _s5_scan(su_m, jnp.zeros((1, N_SLABS, mb, 2 * SLAB_STATE), F32), p["s5_wb"][:1],
                       p["s5_wc"][:1], a_m, bsz=mb, tc=N_META, dirs=(False,), emit_y=False,
                       emit_state=True)
    meta_state = st_m[0, :, :1, :]
    u_meta_last = u_m[0, N_META - 1]

    h1_p, h1p_p = _mixer(x_prompt, meta_state, u_meta_last, p)
    h1_s, h1p_s = _mixer(x_sample, meta_state, u_meta_last, p)
    n = h1_p.shape[0] + h1_s.shape[0]

    tr = min(h1_p.shape[0], h1_s.shape[0], ROUTER_TILE)
    eidx, rank, gate, cnt = _router(h1_p, h1_s, w_router[l], router_bias[l], tr)
    counts = cnt[:, 0].astype(jnp.int32)
    tile_of, exp_of, n_steps, offsets = _expert_schedule(counts, n * TOP_K, EXPERT_TILE)
    pos = _positions(eidx, rank, offsets[:-1], tr)
    pos_c = pos.reshape(TOP_K, n // SC_CHUNK, SC_CHUNK).transpose(1, 0, 2)
    xs = _sc_dispatch(h1p_p, h1p_s, pos_c, SC_CHUNK)
    ys = _experts(xs, tile_of, exp_of, n_steps, offsets, w_exp_gate[l], w_exp_up[l],
                  w_exp_down[l], EXPERT_TILE)
    n_p = h1_p.shape[0]
    nch_p = n_p // SC_CHUNK
    td = min(n_p, h1_s.shape[0], COMBINE_TILE)
    gate_t = gate.T
    shared = (w_sh_gate[l].astype(BF16), w_sh_up[l].astype(BF16), w_sh_down[l].astype(BF16),
              ln2_g[l], ln2_b[l], td)
    yk_p = _sc_gather(ys, pos_c[:nch_p], SC_CHUNK)
    yk_s = _sc_gather(ys, pos_c[nch_p:], SC_CHUNK)
    out_p = _combine(h1_p, gate_t, 0, yk_p, *shared)
    out_s = _combine(h1_s, gate_t, n_p, yk_s, *shared)
    return (out_p.reshape(x_prompt.shape), out_s.reshape(x_sample.shape))
```

```python
import functools
import math

import jax
import jax.numpy as jnp
from jax import lax
from jax.experimental import pallas as pl
from jax.experimental.pallas import tpu as pltpu
from jax.experimental.pallas import tpu_sc as plsc

F32 = jnp.float32
BF16 = jnp.bfloat16

N_META = 16
CONV_WIDTH = 3
SSM_GROUP = 16
SSM_STATE = 64
N_EXPERTS = 256
TOP_K = 8
N_EXPERT_GROUPS = 8
TOPK_GROUPS = 4
ROUTED_SCALE = 2.5
DEPTH = 1
DEEPNORM_ALPHA = (2.0 * DEPTH) ** 0.25
LN_EPS = 1e-5
RMS_EPS = 1e-6

LANES = 128
SUBLANES = 8
N_SLABS = 4
GROUPS_PER_SLAB = LANES // SSM_GROUP
SLAB_STATE = GROUPS_PER_SLAB * SSM_STATE
EXPERT_TILE = 2048
EXPERT_SUB = 512
EXPERT_EDGE = 128
ROUTER_TILE = 256
COMBINE_TILE = 512
POSITIONS_TILE = 2048
SC_CHUNK = 64
VMEM_LIMIT = 48 * 1024 * 1024


def _cparams(*sem):
    return pltpu.CompilerParams(dimension_semantics=sem, vmem_limit_bytes=VMEM_LIMIT)


def _layer_norm(x, g, b):
    mu = jnp.mean(x, axis=-1, keepdims=True)
    xc = x - mu
    var = jnp.mean(xc * xc, axis=-1, keepdims=True)
    return xc * lax.rsqrt(var + LN_EPS) * g + b


def _rms_norm(x, g):
    return x * lax.rsqrt(jnp.mean(x * x, axis=-1, keepdims=True) + RMS_EPS) * g


def _pack_halves(x):
    half = x.shape[-1] // 2
    bits = pltpu.bitcast(x.astype(BF16).astype(F32), jnp.uint32)
    return (bits[:, :half] >> 16) | (bits[:, half:] & jnp.uint32(0xFFFF0000))


def _unpack_halves(p):
    lo = pltpu.bitcast(p << 16, F32).astype(BF16)
    hi = pltpu.bitcast(p & jnp.uint32(0xFFFF0000), F32).astype(BF16)
    return lo, hi


def _embed_inproj_kernel(x_ref, g_ref, b_ref, w_ref, gb_ref, u_ref, su_ref, *, bsz, tt):
    d = x_ref.shape[-1]
    x = x_ref[...].reshape(bsz * tt, d)
    h0 = _layer_norm(x, g_ref[...], b_ref[...])
    proj = jnp.dot(h0.astype(BF16), w_ref[...], preferred_element_type=F32)
    wc = gb_ref.shape[-1]
    gb_ref[...] = proj[:, :wc].reshape(bsz, tt, wc)
    u_ref[...] = (proj[:, wc:2 * wc] * proj[:, 2 * wc:3 * wc]).reshape(bsz, tt, wc)
    s_u = proj[:, 3 * wc:]
    for b in range(bsz):
        for j in range(N_SLABS):
            val = s_u[b * tt:(b + 1) * tt, j * LANES:(j + 1) * LANES]
            if bsz == 1:
                su_ref[j] = val
            else:
                su_ref[j, pl.ds(b, tt, stride=bsz), :] = val


def _embed_inproj(x, ln_g, ln_b, w_in_bf, tt):
    bsz, seq, d = x.shape
    e = w_in_bf.shape[1]
    wc = (e - N_SLABS * LANES) // 3
    nt = seq // tt
    kern = functools.partial(_embed_inproj_kernel, bsz=bsz, tt=tt)
    return pl.pallas_call(
        kern,
        grid=(nt,),
        in_specs=[
            pl.BlockSpec((bsz, tt, d), lambda i: (0, i, 0)),
            pl.BlockSpec((1, d), lambda i: (0, 0)),
            pl.BlockSpec((1, d), lambda i: (0, 0)),
            pl.BlockSpec((d, e), lambda i: (0, 0)),
        ],
        out_specs=[
            pl.BlockSpec((bsz, tt, wc), lambda i: (0, i, 0)),
            pl.BlockSpec((bsz, tt, wc), lambda i: (0, i, 0)),
            pl.BlockSpec((N_SLABS, tt * bsz, LANES), lambda i: (0, i, 0)),
        ],
        out_shape=[
            jax.ShapeDtypeStruct((bsz, seq, wc), F32),
            jax.ShapeDtypeStruct((bsz, seq, wc), F32),
            jax.ShapeDtypeStruct((N_SLABS, seq * bsz, LANES), F32),
        ],
        compiler_params=_cparams("parallel"),
        name="embed_inproj",
    )(x, ln_g.reshape(1, d), ln_b.reshape(1, d), w_in_bf)


def _s5_kernel(*refs, bsz, tc, dirs, emit_y, emit_state):
    nd = len(dirs)
    it = iter(refs)
    u_refs = [next(it) for _ in range(nd)]
    s0_ref, wb_ref, wc_ref, a_ref = next(it), next(it), next(it), next(it)
    y_refs = [next(it) for _ in range(nd)] if emit_y else []
    sf_ref = next(it) if emit_state else None
    bu_ref, st_ref = next(it), next(it)
    half = SLAB_STATE

    @pl.when(pl.program_id(0) == 0)
    def _():
        st_ref[...] = s0_ref[...]

    def project_in(j):
        for k in range(nd):
            bu_ref[k, j] = jnp.dot(u_refs[k][j].astype(BF16), wb_ref[k, j],
                                   preferred_element_type=F32)

    def project_out(j):
        for k in range(nd):
            y_refs[k][j] = jnp.dot(bu_ref[k, j].astype(BF16), wc_ref[k, j],
                                   preferred_element_type=F32)

    per_tile = SUBLANES // bsz
    n_tiles = tc // per_tile

    def cmul_add(a, s, x):
        return (a[0] * s[0] - a[1] * s[1] + x[0], a[0] * s[1] + a[1] * s[0] + x[1])

    def load(k, j, ti):
        rows = pl.ds(ti * SUBLANES, SUBLANES)
        return bu_ref[k, j, rows, :half], bu_ref[k, j, rows, half:]

    def store(k, j, ti, v):
        rows = pl.ds(ti * SUBLANES, SUBLANES)
        bu_ref[k, j, rows, :half] = v[0]
        bu_ref[k, j, rows, half:] = v[1]

    if per_tile == 1:
        def recur(j):
            for k in range(nd):
                a = (a_ref[k, j, :, :half], a_ref[k, j, :, half:])
                s = (st_ref[k, j, :, :half], st_ref[k, j, :, half:])
                for i in range(n_tiles):
                    ti = (n_tiles - 1 - i) if dirs[k] else i
                    s = cmul_add(a, s, load(k, j, ti))
                    store(k, j, ti, s)
                st_ref[k, j, :, :half] = s[0]
                st_ref[k, j, :, half:] = s[1]
    else:
        assert per_tile == 2 and tuple(dirs) == (False, True)
        low = lax.broadcasted_iota(jnp.int32, (SUBLANES, half), 0) < bsz

        def pick(p, q):
            return (jnp.where(low, p[0], q[0]), jnp.where(low, p[1], q[1]))

        def swap(v):
            return (pltpu.roll(v[0], bsz, axis=0), pltpu.roll(v[1], bsz, axis=0))

        def recur(j):
            af = (a_ref[0, j, :, :half], a_ref[0, j, :, half:])
            ar = (a_ref[1, j, :, :half], a_ref[1, j, :, half:])
            a_fr, a_rf = pick(af, ar), pick(ar, af)
            s = pick((st_ref[0, j, :, :half], st_ref[0, j, :, half:]),
                     (st_ref[1, j, :, :half], st_ref[1, j, :, half:]))
            for i in range(n_tiles):
                tf, tr = i, n_tiles - 1 - i
                xf, xr = load(0, j, tf), load(1, j, tr)
                s1 = cmul_add(a_fr, s, pick(xf, xr))
                s2 = cmul_add(a_rf, swap(s1), pick(xr, xf))
                store(0, j, tf, pick(s1, s2))
                store(1, j, tr, pick(s2, s1))
                s = swap(s2)
            for k in range(nd):
                st_ref[k, j, :, :half] = s[0]
                st_ref[k, j, :, half:] = s[1]

    project_in(0)
    for j in range(N_SLABS):
        if j + 1 < N_SLABS:
            project_in(j + 1)
        recur(j)
        if emit_y:
            project_out(j)
    if emit_state:
        sf_ref[...] = st_ref[...]


def _s5_scan(su, s0, wb, wc, a_b, *, bsz, tc, dirs, emit_y, emit_state):
    rows = su.shape[1]
    seq = rows // bsz
    nc = seq // tc
    r = tc * bsz
    nd = len(dirs)
    sw = 2 * SLAB_STATE

    def u_map(rev):
        return (lambda c: (0, nc - 1 - c, 0)) if rev else (lambda c: (0, c, 0))

    in_specs = [pl.BlockSpec((N_SLABS, r, LANES), u_map(rev)) for rev in dirs]
    in_specs += [
        pl.BlockSpec((nd, N_SLABS, SUBLANES, sw), lambda c: (0, 0, 0, 0)),
        pl.BlockSpec((nd, N_SLABS, LANES, sw), lambda c: (0, 0, 0, 0)),
        pl.BlockSpec((nd, N_SLABS, sw, LANES), lambda c: (0, 0, 0, 0)),
        pl.BlockSpec((nd, N_SLABS, SUBLANES, sw), lambda c: (0, 0, 0, 0)),
    ]
    out_specs, out_shape = [], []
    if emit_y:
        for rev in dirs:
            out_specs.append(pl.BlockSpec((N_SLABS, r, LANES), u_map(rev)))
            out_shape.append(jax.ShapeDtypeStruct((N_SLABS, rows, LANES), F32))
    if emit_state:
        out_specs.append(pl.BlockSpec((nd, N_SLABS, SUBLANES, sw), lambda c: (0, 0, 0, 0)))
        out_shape.append(jax.ShapeDtypeStruct((nd, N_SLABS, SUBLANES, sw), F32))
    kern = functools.partial(_s5_kernel, bsz=bsz, tc=tc, dirs=dirs, emit_y=emit_y,
                             emit_state=emit_state)
    return pl.pallas_call(
        kern,
        grid=(nc,),
        in_specs=in_specs,
        out_specs=out_specs,
        out_shape=out_shape,
        scratch_shapes=[
            pltpu.VMEM((nd, N_SLABS, r, sw), F32),
            pltpu.VMEM((nd, N_SLABS, SUBLANES, sw), F32),
        ],
        compiler_params=_cparams("arbitrary"),
        name="s5_scan",
    )(*([su] * nd), s0, wb, wc, a_b)


def _s5_params(lam_re, lam_im, log_step, b_re, b_im, c_re, c_im):
    g, p = lam_re.shape
    h = b_re.shape[-1]
    dt = jnp.exp(log_step.astype(F32))[:, None]
    mag = jnp.exp(lam_re * dt)
    ab_re = mag * jnp.cos(lam_im * dt)
    ab_im = mag * jnp.sin(lam_im * dt)
    den = lam_re * lam_re + lam_im * lam_im
    nr, ni = ab_re - 1.0, ab_im
    f_re = (nr * lam_re + ni * lam_im) / den
    f_im = (ni * lam_re - nr * lam_im) / den
    bb_re = f_re[..., None] * b_re - f_im[..., None] * b_im
    bb_im = f_re[..., None] * b_im + f_im[..., None] * b_re
    ns, gl = N_SLABS, GROUPS_PER_SLAB
    eye = jnp.eye(gl, dtype=F32)

    def in_block(bb):
        bb = bb.reshape(ns, gl, p, h)
        return jnp.einsum('sgph,gk->sghkp', bb, eye).reshape(ns, gl * h, gl * p)

    def out_block(cc):
        cc = cc.reshape(ns, gl, h, p)
        return jnp.einsum('sghp,gk->sgpkh', cc, eye).reshape(ns, gl * p, gl * h)

    wb = jnp.concatenate([in_block(bb_re), in_block(bb_im)], axis=-1)
    wc = jnp.concatenate([out_block(c_re.astype(F32)), -out_block(c_im.astype(F32))], axis=1)
    a = jnp.concatenate([ab_re.reshape(ns, gl * p), ab_im.reshape(ns, gl * p)], axis=-1)
    return wb.astype(BF16), wc.astype(BF16), a


def _mixer_tail_kernel(x_ref, ge_ref, be_ref, gb_ref, u_ref, up_ref, un_ref, um_ref, su_ref,
                       yf_ref, yr_ref, cw_ref, cb_ref, sd_ref, wg_ref, bg_ref, na_ref, nb_ref,
                       wo_ref, g1_ref, b1_ref, h1_ref, h1p_ref, *, bsz, tt):
    i = pl.program_id(0)
    nt = pl.num_programs(0)
    d = x_ref.shape[-1]
    wcv = gb_ref.shape[-1]
    row_id = lax.broadcasted_iota(jnp.int32, (tt, wcv), 0)
    ya, ys = [], []
    for b in range(bsz):
        u = u_ref[b]
        prev_edge = jnp.where(i == 0, um_ref[...], up_ref[b, SUBLANES - 1:SUBLANES, :])
        next_edge = jnp.where(i == nt - 1, jnp.zeros((1, wcv), F32), un_ref[b, 0:1, :])
        u_prev = jnp.where(row_id == 0, prev_edge, pltpu.roll(u, 1, axis=0))
        u_next = jnp.where(row_id == tt - 1, next_edge, pltpu.roll(u, tt - 1, axis=0))
        conv = u_prev * cw_ref[0:1, :] + u * cw_ref[1:2, :] + u_next * cw_ref[2:3, :] + cb_ref[...]
        ya.append(gb_ref[b] * conv)

        def slab(ref, b=b):
            parts = []
            for j in range(N_SLABS):
                if bsz == 1:
                    parts.append(ref[j])
                else:
                    parts.append(ref[j, pl.ds(b, tt, stride=bsz), :])
            return jnp.concatenate(parts, axis=-1)

        ys.append(slab(yf_ref) + slab(yr_ref) + sd_ref[...] * slab(su_ref))
    y_a = jnp.concatenate(ya, axis=0)
    y_s = jnp.concatenate(ys, axis=0)
    z = jax.nn.gelu(y_s)
    glu = jnp.dot(z.astype(BF16), wg_ref[...], preferred_element_type=F32) + bg_ref[...]
    y_b = z * jax.nn.sigmoid(glu)
    merged = jnp.concatenate([_rms_norm(y_a, na_ref[...]), _rms_norm(y_b, nb_ref[...])], axis=-1)
    m = jnp.dot(merged.astype(BF16), wo_ref[...], preferred_element_type=F32)
    h0 = _layer_norm(x_ref[...].reshape(bsz * tt, d), ge_ref[...], be_ref[...])
    h1 = _layer_norm(DEEPNORM_ALPHA * h0 + m, g1_ref[...], b1_ref[...])
    h1_ref[...] = h1.reshape(bsz, tt, d)
    h1p_ref[...] = _pack_halves(h1).reshape(bsz, tt, d // 2)


def _mixer_tail(x, ln_emb_g, ln_emb_b, gb, u, u_meta_last, su, yf, yr, conv_w, conv_b, ssm_d,
                w_glu_bf, b_glu, norm_a_g, norm_b_g, w_out_bf, ln1_g, ln1_b, tt):
    bsz, seq, d = x.shape
    wcv = gb.shape[-1]
    ws = N_SLABS * LANES
    nt = seq // tt
    tb = tt // SUBLANES
    nb8 = seq // SUBLANES
    kern = functools.partial(_mixer_tail_kernel, bsz=bsz, tt=tt)
    row = lambda n: pl.BlockSpec((1, n), lambda i: (0, 0))
    slab_spec = pl.BlockSpec((N_SLABS, tt * bsz, LANES), lambda i: (0, i, 0))
    return pl.pallas_call(
        kern,
        grid=(nt,),
        in_specs=[
            pl.BlockSpec((bsz, tt, d), lambda i: (0, i, 0)),
            row(d), row(d),
            pl.BlockSpec((bsz, tt, wcv), lambda i: (0, i, 0)),
            pl.BlockSpec((bsz, tt, wcv), lambda i: (0, i, 0)),
            pl.BlockSpec((bsz, SUBLANES, wcv), lambda i: (0, jnp.maximum(i * tb - 1, 0), 0)),
            pl.BlockSpec((bsz, SUBLANES, wcv), lambda i: (0, jnp.minimum((i + 1) * tb, nb8 - 1), 0)),
            row(wcv),
            slab_spec, slab_spec, slab_spec,
            pl.BlockSpec((CONV_WIDTH, wcv), lambda i: (0, 0)),
            row(wcv), row(ws),
            pl.BlockSpec((ws, ws), lambda i: (0, 0)),
            row(ws), row(wcv), row(ws),
            pl.BlockSpec((wcv + ws, d), lambda i: (0, 0)),
            row(d), row(d),
        ],
        out_specs=[
            pl.BlockSpec((bsz, tt, d), lambda i: (0, i, 0)),
            pl.BlockSpec((bsz, tt, d // 2), lambda i: (0, i, 0)),
        ],
        out_shape=[
            jax.ShapeDtypeStruct((bsz, seq, d), F32),
            jax.ShapeDtypeStruct((bsz, seq, d // 2), jnp.uint32),
        ],
        compiler_params=_cparams("parallel"),
        name="mixer_tail",
    )(x, ln_emb_g.reshape(1, d), ln_emb_b.reshape(1, d), gb, u, u, u,
      u_meta_last.reshape(1, wcv), su, yf, yr, conv_w, conv_b.reshape(1, wcv),
      ssm_d.reshape(1, ws), w_glu_bf, b_glu.reshape(1, ws), norm_a_g.reshape(1, wcv),
      norm_b_g.reshape(1, ws), w_out_bf, ln1_g.reshape(1, d), ln1_b.reshape(1, d))


def _dual_row_specs(rows, width, nq_p):
    return [pl.BlockSpec((rows, width), lambda q, *_: (jnp.minimum(q, nq_p - 1), 0)),
            pl.BlockSpec((rows, width), lambda q, *_: (jnp.maximum(q - nq_p, 0), 0))]


def _router_kernel(hp_ref, hs_ref, wh_ref, bias_ref, eidx_ref, rank_ref, gate_ref, cnt_ref,
                   cnt_scr, *, tt, nq_p):
    ne = wh_ref.shape[0]
    epg = ne // N_EXPERT_GROUPS
    neg = jnp.float32(-jnp.inf)

    @pl.when(pl.program_id(0) == 0)
    def _():
        cnt_scr[...] = jnp.zeros_like(cnt_scr)

    h = jnp.where(pl.program_id(0) < nq_p, hp_ref[...], hs_ref[...])
    dn = (((1,), (1,)), ((), ()))
    logits = lax.dot_general(wh_ref[...], h.astype(BF16), dn,
                             preferred_element_type=F32)
    scores = jax.nn.sigmoid(logits)
    sel = scores + bias_ref[...]

    gi = lax.broadcasted_iota(jnp.int32, (epg, tt), 0)
    gs = []
    for g in range(N_EXPERT_GROUPS):
        x = sel[g * epg:(g + 1) * epg, :]
        m1 = jnp.max(x, axis=0, keepdims=True)
        i1 = jnp.min(jnp.where(x == m1, gi, epg), axis=0, keepdims=True)
        m2 = jnp.max(jnp.where(gi == i1, neg, x), axis=0, keepdims=True)
        gs.append(m1 + m2)
    chosen = [jnp.zeros((1, tt), F32) for _ in range(N_EXPERT_GROUPS)]
    for _ in range(TOPK_GROUPS):
        m = gs[0]
        for g in range(1, N_EXPERT_GROUPS):
            m = jnp.maximum(m, gs[g])
        found = jnp.zeros((1, tt), F32)
        for g in range(N_EXPERT_GROUPS):
            hit = jnp.where((gs[g] == m) & (found == 0.0), 1.0, 0.0)
            chosen[g] = chosen[g] + hit
            found = found + hit
            gs[g] = jnp.where(hit > 0.0, neg, gs[g])
    selm = jnp.concatenate(
        [jnp.where(chosen[g] > 0.0, sel[g * epg:(g + 1) * epg, :], neg)
         for g in range(N_EXPERT_GROUPS)], axis=0)

    ei = lax.broadcasted_iota(jnp.int32, (ne, tt), 0)
    msel = jnp.zeros((ne, tt), F32)
    idxs, gvals = [], []
    for _ in range(TOP_K):
        m = jnp.max(selm, axis=0, keepdims=True)
        idx = jnp.min(jnp.where(selm == m, ei, ne), axis=0, keepdims=True)
        hit = ei == idx
        gvals.append(jnp.sum(jnp.where(hit, scores, 0.0), axis=0, keepdims=True))
        selm = jnp.where(hit, neg, selm)
        msel = jnp.where(hit, 1.0, msel)
        idxs.append(idx)
    gsum = gvals[0]
    for k in range(1, TOP_K):
        gsum = gsum + gvals[k]
    gate_ref[...] = jnp.concatenate([gv / gsum * ROUTED_SCALE for gv in gvals], axis=0)
    eidx_ref[...] = jnp.concatenate(idxs, axis=0)

    r_i = lax.broadcasted_iota(jnp.int32, (tt, tt), 0)
    c_i = lax.broadcasted_iota(jnp.int32, (tt, tt), 1)
    upper = jnp.where(r_i < c_i, 1.0, 0.0).astype(BF16)
    rank_full = jnp.dot(msel.astype(BF16), upper, preferred_element_type=F32) + cnt_scr[...]
    ranks = [jnp.sum(jnp.where(ei == idxs[k], rank_full, 0.0), axis=0, keepdims=True)
             for k in range(TOP_K)]
    rank_ref[...] = jnp.concatenate(ranks, axis=0).astype(jnp.int32)
    cnt_scr[...] = cnt_scr[...] + jnp.sum(msel, axis=1, keepdims=True)
    cnt_ref[...] = cnt_scr[...]


def _router(h_p, h_s, w_router, router_bias, tt):
    d = h_p.shape[1]
    n = h_p.shape[0] + h_s.shape[0]
    nq_p = h_p.shape[0] // tt
    ne = w_router.shape[1]
    wh = w_router.T.astype(BF16)
    kern = functools.partial(_router_kernel, tt=tt, nq_p=nq_p)
    return pl.pallas_call(
        kern,
        grid=(n // tt,),
        in_specs=_dual_row_specs(tt, d, nq_p) + [
            pl.BlockSpec((ne, d), lambda i: (0, 0)),
            pl.BlockSpec((ne, 1), lambda i: (0, 0)),
        ],
        out_specs=[
            pl.BlockSpec((TOP_K, tt), lambda i: (0, i)),
            pl.BlockSpec((TOP_K, tt), lambda i: (0, i)),
            pl.BlockSpec((TOP_K, tt), lambda i: (0, i)),
            pl.BlockSpec((ne, 1), lambda i: (0, 0)),
        ],
        out_shape=[
            jax.ShapeDtypeStruct((TOP_K, n), jnp.int32),
            jax.ShapeDtypeStruct((TOP_K, n), jnp.int32),
            jax.ShapeDtypeStruct((TOP_K, n), F32),
            jax.ShapeDtypeStruct((ne, 1), F32),
        ],
        scratch_shapes=[pltpu.VMEM((ne, 1), F32)],
        compiler_params=_cparams("arbitrary"),
        name="router",
    )(h_p, h_s, wh, router_bias.astype(F32).reshape(ne, 1))


def _positions_kernel(eidx_ref, rank_ref, start_ref, pos_ref):
    ne = start_ref.shape[0]
    tt = eidx_ref.shape[1]
    ei = lax.broadcasted_iota(jnp.int32, (ne, tt), 0)
    start = start_ref[...]
    rows = [jnp.sum(jnp.where(ei == eidx_ref[k:k + 1, :], start, 0.0), axis=0, keepdims=True)
            for k in range(TOP_K)]
    pos_ref[...] = jnp.concatenate(rows, axis=0).astype(jnp.int32) + rank_ref[...]


def _positions(eidx, rank, start, tt):
    n = eidx.shape[1]
    ne = start.shape[0]
    return pl.pallas_call(
        _positions_kernel,
        grid=(n // tt,),
        in_specs=[
            pl.BlockSpec((TOP_K, tt), lambda i: (0, i)),
            pl.BlockSpec((TOP_K, tt), lambda i: (0, i)),
            pl.BlockSpec((ne, 1), lambda i: (0, 0)),
        ],
        out_specs=pl.BlockSpec((TOP_K, tt), lambda i: (0, i)),
        out_shape=jax.ShapeDtypeStruct((TOP_K, n), jnp.int32),
        compiler_params=_cparams("parallel"),
        name="positions",
    )(eidx, rank, start.astype(F32).reshape(ne, 1))


def _sc_workers():
    info = plsc.get_sparse_core_info()
    return info.num_cores, info.num_subcores


def _sc_dispatch(hp_p, hp_s, pos_c, chunk):
    w = hp_p.shape[1]
    n = hp_p.shape[0] + hp_s.shape[0]
    nch_p = hp_p.shape[0] // chunk
    nc, ns = _sc_workers()
    per_worker = (n // chunk) // (nc * ns)
    mesh = plsc.VectorSubcoreMesh(core_axis_name="c", subcore_axis_name="s")

    @functools.partial(
        pl.kernel, mesh=mesh,
        out_type=jax.ShapeDtypeStruct((n * TOP_K, w), hp_p.dtype),
        scratch_types=[pltpu.VMEM((TOP_K, chunk), jnp.int32),
                       pltpu.VMEM((chunk, w), hp_p.dtype),
                       pltpu.SemaphoreType.DMA],
    )
    def dispatch(hp_hbm, hs_hbm, pos_hbm, xs_hbm, idx_v, rows_v, sem):
        wid = lax.axis_index("s") * nc + lax.axis_index("c")

        @pl.loop(0, per_worker)
        def _(ci):
            c = wid * per_worker + ci

            @pl.when(c < nch_p)
            def _():
                pltpu.sync_copy(hp_hbm.at[pl.ds(pl.multiple_of(c * chunk, chunk), chunk)], rows_v)

            @pl.when(c >= nch_p)
            def _():
                pltpu.sync_copy(
                    hs_hbm.at[pl.ds(pl.multiple_of((c - nch_p) * chunk, chunk), chunk)], rows_v)

            pltpu.sync_copy(pos_hbm.at[c], idx_v)
            copies = [pltpu.async_copy(rows_v, xs_hbm.at[idx_v.at[k]], sem) for k in range(TOP_K)]
            for cp in copies:
                cp.wait()

    return dispatch(hp_p, hp_s, pos_c)


def _sc_gather(ys, pos_c, chunk):
    w = ys.shape[1]
    n = pos_c.shape[0] * chunk
    nc, ns = _sc_workers()
    per_worker = pos_c.shape[0] // (nc * ns)
    mesh = plsc.VectorSubcoreMesh(core_axis_name="c", subcore_axis_name="s")
    nbuf = 3

    @functools.partial(
        pl.kernel, mesh=mesh,
        out_type=jax.ShapeDtypeStruct((TOP_K, n, w), ys.dtype),
        scratch_types=[pltpu.VMEM((TOP_K, chunk), jnp.int32),
                       pltpu.VMEM((nbuf, chunk, w), ys.dtype),
                       pltpu.SemaphoreType.DMA((nbuf,)),
                       pltpu.SemaphoreType.DMA((nbuf,))],
    )
    def gather(ys_hbm, pos_hbm, out_hbm, idx_v, rows_v, gsem, wsem):
        wid = lax.axis_index("s") * nc + lax.axis_index("c")

        @pl.loop(0, per_worker)
        def _(ci):
            c = wid * per_worker + ci
            off = pl.multiple_of(c * chunk, chunk)
            pltpu.sync_copy(pos_hbm.at[c], idx_v)

            def start_gather(k):
                b = k % nbuf
                return pltpu.async_copy(ys_hbm.at[idx_v.at[k]], rows_v.at[b], gsem.at[b])

            gathers = {0: start_gather(0)}
            writes = {}
            for k in range(TOP_K):
                if k + 1 < TOP_K:
                    if k + 1 - nbuf >= 0:
                        writes.pop(k + 1 - nbuf).wait()
                    gathers[k + 1] = start_gather(k + 1)
                gathers.pop(k).wait()
                b = k % nbuf
                writes[k] = pltpu.async_copy(rows_v.at[b], out_hbm.at[k, pl.ds(off, chunk)],
                                             wsem.at[b])
            for k in sorted(writes):
                writes[k].wait()

    return gather(ys, pos_c)


def _experts_kernel(tile_ref, exp_ref, ns_ref, off_ref, xs_ref, wg_ref, wu_ref, wd_ref, ys_ref,
                    wgb, wub, wdb, acc, *, tm):
    s = pl.program_id(0)
    prev = jnp.maximum(s - 1, 0)
    e = exp_ref[s]
    t = tile_ref[s]
    half = xs_ref.shape[1]
    lo_row, hi_row = off_ref[e], off_ref[e + 1]
    base = t * tm

    @pl.when((s == 0) | (e != exp_ref[prev]))
    def _():
        wgb[...] = wg_ref[0].astype(BF16)
        wub[...] = wu_ref[0].astype(BF16)
        wdb[...] = wd_ref[0].astype(BF16)

    def ffn(rows, m, r0, masked):
        lo, hi = _unpack_halves(xs_ref[rows, :])
        g = (jnp.dot(lo, wgb[:half], preferred_element_type=F32)
             + jnp.dot(hi, wgb[half:], preferred_element_type=F32))
        u = (jnp.dot(lo, wub[:half], preferred_element_type=F32)
             + jnp.dot(hi, wub[half:], preferred_element_type=F32))
        act = (g * jax.nn.sigmoid(g)) * u
        if masked:
            row = r0 + lax.broadcasted_iota(jnp.int32, (m, 1), 0)
            act = jnp.where((row >= lo_row) & (row < hi_row), act, 0.0)
        return jnp.dot(act.astype(BF16), wdb[...], preferred_element_type=F32)

    def shared_block(rows, m, r0):
        y = ffn(rows, m, r0, True)
        opens = lo_row <= r0

        @pl.when(opens)
        def _():
            acc[rows, :] = y

        @pl.when(jnp.logical_not(opens))
        def _():
            acc[rows, :] = acc[rows, :] + y

        ys_ref[rows, :] = _pack_halves(acc[rows, :])

    live = s < ns_ref[0]
    for b in range(tm // EXPERT_SUB):
        rows = pl.ds(b * EXPERT_SUB, EXPERT_SUB)
        r0 = base + b * EXPERT_SUB
        touched = live & (lo_row < r0 + EXPERT_SUB) & (hi_row > r0)
        whole = (lo_row <= r0) & (hi_row >= r0 + EXPERT_SUB)

        @pl.when(touched & whole)
        def _(rows=rows, r0=r0):
            ys_ref[rows, :] = _pack_halves(ffn(rows, EXPERT_SUB, r0, False))

        for c in range(EXPERT_SUB // EXPERT_EDGE):
            crows = pl.ds(b * EXPERT_SUB + c * EXPERT_EDGE, EXPERT_EDGE)
            c0 = r0 + c * EXPERT_EDGE
            edge = touched & jnp.logical_not(whole) & (lo_row < c0 + EXPERT_EDGE) & (hi_row > c0)
            pl.when(edge)(functools.partial(shared_block, crows, EXPERT_EDGE, c0))


def _experts(xs, tile_of, exp_of, n_steps, offsets, w_gate, w_up, w_down, tm):
    n_rows, half = xs.shape
    ne, d, de = w_gate.shape
    s_max = tile_of.shape[0]

    def row_map(s, tile, ex, ns, off):
        return (tile[s], 0)

    def w_map(s, tile, ex, ns, off):
        return (ex[s], 0, 0)

    grid_spec = pltpu.PrefetchScalarGridSpec(
        num_scalar_prefetch=4,
        grid=(s_max,),
        in_specs=[
            pl.BlockSpec((tm, half), row_map),
            pl.BlockSpec((1, d, de), w_map),
            pl.BlockSpec((1, d, de), w_map),
            pl.BlockSpec((1, de, d), w_map),
        ],
        out_specs=pl.BlockSpec((tm, half), row_map),
        scratch_shapes=[
            pltpu.VMEM((d, de), BF16),
            pltpu.VMEM((d, de), BF16),
            pltpu.VMEM((de, d), BF16),
            pltpu.VMEM((tm, d), F32),
        ],
    )
    return pl.pallas_call(
        functools.partial(_experts_kernel, tm=tm),
        grid_spec=grid_spec,
        out_shape=jax.ShapeDtypeStruct((n_rows, half), jnp.uint32),
        compiler_params=_cparams("arbitrary"),
        name="experts",
    )(tile_of, exp_of, n_steps, offsets, xs, w_gate, w_up, w_down)


def _expert_schedule(counts, n_rows, tm):
    ne = counts.shape[0]
    s_max = n_rows // tm + ne
    off = jnp.concatenate([jnp.zeros((1,), jnp.int32), jnp.cumsum(counts)]).astype(jnp.int32)
    first_tile = off[:-1] // tm
    last_tile = (off[1:] - 1) // tm
    visits = jnp.where(counts > 0, last_tile - first_tile + 1, 0)
    cum = jnp.cumsum(visits)
    n_steps = cum[-1]
    step = jnp.minimum(jnp.arange(s_max, dtype=jnp.int32), n_steps - 1)
    exp_of = jnp.sum((cum[None, :] <= step[:, None]).astype(jnp.int32), axis=1)
    onehot = exp_of[:, None] == jnp.arange(ne, dtype=jnp.int32)[None, :]
    pick = lambda v: jnp.sum(jnp.where(onehot, v[None, :], 0), axis=1)
    tile_of = pick(first_tile) + step - pick(cum - visits)
    return (tile_of.astype(jnp.int32), exp_of.astype(jnp.int32),
            n_steps.reshape(1).astype(jnp.int32), off)


def _combine_kernel(h_ref, gate_ref, yk_ref, wsg_ref, wsu_ref, wsd_ref, g2_ref, b2_ref, out_ref,
                    *, tt):
    h = h_ref[...]
    hb = h.astype(BF16)
    g = jnp.dot(hb, wsg_ref[...], preferred_element_type=F32)
    u = jnp.dot(hb, wsu_ref[...], preferred_element_type=F32)
    act = (g * jax.nn.sigmoid(g)) * u
    f = jnp.dot(act.astype(BF16), wsd_ref[...], preferred_element_type=F32)
    gate = gate_ref[...]
    r_lo = jnp.zeros((tt, yk_ref.shape[-1]), F32)
    r_hi = jnp.zeros((tt, yk_ref.shape[-1]), F32)
    for k in range(TOP_K):
        p = yk_ref[k]
        gk = gate[:, k:k + 1]
        r_lo = r_lo + gk * pltpu.bitcast(p << 16, F32)
        r_hi = r_hi + gk * pltpu.bitcast(p & jnp.uint32(0xFFFF0000), F32)
    f = f + jnp.concatenate([r_lo, r_hi], axis=-1)
    out_ref[...] = _layer_norm(DEEPNORM_ALPHA * h + f, g2_ref[...], b2_ref[...])


def _combine(h, gate_t, gate_row0, yk, wsg_bf, wsu_bf, wsd_bf, ln2_g, ln2_b, tt):
    n, d = h.shape
    q0 = gate_row0 // tt
    ds_ = wsg_bf.shape[1]
    kern = functools.partial(_combine_kernel, tt=tt)
    return pl.pallas_call(
        kern,
        grid=(n // tt,),
        in_specs=[
            pl.BlockSpec((tt, d), lambda i: (i, 0)),
            pl.BlockSpec((tt, TOP_K), lambda i: (i + q0, 0)),
            pl.BlockSpec((TOP_K, tt, yk.shape[2]), lambda i: (0, i, 0)),
            pl.BlockSpec((d, ds_), lambda i: (0, 0)),
            pl.BlockSpec((d, ds_), lambda i: (0, 0)),
            pl.BlockSpec((ds_, d), lambda i: (0, 0)),
            pl.BlockSpec((1, d), lambda i: (0, 0)),
            pl.BlockSpec((1, d), lambda i: (0, 0)),
        ],
        out_specs=pl.BlockSpec((tt, d), lambda i: (i, 0)),
        out_shape=jax.ShapeDtypeStruct((n, d), F32),
        compiler_params=_cparams("parallel"),
        name="combine",
    )(h, gate_t, yk, wsg_bf, wsu_bf, wsd_bf, ln2_g.reshape(1, d), ln2_b.reshape(1, d))


def _pick_tile(seq, bsz, rows):
    return max(SUBLANES, min(seq, rows // bsz))


def _mixer(x, meta_state, u_meta_last, p):
    bsz, seq, _ = x.shape
    tt = _pick_tile(seq, bsz, 1024)
    gb, u, su = _embed_inproj(x, p["ln_emb_g"], p["ln_emb_b"], p["w_in_bf"], tt)
    tc = _pick_tile(seq, bsz, 256)
    s0 = jnp.stack([jnp.broadcast_to(meta_state, (N_SLABS, SUBLANES, 2 * SLAB_STATE)),
                    jnp.zeros((N_SLABS, SUBLANES, 2 * SLAB_STATE), F32)])
    a_b = jnp.broadcast_to(p["s5_a"][:, :, None, :], (2, N_SLABS, SUBLANES, 2 * SLAB_STATE))
    yf, yr = _s5_scan(su, s0, p["s5_wb"], p["s5_wc"], a_b, bsz=bsz, tc=tc, dirs=(False, True),
                      emit_y=True, emit_state=False)
    tt3 = _pick_tile(seq, bsz, 512)
    h1, h1p = _mixer_tail(x, p["ln_emb_g"], p["ln_emb_b"], gb, u, u_meta_last, su, yf, yr,
                          p["conv_w"], p["conv_b"], p["ssm_d"], p["w_glu_bf"], p["b_glu"],
                          p["norm_a_g"], p["norm_b_g"], p["w_out_bf"], p["ln1_g"], p["ln1_b"], tt3)
    d = h1.shape[-1]
    return h1.reshape(bsz * seq, d), h1p.reshape(bsz * seq, d // 2)


def kernel(x_prompt, x_sample, meta_tokens, ln_emb_g, ln_emb_b, w_in, conv_w, conv_b, ssm_lambda_re, ssm_lambda_im, ssm_log_step, ssm_b_re, ssm_b_im, ssm_c_re, ssm_c_im, ssm_d, w_glu, b_glu, norm_a_g, norm_b_g, w_out, ln1_g, ln1_b, w_router, router_bias, w_exp_gate, w_exp_up, w_exp_down, w_sh_gate, w_sh_up, w_sh_down, ln2_g, ln2_b):
    l = 0
    d = x_prompt.shape[-1]
    dirs = [_s5_params(ssm_lambda_re[l, k].astype(F32), ssm_lambda_im[l, k].astype(F32),
                       ssm_log_step[l, k], ssm_b_re[l, k].astype(F32), ssm_b_im[l, k].astype(F32),
                       ssm_c_re[l, k], ssm_c_im[l, k]) for k in range(2)]
    p = dict(
        ln_emb_g=ln_emb_g, ln_emb_b=ln_emb_b, w_in_bf=w_in[l].astype(BF16),
        conv_w=conv_w[l], conv_b=conv_b[l], ssm_d=ssm_d[l],
        w_glu_bf=w_glu[l].astype(BF16), b_glu=b_glu[l], norm_a_g=norm_a_g[l],
        norm_b_g=norm_b_g[l], w_out_bf=w_out[l].astype(BF16), ln1_g=ln1_g[l], ln1_b=ln1_b[l],
        s5_wb=jnp.stack([dirs[0][0], dirs[1][0]]), s5_wc=jnp.stack([dirs[0][1], dirs[1][1]]),
        s5_a=jnp.stack([dirs[0][2], dirs[1][2]]),
    )
    mb = SUBLANES
    xm = jnp.broadcast_to(meta_tokens.astype(F32)[None], (mb, N_META, d))
    _, u_m, su_m = _embed_inproj(xm, ln_emb_g, ln_emb_b, p["w_in_bf"], N_META)
    a_m = jnp.broadcast_to(p["s5_a"][:1, :, None, :], (1, N_SLABS, mb, 2 * SLAB_STATE))
    (st_m,) = _s5_scan(su_m, jnp.zeros((1, N_SLABS, mb, 2 * SLAB_STATE), F32), p["s5_wb"][:1],
                       p["s5_wc"][:1], a_m, bsz=mb, tc=N_META, dirs=(False,), emit_y=False,
                       emit_state=True)
    meta_state = st_m[0, :, :1, :]
    u_meta_last = u_m[0, N_META - 1]

    h1_p, h1p_p = _mixer(x_prompt, meta_state, u_meta_last, p)
    h1_s, h1p_s = _mixer(x_sample, meta_state, u_meta_last, p)
    n = h1_p.shape[0] + h1_s.shape[0]

    tr = min(h1_p.shape[0], h1_s.shape[0], ROUTER_TILE)
    eidx, rank, gate, cnt = _router(h1_p, h1_s, w_router[l], router_bias[l], tr)
    counts = cnt[:, 0].astype(jnp.int32)
    tile_of, exp_of, n_steps, offsets = _expert_schedule(counts, n * TOP_K, EXPERT_TILE)
    pos = _positions(eidx, rank, offsets[:-1], min(n, POSITIONS_TILE))
    pos_c = pos.reshape(TOP_K, n // SC_CHUNK, SC_CHUNK).transpose(1, 0, 2)
    xs = _sc_dispatch(h1p_p, h1p_s, pos_c, SC_CHUNK)
    ys = _experts(xs, tile_of, exp_of, n_steps, offsets, w_exp_gate[l], w_exp_up[l],
                  w_exp_down[l], EXPERT_TILE)
    n_p = h1_p.shape[0]
    nch_p = n_p // SC_CHUNK
    td = min(n_p, h1_s.shape[0], COMBINE_TILE)
    gate_t = gate.T
    shared = (w_sh_gate[l].astype(BF16), w_sh_up[l].astype(BF16), w_sh_down[l].astype(BF16),
              ln2_g[l], ln2_b[l], td)
    yk_p = _sc_gather(ys, pos_c[:nch_p], SC_CHUNK)
    yk_s = _sc_gather(ys, pos_c[nch_p:], SC_CHUNK)
    out_p = _combine(h1_p, gate_t, 0, yk_p, *shared)
    out_s = _combine(h1_s, gate_t, n_p, yk_s, *shared)
    return (out_p.reshape(x_prompt.shape), out_s.reshape(x_sample.shape))
```

```python
import functools
import math

import jax
import jax.numpy as jnp
from jax import lax
from jax.experimental import pallas as pl
from jax.experimental.pallas import tpu as pltpu
from jax.experimental.pallas import tpu_sc as plsc

F32 = jnp.float32
BF16 = jnp.bfloat16

N_META = 16
CONV_WIDTH = 3
SSM_GROUP = 16
SSM_STATE = 64
N_EXPERTS = 256
TOP_K = 8
N_EXPERT_GROUPS = 8
TOPK_GROUPS = 4
ROUTED_SCALE = 2.5
DEPTH = 1
DEEPNORM_ALPHA = (2.0 * DEPTH) ** 0.25
LN_EPS = 1e-5
RMS_EPS = 1e-6

LANES = 128
SUBLANES = 8
N_SLABS = 4
GROUPS_PER_SLAB = LANES // SSM_GROUP
SLAB_STATE = GROUPS_PER_SLAB * SSM_STATE
EXPERT_TILE = 2048
EXPERT_SUB = 512
ROUTER_TILE = 256
COMBINE_TILE = 512
POSITIONS_TILE = 2048
SC_CHUNK = 64
VMEM_LIMIT = 48 * 1024 * 1024


def _cparams(*sem):
    return pltpu.CompilerParams(dimension_semantics=sem, vmem_limit_bytes=VMEM_LIMIT)


def _layer_norm(x, g, b):
    mu = jnp.mean(x, axis=-1, keepdims=True)
    xc = x - mu
    var = jnp.mean(xc * xc, axis=-1, keepdims=True)
    return xc * lax.rsqrt(var + LN_EPS) * g + b


def _rms_norm(x, g):
    return x * lax.rsqrt(jnp.mean(x * x, axis=-1, keepdims=True) + RMS_EPS) * g


def _pack_halves(x):
    half = x.shape[-1] // 2
    bits = pltpu.bitcast(x.astype(BF16).astype(F32), jnp.uint32)
    return (bits[:, :half] >> 16) | (bits[:, half:] & jnp.uint32(0xFFFF0000))


def _unpack_halves(p):
    lo = pltpu.bitcast(p << 16, F32).astype(BF16)
    hi = pltpu.bitcast(p & jnp.uint32(0xFFFF0000), F32).astype(BF16)
    return lo, hi


def _embed_inproj_kernel(x_ref, g_ref, b_ref, w_ref, gb_ref, u_ref, su_ref, *, bsz, tt):
    d = x_ref.shape[-1]
    x = x_ref[...].reshape(bsz * tt, d)
    h0 = _layer_norm(x, g_ref[...], b_ref[...])
    proj = jnp.dot(h0.astype(BF16), w_ref[...], preferred_element_type=F32)
    wc = gb_ref.shape[-1]
    gb_ref[...] = proj[:, :wc].reshape(bsz, tt, wc)
    u_ref[...] = (proj[:, wc:2 * wc] * proj[:, 2 * wc:3 * wc]).reshape(bsz, tt, wc)
    s_u = proj[:, 3 * wc:]
    for b in range(bsz):
        for j in range(N_SLABS):
            val = s_u[b * tt:(b + 1) * tt, j * LANES:(j + 1) * LANES]
            if bsz == 1:
                su_ref[j] = val
            else:
                su_ref[j, pl.ds(b, tt, stride=bsz), :] = val


def _embed_inproj(x, ln_g, ln_b, w_in_bf, tt):
    bsz, seq, d = x.shape
    e = w_in_bf.shape[1]
    wc = (e - N_SLABS * LANES) // 3
    nt = seq // tt
    kern = functools.partial(_embed_inproj_kernel, bsz=bsz, tt=tt)
    return pl.pallas_call(
        kern,
        grid=(nt,),
        in_specs=[
            pl.BlockSpec((bsz, tt, d), lambda i: (0, i, 0)),
            pl.BlockSpec((1, d), lambda i: (0, 0)),
            pl.BlockSpec((1, d), lambda i: (0, 0)),
            pl.BlockSpec((d, e), lambda i: (0, 0)),
        ],
        out_specs=[
            pl.BlockSpec((bsz, tt, wc), lambda i: (0, i, 0)),
            pl.BlockSpec((bsz, tt, wc), lambda i: (0, i, 0)),
            pl.BlockSpec((N_SLABS, tt * bsz, LANES), lambda i: (0, i, 0)),
        ],
        out_shape=[
            jax.ShapeDtypeStruct((bsz, seq, wc), F32),
            jax.ShapeDtypeStruct((bsz, seq, wc), F32),
            jax.ShapeDtypeStruct((N_SLABS, seq * bsz, LANES), F32),
        ],
        compiler_params=_cparams("parallel"),
        name="embed_inproj",
    )(x, ln_g.reshape(1, d), ln_b.reshape(1, d), w_in_bf)


def _s5_kernel(*refs, bsz, tc, dirs, emit_y, emit_state):
    nd = len(dirs)
    it = iter(refs)
    u_refs = [next(it) for _ in range(nd)]
    s0_ref, wb_ref, wc_ref, a_ref = next(it), next(it), next(it), next(it)
    y_refs = [next(it) for _ in range(nd)] if emit_y else []
    sf_ref = next(it) if emit_state else None
    bu_ref, st_ref = next(it), next(it)
    half = SLAB_STATE

    @pl.when(pl.program_id(0) == 0)
    def _():
        st_ref[...] = s0_ref[...]

    def project_in(j):
        for k in range(nd):
            bu_ref[k, j] = jnp.dot(u_refs[k][j].astype(BF16), wb_ref[k, j],
                                   preferred_element_type=F32)

    def project_out(j):
        for k in range(nd):
            y_refs[k][j] = jnp.dot(bu_ref[k, j].astype(BF16), wc_ref[k, j],
                                   preferred_element_type=F32)

    per_tile = SUBLANES // bsz
    n_tiles = tc // per_tile

    def cmul_add(a, s, x):
        return (a[0] * s[0] - a[1] * s[1] + x[0], a[0] * s[1] + a[1] * s[0] + x[1])

    def load(k, j, ti):
        rows = pl.ds(ti * SUBLANES, SUBLANES)
        return bu_ref[k, j, rows, :half], bu_ref[k, j, rows, half:]

    def store(k, j, ti, v):
        rows = pl.ds(ti * SUBLANES, SUBLANES)
        bu_ref[k, j, rows, :half] = v[0]
        bu_ref[k, j, rows, half:] = v[1]

    if per_tile == 1:
        def recur(j):
            for k in range(nd):
                a = (a_ref[k, j, :, :half], a_ref[k, j, :, half:])
                s = (st_ref[k, j, :, :half], st_ref[k, j, :, half:])
                for i in range(n_tiles):
                    ti = (n_tiles - 1 - i) if dirs[k] else i
                    s = cmul_add(a, s, load(k, j, ti))
                    store(k, j, ti, s)
                st_ref[k, j, :, :half] = s[0]
                st_ref[k, j, :, half:] = s[1]
    else:
        assert per_tile == 2 and tuple(dirs) == (False, True)
        low = lax.broadcasted_iota(jnp.int32, (SUBLANES, half), 0) < bsz

        def pick(p, q):
            return (jnp.where(low, p[0], q[0]), jnp.where(low, p[1], q[1]))

        def swap(v):
            return (pltpu.roll(v[0], bsz, axis=0), pltpu.roll(v[1], bsz, axis=0))

        def recur(j):
            af = (a_ref[0, j, :, :half], a_ref[0, j, :, half:])
            ar = (a_ref[1, j, :, :half], a_ref[1, j, :, half:])
            a_fr, a_rf = pick(af, ar), pick(ar, af)
            s = pick((st_ref[0, j, :, :half], st_ref[0, j, :, half:]),
                     (st_ref[1, j, :, :half], st_ref[1, j, :, half:]))
            for i in range(n_tiles):
                tf, tr = i, n_tiles - 1 - i
                xf, xr = load(0, j, tf), load(1, j, tr)
                s1 = cmul_add(a_fr, s, pick(xf, xr))
                s2 = cmul_add(a_rf, swap(s1), pick(xr, xf))
                store(0, j, tf, pick(s1, s2))
                store(1, j, tr, pick(s2, s1))
                s = swap(s2)
            for k in range(nd):
                st_ref[k, j, :, :half] = s[0]
                st_ref[k, j, :, half:] = s[1]

    project_in(0)
    for j in range(N_SLABS):
        if j + 1 < N_SLABS:
            project_in(j + 1)
        recur(j)
        if emit_y:
            project_out(j)
    if emit_state:
        sf_ref[...] = st_ref[...]


def _s5_scan(su, s0, wb, wc, a_b, *, bsz, tc, dirs, emit_y, emit_state):
    rows = su.shape[1]
    seq = rows // bsz
    nc = seq // tc
    r = tc * bsz
    nd = len(dirs)
    sw = 2 * SLAB_STATE

    def u_map(rev):
        return (lambda c: (0, nc - 1 - c, 0)) if rev else (lambda c: (0, c, 0))

    in_specs = [pl.BlockSpec((N_SLABS, r, LANES), u_map(rev)) for rev in dirs]
    in_specs += [
        pl.BlockSpec((nd, N_SLABS, SUBLANES, sw), lambda c: (0, 0, 0, 0)),
        pl.BlockSpec((nd, N_SLABS, LANES, sw), lambda c: (0, 0, 0, 0)),
        pl.BlockSpec((nd, N_SLABS, sw, LANES), lambda c: (0, 0, 0, 0)),
        pl.BlockSpec((nd, N_SLABS, SUBLANES, sw), lambda c: (0, 0, 0, 0)),
    ]
    out_specs, out_shape = [], []
    if emit_y:
        for rev in dirs:
            out_specs.append(pl.BlockSpec((N_SLABS, r, LANES), u_map(rev)))
            out_shape.append(jax.ShapeDtypeStruct((N_SLABS, rows, LANES), F32))
    if emit_state:
        out_specs.append(pl.BlockSpec((nd, N_SLABS, SUBLANES, sw), lambda c: (0, 0, 0, 0)))
        out_shape.append(jax.ShapeDtypeStruct((nd, N_SLABS, SUBLANES, sw), F32))
    kern = functools.partial(_s5_kernel, bsz=bsz, tc=tc, dirs=dirs, emit_y=emit_y,
                             emit_state=emit_state)
    return pl.pallas_call(
        kern,
        grid=(nc,),
        in_specs=in_specs,
        out_specs=out_specs,
        out_shape=out_shape,
        scratch_shapes=[
            pltpu.VMEM((nd, N_SLABS, r, sw), F32),
            pltpu.VMEM((nd, N_SLABS, SUBLANES, sw), F32),
        ],
        compiler_params=_cparams("arbitrary"),
        name="s5_scan",
    )(*([su] * nd), s0, wb, wc, a_b)


def _s5_params(lam_re, lam_im, log_step, b_re, b_im, c_re, c_im):
    g, p = lam_re.shape
    h = b_re.shape[-1]
    dt = jnp.exp(log_step.astype(F32))[:, None]
    mag = jnp.exp(lam_re * dt)
    ab_re = mag * jnp.cos(lam_im * dt)
    ab_im = mag * jnp.sin(lam_im * dt)
    den = lam_re * lam_re + lam_im * lam_im
    nr, ni = ab_re - 1.0, ab_im
    f_re = (nr * lam_re + ni * lam_im) / den
    f_im = (ni * lam_re - nr * lam_im) / den
    bb_re = f_re[..., None] * b_re - f_im[..., None] * b_im
    bb_im = f_re[..., None] * b_im + f_im[..., None] * b_re
    ns, gl = N_SLABS, GROUPS_PER_SLAB
    eye = jnp.eye(gl, dtype=F32)

    def in_block(bb):
        bb = bb.reshape(ns, gl, p, h)
        return jnp.einsum('sgph,gk->sghkp', bb, eye).reshape(ns, gl * h, gl * p)

    def out_block(cc):
        cc = cc.reshape(ns, gl, h, p)
        return jnp.einsum('sghp,gk->sgpkh', cc, eye).reshape(ns, gl * p, gl * h)

    wb = jnp.concatenate([in_block(bb_re), in_block(bb_im)], axis=-1)
    wc = jnp.concatenate([out_block(c_re.astype(F32)), -out_block(c_im.astype(F32))], axis=1)
    a = jnp.concatenate([ab_re.reshape(ns, gl * p), ab_im.reshape(ns, gl * p)], axis=-1)
    return wb.astype(BF16), wc.astype(BF16), a


def _mixer_tail_kernel(x_ref, ge_ref, be_ref, gb_ref, u_ref, up_ref, un_ref, um_ref, su_ref,
                       yf_ref, yr_ref, cw_ref, cb_ref, sd_ref, wg_ref, bg_ref, na_ref, nb_ref,
                       wo_ref, g1_ref, b1_ref, h1_ref, h1p_ref, *, bsz, tt):
    i = pl.program_id(0)
    nt = pl.num_programs(0)
    d = x_ref.shape[-1]
    wcv = gb_ref.shape[-1]
    row_id = lax.broadcasted_iota(jnp.int32, (tt, wcv), 0)
    ya, ys = [], []
    for b in range(bsz):
        u = u_ref[b]
        prev_edge = jnp.where(i == 0, um_ref[...], up_ref[b, SUBLANES - 1:SUBLANES, :])
        next_edge = jnp.where(i == nt - 1, jnp.zeros((1, wcv), F32), un_ref[b, 0:1, :])
        u_prev = jnp.where(row_id == 0, prev_edge, pltpu.roll(u, 1, axis=0))
        u_next = jnp.where(row_id == tt - 1, next_edge, pltpu.roll(u, tt - 1, axis=0))
        conv = u_prev * cw_ref[0:1, :] + u * cw_ref[1:2, :] + u_next * cw_ref[2:3, :] + cb_ref[...]
        ya.append(gb_ref[b] * conv)

        def slab(ref, b=b):
            parts = []
            for j in range(N_SLABS):
                if bsz == 1:
                    parts.append(ref[j])
                else:
                    parts.append(ref[j, pl.ds(b, tt, stride=bsz), :])
            return jnp.concatenate(parts, axis=-1)

        ys.append(slab(yf_ref) + slab(yr_ref) + sd_ref[...] * slab(su_ref))
    y_a = jnp.concatenate(ya, axis=0)
    y_s = jnp.concatenate(ys, axis=0)
    z = jax.nn.gelu(y_s)
    glu = jnp.dot(z.astype(BF16), wg_ref[...], preferred_element_type=F32) + bg_ref[...]
    y_b = z * jax.nn.sigmoid(glu)
    merged = jnp.concatenate([_rms_norm(y_a, na_ref[...]), _rms_norm(y_b, nb_ref[...])], axis=-1)
    m = jnp.dot(merged.astype(BF16), wo_ref[...], preferred_element_type=F32)
    h0 = _layer_norm(x_ref[...].reshape(bsz * tt, d), ge_ref[...], be_ref[...])
    h1 = _layer_norm(DEEPNORM_ALPHA * h0 + m, g1_ref[...], b1_ref[...])
    h1_ref[...] = h1.reshape(bsz, tt, d)
    h1p_ref[...] = _pack_halves(h1).reshape(bsz, tt, d // 2)


def _mixer_tail(x, ln_emb_g, ln_emb_b, gb, u, u_meta_last, su, yf, yr, conv_w, conv_b, ssm_d,
                w_glu_bf, b_glu, norm_a_g, norm_b_g, w_out_bf, ln1_g, ln1_b, tt):
    bsz, seq, d = x.shape
    wcv = gb.shape[-1]
    ws = N_SLABS * LANES
    nt = seq // tt
    tb = tt // SUBLANES
    nb8 = seq // SUBLANES
    kern = functools.partial(_mixer_tail_kernel, bsz=bsz, tt=tt)
    row = lambda n: pl.BlockSpec((1, n), lambda i: (0, 0))
    slab_spec = pl.BlockSpec((N_SLABS, tt * bsz, LANES), lambda i: (0, i, 0))
    return pl.pallas_call(
        kern,
        grid=(nt,),
        in_specs=[
            pl.BlockSpec((bsz, tt, d), lambda i: (0, i, 0)),
            row(d), row(d),
            pl.BlockSpec((bsz, tt, wcv), lambda i: (0, i, 0)),
            pl.BlockSpec((bsz, tt, wcv), lambda i: (0, i, 0)),
            pl.BlockSpec((bsz, SUBLANES, wcv), lambda i: (0, jnp.maximum(i * tb - 1, 0), 0)),
            pl.BlockSpec((bsz, SUBLANES, wcv), lambda i: (0, jnp.minimum((i + 1) * tb, nb8 - 1), 0)),
            row(wcv),
            slab_spec, slab_spec, slab_spec,
            pl.BlockSpec((CONV_WIDTH, wcv), lambda i: (0, 0)),
            row(wcv), row(ws),
            pl.BlockSpec((ws, ws), lambda i: (0, 0)),
            row(ws), row(wcv), row(ws),
            pl.BlockSpec((wcv + ws, d), lambda i: (0, 0)),
            row(d), row(d),
        ],
        out_specs=[
            pl.BlockSpec((bsz, tt, d), lambda i: (0, i, 0)),
            pl.BlockSpec((bsz, tt, d // 2), lambda i: (0, i, 0)),
        ],
        out_shape=[
            jax.ShapeDtypeStruct((bsz, seq, d), F32),
            jax.ShapeDtypeStruct((bsz, seq, d // 2), jnp.uint32),
        ],
        compiler_params=_cparams("parallel"),
        name="mixer_tail",
    )(x, ln_emb_g.reshape(1, d), ln_emb_b.reshape(1, d), gb, u, u, u,
      u_meta_last.reshape(1, wcv), su, yf, yr, conv_w, conv_b.reshape(1, wcv),
      ssm_d.reshape(1, ws), w_glu_bf, b_glu.reshape(1, ws), norm_a_g.reshape(1, wcv),
      norm_b_g.reshape(1, ws), w_out_bf, ln1_g.reshape(1, d), ln1_b.reshape(1, d))


def _dual_row_specs(rows, width, nq_p):
    return [pl.BlockSpec((rows, width), lambda q, *_: (jnp.minimum(q, nq_p - 1), 0)),
            pl.BlockSpec((rows, width), lambda q, *_: (jnp.maximum(q - nq_p, 0), 0))]


def _router_kernel(hp_ref, hs_ref, wh_ref, bias_ref, eidx_ref, rank_ref, gate_ref, cnt_ref,
                   cnt_scr, *, tt, nq_p):
    ne = wh_ref.shape[0]
    epg = ne // N_EXPERT_GROUPS
    neg = jnp.float32(-jnp.inf)

    @pl.when(pl.program_id(0) == 0)
    def _():
        cnt_scr[...] = jnp.zeros_like(cnt_scr)

    h = jnp.where(pl.program_id(0) < nq_p, hp_ref[...], hs_ref[...])
    dn = (((1,), (1,)), ((), ()))
    logits = lax.dot_general(wh_ref[...], h.astype(BF16), dn,
                             preferred_element_type=F32)
    scores = jax.nn.sigmoid(logits)
    sel = scores + bias_ref[...]

    gi = lax.broadcasted_iota(jnp.int32, (epg, tt), 0)
    gs = []
    for g in range(N_EXPERT_GROUPS):
        x = sel[g * epg:(g + 1) * epg, :]
        m1 = jnp.max(x, axis=0, keepdims=True)
        i1 = jnp.min(jnp.where(x == m1, gi, epg), axis=0, keepdims=True)
        m2 = jnp.max(jnp.where(gi == i1, neg, x), axis=0, keepdims=True)
        gs.append(m1 + m2)
    chosen = [jnp.zeros((1, tt), F32) for _ in range(N_EXPERT_GROUPS)]
    for _ in range(TOPK_GROUPS):
        m = gs[0]
        for g in range(1, N_EXPERT_GROUPS):
            m = jnp.maximum(m, gs[g])
        found = jnp.zeros((1, tt), F32)
        for g in range(N_EXPERT_GROUPS):
            hit = jnp.where((gs[g] == m) & (found == 0.0), 1.0, 0.0)
            chosen[g] = chosen[g] + hit
            found = found + hit
            gs[g] = jnp.where(hit > 0.0, neg, gs[g])
    selm = jnp.concatenate(
        [jnp.where(chosen[g] > 0.0, sel[g * epg:(g + 1) * epg, :], neg)
         for g in range(N_EXPERT_GROUPS)], axis=0)

    ei = lax.broadcasted_iota(jnp.int32, (ne, tt), 0)
    msel = jnp.zeros((ne, tt), F32)
    idxs, gvals = [], []
    for _ in range(TOP_K):
        m = jnp.max(selm, axis=0, keepdims=True)
        idx = jnp.min(jnp.where(selm == m, ei, ne), axis=0, keepdims=True)
        hit = ei == idx
        gvals.append(jnp.sum(jnp.where(hit, scores, 0.0), axis=0, keepdims=True))
        selm = jnp.where(hit, neg, selm)
        msel = jnp.where(hit, 1.0, msel)
        idxs.append(idx)
    gsum = gvals[0]
    for k in range(1, TOP_K):
        gsum = gsum + gvals[k]
    gate_ref[...] = jnp.concatenate([gv / gsum * ROUTED_SCALE for gv in gvals], axis=0)
    eidx_ref[...] = jnp.concatenate(idxs, axis=0)

    r_i = lax.broadcasted_iota(jnp.int32, (tt, tt), 0)
    c_i = lax.broadcasted_iota(jnp.int32, (tt, tt), 1)
    upper = jnp.where(r_i < c_i, 1.0, 0.0).astype(BF16)
    rank_full = jnp.dot(msel.astype(BF16), upper, preferred_element_type=F32) + cnt_scr[...]
    ranks = [jnp.sum(jnp.where(ei == idxs[k], rank_full, 0.0), axis=0, keepdims=True)
             for k in range(TOP_K)]
    rank_ref[...] = jnp.concatenate(ranks, axis=0).astype(jnp.int32)
    cnt_scr[...] = cnt_scr[...] + jnp.sum(msel, axis=1, keepdims=True)
    cnt_ref[...] = cnt_scr[...]


def _router(h_p, h_s, w_router, router_bias, tt):
    d = h_p.shape[1]
    n = h_p.shape[0] + h_s.shape[0]
    nq_p = h_p.shape[0] // tt
    ne = w_router.shape[1]
    wh = w_router.T.astype(BF16)
    kern = functools.partial(_router_kernel, tt=tt, nq_p=nq_p)
    return pl.pallas_call(
        kern,
        grid=(n // tt,),
        in_specs=_dual_row_specs(tt, d, nq_p) + [
            pl.BlockSpec((ne, d), lambda i: (0, 0)),
            pl.BlockSpec((ne, 1), lambda i: (0, 0)),
        ],
        out_specs=[
            pl.BlockSpec((TOP_K, tt), lambda i: (0, i)),
            pl.BlockSpec((TOP_K, tt), lambda i: (0, i)),
            pl.BlockSpec((TOP_K, tt), lambda i: (0, i)),
            pl.BlockSpec((ne, 1), lambda i: (0, 0)),
        ],
        out_shape=[
            jax.ShapeDtypeStruct((TOP_K, n), jnp.int32),
            jax.ShapeDtypeStruct((TOP_K, n), jnp.int32),
            jax.ShapeDtypeStruct((TOP_K, n), F32),
            jax.ShapeDtypeStruct((ne, 1), F32),
        ],
        scratch_shapes=[pltpu.VMEM((ne, 1), F32)],
        compiler_params=_cparams("arbitrary"),
        name="router",
    )(h_p, h_s, wh, router_bias.astype(F32).reshape(ne, 1))


def _positions_kernel(eidx_ref, rank_ref, start_ref, pos_ref):
    ne = start_ref.shape[0]
    tt = eidx_ref.shape[1]
    ei = lax.broadcasted_iota(jnp.int32, (ne, tt), 0)
    start = start_ref[...]
    rows = [jnp.sum(jnp.where(ei == eidx_ref[k:k + 1, :], start, 0.0), axis=0, keepdims=True)
            for k in range(TOP_K)]
    pos_ref[...] = jnp.concatenate(rows, axis=0).astype(jnp.int32) + rank_ref[...]


def _positions(eidx, rank, start, tt):
    n = eidx.shape[1]
    ne = start.shape[0]
    return pl.pallas_call(
        _positions_kernel,
        grid=(n // tt,),
        in_specs=[
            pl.BlockSpec((TOP_K, tt), lambda i: (0, i)),
            pl.BlockSpec((TOP_K, tt), lambda i: (0, i)),
            pl.BlockSpec((ne, 1), lambda i: (0, 0)),
        ],
        out_specs=pl.BlockSpec((TOP_K, tt), lambda i: (0, i)),
        out_shape=jax.ShapeDtypeStruct((TOP_K, n), jnp.int32),
        compiler_params=_cparams("parallel"),
        name="positions",
    )(eidx, rank, start.astype(F32).reshape(ne, 1))


def _sc_workers():
    info = plsc.get_sparse_core_info()
    return info.num_cores, info.num_subcores


def _sc_dispatch(hp_p, hp_s, pos_c, chunk):
    w = hp_p.shape[1]
    n = hp_p.shape[0] + hp_s.shape[0]
    nch_p = hp_p.shape[0] // chunk
    nc, ns = _sc_workers()
    per_worker = (n // chunk) // (nc * ns)
    mesh = plsc.VectorSubcoreMesh(core_axis_name="c", subcore_axis_name="s")

    @functools.partial(
        pl.kernel, mesh=mesh,
        out_type=jax.ShapeDtypeStruct((n * TOP_K, w), hp_p.dtype),
        scratch_types=[pltpu.VMEM((TOP_K, chunk), jnp.int32),
                       pltpu.VMEM((chunk, w), hp_p.dtype),
                       pltpu.SemaphoreType.DMA],
    )
    def dispatch(hp_hbm, hs_hbm, pos_hbm, xs_hbm, idx_v, rows_v, sem):
        wid = lax.axis_index("s") * nc + lax.axis_index("c")

        @pl.loop(0, per_worker)
        def _(ci):
            c = wid * per_worker + ci

            @pl.when(c < nch_p)
            def _():
                pltpu.sync_copy(hp_hbm.at[pl.ds(pl.multiple_of(c * chunk, chunk), chunk)], rows_v)

            @pl.when(c >= nch_p)
            def _():
                pltpu.sync_copy(
                    hs_hbm.at[pl.ds(pl.multiple_of((c - nch_p) * chunk, chunk), chunk)], rows_v)

            pltpu.sync_copy(pos_hbm.at[c], idx_v)
            copies = [pltpu.async_copy(rows_v, xs_hbm.at[idx_v.at[k]], sem) for k in range(TOP_K)]
            for cp in copies:
                cp.wait()

    return dispatch(hp_p, hp_s, pos_c)


def _sc_gather(ys, pos_c, chunk):
    w = ys.shape[1]
    n = pos_c.shape[0] * chunk
    nc, ns = _sc_workers()
    per_worker = pos_c.shape[0] // (nc * ns)
    mesh = plsc.VectorSubcoreMesh(core_axis_name="c", subcore_axis_name="s")
    nbuf = 3

    @functools.partial(
        pl.kernel, mesh=mesh,
        out_type=jax.ShapeDtypeStruct((TOP_K, n, w), ys.dtype),
        scratch_types=[pltpu.VMEM((TOP_K, chunk), jnp.int32),
                       pltpu.VMEM((nbuf, chunk, w), ys.dtype),
                       pltpu.SemaphoreType.DMA((nbuf,)),
                       pltpu.SemaphoreType.DMA((nbuf,))],
    )
    def gather(ys_hbm, pos_hbm, out_hbm, idx_v, rows_v, gsem, wsem):
        wid = lax.axis_index("s") * nc + lax.axis_index("c")

        @pl.loop(0, per_worker)
        def _(ci):
            c = wid * per_worker + ci
            off = pl.multiple_of(c * chunk, chunk)
            pltpu.sync_copy(pos_hbm.at[c], idx_v)

            def start_gather(k):
                b = k % nbuf
                return pltpu.async_copy(ys_hbm.at[idx_v.at[k]], rows_v.at[b], gsem.at[b])

            gathers = {0: start_gather(0)}
            writes = {}
            for k in range(TOP_K):
                if k + 1 < TOP_K:
                    if k + 1 - nbuf >= 0:
                        writes.pop(k + 1 - nbuf).wait()
                    gathers[k + 1] = start_gather(k + 1)
                gathers.pop(k).wait()
                b = k % nbuf
                writes[k] = pltpu.async_copy(rows_v.at[b], out_hbm.at[k, pl.ds(off, chunk)],
                                             wsem.at[b])
            for k in sorted(writes):
                writes[k].wait()

    return gather(ys, pos_c)


def _experts_kernel(tile_ref, exp_ref, ns_ref, off_ref, xs_ref, wg_ref, wu_ref, wd_ref, ys_ref,
                    wgb, wub, wdb, acc, *, tm):
    s = pl.program_id(0)
    prev = jnp.maximum(s - 1, 0)
    e = exp_ref[s]
    t = tile_ref[s]
    half = xs_ref.shape[1]
    lo_row, hi_row = off_ref[e], off_ref[e + 1]
    base = t * tm

    @pl.when((s == 0) | (e != exp_ref[prev]))
    def _():
        wgb[...] = wg_ref[0].astype(BF16)
        wub[...] = wu_ref[0].astype(BF16)
        wdb[...] = wd_ref[0].astype(BF16)

    def ffn(rows, m, r0, masked):
        lo, hi = _unpack_halves(xs_ref[rows, :])
        g = (jnp.dot(lo, wgb[:half], preferred_element_type=F32)
             + jnp.dot(hi, wgb[half:], preferred_element_type=F32))
        u = (jnp.dot(lo, wub[:half], preferred_element_type=F32)
             + jnp.dot(hi, wub[half:], preferred_element_type=F32))
        act = (g * jax.nn.sigmoid(g)) * u
        if masked:
            row = r0 + lax.broadcasted_iota(jnp.int32, (m, 1), 0)
            act = jnp.where((row >= lo_row) & (row < hi_row), act, 0.0)
        return jnp.dot(act.astype(BF16), wdb[...], preferred_element_type=F32)

    def shared_block(rows, m, r0):
        y = ffn(rows, m, r0, True)
        opens = lo_row <= r0

        @pl.when(opens)
        def _():
            acc[rows, :] = y

        @pl.when(jnp.logical_not(opens))
        def _():
            acc[rows, :] = acc[rows, :] + y

        ys_ref[rows, :] = _pack_halves(acc[rows, :])

    live = s < ns_ref[0]
    for b in range(tm // EXPERT_SUB):
        rows = pl.ds(b * EXPERT_SUB, EXPERT_SUB)
        r0 = base + b * EXPERT_SUB
        touched = live & (lo_row < r0 + EXPERT_SUB) & (hi_row > r0)
        whole = (lo_row <= r0) & (hi_row >= r0 + EXPERT_SUB)

        @pl.when(touched & whole)
        def _(rows=rows, r0=r0):
            ys_ref[rows, :] = _pack_halves(ffn(rows, EXPERT_SUB, r0, False))

        pl.when(touched & jnp.logical_not(whole))(
            functools.partial(shared_block, rows, EXPERT_SUB, r0))


def _experts(xs, tile_of, exp_of, n_steps, offsets, w_gate, w_up, w_down, tm):
    n_rows, half = xs.shape
    ne, d, de = w_gate.shape
    s_max = tile_of.shape[0]

    def row_map(s, tile, ex, ns, off):
        return (tile[s], 0)

    def w_map(s, tile, ex, ns, off):
        return (ex[s], 0, 0)

    grid_spec = pltpu.PrefetchScalarGridSpec(
        num_scalar_prefetch=4,
        grid=(s_max,),
        in_specs=[
            pl.BlockSpec((tm, half), row_map),
            pl.BlockSpec((1, d, de), w_map),
            pl.BlockSpec((1, d, de), w_map),
            pl.BlockSpec((1, de, d), w_map),
        ],
        out_specs=pl.BlockSpec((tm, half), row_map),
        scratch_shapes=[
            pltpu.VMEM((d, de), BF16),
            pltpu.VMEM((d, de), BF16),
            pltpu.VMEM((de, d), BF16),
            pltpu.VMEM((tm, d), F32),
        ],
    )
    return pl.pallas_call(
        functools.partial(_experts_kernel, tm=tm),
        grid_spec=grid_spec,
        out_shape=jax.ShapeDtypeStruct((n_rows, half), jnp.uint32),
        compiler_params=_cparams("arbitrary"),
        name="experts",
    )(tile_of, exp_of, n_steps, offsets, xs, w_gate, w_up, w_down)


def _expert_schedule(counts, n_rows, tm):
    ne = counts.shape[0]
    s_max = n_rows // tm + ne
    off = jnp.concatenate([jnp.zeros((1,), jnp.int32), jnp.cumsum(counts)]).astype(jnp.int32)
    first_tile = off[:-1] // tm
    last_tile = (off[1:] - 1) // tm
    visits = jnp.where(counts > 0, last_tile - first_tile + 1, 0)
    cum = jnp.cumsum(visits)
    n_steps = cum[-1]
    step = jnp.minimum(jnp.arange(s_max, dtype=jnp.int32), n_steps - 1)
    exp_of = jnp.sum((cum[None, :] <= step[:, None]).astype(jnp.int32), axis=1)
    onehot = exp_of[:, None] == jnp.arange(ne, dtype=jnp.int32)[None, :]
    pick = lambda v: jnp.sum(jnp.where(onehot, v[None, :], 0), axis=1)
    tile_of = pick(first_tile) + step - pick(cum - visits)
    return (tile_of.astype(jnp.int32), exp_of.astype(jnp.int32),
            n_steps.reshape(1).astype(jnp.int32), off)


def _combine_kernel(h_ref, gate_ref, yk_ref, wsg_ref, wsu_ref, wsd_ref, g2_ref, b2_ref, out_ref,
                    *, tt):
    h = h_ref[...]
    hb = h.astype(BF16)
    g = jnp.dot(hb, wsg_ref[...], preferred_element_type=F32)
    u = jnp.dot(hb, wsu_ref[...], preferred_element_type=F32)
    act = (g * jax.nn.sigmoid(g)) * u
    f = jnp.dot(act.astype(BF16), wsd_ref[...], preferred_element_type=F32)
    gate = gate_ref[...]
    r_lo = jnp.zeros((tt, yk_ref.shape[-1]), F32)
    r_hi = jnp.zeros((tt, yk_ref.shape[-1]), F32)
    for k in range(TOP_K):
        p = yk_ref[k]
        gk = gate[:, k:k + 1]
        r_lo = r_lo + gk * pltpu.bitcast(p << 16, F32)
        r_hi = r_hi + gk * pltpu.bitcast(p & jnp.uint32(0xFFFF0000), F32)
    f = f + jnp.concatenate([r_lo, r_hi], axis=-1)
    out_ref[...] = _layer_norm(DEEPNORM_ALPHA * h + f, g2_ref[...], b2_ref[...])


def _combine(h, gate_t, gate_row0, yk, wsg_bf, wsu_bf, wsd_bf, ln2_g, ln2_b, tt):
    n, d = h.shape
    q0 = gate_row0 // tt
    ds_ = wsg_bf.shape[1]
    kern = functools.partial(_combine_kernel, tt=tt)
    return pl.pallas_call(
        kern,
        grid=(n // tt,),
        in_specs=[
            pl.BlockSpec((tt, d), lambda i: (i, 0)),
            pl.BlockSpec((tt, TOP_K), lambda i: (i + q0, 0)),
            pl.BlockSpec((TOP_K, tt, yk.shape[2]), lambda i: (0, i, 0)),
            pl.BlockSpec((d, ds_), lambda i: (0, 0)),
            pl.BlockSpec((d, ds_), lambda i: (0, 0)),
            pl.BlockSpec((ds_, d), lambda i: (0, 0)),
            pl.BlockSpec((1, d), lambda i: (0, 0)),
            pl.BlockSpec((1, d), lambda i: (0, 0)),
        ],
        out_specs=pl.BlockSpec((tt, d), lambda i: (i, 0)),
        out_shape=jax.ShapeDtypeStruct((n, d), F32),
        compiler_params=_cparams("parallel"),
        name="combine",
    )(h, gate_t, yk, wsg_bf, wsu_bf, wsd_bf, ln2_g.reshape(1, d), ln2_b.reshape(1, d))


def _pick_tile(seq, bsz, rows):
    return max(SUBLANES, min(seq, rows // bsz))


def _mixer(x, meta_state, u_meta_last, p):
    bsz, seq, _ = x.shape
    tt = _pick_tile(seq, bsz, 1024)
    gb, u, su = _embed_inproj(x, p["ln_emb_g"], p["ln_emb_b"], p["w_in_bf"], tt)
    tc = _pick_tile(seq, bsz, 256)
    s0 = jnp.stack([jnp.broadcast_to(meta_state, (N_SLABS, SUBLANES, 2 * SLAB_STATE)),
                    jnp.zeros((N_SLABS, SUBLANES, 2 * SLAB_STATE), F32)])
    a_b = jnp.broadcast_to(p["s5_a"][:, :, None, :], (2, N_SLABS, SUBLANES, 2 * SLAB_STATE))
    yf, yr = _s5_scan(su, s0, p["s5_wb"], p["s5_wc"], a_b, bsz=bsz, tc=tc, dirs=(False, True),
                      emit_y=True, emit_state=False)
    tt3 = _pick_tile(seq, bsz, 512)
    h1, h1p = _mixer_tail(x, p["ln_emb_g"], p["ln_emb_b"], gb, u, u_meta_last, su, yf, yr,
                          p["conv_w"], p["conv_b"], p["ssm_d"], p["w_glu_bf"], p["b_glu"],
                          p["norm_a_g"], p["norm_b_g"], p["w_out_bf"], p["ln1_g"], p["ln1_b"], tt3)
    d = h1.shape[-1]
    return h1.reshape(bsz * seq, d), h1p.reshape(bsz * seq, d // 2)


def kernel(x_prompt, x_sample, meta_tokens, ln_emb_g, ln_emb_b, w_in, conv_w, conv_b, ssm_lambda_re, ssm_lambda_im, ssm_log_step, ssm_b_re, ssm_b_im, ssm_c_re, ssm_c_im, ssm_d, w_glu, b_glu, norm_a_g, norm_b_g, w_out, ln1_g, ln1_b, w_router, router_bias, w_exp_gate, w_exp_up, w_exp_down, w_sh_gate, w_sh_up, w_sh_down, ln2_g, ln2_b):
    l = 0
    d = x_prompt.shape[-1]
    dirs = [_s5_params(ssm_lambda_re[l, k].astype(F32), ssm_lambda_im[l, k].astype(F32),
                       ssm_log_step[l, k], ssm_b_re[l, k].astype(F32), ssm_b_im[l, k].astype(F32),
                       ssm_c_re[l, k], ssm_c_im[l, k]) for k in range(2)]
    p = dict(
        ln_emb_g=ln_emb_g, ln_emb_b=ln_emb_b, w_in_bf=w_in[l].astype(BF16),
        conv_w=conv_w[l], conv_b=conv_b[l], ssm_d=ssm_d[l],
        w_glu_bf=w_glu[l].astype(BF16), b_glu=b_glu[l], norm_a_g=norm_a_g[l],
        norm_b_g=norm_b_g[l], w_out_bf=w_out[l].astype(BF16), ln1_g=ln1_g[l], ln1_b=ln1_b[l],
        s5_wb=jnp.stack([dirs[0][0], dirs[1][0]]), s5_wc=jnp.stack([dirs[0][1], dirs[1][1]]),
        s5_a=jnp.stack([dirs[0][2], dirs[1][2]]),
    )
    mb = SUBLANES
    xm = jnp.broadcast_to(meta_tokens.astype(F32)[None], (mb, N_META, d))
    _, u_m, su_m = _embed_inproj(xm, ln_emb_g, ln_emb_b, p["w_in_bf"], N_META)
    a_m = jnp.broadcast_to(p["s5_a"][:1, :, None, :], (1, N_SLABS, mb, 2 * SLAB_STATE))
    (st_m,) = _s5_scan(su_m, jnp.zeros((1, N_SLABS, mb, 2 * SLAB_STATE), F32), p["s5_wb"][:1],
                       p["s5_wc"][:1], a_m, bsz=mb, tc=N_META, dirs=(False,), emit_y=False,
                       emit_state=True)
    meta_state = st_m[0, :, :1, :]
    u_meta_last = u_m[0, N_META - 1]

    h1_p, h1p_p = _mixer(x_prompt, meta_state, u_meta_last, p)
    h1_s, h1p_s = _mixer(x_sample, meta_state, u_meta_last, p)
    n = h1_p.shape[0] + h1_s.shape[0]

    tr = min(h1_p.shape[0], h1_s.shape[0], ROUTER_TILE)
    eidx, rank, gate, cnt = _router(h1_p, h1_s, w_router[l], router_bias[l], tr)
    counts = cnt[:, 0].astype(jnp.int32)
    tile_of, exp_of, n_steps, offsets = _expert_schedule(counts, n * TOP_K, EXPERT_TILE)
    pos = _positions(eidx, rank, offsets[:-1], min(n, POSITIONS_TILE))
    pos_c = pos.reshape(TOP_K, n // SC_CHUNK, SC_CHUNK).transpose(1, 0, 2)
    xs = _sc_dispatch(h1p_p, h1p_s, pos_c, SC_CHUNK)
    ys = _experts(xs, tile_of, exp_of, n_steps, offsets, w_exp_gate[l], w_exp_up[l],
                  w_exp_down[l], EXPERT_TILE)
    n_p = h1_p.shape[0]
    nch_p = n_p // SC_CHUNK
    td = min(n_p, h1_s.shape[0], COMBINE_TILE)
    gate_t = gate.T
    shared = (w_sh_gate[l].astype(BF16), w_sh_up[l].astype(BF16), w_sh_down[l].astype(BF16),
              ln2_g[l], ln2_b[l], td)
    yk_p = _sc_gather(ys, pos_c[:nch_p], SC_CHUNK)
    yk_s = _sc_gather(ys, pos_c[nch_p:], SC_CHUNK)
    out_p = _combine(h1_p, gate_t, 0, yk_p, *shared)
    out_s = _combine(h1_s, gate_t, n_p, yk_s, *shared)
    return (out_p.reshape(x_prompt.shape), out_s.reshape(x_sample.shape))
```

```python
import functools

import jax
import jax.numpy as jnp
from jax import lax
from jax.experimental import pallas as pl
from jax.experimental.pallas import tpu as pltpu
from jax.experimental.pallas import tpu_sc as plsc

F32 = jnp.float32
BF16 = jnp.bfloat16

N_META = 16
CONV_WIDTH = 3
SSM_GROUP = 16
SSM_STATE = 64
N_EXPERTS = 256
TOP_K = 8
N_EXPERT_GROUPS = 8
TOPK_GROUPS = 4
ROUTED_SCALE = 2.5
DEPTH = 1
DEEPNORM_ALPHA = (2.0 * DEPTH) ** 0.25
LN_EPS = 1e-5
RMS_EPS = 1e-6

LANES = 128
SUBLANES = 8
N_SLABS = 4
GROUPS_PER_SLAB = LANES // SSM_GROUP
SLAB_STATE = GROUPS_PER_SLAB * SSM_STATE
EXPERT_TILE = 2048
EXPERT_SUB = 512
ROUTER_TILE = 256
COMBINE_TILE = 512
POSITIONS_TILE = 2048
SC_CHUNK = 64
VMEM_LIMIT = 48 * 1024 * 1024


def _cparams(*sem):
    return pltpu.CompilerParams(dimension_semantics=sem, vmem_limit_bytes=VMEM_LIMIT)


def _layer_norm(x, g, b):
    mu = jnp.mean(x, axis=-1, keepdims=True)
    xc = x - mu
    var = jnp.mean(xc * xc, axis=-1, keepdims=True)
    return xc * lax.rsqrt(var + LN_EPS) * g + b


def _rms_norm(x, g):
    return x * lax.rsqrt(jnp.mean(x * x, axis=-1, keepdims=True) + RMS_EPS) * g


def _pack_halves(x):
    half = x.shape[-1] // 2
    bits = pltpu.bitcast(x.astype(BF16).astype(F32), jnp.uint32)
    return (bits[:, :half] >> 16) | (bits[:, half:] & jnp.uint32(0xFFFF0000))


def _unpack_halves(p):
    lo = pltpu.bitcast(p << 16, F32).astype(BF16)
    hi = pltpu.bitcast(p & jnp.uint32(0xFFFF0000), F32).astype(BF16)
    return lo, hi


def _embed_inproj_kernel(x_ref, g_ref, b_ref, w_ref, gb_ref, u_ref, su_ref, *, bsz, tt):
    d = x_ref.shape[-1]
    x = x_ref[...].reshape(bsz * tt, d)
    h0 = _layer_norm(x, g_ref[...], b_ref[...])
    proj = jnp.dot(h0.astype(BF16), w_ref[...], preferred_element_type=F32)
    wc = gb_ref.shape[-1]
    gb_ref[...] = proj[:, :wc].reshape(bsz, tt, wc)
    u_ref[...] = (proj[:, wc:2 * wc] * proj[:, 2 * wc:3 * wc]).reshape(bsz, tt, wc)
    s_u = proj[:, 3 * wc:]
    for b in range(bsz):
        for j in range(N_SLABS):
            val = s_u[b * tt:(b + 1) * tt, j * LANES:(j + 1) * LANES]
            if bsz == 1:
                su_ref[j] = val
            else:
                su_ref[j, pl.ds(b, tt, stride=bsz), :] = val


def _embed_inproj(x, ln_g, ln_b, w_in_bf, tt):
    bsz, seq, d = x.shape
    e = w_in_bf.shape[1]
    wc = (e - N_SLABS * LANES) // 3
    nt = seq // tt
    kern = functools.partial(_embed_inproj_kernel, bsz=bsz, tt=tt)
    return pl.pallas_call(
        kern,
        grid=(nt,),
        in_specs=[
            pl.BlockSpec((bsz, tt, d), lambda i: (0, i, 0)),
            pl.BlockSpec((1, d), lambda i: (0, 0)),
            pl.BlockSpec((1, d), lambda i: (0, 0)),
            pl.BlockSpec((d, e), lambda i: (0, 0)),
        ],
        out_specs=[
            pl.BlockSpec((bsz, tt, wc), lambda i: (0, i, 0)),
            pl.BlockSpec((bsz, tt, wc), lambda i: (0, i, 0)),
            pl.BlockSpec((N_SLABS, tt * bsz, LANES), lambda i: (0, i, 0)),
        ],
        out_shape=[
            jax.ShapeDtypeStruct((bsz, seq, wc), F32),
            jax.ShapeDtypeStruct((bsz, seq, wc), F32),
            jax.ShapeDtypeStruct((N_SLABS, seq * bsz, LANES), F32),
        ],
        compiler_params=_cparams("parallel"),
        name="embed_inproj",
    )(x, ln_g.reshape(1, d), ln_b.reshape(1, d), w_in_bf)


def _s5_kernel(*refs, bsz, tc, dirs, emit_y, emit_state):
    nd = len(dirs)
    it = iter(refs)
    u_refs = [next(it) for _ in range(nd)]
    s0_ref, wb_ref, wc_ref, a_ref = next(it), next(it), next(it), next(it)
    y_refs = [next(it) for _ in range(nd)] if emit_y else []
    sf_ref = next(it) if emit_state else None
    bu_ref, st_ref = next(it), next(it)
    half = SLAB_STATE

    @pl.when(pl.program_id(0) == 0)
    def _():
        st_ref[...] = s0_ref[...]

    def project_in(j):
        for k in range(nd):
            bu_ref[k, j] = jnp.dot(u_refs[k][j].astype(BF16), wb_ref[k, j],
                                   preferred_element_type=F32)

    def project_out(j):
        for k in range(nd):
            y_refs[k][j] = jnp.dot(bu_ref[k, j].astype(BF16), wc_ref[k, j],
                                   preferred_element_type=F32)

    per_tile = SUBLANES // bsz
    n_tiles = tc // per_tile

    def cmul_add(a, s, x):
        return (a[0] * s[0] - a[1] * s[1] + x[0], a[0] * s[1] + a[1] * s[0] + x[1])

    def load(k, j, ti):
        rows = pl.ds(ti * SUBLANES, SUBLANES)
        return bu_ref[k, j, rows, :half], bu_ref[k, j, rows, half:]

    def store(k, j, ti, v):
        rows = pl.ds(ti * SUBLANES, SUBLANES)
        bu_ref[k, j, rows, :half] = v[0]
        bu_ref[k, j, rows, half:] = v[1]

    if per_tile == 1:
        def recur(j):
            for k in range(nd):
                a = (a_ref[k, j, :, :half], a_ref[k, j, :, half:])
                s = (st_ref[k, j, :, :half], st_ref[k, j, :, half:])
                for i in range(n_tiles):
                    ti = (n_tiles - 1 - i) if dirs[k] else i
                    s = cmul_add(a, s, load(k, j, ti))
                    store(k, j, ti, s)
                st_ref[k, j, :, :half] = s[0]
                st_ref[k, j, :, half:] = s[1]
    else:
        assert per_tile == 2 and tuple(dirs) == (False, True)
        low = lax.broadcasted_iota(jnp.int32, (SUBLANES, half), 0) < bsz

        def pick(p, q):
            return (jnp.where(low, p[0], q[0]), jnp.where(low, p[1], q[1]))

        def swap(v):
            return (pltpu.roll(v[0], bsz, axis=0), pltpu.roll(v[1], bsz, axis=0))

        def recur(j):
            af = (a_ref[0, j, :, :half], a_ref[0, j, :, half:])
            ar = (a_ref[1, j, :, :half], a_ref[1, j, :, half:])
            a_fr, a_rf = pick(af, ar), pick(ar, af)
            s = pick((st_ref[0, j, :, :half], st_ref[0, j, :, half:]),
                     (st_ref[1, j, :, :half], st_ref[1, j, :, half:]))
            for i in range(n_tiles):
                tf, tr = i, n_tiles - 1 - i
                xf, xr = load(0, j, tf), load(1, j, tr)
                s1 = cmul_add(a_fr, s, pick(xf, xr))
                s2 = cmul_add(a_rf, swap(s1), pick(xr, xf))
                store(0, j, tf, pick(s1, s2))
                store(1, j, tr, pick(s2, s1))
                s = swap(s2)
            for k in range(nd):
                st_ref[k, j, :, :half] = s[0]
                st_ref[k, j, :, half:] = s[1]

    project_in(0)
    for j in range(N_SLABS):
        if j + 1 < N_SLABS:
            project_in(j + 1)
        recur(j)
        if emit_y:
            project_out(j)
    if emit_state:
        sf_ref[...] = st_ref[...]


def _s5_scan(su, s0, wb, wc, a_b, *, bsz, tc, dirs, emit_y, emit_state):
    rows = su.shape[1]
    seq = rows // bsz
    nc = seq // tc
    r = tc * bsz
    nd = len(dirs)
    sw = 2 * SLAB_STATE

    def u_map(rev):
        return (lambda c: (0, nc - 1 - c, 0)) if rev else (lambda c: (0, c, 0))

    in_specs = [pl.BlockSpec((N_SLABS, r, LANES), u_map(rev)) for rev in dirs]
    in_specs += [
        pl.BlockSpec((nd, N_SLABS, SUBLANES, sw), lambda c: (0, 0, 0, 0)),
        pl.BlockSpec((nd, N_SLABS, LANES, sw), lambda c: (0, 0, 0, 0)),
        pl.BlockSpec((nd, N_SLABS, sw, LANES), lambda c: (0, 0, 0, 0)),
        pl.BlockSpec((nd, N_SLABS, SUBLANES, sw), lambda c: (0, 0, 0, 0)),
    ]
    out_specs, out_shape = [], []
    if emit_y:
        for rev in dirs:
            out_specs.append(pl.BlockSpec((N_SLABS, r, LANES), u_map(rev)))
            out_shape.append(jax.ShapeDtypeStruct((N_SLABS, rows, LANES), F32))
    if emit_state:
        out_specs.append(pl.BlockSpec((nd, N_SLABS, SUBLANES, sw), lambda c: (0, 0, 0, 0)))
        out_shape.append(jax.ShapeDtypeStruct((nd, N_SLABS, SUBLANES, sw), F32))
    kern = functools.partial(_s5_kernel, bsz=bsz, tc=tc, dirs=dirs, emit_y=emit_y,
                             emit_state=emit_state)
    return pl.pallas_call(
        kern,
        grid=(nc,),
        in_specs=in_specs,
        out_specs=out_specs,
        out_shape=out_shape,
        scratch_shapes=[
            pltpu.VMEM((nd, N_SLABS, r, sw), F32),
            pltpu.VMEM((nd, N_SLABS, SUBLANES, sw), F32),
        ],
        compiler_params=_cparams("arbitrary"),
        name="s5_scan",
    )(*([su] * nd), s0, wb, wc, a_b)


def _s5_params(lam_re, lam_im, log_step, b_re, b_im, c_re, c_im):
    g, p = lam_re.shape
    h = b_re.shape[-1]
    dt = jnp.exp(log_step.astype(F32))[:, None]
    mag = jnp.exp(lam_re * dt)
    ab_re = mag * jnp.cos(lam_im * dt)
    ab_im = mag * jnp.sin(lam_im * dt)
    den = lam_re * lam_re + lam_im * lam_im
    nr, ni = ab_re - 1.0, ab_im
    f_re = (nr * lam_re + ni * lam_im) / den
    f_im = (ni * lam_re - nr * lam_im) / den
    bb_re = f_re[..., None] * b_re - f_im[..., None] * b_im
    bb_im = f_re[..., None] * b_im + f_im[..., None] * b_re
    ns, gl = N_SLABS, GROUPS_PER_SLAB
    eye = jnp.eye(gl, dtype=F32)

    def in_block(bb):
        bb = bb.reshape(ns, gl, p, h)
        return jnp.einsum('sgph,gk->sghkp', bb, eye).reshape(ns, gl * h, gl * p)

    def out_block(cc):
        cc = cc.reshape(ns, gl, h, p)
        return jnp.einsum('sghp,gk->sgpkh', cc, eye).reshape(ns, gl * p, gl * h)

    wb = jnp.concatenate([in_block(bb_re), in_block(bb_im)], axis=-1)
    wc = jnp.concatenate([out_block(c_re.astype(F32)), -out_block(c_im.astype(F32))], axis=1)
    a = jnp.concatenate([ab_re.reshape(ns, gl * p), ab_im.reshape(ns, gl * p)], axis=-1)
    return wb.astype(BF16), wc.astype(BF16), a


def _mixer_tail_kernel(x_ref, ge_ref, be_ref, gb_ref, u_ref, up_ref, un_ref, um_ref, su_ref,
                       yf_ref, yr_ref, cw_ref, cb_ref, sd_ref, wg_ref, bg_ref, na_ref, nb_ref,
                       wo_ref, g1_ref, b1_ref, h1_ref, h1p_ref, *, bsz, tt):
    i = pl.program_id(0)
    nt = pl.num_programs(0)
    d = x_ref.shape[-1]
    wcv = gb_ref.shape[-1]
    row_id = lax.broadcasted_iota(jnp.int32, (tt, wcv), 0)
    ya, ys = [], []
    for b in range(bsz):
        u = u_ref[b]
        prev_edge = jnp.where(i == 0, um_ref[...], up_ref[b, SUBLANES - 1:SUBLANES, :])
        next_edge = jnp.where(i == nt - 1, jnp.zeros((1, wcv), F32), un_ref[b, 0:1, :])
        u_prev = jnp.where(row_id == 0, prev_edge, pltpu.roll(u, 1, axis=0))
        u_next = jnp.where(row_id == tt - 1, next_edge, pltpu.roll(u, tt - 1, axis=0))
        conv = u_prev * cw_ref[0:1, :] + u * cw_ref[1:2, :] + u_next * cw_ref[2:3, :] + cb_ref[...]
        ya.append(gb_ref[b] * conv)

        def slab(ref, b=b):
            parts = []
            for j in range(N_SLABS):
                if bsz == 1:
                    parts.append(ref[j])
                else:
                    parts.append(ref[j, pl.ds(b, tt, stride=bsz), :])
            return jnp.concatenate(parts, axis=-1)

        ys.append(slab(yf_ref) + slab(yr_ref) + sd_ref[...] * slab(su_ref))
    y_a = jnp.concatenate(ya, axis=0)
    y_s = jnp.concatenate(ys, axis=0)
    z = jax.nn.gelu(y_s)
    glu = jnp.dot(z.astype(BF16), wg_ref[...], preferred_element_type=F32) + bg_ref[...]
    y_b = z * jax.nn.sigmoid(glu)
    merged = jnp.concatenate([_rms_norm(y_a, na_ref[...]), _rms_norm(y_b, nb_ref[...])], axis=-1)
    m = jnp.dot(merged.astype(BF16), wo_ref[...], preferred_element_type=F32)
    h0 = _layer_norm(x_ref[...].reshape(bsz * tt, d), ge_ref[...], be_ref[...])
    h1 = _layer_norm(DEEPNORM_ALPHA * h0 + m, g1_ref[...], b1_ref[...])
    h1_ref[...] = h1.reshape(bsz, tt, d)
    h1p_ref[...] = _pack_halves(h1).reshape(bsz, tt, d // 2)


def _mixer_tail(x, ln_emb_g, ln_emb_b, gb, u, u_meta_last, su, yf, yr, conv_w, conv_b, ssm_d,
                w_glu_bf, b_glu, norm_a_g, norm_b_g, w_out_bf, ln1_g, ln1_b, tt):
    bsz, seq, d = x.shape
    wcv = gb.shape[-1]
    ws = N_SLABS * LANES
    nt = seq // tt
    tb = tt // SUBLANES
    nb8 = seq // SUBLANES
    kern = functools.partial(_mixer_tail_kernel, bsz=bsz, tt=tt)
    row = lambda n: pl.BlockSpec((1, n), lambda i: (0, 0))
    slab_spec = pl.BlockSpec((N_SLABS, tt * bsz, LANES), lambda i: (0, i, 0))
    return pl.pallas_call(
        kern,
        grid=(nt,),
        in_specs=[
            pl.BlockSpec((bsz, tt, d), lambda i: (0, i, 0)),
            row(d), row(d),
            pl.BlockSpec((bsz, tt, wcv), lambda i: (0, i, 0)),
            pl.BlockSpec((bsz, tt, wcv), lambda i: (0, i, 0)),
            pl.BlockSpec((bsz, SUBLANES, wcv), lambda i: (0, jnp.maximum(i * tb - 1, 0), 0)),
            pl.BlockSpec((bsz, SUBLANES, wcv), lambda i: (0, jnp.minimum((i + 1) * tb, nb8 - 1), 0)),
            row(wcv),
            slab_spec, slab_spec, slab_spec,
            pl.BlockSpec((CONV_WIDTH, wcv), lambda i: (0, 0)),
            row(wcv), row(ws),
            pl.BlockSpec((ws, ws), lambda i: (0, 0)),
            row(ws), row(wcv), row(ws),
            pl.BlockSpec((wcv + ws, d), lambda i: (0, 0)),
            row(d), row(d),
        ],
        out_specs=[
            pl.BlockSpec((bsz, tt, d), lambda i: (0, i, 0)),
            pl.BlockSpec((bsz, tt, d // 2), lambda i: (0, i, 0)),
        ],
        out_shape=[
            jax.ShapeDtypeStruct((bsz, seq, d), F32),
            jax.ShapeDtypeStruct((bsz, seq, d // 2), jnp.uint32),
        ],
        compiler_params=_cparams("parallel"),
        name="mixer_tail",
    )(x, ln_emb_g.reshape(1, d), ln_emb_b.reshape(1, d), gb, u, u, u,
      u_meta_last.reshape(1, wcv), su, yf, yr, conv_w, conv_b.reshape(1, wcv),
      ssm_d.reshape(1, ws), w_glu_bf, b_glu.reshape(1, ws), norm_a_g.reshape(1, wcv),
      norm_b_g.reshape(1, ws), w_out_bf, ln1_g.reshape(1, d), ln1_b.reshape(1, d))


def _dual_row_specs(rows, width, nq_p):
    return [pl.BlockSpec((rows, width), lambda q, *_: (jnp.minimum(q, nq_p - 1), 0)),
            pl.BlockSpec((rows, width), lambda q, *_: (jnp.maximum(q - nq_p, 0), 0))]


def _router_kernel(hp_ref, hs_ref, wh_ref, bias_ref, eidx_ref, rank_ref, gate_ref, cnt_ref,
                   cnt_scr, *, tt, nq_p):
    ne = wh_ref.shape[0]
    epg = ne // N_EXPERT_GROUPS
    neg = jnp.float32(-jnp.inf)

    @pl.when(pl.program_id(0) == 0)
    def _():
        cnt_scr[...] = jnp.zeros_like(cnt_scr)

    h = jnp.where(pl.program_id(0) < nq_p, hp_ref[...], hs_ref[...])
    dn = (((1,), (1,)), ((), ()))
    logits = lax.dot_general(wh_ref[...], h.astype(BF16), dn,
                             preferred_element_type=F32)
    scores = jax.nn.sigmoid(logits)
    sel = scores + bias_ref[...]

    gi = lax.broadcasted_iota(jnp.int32, (epg, tt), 0)
    gs = []
    for g in range(N_EXPERT_GROUPS):
        x = sel[g * epg:(g + 1) * epg, :]
        m1 = jnp.max(x, axis=0, keepdims=True)
        i1 = jnp.min(jnp.where(x == m1, gi, epg), axis=0, keepdims=True)
        m2 = jnp.max(jnp.where(gi == i1, neg, x), axis=0, keepdims=True)
        gs.append(m1 + m2)
    chosen = [jnp.zeros((1, tt), F32) for _ in range(N_EXPERT_GROUPS)]
    for _ in range(TOPK_GROUPS):
        m = gs[0]
        for g in range(1, N_EXPERT_GROUPS):
            m = jnp.maximum(m, gs[g])
        found = jnp.zeros((1, tt), F32)
        for g in range(N_EXPERT_GROUPS):
            hit = jnp.where((gs[g] == m) & (found == 0.0), 1.0, 0.0)
            chosen[g] = chosen[g] + hit
            found = found + hit
            gs[g] = jnp.where(hit > 0.0, neg, gs[g])
    selm = jnp.concatenate(
        [jnp.where(chosen[g] > 0.0, sel[g * epg:(g + 1) * epg, :], neg)
         for g in range(N_EXPERT_GROUPS)], axis=0)

    ei = lax.broadcasted_iota(jnp.int32, (ne, tt), 0)
    msel = jnp.zeros((ne, tt), F32)
    idxs, gvals = [], []
    for _ in range(TOP_K):
        m = jnp.max(selm, axis=0, keepdims=True)
        idx = jnp.min(jnp.where(selm == m, ei, ne), axis=0, keepdims=True)
        hit = ei == idx
        gvals.append(jnp.sum(jnp.where(hit, scores, 0.0), axis=0, keepdims=True))
        selm = jnp.where(hit, neg, selm)
        msel = jnp.where(hit, 1.0, msel)
        idxs.append(idx)
    gsum = gvals[0]
    for k in range(1, TOP_K):
        gsum = gsum + gvals[k]
    gate_ref[...] = jnp.concatenate([gv / gsum * ROUTED_SCALE for gv in gvals], axis=0)
    eidx_ref[...] = jnp.concatenate(idxs, axis=0)

    r_i = lax.broadcasted_iota(jnp.int32, (tt, tt), 0)
    c_i = lax.broadcasted_iota(jnp.int32, (tt, tt), 1)
    upper = jnp.where(r_i < c_i, 1.0, 0.0).astype(BF16)
    rank_full = jnp.dot(msel.astype(BF16), upper, preferred_element_type=F32) + cnt_scr[...]
    ranks = [jnp.sum(jnp.where(ei == idxs[k], rank_full, 0.0), axis=0, keepdims=True)
             for k in range(TOP_K)]
    rank_ref[...] = jnp.concatenate(ranks, axis=0).astype(jnp.int32)
    cnt_scr[...] = cnt_scr[...] + jnp.sum(msel, axis=1, keepdims=True)
    cnt_ref[...] = cnt_scr[...]


def _router(h_p, h_s, w_router, router_bias, tt):
    d = h_p.shape[1]
    n = h_p.shape[0] + h_s.shape[0]
    nq_p = h_p.shape[0] // tt
    ne = w_router.shape[1]
    wh = w_router.T.astype(BF16)
    kern = functools.partial(_router_kernel, tt=tt, nq_p=nq_p)
    return pl.pallas_call(
        kern,
        grid=(n // tt,),
        in_specs=_dual_row_specs(tt, d, nq_p) + [
            pl.BlockSpec((ne, d), lambda i: (0, 0)),
            pl.BlockSpec((ne, 1), lambda i: (0, 0)),
        ],
        out_specs=[
            pl.BlockSpec((TOP_K, tt), lambda i: (0, i)),
            pl.BlockSpec((TOP_K, tt), lambda i: (0, i)),
            pl.BlockSpec((TOP_K, tt), lambda i: (0, i)),
            pl.BlockSpec((ne, 1), lambda i: (0, 0)),
        ],
        out_shape=[
            jax.ShapeDtypeStruct((TOP_K, n), jnp.int32),
            jax.ShapeDtypeStruct((TOP_K, n), jnp.int32),
            jax.ShapeDtypeStruct((TOP_K, n), F32),
            jax.ShapeDtypeStruct((ne, 1), F32),
        ],
        scratch_shapes=[pltpu.VMEM((ne, 1), F32)],
        compiler_params=_cparams("arbitrary"),
        name="router",
    )(h_p, h_s, wh, router_bias.astype(F32).reshape(ne, 1))


def _positions_kernel(eidx_ref, rank_ref, start_ref, pos_ref):
    ne = start_ref.shape[0]
    tt = eidx_ref.shape[1]
    ei = lax.broadcasted_iota(jnp.int32, (ne, tt), 0)
    start = start_ref[...]
    rows = [jnp.sum(jnp.where(ei == eidx_ref[k:k + 1, :], start, 0.0), axis=0, keepdims=True)
            for k in range(TOP_K)]
    pos_ref[...] = jnp.concatenate(rows, axis=0).astype(jnp.int32) + rank_ref[...]


def _positions(eidx, rank, start, tt):
    n = eidx.shape[1]
    ne = start.shape[0]
    return pl.pallas_call(
        _positions_kernel,
        grid=(n // tt,),
        in_specs=[
            pl.BlockSpec((TOP_K, tt), lambda i: (0, i)),
            pl.BlockSpec((TOP_K, tt), lambda i: (0, i)),
            pl.BlockSpec((ne, 1), lambda i: (0, 0)),
        ],
        out_specs=pl.BlockSpec((TOP_K, tt), lambda i: (0, i)),
        out_shape=jax.ShapeDtypeStruct((TOP_K, n), jnp.int32),
        compiler_params=_cparams("parallel"),
        name="positions",
    )(eidx, rank, start.astype(F32).reshape(ne, 1))


def _sc_workers():
    info = plsc.get_sparse_core_info()
    return info.num_cores, info.num_subcores


def _sc_dispatch(hp_p, hp_s, pos_c, chunk):
    w = hp_p.shape[1]
    n = hp_p.shape[0] + hp_s.shape[0]
    nch_p = hp_p.shape[0] // chunk
    nc, ns = _sc_workers()
    per_worker = (n // chunk) // (nc * ns)
    mesh = plsc.VectorSubcoreMesh(core_axis_name="c", subcore_axis_name="s")

    @functools.partial(
        pl.kernel, mesh=mesh,
        out_type=jax.ShapeDtypeStruct((n * TOP_K, w), hp_p.dtype),
        scratch_types=[pltpu.VMEM((TOP_K, chunk), jnp.int32),
                       pltpu.VMEM((chunk, w), hp_p.dtype),
                       pltpu.SemaphoreType.DMA],
    )
    def dispatch(hp_hbm, hs_hbm, pos_hbm, xs_hbm, idx_v, rows_v, sem):
        wid = lax.axis_index("s") * nc + lax.axis_index("c")

        @pl.loop(0, per_worker)
        def _(ci):
            c = wid * per_worker + ci

            @pl.when(c < nch_p)
            def _():
                pltpu.sync_copy(hp_hbm.at[pl.ds(pl.multiple_of(c * chunk, chunk), chunk)], rows_v)

            @pl.when(c >= nch_p)
            def _():
                pltpu.sync_copy(
                    hs_hbm.at[pl.ds(pl.multiple_of((c - nch_p) * chunk, chunk), chunk)], rows_v)

            pltpu.sync_copy(pos_hbm.at[c], idx_v)
            copies = [pltpu.async_copy(rows_v, xs_hbm.at[idx_v.at[k]], sem) for k in range(TOP_K)]
            for cp in copies:
                cp.wait()

    return dispatch(hp_p, hp_s, pos_c)


def _sc_gather(ys, pos_c, chunk):
    w = ys.shape[1]
    n = pos_c.shape[0] * chunk
    nc, ns = _sc_workers()
    per_worker = pos_c.shape[0] // (nc * ns)
    mesh = plsc.VectorSubcoreMesh(core_axis_name="c", subcore_axis_name="s")
    nbuf = 3

    @functools.partial(
        pl.kernel, mesh=mesh,
        out_type=jax.ShapeDtypeStruct((TOP_K, n, w), ys.dtype),
        scratch_types=[pltpu.VMEM((TOP_K, chunk), jnp.int32),
                       pltpu.VMEM((nbuf, chunk, w), ys.dtype),
                       pltpu.SemaphoreType.DMA((nbuf,)),
                       pltpu.SemaphoreType.DMA((nbuf,))],
    )
    def gather(ys_hbm, pos_hbm, out_hbm, idx_v, rows_v, gsem, wsem):
        wid = lax.axis_index("s") * nc + lax.axis_index("c")

        @pl.loop(0, per_worker)
        def _(ci):
            c = wid * per_worker + ci
            off = pl.multiple_of(c * chunk, chunk)
            pltpu.sync_copy(pos_hbm.at[c], idx_v)

            def start_gather(k):
                b = k % nbuf
                return pltpu.async_copy(ys_hbm.at[idx_v.at[k]], rows_v.at[b], gsem.at[b])

            gathers = {0: start_gather(0)}
            writes = {}
            for k in range(TOP_K):
                if k + 1 < TOP_K:
                    if k + 1 - nbuf >= 0:
                        writes.pop(k + 1 - nbuf).wait()
                    gathers[k + 1] = start_gather(k + 1)
                gathers.pop(k).wait()
                b = k % nbuf
                writes[k] = pltpu.async_copy(rows_v.at[b], out_hbm.at[k, pl.ds(off, chunk)],
                                             wsem.at[b])
            for k in sorted(writes):
                writes[k].wait()

    return gather(ys, pos_c)


def _experts_kernel(tile_ref, exp_ref, ns_ref, off_ref, chg_ref, nxt_ref, slot_ref, xs_ref,
                    wg_hbm, wu_hbm, wd_hbm, ys_ref, wgb, wub, wdb, acc, wgf, wuf, wdf, wsem, *, tm):
    s = pl.program_id(0)
    e = exp_ref[s]
    t = tile_ref[s]
    half = xs_ref.shape[1]
    lo_row, hi_row = off_ref[e], off_ref[e + 1]
    base = t * tm

    def weight_copies(expert, slot):
        return (pltpu.make_async_copy(wg_hbm.at[expert], wgf.at[slot], wsem.at[slot, 0]),
                pltpu.make_async_copy(wu_hbm.at[expert], wuf.at[slot], wsem.at[slot, 1]),
                pltpu.make_async_copy(wd_hbm.at[expert], wdf.at[slot], wsem.at[slot, 2]))

    @pl.when(s == 0)
    def _():
        for cp in weight_copies(e, slot_ref[0]):
            cp.start()

    @pl.when(chg_ref[s] == 1)
    def _():
        slot = slot_ref[s]
        for cp in weight_copies(e, slot):
            cp.wait()
        wgb[...] = wgf[slot].astype(BF16)
        wub[...] = wuf[slot].astype(BF16)
        wdb[...] = wdf[slot].astype(BF16)

        @pl.when(nxt_ref[s] != e)
        def _():
            for cp in weight_copies(nxt_ref[s], 1 - slot):
                cp.start()

    def ffn(rows, m, r0, masked):
        lo, hi = _unpack_halves(xs_ref[rows, :])
        g = (jnp.dot(lo, wgb[:half], preferred_element_type=F32)
             + jnp.dot(hi, wgb[half:], preferred_element_type=F32))
        u = (jnp.dot(lo, wub[:half], preferred_element_type=F32)
             + jnp.dot(hi, wub[half:], preferred_element_type=F32))
        act = (g * jax.nn.sigmoid(g)) * u
        if masked:
            row = r0 + lax.broadcasted_iota(jnp.int32, (m, 1), 0)
            act = jnp.where((row >= lo_row) & (row < hi_row), act, 0.0)
        return jnp.dot(act.astype(BF16), wdb[...], preferred_element_type=F32)

    def shared_block(rows, m, r0):
        y = ffn(rows, m, r0, True)
        opens = lo_row <= r0

        @pl.when(opens)
        def _():
            acc[rows, :] = y

        @pl.when(jnp.logical_not(opens))
        def _():
            acc[rows, :] = acc[rows, :] + y

        ys_ref[rows, :] = _pack_halves(acc[rows, :])

    live = s < ns_ref[0]
    for b in range(tm // EXPERT_SUB):
        rows = pl.ds(b * EXPERT_SUB, EXPERT_SUB)
        r0 = base + b * EXPERT_SUB
        touched = live & (lo_row < r0 + EXPERT_SUB) & (hi_row > r0)
        whole = (lo_row <= r0) & (hi_row >= r0 + EXPERT_SUB)

        @pl.when(touched & whole)
        def _(rows=rows, r0=r0):
            ys_ref[rows, :] = _pack_halves(ffn(rows, EXPERT_SUB, r0, False))

        pl.when(touched & jnp.logical_not(whole))(
            functools.partial(shared_block, rows, EXPERT_SUB, r0))


def _experts(xs, sched, w_gate, w_up, w_down, tm):
    n_rows, half = xs.shape
    ne, d, de = w_gate.shape
    s_max = sched[0].shape[0]

    def row_map(s, tile, *_):
        return (tile[s], 0)

    grid_spec = pltpu.PrefetchScalarGridSpec(
        num_scalar_prefetch=len(sched),
        grid=(s_max,),
        in_specs=[
            pl.BlockSpec((tm, half), row_map),
            pl.BlockSpec(memory_space=pl.ANY),
            pl.BlockSpec(memory_space=pl.ANY),
            pl.BlockSpec(memory_space=pl.ANY),
        ],
        out_specs=pl.BlockSpec((tm, half), row_map),
        scratch_shapes=[
            pltpu.VMEM((d, de), BF16),
            pltpu.VMEM((d, de), BF16),
            pltpu.VMEM((de, d), BF16),
            pltpu.VMEM((tm, d), F32),
            pltpu.VMEM((2, d, de), F32),
            pltpu.VMEM((2, d, de), F32),
            pltpu.VMEM((2, de, d), F32),
            pltpu.SemaphoreType.DMA((2, 3)),
        ],
    )
    return pl.pallas_call(
        functools.partial(_experts_kernel, tm=tm),
        grid_spec=grid_spec,
        out_shape=jax.ShapeDtypeStruct((n_rows, half), jnp.uint32),
        compiler_params=_cparams("arbitrary"),
        name="experts",
    )(*sched, xs, w_gate, w_up, w_down)


def _expert_schedule(counts, n_rows, tm):
    ne = counts.shape[0]
    s_max = n_rows // tm + ne
    off = jnp.concatenate([jnp.zeros((1,), jnp.int32), jnp.cumsum(counts)]).astype(jnp.int32)
    first_tile = off[:-1] // tm
    last_tile = (off[1:] - 1) // tm
    visits = jnp.where(counts > 0, last_tile - first_tile + 1, 0)
    cum = jnp.cumsum(visits)
    n_steps = cum[-1]
    step = jnp.minimum(jnp.arange(s_max, dtype=jnp.int32), n_steps - 1)
    exp_of = jnp.sum((cum[None, :] <= step[:, None]).astype(jnp.int32), axis=1)
    onehot = exp_of[:, None] == jnp.arange(ne, dtype=jnp.int32)[None, :]
    pick = lambda v: jnp.sum(jnp.where(onehot, v[None, :], 0), axis=1)
    tile_of = pick(first_tile) + step - pick(cum - visits)
    first_visit = jnp.concatenate([jnp.ones((1,), jnp.int32),
                                   (exp_of[1:] != exp_of[:-1]).astype(jnp.int32)])
    ids = jnp.arange(ne, dtype=jnp.int32)
    later = (ids[None, :] > ids[:, None]) & (counts[None, :] > 0)
    next_e = jnp.min(jnp.where(later, ids[None, :], ne), axis=1)
    next_e = jnp.where(next_e == ne, ids, next_e)
    ordinal = jnp.cumsum((counts > 0).astype(jnp.int32)) - 1
    i32 = lambda v: v.astype(jnp.int32)
    return (i32(tile_of), i32(exp_of), i32(n_steps.reshape(1)), off, first_visit,
            i32(pick(next_e)), i32(pick(ordinal) % 2))


def _combine_kernel(h_ref, gate_ref, yk_ref, wsg_ref, wsu_ref, wsd_ref, g2_ref, b2_ref, out_ref,
                    *, tt):
    h = h_ref[...]
    hb = h.astype(BF16)
    g = jnp.dot(hb, wsg_ref[...], preferred_element_type=F32)
    u = jnp.dot(hb, wsu_ref[...], preferred_element_type=F32)
    act = (g * jax.nn.sigmoid(g)) * u
    f = jnp.dot(act.astype(BF16), wsd_ref[...], preferred_element_type=F32)
    gate = gate_ref[...]
    r_lo = jnp.zeros((tt, yk_ref.shape[-1]), F32)
    r_hi = jnp.zeros((tt, yk_ref.shape[-1]), F32)
    for k in range(TOP_K):
        p = yk_ref[k]
        gk = gate[:, k:k + 1]
        r_lo = r_lo + gk * pltpu.bitcast(p << 16, F32)
        r_hi = r_hi + gk * pltpu.bitcast(p & jnp.uint32(0xFFFF0000), F32)
    f = f + jnp.concatenate([r_lo, r_hi], axis=-1)
    out_ref[...] = _layer_norm(DEEPNORM_ALPHA * h + f, g2_ref[...], b2_ref[...])


def _combine(h, gate_t, gate_row0, yk, wsg_bf, wsu_bf, wsd_bf, ln2_g, ln2_b, tt):
    n, d = h.shape
    q0 = gate_row0 // tt
    ds_ = wsg_bf.shape[1]
    kern = functools.partial(_combine_kernel, tt=tt)
    return pl.pallas_call(
        kern,
        grid=(n // tt,),
        in_specs=[
            pl.BlockSpec((tt, d), lambda i: (i, 0)),
            pl.BlockSpec((tt, TOP_K), lambda i: (i + q0, 0)),
            pl.BlockSpec((TOP_K, tt, yk.shape[2]), lambda i: (0, i, 0)),
            pl.BlockSpec((d, ds_), lambda i: (0, 0)),
            pl.BlockSpec((d, ds_), lambda i: (0, 0)),
            pl.BlockSpec((ds_, d), lambda i: (0, 0)),
            pl.BlockSpec((1, d), lambda i: (0, 0)),
            pl.BlockSpec((1, d), lambda i: (0, 0)),
        ],
        out_specs=pl.BlockSpec((tt, d), lambda i: (i, 0)),
        out_shape=jax.ShapeDtypeStruct((n, d), F32),
        compiler_params=_cparams("parallel"),
        name="combine",
    )(h, gate_t, yk, wsg_bf, wsu_bf, wsd_bf, ln2_g.reshape(1, d), ln2_b.reshape(1, d))


def _pick_tile(seq, bsz, rows):
    return max(SUBLANES, min(seq, rows // bsz))


def _mixer(x, meta_state, u_meta_last, p):
    bsz, seq, _ = x.shape
    tt = _pick_tile(seq, bsz, 1024)
    gb, u, su = _embed_inproj(x, p["ln_emb_g"], p["ln_emb_b"], p["w_in_bf"], tt)
    tc = _pick_tile(seq, bsz, 256)
    s0 = jnp.stack([jnp.broadcast_to(meta_state, (N_SLABS, SUBLANES, 2 * SLAB_STATE)),
                    jnp.zeros((N_SLABS, SUBLANES, 2 * SLAB_STATE), F32)])
    a_b = jnp.broadcast_to(p["s5_a"][:, :, None, :], (2, N_SLABS, SUBLANES, 2 * SLAB_STATE))
    yf, yr = _s5_scan(su, s0, p["s5_wb"], p["s5_wc"], a_b, bsz=bsz, tc=tc, dirs=(False, True),
                      emit_y=True, emit_state=False)
    tt3 = _pick_tile(seq, bsz, 512)
    h1, h1p = _mixer_tail(x, p["ln_emb_g"], p["ln_emb_b"], gb, u, u_meta_last, su, yf, yr,
                          p["conv_w"], p["conv_b"], p["ssm_d"], p["w_glu_bf"], p["b_glu"],
                          p["norm_a_g"], p["norm_b_g"], p["w_out_bf"], p["ln1_g"], p["ln1_b"], tt3)
    d = h1.shape[-1]
    return h1.reshape(bsz * seq, d), h1p.reshape(bsz * seq, d // 2)


def kernel(x_prompt, x_sample, meta_tokens, ln_emb_g, ln_emb_b, w_in, conv_w, conv_b, ssm_lambda_re, ssm_lambda_im, ssm_log_step, ssm_b_re, ssm_b_im, ssm_c_re, ssm_c_im, ssm_d, w_glu, b_glu, norm_a_g, norm_b_g, w_out, ln1_g, ln1_b, w_router, router_bias, w_exp_gate, w_exp_up, w_exp_down, w_sh_gate, w_sh_up, w_sh_down, ln2_g, ln2_b):
    l = 0
    d = x_prompt.shape[-1]
    dirs = [_s5_params(ssm_lambda_re[l, k].astype(F32), ssm_lambda_im[l, k].astype(F32),
                       ssm_log_step[l, k], ssm_b_re[l, k].astype(F32), ssm_b_im[l, k].astype(F32),
                       ssm_c_re[l, k], ssm_c_im[l, k]) for k in range(2)]
    p = dict(
        ln_emb_g=ln_emb_g, ln_emb_b=ln_emb_b, w_in_bf=w_in[l].astype(BF16),
        conv_w=conv_w[l], conv_b=conv_b[l], ssm_d=ssm_d[l],
        w_glu_bf=w_glu[l].astype(BF16), b_glu=b_glu[l], norm_a_g=norm_a_g[l],
        norm_b_g=norm_b_g[l], w_out_bf=w_out[l].astype(BF16), ln1_g=ln1_g[l], ln1_b=ln1_b[l],
        s5_wb=jnp.stack([dirs[0][0], dirs[1][0]]), s5_wc=jnp.stack([dirs[0][1], dirs[1][1]]),
        s5_a=jnp.stack([dirs[0][2], dirs[1][2]]),
    )
    mb = SUBLANES
    xm = jnp.broadcast_to(meta_tokens.astype(F32)[None], (mb, N_META, d))
    _, u_m, su_m = _embed_inproj(xm, ln_emb_g, ln_emb_b, p["w_in_bf"], N_META)
    a_m = jnp.broadcast_to(p["s5_a"][:1, :, None, :], (1, N_SLABS, mb, 2 * SLAB_STATE))
    (st_m,) = _s5_scan(su_m, jnp.zeros((1, N_SLABS, mb, 2 * SLAB_STATE), F32), p["s5_wb"][:1],
                       p["s5_wc"][:1], a_m, bsz=mb, tc=N_META, dirs=(False,), emit_y=False,
                       emit_state=True)
    meta_state = st_m[0, :, :1, :]
    u_meta_last = u_m[0, N_META - 1]

    h1_p, h1p_p = _mixer(x_prompt, meta_state, u_meta_last, p)
    h1_s, h1p_s = _mixer(x_sample, meta_state, u_meta_last, p)
    n = h1_p.shape[0] + h1_s.shape[0]

    tr = min(h1_p.shape[0], h1_s.shape[0], ROUTER_TILE)
    eidx, rank, gate, cnt = _router(h1_p, h1_s, w_router[l], router_bias[l], tr)
    counts = cnt[:, 0].astype(jnp.int32)
    sched = _expert_schedule(counts, n * TOP_K, EXPERT_TILE)
    offsets = sched[3]
    pos = _positions(eidx, rank, offsets[:-1], min(n, POSITIONS_TILE))
    pos_c = pos.reshape(TOP_K, n // SC_CHUNK, SC_CHUNK).transpose(1, 0, 2)
    xs = _sc_dispatch(h1p_p, h1p_s, pos_c, SC_CHUNK)
    ys = _experts(xs, sched, w_exp_gate[l], w_exp_up[l], w_exp_down[l], EXPERT_TILE)
    n_p = h1_p.shape[0]
    nch_p = n_p // SC_CHUNK
    td = min(n_p, h1_s.shape[0], COMBINE_TILE)
    gate_t = gate.T
    shared = (w_sh_gate[l].astype(BF16), w_sh_up[l].astype(BF16), w_sh_down[l].astype(BF16),
              ln2_g[l], ln2_b[l], td)
    yk_p = _sc_gather(ys, pos_c[:nch_p], SC_CHUNK)
    yk_s = _sc_gather(ys, pos_c[nch_p:], SC_CHUNK)
    out_p = _combine(h1_p, gate_t, 0, yk_p, *shared)
    out_s = _combine(h1_s, gate_t, n_p, yk_s, *shared)
    return (out_p.reshape(x_prompt.shape), out_s.reshape(x_sample.shape))
```

```python
import functools

import jax
import jax.numpy as jnp
from jax import lax
from jax.experimental import pallas as pl
from jax.experimental.pallas import tpu as pltpu
from jax.experimental.pallas import tpu_sc as plsc

F32 = jnp.float32
BF16 = jnp.bfloat16

N_META = 16
CONV_WIDTH = 3
SSM_GROUP = 16
SSM_STATE = 64
N_EXPERTS = 256
TOP_K = 8
N_EXPERT_GROUPS = 8
TOPK_GROUPS = 4
ROUTED_SCALE = 2.5
DEPTH = 1
DEEPNORM_ALPHA = (2.0 * DEPTH) ** 0.25
LN_EPS = 1e-5
RMS_EPS = 1e-6

LANES = 128
SUBLANES = 8
N_SLABS = 4
GROUPS_PER_SLAB = LANES // SSM_GROUP
SLAB_STATE = GROUPS_PER_SLAB * SSM_STATE
EXPERT_TILE = 2048
EXPERT_SUB = 512
ROUTER_TILE = 256
COMBINE_TILE = 512
POSITIONS_TILE = 2048
SC_CHUNK = 64
VMEM_LIMIT = 48 * 1024 * 1024


def _cparams(*sem):
    return pltpu.CompilerParams(dimension_semantics=sem, vmem_limit_bytes=VMEM_LIMIT)


def _layer_norm(x, g, b):
    mu = jnp.mean(x, axis=-1, keepdims=True)
    xc = x - mu
    var = jnp.mean(xc * xc, axis=-1, keepdims=True)
    return xc * lax.rsqrt(var + LN_EPS) * g + b


def _rms_norm(x, g):
    return x * lax.rsqrt(jnp.mean(x * x, axis=-1, keepdims=True) + RMS_EPS) * g


def _pack_halves(x):
    half = x.shape[-1] // 2
    bits = pltpu.bitcast(x.astype(BF16).astype(F32), jnp.uint32)
    return (bits[:, :half] >> 16) | (bits[:, half:] & jnp.uint32(0xFFFF0000))


def _unpack_halves(p):
    lo = pltpu.bitcast(p << 16, F32).astype(BF16)
    hi = pltpu.bitcast(p & jnp.uint32(0xFFFF0000), F32).astype(BF16)
    return lo, hi


def _embed_inproj_kernel(x_ref, g_ref, b_ref, w_ref, gb_ref, u_ref, su_ref, *, bsz, tt):
    d = x_ref.shape[-1]
    x = x_ref[...].reshape(bsz * tt, d)
    h0 = _layer_norm(x, g_ref[...], b_ref[...])
    proj = jnp.dot(h0.astype(BF16), w_ref[...], preferred_element_type=F32)
    wc = gb_ref.shape[-1]
    gb_ref[...] = proj[:, :wc].reshape(bsz, tt, wc)
    u_ref[...] = (proj[:, wc:2 * wc] * proj[:, 2 * wc:3 * wc]).reshape(bsz, tt, wc)
    s_u = proj[:, 3 * wc:]
    for b in range(bsz):
        for j in range(N_SLABS):
            val = s_u[b * tt:(b + 1) * tt, j * LANES:(j + 1) * LANES]
            if bsz == 1:
                su_ref[j] = val
            else:
                su_ref[j, pl.ds(b, tt, stride=bsz), :] = val


def _embed_inproj(x, ln_g, ln_b, w_in_bf, tt):
    bsz, seq, d = x.shape
    e = w_in_bf.shape[1]
    wc = (e - N_SLABS * LANES) // 3
    nt = seq // tt
    kern = functools.partial(_embed_inproj_kernel, bsz=bsz, tt=tt)
    return pl.pallas_call(
        kern,
        grid=(nt,),
        in_specs=[
            pl.BlockSpec((bsz, tt, d), lambda i: (0, i, 0)),
            pl.BlockSpec((1, d), lambda i: (0, 0)),
            pl.BlockSpec((1, d), lambda i: (0, 0)),
            pl.BlockSpec((d, e), lambda i: (0, 0)),
        ],
        out_specs=[
            pl.BlockSpec((bsz, tt, wc), lambda i: (0, i, 0)),
            pl.BlockSpec((bsz, tt, wc), lambda i: (0, i, 0)),
            pl.BlockSpec((N_SLABS, tt * bsz, LANES), lambda i: (0, i, 0)),
        ],
        out_shape=[
            jax.ShapeDtypeStruct((bsz, seq, wc), F32),
            jax.ShapeDtypeStruct((bsz, seq, wc), F32),
            jax.ShapeDtypeStruct((N_SLABS, seq * bsz, LANES), F32),
        ],
        compiler_params=_cparams("parallel"),
        name="embed_inproj",
    )(x, ln_g.reshape(1, d), ln_b.reshape(1, d), w_in_bf)


def _s5_kernel(*refs, bsz, tc, dirs, emit_y, emit_state):
    nd = len(dirs)
    it = iter(refs)
    u_refs = [next(it) for _ in range(nd)]
    s0_ref, wb_ref, wc_ref, a_ref = next(it), next(it), next(it), next(it)
    y_refs = [next(it) for _ in range(nd)] if emit_y else []
    sf_ref = next(it) if emit_state else None
    bu_ref, st_ref = next(it), next(it)
    half = SLAB_STATE

    @pl.when(pl.program_id(0) == 0)
    def _():
        st_ref[...] = s0_ref[...]

    def project_in(j):
        for k in range(nd):
            bu_ref[k, j] = jnp.dot(u_refs[k][j].astype(BF16), wb_ref[k, j],
                                   preferred_element_type=F32)

    def project_out(j):
        for k in range(nd):
            y_refs[k][j] = jnp.dot(bu_ref[k, j].astype(BF16), wc_ref[k, j],
                                   preferred_element_type=F32)

    per_tile = SUBLANES // bsz
    n_tiles = tc // per_tile

    def cmul_add(a, s, x):
        return (a[0] * s[0] - a[1] * s[1] + x[0], a[0] * s[1] + a[1] * s[0] + x[1])

    def load(k, j, ti):
        rows = pl.ds(ti * SUBLANES, SUBLANES)
        return bu_ref[k, j, rows, :half], bu_ref[k, j, rows, half:]

    def store(k, j, ti, v):
        rows = pl.ds(ti * SUBLANES, SUBLANES)
        bu_ref[k, j, rows, :half] = v[0]
        bu_ref[k, j, rows, half:] = v[1]

    if per_tile == 1:
        def recur(j):
            for k in range(nd):
                a = (a_ref[k, j, :, :half], a_ref[k, j, :, half:])
                s = (st_ref[k, j, :, :half], st_ref[k, j, :, half:])
                for i in range(n_tiles):
                    ti = (n_tiles - 1 - i) if dirs[k] else i
                    s = cmul_add(a, s, load(k, j, ti))
                    store(k, j, ti, s)
                st_ref[k, j, :, :half] = s[0]
                st_ref[k, j, :, half:] = s[1]
    else:
        assert per_tile == 2 and tuple(dirs) == (False, True)
        low = lax.broadcasted_iota(jnp.int32, (SUBLANES, half), 0) < bsz

        def pick(p, q):
            return (jnp.where(low, p[0], q[0]), jnp.where(low, p[1], q[1]))

        def swap(v):
            return (pltpu.roll(v[0], bsz, axis=0), pltpu.roll(v[1], bsz, axis=0))

        def recur(j):
            af = (a_ref[0, j, :, :half], a_ref[0, j, :, half:])
            ar = (a_ref[1, j, :, :half], a_ref[1, j, :, half:])
            a_fr, a_rf = pick(af, ar), pick(ar, af)
            s = pick((st_ref[0, j, :, :half], st_ref[0, j, :, half:]),
                     (st_ref[1, j, :, :half], st_ref[1, j, :, half:]))
            for i in range(n_tiles):
                tf, tr = i, n_tiles - 1 - i
                xf, xr = load(0, j, tf), load(1, j, tr)
                s1 = cmul_add(a_fr, s, pick(xf, xr))
                s2 = cmul_add(a_rf, swap(s1), pick(xr, xf))
                store(0, j, tf, pick(s1, s2))
                store(1, j, tr, pick(s2, s1))
                s = swap(s2)
            for k in range(nd):
                st_ref[k, j, :, :half] = s[0]
                st_ref[k, j, :, half:] = s[1]

    project_in(0)
    for j in range(N_SLABS):
        if j + 1 < N_SLABS:
            project_in(j + 1)
        recur(j)
        if emit_y:
            project_out(j)
    if emit_state:
        sf_ref[...] = st_ref[...]


def _s5_scan(su, s0, wb, wc, a_b, *, bsz, tc, dirs, emit_y, emit_state):
    rows = su.shape[1]
    seq = rows // bsz
    nc = seq // tc
    r = tc * bsz
    nd = len(dirs)
    sw = 2 * SLAB_STATE

    def u_map(rev):
        return (lambda c: (0, nc - 1 - c, 0)) if rev else (lambda c: (0, c, 0))

    in_specs = [pl.BlockSpec((N_SLABS, r, LANES), u_map(rev)) for rev in dirs]
    in_specs += [
        pl.BlockSpec((nd, N_SLABS, SUBLANES, sw), lambda c: (0, 0, 0, 0)),
        pl.BlockSpec((nd, N_SLABS, LANES, sw), lambda c: (0, 0, 0, 0)),
        pl.BlockSpec((nd, N_SLABS, sw, LANES), lambda c: (0, 0, 0, 0)),
        pl.BlockSpec((nd, N_SLABS, SUBLANES, sw), lambda c: (0, 0, 0, 0)),
    ]
    out_specs, out_shape = [], []
    if emit_y:
        for rev in dirs:
            out_specs.append(pl.BlockSpec((N_SLABS, r, LANES), u_map(rev)))
            out_shape.append(jax.ShapeDtypeStruct((N_SLABS, rows, LANES), F32))
    if emit_state:
        out_specs.append(pl.BlockSpec((nd, N_SLABS, SUBLANES, sw), lambda c: (0, 0, 0, 0)))
        out_shape.append(jax.ShapeDtypeStruct((nd, N_SLABS, SUBLANES, sw), F32))
    kern = functools.partial(_s5_kernel, bsz=bsz, tc=tc, dirs=dirs, emit_y=emit_y,
                             emit_state=emit_state)
    return pl.pallas_call(
        kern,
        grid=(nc,),
        in_specs=in_specs,
        out_specs=out_specs,
        out_shape=out_shape,
        scratch_shapes=[
            pltpu.VMEM((nd, N_SLABS, r, sw), F32),
            pltpu.VMEM((nd, N_SLABS, SUBLANES, sw), F32),
        ],
        compiler_params=_cparams("arbitrary"),
        name="s5_scan",
    )(*([su] * nd), s0, wb, wc, a_b)


def _s5_params(lam_re, lam_im, log_step, b_re, b_im, c_re, c_im):
    g, p = lam_re.shape
    h = b_re.shape[-1]
    dt = jnp.exp(log_step.astype(F32))[:, None]
    mag = jnp.exp(lam_re * dt)
    ab_re = mag * jnp.cos(lam_im * dt)
    ab_im = mag * jnp.sin(lam_im * dt)
    den = lam_re * lam_re + lam_im * lam_im
    nr, ni = ab_re - 1.0, ab_im
    f_re = (nr * lam_re + ni * lam_im) / den
    f_im = (ni * lam_re - nr * lam_im) / den
    bb_re = f_re[..., None] * b_re - f_im[..., None] * b_im
    bb_im = f_re[..., None] * b_im + f_im[..., None] * b_re
    ns, gl = N_SLABS, GROUPS_PER_SLAB
    eye = jnp.eye(gl, dtype=F32)

    def in_block(bb):
        bb = bb.reshape(ns, gl, p, h)
        return jnp.einsum('sgph,gk->sghkp', bb, eye).reshape(ns, gl * h, gl * p)

    def out_block(cc):
        cc = cc.reshape(ns, gl, h, p)
        return jnp.einsum('sghp,gk->sgpkh', cc, eye).reshape(ns, gl * p, gl * h)

    wb = jnp.concatenate([in_block(bb_re), in_block(bb_im)], axis=-1)
    wc = jnp.concatenate([out_block(c_re.astype(F32)), -out_block(c_im.astype(F32))], axis=1)
    a = jnp.concatenate([ab_re.reshape(ns, gl * p), ab_im.reshape(ns, gl * p)], axis=-1)
    return wb.astype(BF16), wc.astype(BF16), a


def _mixer_tail_kernel(x_ref, ge_ref, be_ref, gb_ref, u_ref, up_ref, un_ref, um_ref, su_ref,
                       yf_ref, yr_ref, cw_ref, cb_ref, sd_ref, wg_ref, bg_ref, na_ref, nb_ref,
                       wo_ref, g1_ref, b1_ref, h1_ref, h1p_ref, *, bsz, tt):
    i = pl.program_id(0)
    nt = pl.num_programs(0)
    d = x_ref.shape[-1]
    wcv = gb_ref.shape[-1]
    row_id = lax.broadcasted_iota(jnp.int32, (tt, wcv), 0)
    ya, ys = [], []
    for b in range(bsz):
        u = u_ref[b]
        prev_edge = jnp.where(i == 0, um_ref[...], up_ref[b, SUBLANES - 1:SUBLANES, :])
        next_edge = jnp.where(i == nt - 1, jnp.zeros((1, wcv), F32), un_ref[b, 0:1, :])
        u_prev = jnp.where(row_id == 0, prev_edge, pltpu.roll(u, 1, axis=0))
        u_next = jnp.where(row_id == tt - 1, next_edge, pltpu.roll(u, tt - 1, axis=0))
        conv = u_prev * cw_ref[0:1, :] + u * cw_ref[1:2, :] + u_next * cw_ref[2:3, :] + cb_ref[...]
        ya.append(gb_ref[b] * conv)

        def slab(ref, b=b):
            parts = []
            for j in range(N_SLABS):
                if bsz == 1:
                    parts.append(ref[j])
                else:
                    parts.append(ref[j, pl.ds(b, tt, stride=bsz), :])
            return jnp.concatenate(parts, axis=-1)

        ys.append(slab(yf_ref) + slab(yr_ref) + sd_ref[...] * slab(su_ref))
    y_a = jnp.concatenate(ya, axis=0)
    y_s = jnp.concatenate(ys, axis=0)
    z = jax.nn.gelu(y_s)
    glu = jnp.dot(z.astype(BF16), wg_ref[...], preferred_element_type=F32) + bg_ref[...]
    y_b = z * jax.nn.sigmoid(glu)
    merged = jnp.concatenate([_rms_norm(y_a, na_ref[...]), _rms_norm(y_b, nb_ref[...])], axis=-1)
    m = jnp.dot(merged.astype(BF16), wo_ref[...], preferred_element_type=F32)
    h0 = _layer_norm(x_ref[...].reshape(bsz * tt, d), ge_ref[...], be_ref[...])
    h1 = _layer_norm(DEEPNORM_ALPHA * h0 + m, g1_ref[...], b1_ref[...])
    h1_ref[...] = h1.reshape(bsz, tt, d)
    h1p_ref[...] = _pack_halves(h1).reshape(bsz, tt, d // 2)


def _mixer_tail(x, ln_emb_g, ln_emb_b, gb, u, u_meta_last, su, yf, yr, conv_w, conv_b, ssm_d,
                w_glu_bf, b_glu, norm_a_g, norm_b_g, w_out_bf, ln1_g, ln1_b, tt):
    bsz, seq, d = x.shape
    wcv = gb.shape[-1]
    ws = N_SLABS * LANES
    nt = seq // tt
    tb = tt // SUBLANES
    nb8 = seq // SUBLANES
    kern = functools.partial(_mixer_tail_kernel, bsz=bsz, tt=tt)
    row = lambda n: pl.BlockSpec((1, n), lambda i: (0, 0))
    slab_spec = pl.BlockSpec((N_SLABS, tt * bsz, LANES), lambda i: (0, i, 0))
    return pl.pallas_call(
        kern,
        grid=(nt,),
        in_specs=[
            pl.BlockSpec((bsz, tt, d), lambda i: (0, i, 0)),
            row(d), row(d),
            pl.BlockSpec((bsz, tt, wcv), lambda i: (0, i, 0)),
            pl.BlockSpec((bsz, tt, wcv), lambda i: (0, i, 0)),
            pl.BlockSpec((bsz, SUBLANES, wcv), lambda i: (0, jnp.maximum(i * tb - 1, 0), 0)),
            pl.BlockSpec((bsz, SUBLANES, wcv), lambda i: (0, jnp.minimum((i + 1) * tb, nb8 - 1), 0)),
            row(wcv),
            slab_spec, slab_spec, slab_spec,
            pl.BlockSpec((CONV_WIDTH, wcv), lambda i: (0, 0)),
            row(wcv), row(ws),
            pl.BlockSpec((ws, ws), lambda i: (0, 0)),
            row(ws), row(wcv), row(ws),
            pl.BlockSpec((wcv + ws, d), lambda i: (0, 0)),
            row(d), row(d),
        ],
        out_specs=[
            pl.BlockSpec((bsz, tt, d), lambda i: (0, i, 0)),
            pl.BlockSpec((bsz, tt, d // 2), lambda i: (0, i, 0)),
        ],
        out_shape=[
            jax.ShapeDtypeStruct((bsz, seq, d), F32),
            jax.ShapeDtypeStruct((bsz, seq, d // 2), jnp.uint32),
        ],
        compiler_params=_cparams("parallel"),
        name="mixer_tail",
    )(x, ln_emb_g.reshape(1, d), ln_emb_b.reshape(1, d), gb, u, u, u,
      u_meta_last.reshape(1, wcv), su, yf, yr, conv_w, conv_b.reshape(1, wcv),
      ssm_d.reshape(1, ws), w_glu_bf, b_glu.reshape(1, ws), norm_a_g.reshape(1, wcv),
      norm_b_g.reshape(1, ws), w_out_bf, ln1_g.reshape(1, d), ln1_b.reshape(1, d))


def _dual_row_specs(rows, width, nq_p):
    return [pl.BlockSpec((rows, width), lambda q, *_: (jnp.minimum(q, nq_p - 1), 0)),
            pl.BlockSpec((rows, width), lambda q, *_: (jnp.maximum(q - nq_p, 0), 0))]


def _router_kernel(hp_ref, hs_ref, wh_ref, bias_ref, eidx_ref, rank_ref, gate_ref, cnt_ref,
                   cnt_scr, *, tt, nq_p):
    ne = wh_ref.shape[0]
    epg = ne // N_EXPERT_GROUPS
    neg = jnp.float32(-jnp.inf)

    @pl.when(pl.program_id(0) == 0)
    def _():
        cnt_scr[...] = jnp.zeros_like(cnt_scr)

    h = jnp.where(pl.program_id(0) < nq_p, hp_ref[...], hs_ref[...])
    dn = (((1,), (1,)), ((), ()))
    logits = lax.dot_general(wh_ref[...], h.astype(BF16), dn,
                             preferred_element_type=F32)
    scores = jax.nn.sigmoid(logits)
    sel = scores + bias_ref[...]

    gi = lax.broadcasted_iota(jnp.int32, (epg, tt), 0)
    gs = []
    for g in range(N_EXPERT_GROUPS):
        x = sel[g * epg:(g + 1) * epg, :]
        m1 = jnp.max(x, axis=0, keepdims=True)
        i1 = jnp.min(jnp.where(x == m1, gi, epg), axis=0, keepdims=True)
        m2 = jnp.max(jnp.where(gi == i1, neg, x), axis=0, keepdims=True)
        gs.append(m1 + m2)
    chosen = [jnp.zeros((1, tt), F32) for _ in range(N_EXPERT_GROUPS)]
    for _ in range(TOPK_GROUPS):
        m = gs[0]
        for g in range(1, N_EXPERT_GROUPS):
            m = jnp.maximum(m, gs[g])
        found = jnp.zeros((1, tt), F32)
        for g in range(N_EXPERT_GROUPS):
            hit = jnp.where((gs[g] == m) & (found == 0.0), 1.0, 0.0)
            chosen[g] = chosen[g] + hit
            found = found + hit
            gs[g] = jnp.where(hit > 0.0, neg, gs[g])
    selm = jnp.concatenate(
        [jnp.where(chosen[g] > 0.0, sel[g * epg:(g + 1) * epg, :], neg)
         for g in range(N_EXPERT_GROUPS)], axis=0)

    ei = lax.broadcasted_iota(jnp.int32, (ne, tt), 0)
    msel = jnp.zeros((ne, tt), F32)
    idxs, gvals = [], []
    for _ in range(TOP_K):
        m = jnp.max(selm, axis=0, keepdims=True)
        idx = jnp.min(jnp.where(selm == m, ei, ne), axis=0, keepdims=True)
        hit = ei == idx
        gvals.append(jnp.sum(jnp.where(hit, scores, 0.0), axis=0, keepdims=True))
        selm = jnp.where(hit, neg, selm)
        msel = jnp.where(hit, 1.0, msel)
        idxs.append(idx)
    gsum = gvals[0]
    for k in range(1, TOP_K):
        gsum = gsum + gvals[k]
    gate_ref[...] = jnp.concatenate([gv / gsum * ROUTED_SCALE for gv in gvals], axis=0)
    eidx_ref[...] = jnp.concatenate(idxs, axis=0)

    r_i = lax.broadcasted_iota(jnp.int32, (tt, tt), 0)
    c_i = lax.broadcasted_iota(jnp.int32, (tt, tt), 1)
    upper = jnp.where(r_i < c_i, 1.0, 0.0).astype(BF16)
    rank_full = jnp.dot(msel.astype(BF16), upper, preferred_element_type=F32) + cnt_scr[...]
    ranks = [jnp.sum(jnp.where(ei == idxs[k], rank_full, 0.0), axis=0, keepdims=True)
             for k in range(TOP_K)]
    rank_ref[...] = jnp.concatenate(ranks, axis=0).astype(jnp.int32)
    cnt_scr[...] = cnt_scr[...] + jnp.sum(msel, axis=1, keepdims=True)
    cnt_ref[...] = cnt_scr[...]


def _router(h_p, h_s, w_router, router_bias, tt):
    d = h_p.shape[1]
    n = h_p.shape[0] + h_s.shape[0]
    nq_p = h_p.shape[0] // tt
    ne = w_router.shape[1]
    wh = w_router.T.astype(BF16)
    kern = functools.partial(_router_kernel, tt=tt, nq_p=nq_p)
    return pl.pallas_call(
        kern,
        grid=(n // tt,),
        in_specs=_dual_row_specs(tt, d, nq_p) + [
            pl.BlockSpec((ne, d), lambda i: (0, 0)),
            pl.BlockSpec((ne, 1), lambda i: (0, 0)),
        ],
        out_specs=[
            pl.BlockSpec((TOP_K, tt), lambda i: (0, i)),
            pl.BlockSpec((TOP_K, tt), lambda i: (0, i)),
            pl.BlockSpec((TOP_K, tt), lambda i: (0, i)),
            pl.BlockSpec((ne, 1), lambda i: (0, 0)),
        ],
        out_shape=[
            jax.ShapeDtypeStruct((TOP_K, n), jnp.int32),
            jax.ShapeDtypeStruct((TOP_K, n), jnp.int32),
            jax.ShapeDtypeStruct((TOP_K, n), F32),
            jax.ShapeDtypeStruct((ne, 1), F32),
        ],
        scratch_shapes=[pltpu.VMEM((ne, 1), F32)],
        compiler_params=_cparams("arbitrary"),
        name="router",
    )(h_p, h_s, wh, router_bias.astype(F32).reshape(ne, 1))


def _positions_kernel(eidx_ref, rank_ref, start_ref, pos_ref):
    ne = start_ref.shape[0]
    tt = eidx_ref.shape[1]
    ei = lax.broadcasted_iota(jnp.int32, (ne, tt), 0)
    start = start_ref[...]
    rows = [jnp.sum(jnp.where(ei == eidx_ref[k:k + 1, :], start, 0.0), axis=0, keepdims=True)
            for k in range(TOP_K)]
    pos_ref[...] = jnp.concatenate(rows, axis=0).astype(jnp.int32) + rank_ref[...]


def _positions(eidx, rank, start, tt):
    n = eidx.shape[1]
    ne = start.shape[0]
    return pl.pallas_call(
        _positions_kernel,
        grid=(n // tt,),
        in_specs=[
            pl.BlockSpec((TOP_K, tt), lambda i: (0, i)),
            pl.BlockSpec((TOP_K, tt), lambda i: (0, i)),
            pl.BlockSpec((ne, 1), lambda i: (0, 0)),
        ],
        out_specs=pl.BlockSpec((TOP_K, tt), lambda i: (0, i)),
        out_shape=jax.ShapeDtypeStruct((TOP_K, n), jnp.int32),
        compiler_params=_cparams("parallel"),
        name="positions",
    )(eidx, rank, start.astype(F32).reshape(ne, 1))


def _sc_workers():
    info = plsc.get_sparse_core_info()
    return info.num_cores, info.num_subcores


def _sc_dispatch(hp_p, hp_s, pos_c, chunk):
    w = hp_p.shape[1]
    n = hp_p.shape[0] + hp_s.shape[0]
    nch_p = hp_p.shape[0] // chunk
    nc, ns = _sc_workers()
    per_worker = (n // chunk) // (nc * ns)
    mesh = plsc.VectorSubcoreMesh(core_axis_name="c", subcore_axis_name="s")

    @functools.partial(
        pl.kernel, mesh=mesh,
        out_type=jax.ShapeDtypeStruct((n * TOP_K, w), hp_p.dtype),
        scratch_types=[pltpu.VMEM((TOP_K, chunk), jnp.int32),
                       pltpu.VMEM((chunk, w), hp_p.dtype),
                       pltpu.SemaphoreType.DMA],
    )
    def dispatch(hp_hbm, hs_hbm, pos_hbm, xs_hbm, idx_v, rows_v, sem):
        wid = lax.axis_index("s") * nc + lax.axis_index("c")

        @pl.loop(0, per_worker)
        def _(ci):
            c = wid * per_worker + ci

            @pl.when(c < nch_p)
            def _():
                pltpu.sync_copy(hp_hbm.at[pl.ds(pl.multiple_of(c * chunk, chunk), chunk)], rows_v)

            @pl.when(c >= nch_p)
            def _():
                pltpu.sync_copy(
                    hs_hbm.at[pl.ds(pl.multiple_of((c - nch_p) * chunk, chunk), chunk)], rows_v)

            pltpu.sync_copy(pos_hbm.at[c], idx_v)
            copies = [pltpu.async_copy(rows_v, xs_hbm.at[idx_v.at[k]], sem) for k in range(TOP_K)]
            for cp in copies:
                cp.wait()

    return dispatch(hp_p, hp_s, pos_c)


def _sc_gather(ys, pos_c, chunk):
    w = ys.shape[1]
    n = pos_c.shape[0] * chunk
    nc, ns = _sc_workers()
    per_worker = pos_c.shape[0] // (nc * ns)
    mesh = plsc.VectorSubcoreMesh(core_axis_name="c", subcore_axis_name="s")
    nbuf = 3

    @functools.partial(
        pl.kernel, mesh=mesh,
        out_type=jax.ShapeDtypeStruct((TOP_K, n, w), ys.dtype),
        scratch_types=[pltpu.VMEM((TOP_K, chunk), jnp.int32),
                       pltpu.VMEM((nbuf, chunk, w), ys.dtype),
                       pltpu.SemaphoreType.DMA((nbuf,)),
                       pltpu.SemaphoreType.DMA((nbuf,))],
    )
    def gather(ys_hbm, pos_hbm, out_hbm, idx_v, rows_v, gsem, wsem):
        wid = lax.axis_index("s") * nc + lax.axis_index("c")

        @pl.loop(0, per_worker)
        def _(ci):
            c = wid * per_worker + ci
            off = pl.multiple_of(c * chunk, chunk)
            pltpu.sync_copy(pos_hbm.at[c], idx_v)

            def start_gather(k):
                b = k % nbuf
                return pltpu.async_copy(ys_hbm.at[idx_v.at[k]], rows_v.at[b], gsem.at[b])

            gathers = {0: start_gather(0)}
            writes = {}
            for k in range(TOP_K):
                if k + 1 < TOP_K:
                    if k + 1 - nbuf >= 0:
                        writes.pop(k + 1 - nbuf).wait()
                    gathers[k + 1] = start_gather(k + 1)
                gathers.pop(k).wait()
                b = k % nbuf
                writes[k] = pltpu.async_copy(rows_v.at[b], out_hbm.at[k, pl.ds(off, chunk)],
                                             wsem.at[b])
            for k in sorted(writes):
                writes[k].wait()

    return gather(ys, pos_c)


def _experts_kernel(fe_ref, le_ref, off_ref, nxt_ref, slot_ref, xs_ref, wg_hbm, wu_hbm, wd_hbm,
                    ys_ref, wgb, wub, wdb, acc, wgf, wuf, wdf, wsem, cur_ref, *, tm):
    t = pl.program_id(0)
    base = t * tm
    half = xs_ref.shape[1]

    @pl.when(t == 0)
    def _():
        cur_ref[0] = -1

    def weight_copies(expert, slot):
        return (pltpu.make_async_copy(wg_hbm.at[expert], wgf.at[slot], wsem.at[slot, 0]),
                pltpu.make_async_copy(wu_hbm.at[expert], wuf.at[slot], wsem.at[slot, 1]),
                pltpu.make_async_copy(wd_hbm.at[expert], wdf.at[slot], wsem.at[slot, 2]))

    def load_weights(e):
        slot = slot_ref[e]

        @pl.when(cur_ref[0] < 0)
        def _():
            for cp in weight_copies(e, slot):
                cp.start()

        for cp in weight_copies(e, slot):
            cp.wait()
        wgb[...] = wgf[slot].astype(BF16)
        wub[...] = wuf[slot].astype(BF16)
        wdb[...] = wdf[slot].astype(BF16)

        @pl.when(nxt_ref[e] != e)
        def _():
            for cp in weight_copies(nxt_ref[e], 1 - slot):
                cp.start()

        cur_ref[0] = e

    def visit(e, carry):
        lo_row, hi_row = off_ref[e], off_ref[e + 1]
        _visit_expert(e, lo_row, hi_row)
        return carry

    def ffn(rows, m, r0, masked, lo_row, hi_row):
        lo, hi = _unpack_halves(xs_ref[rows, :])
        g = (jnp.dot(lo, wgb[:half], preferred_element_type=F32)
             + jnp.dot(hi, wgb[half:], preferred_element_type=F32))
        u = (jnp.dot(lo, wub[:half], preferred_element_type=F32)
             + jnp.dot(hi, wub[half:], preferred_element_type=F32))
        act = (g * jax.nn.sigmoid(g)) * u
        if masked:
            row = r0 + lax.broadcasted_iota(jnp.int32, (m, 1), 0)
            act = jnp.where((row >= lo_row) & (row < hi_row), act, 0.0)
        return jnp.dot(act.astype(BF16), wdb[...], preferred_element_type=F32)

    def shared_block(rows, m, r0, lo_row, hi_row):
        y = ffn(rows, m, r0, True, lo_row, hi_row)
        opens = lo_row <= r0

        @pl.when(opens)
        def _():
            acc[rows, :] = y

        @pl.when(jnp.logical_not(opens))
        def _():
            acc[rows, :] = acc[rows, :] + y

        ys_ref[rows, :] = _pack_halves(acc[rows, :])

    def _visit_expert(e, lo_row, hi_row):
        @pl.when(hi_row > lo_row)
        def _():
            pl.when(cur_ref[0] != e)(functools.partial(load_weights, e))
            for b in range(tm // EXPERT_SUB):
                rows = pl.ds(b * EXPERT_SUB, EXPERT_SUB)
                r0 = base + b * EXPERT_SUB
                touched = (lo_row < r0 + EXPERT_SUB) & (hi_row > r0)
                whole = (lo_row <= r0) & (hi_row >= r0 + EXPERT_SUB)

                @pl.when(touched & whole)
                def _(rows=rows, r0=r0):
                    ys_ref[rows, :] = _pack_halves(
                        ffn(rows, EXPERT_SUB, r0, False, lo_row, hi_row))

                pl.when(touched & jnp.logical_not(whole))(
                    functools.partial(shared_block, rows, EXPERT_SUB, r0, lo_row, hi_row))

    lax.fori_loop(fe_ref[t], le_ref[t] + 1, visit, 0)


def _experts(xs, sched, w_gate, w_up, w_down, tm):
    n_rows, half = xs.shape
    ne, d, de = w_gate.shape
    assert n_rows % tm == 0 and tm % EXPERT_SUB == 0

    grid_spec = pltpu.PrefetchScalarGridSpec(
        num_scalar_prefetch=len(sched),
        grid=(n_rows // tm,),
        in_specs=[
            pl.BlockSpec((tm, half), lambda t, *_: (t, 0)),
            pl.BlockSpec(memory_space=pl.ANY),
            pl.BlockSpec(memory_space=pl.ANY),
            pl.BlockSpec(memory_space=pl.ANY),
        ],
        out_specs=pl.BlockSpec((tm, half), lambda t, *_: (t, 0)),
        scratch_shapes=[
            pltpu.VMEM((d, de), BF16),
            pltpu.VMEM((d, de), BF16),
            pltpu.VMEM((de, d), BF16),
            pltpu.VMEM((tm, d), F32),
            pltpu.VMEM((2, d, de), F32),
            pltpu.VMEM((2, d, de), F32),
            pltpu.VMEM((2, de, d), F32),
            pltpu.SemaphoreType.DMA((2, 3)),
            pltpu.SMEM((1,), jnp.int32),
        ],
    )
    return pl.pallas_call(
        functools.partial(_experts_kernel, tm=tm),
        grid_spec=grid_spec,
        out_shape=jax.ShapeDtypeStruct((n_rows, half), jnp.uint32),
        compiler_params=_cparams("arbitrary"),
        name="experts",
    )(*sched, xs, w_gate, w_up, w_down)


def _expert_schedule(counts, n_rows, tm):
    ne = counts.shape[0]
    off = jnp.concatenate([jnp.zeros((1,), jnp.int32), jnp.cumsum(counts)]).astype(jnp.int32)
    tile_lo = jnp.arange(n_rows // tm, dtype=jnp.int32) * tm
    owner = lambda row: jnp.sum((off[None, 1:] <= row[:, None]).astype(jnp.int32), axis=1)
    first_e = owner(tile_lo)
    last_e = owner(tile_lo + (tm - 1))
    ids = jnp.arange(ne, dtype=jnp.int32)
    later = (ids[None, :] > ids[:, None]) & (counts[None, :] > 0)
    next_e = jnp.min(jnp.where(later, ids[None, :], ne), axis=1)
    next_e = jnp.where(next_e == ne, ids, next_e)
    slot = (jnp.cumsum((counts > 0).astype(jnp.int32)) - 1) % 2
    i32 = lambda v: v.astype(jnp.int32)
    return (i32(first_e), i32(last_e), off, i32(next_e), i32(slot))


def _combine_kernel(h_ref, gate_ref, yk_ref, wsg_ref, wsu_ref, wsd_ref, g2_ref, b2_ref, out_ref,
                    *, tt):
    h = h_ref[...]
    hb = h.astype(BF16)
    g = jnp.dot(hb, wsg_ref[...], preferred_element_type=F32)
    u = jnp.dot(hb, wsu_ref[...], preferred_element_type=F32)
    act = (g * jax.nn.sigmoid(g)) * u
    f = jnp.dot(act.astype(BF16), wsd_ref[...], preferred_element_type=F32)
    gate = gate_ref[...]
    r_lo = jnp.zeros((tt, yk_ref.shape[-1]), F32)
    r_hi = jnp.zeros((tt, yk_ref.shape[-1]), F32)
    for k in range(TOP_K):
        p = yk_ref[k]
        gk = gate[:, k:k + 1]
        r_lo = r_lo + gk * pltpu.bitcast(p << 16, F32)
        r_hi = r_hi + gk * pltpu.bitcast(p & jnp.uint32(0xFFFF0000), F32)
    f = f + jnp.concatenate([r_lo, r_hi], axis=-1)
    out_ref[...] = _layer_norm(DEEPNORM_ALPHA * h + f, g2_ref[...], b2_ref[...])


def _combine(h, gate_t, gate_row0, yk, wsg_bf, wsu_bf, wsd_bf, ln2_g, ln2_b, tt):
    n, d = h.shape
    q0 = gate_row0 // tt
    ds_ = wsg_bf.shape[1]
    kern = functools.partial(_combine_kernel, tt=tt)
    return pl.pallas_call(
        kern,
        grid=(n // tt,),
        in_specs=[
            pl.BlockSpec((tt, d), lambda i: (i, 0)),
            pl.BlockSpec((tt, TOP_K), lambda i: (i + q0, 0)),
            pl.BlockSpec((TOP_K, tt, yk.shape[2]), lambda i: (0, i, 0)),
            pl.BlockSpec((d, ds_), lambda i: (0, 0)),
            pl.BlockSpec((d, ds_), lambda i: (0, 0)),
            pl.BlockSpec((ds_, d), lambda i: (0, 0)),
            pl.BlockSpec((1, d), lambda i: (0, 0)),
            pl.BlockSpec((1, d), lambda i: (0, 0)),
        ],
        out_specs=pl.BlockSpec((tt, d), lambda i: (i, 0)),
        out_shape=jax.ShapeDtypeStruct((n, d), F32),
        compiler_params=_cparams("parallel"),
        name="combine",
    )(h, gate_t, yk, wsg_bf, wsu_bf, wsd_bf, ln2_g.reshape(1, d), ln2_b.reshape(1, d))


def _pick_tile(seq, bsz, rows):
    return max(SUBLANES, min(seq, rows // bsz))


def _mixer(x, meta_state, u_meta_last, p):
    bsz, seq, _ = x.shape
    tt = _pick_tile(seq, bsz, 1024)
    gb, u, su = _embed_inproj(x, p["ln_emb_g"], p["ln_emb_b"], p["w_in_bf"], tt)
    tc = _pick_tile(seq, bsz, 256)
    s0 = jnp.stack([jnp.broadcast_to(meta_state, (N_SLABS, SUBLANES, 2 * SLAB_STATE)),
                    jnp.zeros((N_SLABS, SUBLANES, 2 * SLAB_STATE), F32)])
    a_b = jnp.broadcast_to(p["s5_a"][:, :, None, :], (2, N_SLABS, SUBLANES, 2 * SLAB_STATE))
    yf, yr = _s5_scan(su, s0, p["s5_wb"], p["s5_wc"], a_b, bsz=bsz, tc=tc, dirs=(False, True),
                      emit_y=True, emit_state=False)
    tt3 = _pick_tile(seq, bsz, 512)
    h1, h1p = _mixer_tail(x, p["ln_emb_g"], p["ln_emb_b"], gb, u, u_meta_last, su, yf, yr,
                          p["conv_w"], p["conv_b"], p["ssm_d"], p["w_glu_bf"], p["b_glu"],
                          p["norm_a_g"], p["norm_b_g"], p["w_out_bf"], p["ln1_g"], p["ln1_b"], tt3)
    d = h1.shape[-1]
    return h1.reshape(bsz * seq, d), h1p.reshape(bsz * seq, d // 2)


def kernel(x_prompt, x_sample, meta_tokens, ln_emb_g, ln_emb_b, w_in, conv_w, conv_b, ssm_lambda_re, ssm_lambda_im, ssm_log_step, ssm_b_re, ssm_b_im, ssm_c_re, ssm_c_im, ssm_d, w_glu, b_glu, norm_a_g, norm_b_g, w_out, ln1_g, ln1_b, w_router, router_bias, w_exp_gate, w_exp_up, w_exp_down, w_sh_gate, w_sh_up, w_sh_down, ln2_g, ln2_b):
    l = 0
    d = x_prompt.shape[-1]
    dirs = [_s5_params(ssm_lambda_re[l, k].astype(F32), ssm_lambda_im[l, k].astype(F32),
                       ssm_log_step[l, k], ssm_b_re[l, k].astype(F32), ssm_b_im[l, k].astype(F32),
                       ssm_c_re[l, k], ssm_c_im[l, k]) for k in range(2)]
    p = dict(
        ln_emb_g=ln_emb_g, ln_emb_b=ln_emb_b, w_in_bf=w_in[l].astype(BF16),
        conv_w=conv_w[l], conv_b=conv_b[l], ssm_d=ssm_d[l],
        w_glu_bf=w_glu[l].astype(BF16), b_glu=b_glu[l], norm_a_g=norm_a_g[l],
        norm_b_g=norm_b_g[l], w_out_bf=w_out[l].astype(BF16), ln1_g=ln1_g[l], ln1_b=ln1_b[l],
        s5_wb=jnp.stack([dirs[0][0], dirs[1][0]]), s5_wc=jnp.stack([dirs[0][1], dirs[1][1]]),
        s5_a=jnp.stack([dirs[0][2], dirs[1][2]]),
    )
    mb = SUBLANES
    xm = jnp.broadcast_to(meta_tokens.astype(F32)[None], (mb, N_META, d))
    _, u_m, su_m = _embed_inproj(xm, ln_emb_g, ln_emb_b, p["w_in_bf"], N_META)
    a_m = jnp.broadcast_to(p["s5_a"][:1, :, None, :], (1, N_SLABS, mb, 2 * SLAB_STATE))
    (st_m,) = _s5_scan(su_m, jnp.zeros((1, N_SLABS, mb, 2 * SLAB_STATE), F32), p["s5_wb"][:1],
                       p["s5_wc"][:1], a_m, bsz=mb, tc=N_META, dirs=(False,), emit_y=False,
                       emit_state=True)
    meta_state = st_m[0, :, :1, :]
    u_meta_last = u_m[0, N_META - 1]

    h1_p, h1p_p = _mixer(x_prompt, meta_state, u_meta_last, p)
    h1_s, h1p_s = _mixer(x_sample, meta_state, u_meta_last, p)
    n = h1_p.shape[0] + h1_s.shape[0]

    tr = min(h1_p.shape[0], h1_s.shape[0], ROUTER_TILE)
    eidx, rank, gate, cnt = _router(h1_p, h1_s, w_router[l], router_bias[l], tr)
    counts = cnt[:, 0].astype(jnp.int32)
    sched = _expert_schedule(counts, n * TOP_K, EXPERT_TILE)
    offsets = sched[2]
    pos = _positions(eidx, rank, offsets[:-1], min(n, POSITIONS_TILE))
    pos_c = pos.reshape(TOP_K, n // SC_CHUNK, SC_CHUNK).transpose(1, 0, 2)
    xs = _sc_dispatch(h1p_p, h1p_s, pos_c, SC_CHUNK)
    ys = _experts(xs, sched, w_exp_gate[l], w_exp_up[l], w_exp_down[l], EXPERT_TILE)
    n_p = h1_p.shape[0]
    nch_p = n_p // SC_CHUNK
    td = min(n_p, h1_s.shape[0], COMBINE_TILE)
    gate_t = gate.T
    shared = (w_sh_gate[l].astype(BF16), w_sh_up[l].astype(BF16), w_sh_down[l].astype(BF16),
              ln2_g[l], ln2_b[l], td)
    yk_p = _sc_gather(ys, pos_c[:nch_p], SC_CHUNK)
    yk_s = _sc_gather(ys, pos_c[nch_p:], SC_CHUNK)
    out_p = _combine(h1_p, gate_t, 0, yk_p, *shared)
    out_s = _combine(h1_s, gate_t, n_p, yk_s, *shared)
    return (out_p.reshape(x_prompt.shape), out_s.reshape(x_sample.shape))
```

```python
import functools

import jax
import jax.numpy as jnp
from jax import lax
from jax.experimental import pallas as pl
from jax.experimental.pallas import tpu as pltpu
from jax.experimental.pallas import tpu_sc as plsc

F32 = jnp.float32
BF16 = jnp.bfloat16

N_META = 16
CONV_WIDTH = 3
SSM_GROUP = 16
SSM_STATE = 64
N_EXPERTS = 256
TOP_K = 8
N_EXPERT_GROUPS = 8
TOPK_GROUPS = 4
ROUTED_SCALE = 2.5
DEPTH = 1
DEEPNORM_ALPHA = (2.0 * DEPTH) ** 0.25
LN_EPS = 1e-5
RMS_EPS = 1e-6

LANES = 128
SUBLANES = 8
N_SLABS = 4
GROUPS_PER_SLAB = LANES // SSM_GROUP
SLAB_STATE = GROUPS_PER_SLAB * SSM_STATE
S5_CHUNK_ROWS = 256
EXPERT_TILE = 2048
EXPERT_SUB = 512
ROUTER_TILE = 256
COMBINE_TILE = 512
POSITIONS_TILE = 2048
SC_CHUNK = 64
VMEM_LIMIT = 48 * 1024 * 1024


def _cparams(*sem):
    return pltpu.CompilerParams(dimension_semantics=sem, vmem_limit_bytes=VMEM_LIMIT)


def _layer_norm(x, g, b):
    mu = jnp.mean(x, axis=-1, keepdims=True)
    xc = x - mu
    var = jnp.mean(xc * xc, axis=-1, keepdims=True)
    return xc * lax.rsqrt(var + LN_EPS) * g + b


def _rms_norm(x, g):
    return x * lax.rsqrt(jnp.mean(x * x, axis=-1, keepdims=True) + RMS_EPS) * g


def _pack_halves(x):
    half = x.shape[-1] // 2
    bits = pltpu.bitcast(x.astype(BF16).astype(F32), jnp.uint32)
    return (bits[:, :half] >> 16) | (bits[:, half:] & jnp.uint32(0xFFFF0000))


def _unpack_halves(p):
    lo = pltpu.bitcast(p << 16, F32).astype(BF16)
    hi = pltpu.bitcast(p & jnp.uint32(0xFFFF0000), F32).astype(BF16)
    return lo, hi


def _embed_inproj_kernel(x_ref, g_ref, b_ref, w_ref, gb_ref, u_ref, su_ref, *, bsz, tt):
    d = x_ref.shape[-1]
    x = x_ref[...].reshape(bsz * tt, d)
    h0 = _layer_norm(x, g_ref[...], b_ref[...])
    proj = jnp.dot(h0.astype(BF16), w_ref[...], preferred_element_type=F32)
    wc = gb_ref.shape[-1]
    gb_ref[...] = proj[:, :wc].reshape(bsz, tt, wc)
    u_ref[...] = (proj[:, wc:2 * wc] * proj[:, 2 * wc:3 * wc]).reshape(bsz, tt, wc)
    s_u = proj[:, 3 * wc:]
    for b in range(bsz):
        for j in range(N_SLABS):
            val = s_u[b * tt:(b + 1) * tt, j * LANES:(j + 1) * LANES]
            if bsz == 1:
                su_ref[j] = val
            else:
                su_ref[j, pl.ds(b, tt, stride=bsz), :] = val


def _embed_inproj(x, ln_g, ln_b, w_in_bf, tt):
    bsz, seq, d = x.shape
    e = w_in_bf.shape[1]
    wc = (e - N_SLABS * LANES) // 3
    nt = seq // tt
    kern = functools.partial(_embed_inproj_kernel, bsz=bsz, tt=tt)
    return pl.pallas_call(
        kern,
        grid=(nt,),
        in_specs=[
            pl.BlockSpec((bsz, tt, d), lambda i: (0, i, 0)),
            pl.BlockSpec((1, d), lambda i: (0, 0)),
            pl.BlockSpec((1, d), lambda i: (0, 0)),
            pl.BlockSpec((d, e), lambda i: (0, 0)),
        ],
        out_specs=[
            pl.BlockSpec((bsz, tt, wc), lambda i: (0, i, 0)),
            pl.BlockSpec((bsz, tt, wc), lambda i: (0, i, 0)),
            pl.BlockSpec((N_SLABS, tt * bsz, LANES), lambda i: (0, i, 0)),
        ],
        out_shape=[
            jax.ShapeDtypeStruct((bsz, seq, wc), F32),
            jax.ShapeDtypeStruct((bsz, seq, wc), F32),
            jax.ShapeDtypeStruct((N_SLABS, seq * bsz, LANES), F32),
        ],
        compiler_params=_cparams("parallel"),
        name="embed_inproj",
    )(x, ln_g.reshape(1, d), ln_b.reshape(1, d), w_in_bf)


def _s5_kernel(*refs, bsz, tc, dirs, emit_y, emit_state):
    nd = len(dirs)
    it = iter(refs)
    u_refs = [next(it) for _ in range(nd)]
    s0_ref, wb_ref, wc_ref, a_ref = next(it), next(it), next(it), next(it)
    y_refs = [next(it) for _ in range(nd)] if emit_y else []
    sf_ref = next(it) if emit_state else None
    bu_ref, st_ref = next(it), next(it)
    half = SLAB_STATE

    @pl.when(pl.program_id(0) == 0)
    def _():
        st_ref[...] = s0_ref[...]

    def project_in(j):
        for k in range(nd):
            bu_ref[k, j] = jnp.dot(u_refs[k][j].astype(BF16), wb_ref[k, j],
                                   preferred_element_type=F32)

    def project_out(j):
        for k in range(nd):
            y_refs[k][j] = jnp.dot(bu_ref[k, j].astype(BF16), wc_ref[k, j],
                                   preferred_element_type=F32)

    per_tile = SUBLANES // bsz
    n_tiles = tc // per_tile

    def cmul_add(a, s, x):
        return (a[0] * s[0] - a[1] * s[1] + x[0], a[0] * s[1] + a[1] * s[0] + x[1])

    def load(k, j, ti):
        rows = pl.ds(ti * SUBLANES, SUBLANES)
        return bu_ref[k, j, rows, :half], bu_ref[k, j, rows, half:]

    def store(k, j, ti, v):
        rows = pl.ds(ti * SUBLANES, SUBLANES)
        bu_ref[k, j, rows, :half] = v[0]
        bu_ref[k, j, rows, half:] = v[1]

    if per_tile == 1:
        def recur(j):
            for k in range(nd):
                a = (a_ref[k, j, :, :half], a_ref[k, j, :, half:])
                s = (st_ref[k, j, :, :half], st_ref[k, j, :, half:])
                for i in range(n_tiles):
                    ti = (n_tiles - 1 - i) if dirs[k] else i
                    s = cmul_add(a, s, load(k, j, ti))
                    store(k, j, ti, s)
                st_ref[k, j, :, :half] = s[0]
                st_ref[k, j, :, half:] = s[1]
    else:
        assert per_tile == 2 and tuple(dirs) == (False, True)
        low = lax.broadcasted_iota(jnp.int32, (SUBLANES, half), 0) < bsz

        def pick(p, q):
            return (jnp.where(low, p[0], q[0]), jnp.where(low, p[1], q[1]))

        def swap(v):
            return (pltpu.roll(v[0], bsz, axis=0), pltpu.roll(v[1], bsz, axis=0))

        def recur(j):
            af = (a_ref[0, j, :, :half], a_ref[0, j, :, half:])
            ar = (a_ref[1, j, :, :half], a_ref[1, j, :, half:])
            a_fr, a_rf = pick(af, ar), pick(ar, af)
            s = pick((st_ref[0, j, :, :half], st_ref[0, j, :, half:]),
                     (st_ref[1, j, :, :half], st_ref[1, j, :, half:]))
            for i in range(n_tiles):
                tf, tr = i, n_tiles - 1 - i
                xf, xr = load(0, j, tf), load(1, j, tr)
                s1 = cmul_add(a_fr, s, pick(xf, xr))
                s2 = cmul_add(a_rf, swap(s1), pick(xr, xf))
                store(0, j, tf, pick(s1, s2))
                store(1, j, tr, pick(s2, s1))
                s = swap(s2)
            for k in range(nd):
                st_ref[k, j, :, :half] = s[0]
                st_ref[k, j, :, half:] = s[1]

    project_in(0)
    for j in range(N_SLABS):
        if j + 1 < N_SLABS:
            project_in(j + 1)
        recur(j)
        if emit_y:
            project_out(j)
    if emit_state:
        sf_ref[...] = st_ref[...]


def _s5_scan(su, s0, wb, wc, a_b, *, bsz, tc, dirs, emit_y, emit_state):
    rows = su.shape[1]
    seq = rows // bsz
    assert SUBLANES % bsz == 0 and seq % tc == 0 and tc % (SUBLANES // bsz) == 0
    nc = seq // tc
    r = tc * bsz
    nd = len(dirs)
    sw = 2 * SLAB_STATE

    def u_map(rev):
        return (lambda c: (0, nc - 1 - c, 0)) if rev else (lambda c: (0, c, 0))

    in_specs = [pl.BlockSpec((N_SLABS, r, LANES), u_map(rev)) for rev in dirs]
    in_specs += [
        pl.BlockSpec((nd, N_SLABS, SUBLANES, sw), lambda c: (0, 0, 0, 0)),
        pl.BlockSpec((nd, N_SLABS, LANES, sw), lambda c: (0, 0, 0, 0)),
        pl.BlockSpec((nd, N_SLABS, sw, LANES), lambda c: (0, 0, 0, 0)),
        pl.BlockSpec((nd, N_SLABS, SUBLANES, sw), lambda c: (0, 0, 0, 0)),
    ]
    out_specs, out_shape = [], []
    if emit_y:
        for rev in dirs:
            out_specs.append(pl.BlockSpec((N_SLABS, r, LANES), u_map(rev)))
            out_shape.append(jax.ShapeDtypeStruct((N_SLABS, rows, LANES), F32))
    if emit_state:
        out_specs.append(pl.BlockSpec((nd, N_SLABS, SUBLANES, sw), lambda c: (0, 0, 0, 0)))
        out_shape.append(jax.ShapeDtypeStruct((nd, N_SLABS, SUBLANES, sw), F32))
    kern = functools.partial(_s5_kernel, bsz=bsz, tc=tc, dirs=dirs, emit_y=emit_y,
                             emit_state=emit_state)
    return pl.pallas_call(
        kern,
        grid=(nc,),
        in_specs=in_specs,
        out_specs=out_specs,
        out_shape=out_shape,
        scratch_shapes=[
            pltpu.VMEM((nd, N_SLABS, r, sw), F32),
            pltpu.VMEM((nd, N_SLABS, SUBLANES, sw), F32),
        ],
        compiler_params=_cparams("arbitrary"),
        name="s5_scan",
    )(*([su] * nd), s0, wb, wc, a_b)


def _s5_params(lam_re, lam_im, log_step, b_re, b_im, c_re, c_im):
    g, p = lam_re.shape
    h = b_re.shape[-1]
    dt = jnp.exp(log_step.astype(F32))[:, None]
    mag = jnp.exp(lam_re * dt)
    ab_re = mag * jnp.cos(lam_im * dt)
    ab_im = mag * jnp.sin(lam_im * dt)
    den = lam_re * lam_re + lam_im * lam_im
    nr, ni = ab_re - 1.0, ab_im
    f_re = (nr * lam_re + ni * lam_im) / den
    f_im = (ni * lam_re - nr * lam_im) / den
    bb_re = f_re[..., None] * b_re - f_im[..., None] * b_im
    bb_im = f_re[..., None] * b_im + f_im[..., None] * b_re
    ns, gl = N_SLABS, GROUPS_PER_SLAB
    eye = jnp.eye(gl, dtype=F32)

    def in_block(bb):
        bb = bb.reshape(ns, gl, p, h)
        return jnp.einsum('sgph,gk->sghkp', bb, eye).reshape(ns, gl * h, gl * p)

    def out_block(cc):
        cc = cc.reshape(ns, gl, h, p)
        return jnp.einsum('sghp,gk->sgpkh', cc, eye).reshape(ns, gl * p, gl * h)

    wb = jnp.concatenate([in_block(bb_re), in_block(bb_im)], axis=-1)
    wc = jnp.concatenate([out_block(c_re.astype(F32)), -out_block(c_im.astype(F32))], axis=1)
    a = jnp.concatenate([ab_re.reshape(ns, gl * p), ab_im.reshape(ns, gl * p)], axis=-1)
    return wb.astype(BF16), wc.astype(BF16), a


def _mixer_tail_kernel(x_ref, ge_ref, be_ref, gb_ref, u_ref, up_ref, un_ref, um_ref, su_ref,
                       yf_ref, yr_ref, cw_ref, cb_ref, sd_ref, wg_ref, bg_ref, na_ref, nb_ref,
                       wo_ref, g1_ref, b1_ref, h1_ref, h1p_ref, *, bsz, tt):
    i = pl.program_id(0)
    nt = pl.num_programs(0)
    d = x_ref.shape[-1]
    wcv = gb_ref.shape[-1]
    row_id = lax.broadcasted_iota(jnp.int32, (tt, wcv), 0)
    ya, ys = [], []
    for b in range(bsz):
        u = u_ref[b]
        prev_edge = jnp.where(i == 0, um_ref[...], up_ref[b, SUBLANES - 1:SUBLANES, :])
        next_edge = jnp.where(i == nt - 1, jnp.zeros((1, wcv), F32), un_ref[b, 0:1, :])
        u_prev = jnp.where(row_id == 0, prev_edge, pltpu.roll(u, 1, axis=0))
        u_next = jnp.where(row_id == tt - 1, next_edge, pltpu.roll(u, tt - 1, axis=0))
        conv = u_prev * cw_ref[0:1, :] + u * cw_ref[1:2, :] + u_next * cw_ref[2:3, :] + cb_ref[...]
        ya.append(gb_ref[b] * conv)

        def slab(ref, b=b):
            parts = []
            for j in range(N_SLABS):
                if bsz == 1:
                    parts.append(ref[j])
                else:
                    parts.append(ref[j, pl.ds(b, tt, stride=bsz), :])
            return jnp.concatenate(parts, axis=-1)

        ys.append(slab(yf_ref) + slab(yr_ref) + sd_ref[...] * slab(su_ref))
    y_a = jnp.concatenate(ya, axis=0)
    y_s = jnp.concatenate(ys, axis=0)
    z = jax.nn.gelu(y_s)
    glu = jnp.dot(z.astype(BF16), wg_ref[...], preferred_element_type=F32) + bg_ref[...]
    y_b = z * jax.nn.sigmoid(glu)
    merged = jnp.concatenate([_rms_norm(y_a, na_ref[...]), _rms_norm(y_b, nb_ref[...])], axis=-1)
    m = jnp.dot(merged.astype(BF16), wo_ref[...], preferred_element_type=F32)
    h0 = _layer_norm(x_ref[...].reshape(bsz * tt, d), ge_ref[...], be_ref[...])
    h1 = _layer_norm(DEEPNORM_ALPHA * h0 + m, g1_ref[...], b1_ref[...])
    h1_ref[...] = h1.reshape(bsz, tt, d)
    h1p_ref[...] = _pack_halves(h1).reshape(bsz, tt, d // 2)


def _mixer_tail(x, ln_emb_g, ln_emb_b, gb, u, u_meta_last, su, yf, yr, conv_w, conv_b, ssm_d,
                w_glu_bf, b_glu, norm_a_g, norm_b_g, w_out_bf, ln1_g, ln1_b, tt):
    bsz, seq, d = x.shape
    wcv = gb.shape[-1]
    ws = N_SLABS * LANES
    nt = seq // tt
    tb = tt // SUBLANES
    nb8 = seq // SUBLANES
    kern = functools.partial(_mixer_tail_kernel, bsz=bsz, tt=tt)
    row = lambda n: pl.BlockSpec((1, n), lambda i: (0, 0))
    slab_spec = pl.BlockSpec((N_SLABS, tt * bsz, LANES), lambda i: (0, i, 0))
    return pl.pallas_call(
        kern,
        grid=(nt,),
        in_specs=[
            pl.BlockSpec((bsz, tt, d), lambda i: (0, i, 0)),
            row(d), row(d),
            pl.BlockSpec((bsz, tt, wcv), lambda i: (0, i, 0)),
            pl.BlockSpec((bsz, tt, wcv), lambda i: (0, i, 0)),
            pl.BlockSpec((bsz, SUBLANES, wcv), lambda i: (0, jnp.maximum(i * tb - 1, 0), 0)),
            pl.BlockSpec((bsz, SUBLANES, wcv), lambda i: (0, jnp.minimum((i + 1) * tb, nb8 - 1), 0)),
            row(wcv),
            slab_spec, slab_spec, slab_spec,
            pl.BlockSpec((CONV_WIDTH, wcv), lambda i: (0, 0)),
            row(wcv), row(ws),
            pl.BlockSpec((ws, ws), lambda i: (0, 0)),
            row(ws), row(wcv), row(ws),
            pl.BlockSpec((wcv + ws, d), lambda i: (0, 0)),
            row(d), row(d),
        ],
        out_specs=[
            pl.BlockSpec((bsz, tt, d), lambda i: (0, i, 0)),
            pl.BlockSpec((bsz, tt, d // 2), lambda i: (0, i, 0)),
        ],
        out_shape=[
            jax.ShapeDtypeStruct((bsz, seq, d), F32),
            jax.ShapeDtypeStruct((bsz, seq, d // 2), jnp.uint32),
        ],
        compiler_params=_cparams("parallel"),
        name="mixer_tail",
    )(x, ln_emb_g.reshape(1, d), ln_emb_b.reshape(1, d), gb, u, u, u,
      u_meta_last.reshape(1, wcv), su, yf, yr, conv_w, conv_b.reshape(1, wcv),
      ssm_d.reshape(1, ws), w_glu_bf, b_glu.reshape(1, ws), norm_a_g.reshape(1, wcv),
      norm_b_g.reshape(1, ws), w_out_bf, ln1_g.reshape(1, d), ln1_b.reshape(1, d))


def _dual_row_specs(rows, width, nq_p):
    return [pl.BlockSpec((rows, width), lambda q, *_: (jnp.minimum(q, nq_p - 1), 0)),
            pl.BlockSpec((rows, width), lambda q, *_: (jnp.maximum(q - nq_p, 0), 0))]


def _router_kernel(hp_ref, hs_ref, wh_ref, bias_ref, eidx_ref, rank_ref, gate_ref, cnt_ref,
                   cnt_scr, *, tt, nq_p):
    ne = wh_ref.shape[0]
    epg = ne // N_EXPERT_GROUPS
    neg = jnp.float32(-jnp.inf)

    @pl.when(pl.program_id(0) == 0)
    def _():
        cnt_scr[...] = jnp.zeros_like(cnt_scr)

    h = jnp.where(pl.program_id(0) < nq_p, hp_ref[...], hs_ref[...])
    dn = (((1,), (1,)), ((), ()))
    logits = lax.dot_general(wh_ref[...], h.astype(BF16), dn,
                             preferred_element_type=F32)
    scores = jax.nn.sigmoid(logits)
    sel = scores + bias_ref[...]

    gi = lax.broadcasted_iota(jnp.int32, (epg, tt), 0)
    gs = []
    for g in range(N_EXPERT_GROUPS):
        x = sel[g * epg:(g + 1) * epg, :]
        m1 = jnp.max(x, axis=0, keepdims=True)
        i1 = jnp.min(jnp.where(x == m1, gi, epg), axis=0, keepdims=True)
        m2 = jnp.max(jnp.where(gi == i1, neg, x), axis=0, keepdims=True)
        gs.append(m1 + m2)
    chosen = [jnp.zeros((1, tt), F32) for _ in range(N_EXPERT_GROUPS)]
    for _ in range(TOPK_GROUPS):
        m = gs[0]
        for g in range(1, N_EXPERT_GROUPS):
            m = jnp.maximum(m, gs[g])
        found = jnp.zeros((1, tt), F32)
        for g in range(N_EXPERT_GROUPS):
            hit = jnp.where((gs[g] == m) & (found == 0.0), 1.0, 0.0)
            chosen[g] = chosen[g] + hit
            found = found + hit
            gs[g] = jnp.where(hit > 0.0, neg, gs[g])
    selm = jnp.concatenate(
        [jnp.where(chosen[g] > 0.0, sel[g * epg:(g + 1) * epg, :], neg)
         for g in range(N_EXPERT_GROUPS)], axis=0)

    ei = lax.broadcasted_iota(jnp.int32, (ne, tt), 0)
    msel = jnp.zeros((ne, tt), F32)
    idxs, gvals = [], []
    for _ in range(TOP_K):
        m = jnp.max(selm, axis=0, keepdims=True)
        idx = jnp.min(jnp.where(selm == m, ei, ne), axis=0, keepdims=True)
        hit = ei == idx
        gvals.append(jnp.sum(jnp.where(hit, scores, 0.0), axis=0, keepdims=True))
        selm = jnp.where(hit, neg, selm)
        msel = jnp.where(hit, 1.0, msel)
        idxs.append(idx)
    gsum = gvals[0]
    for k in range(1, TOP_K):
        gsum = gsum + gvals[k]
    gate_ref[...] = jnp.concatenate([gv / gsum * ROUTED_SCALE for gv in gvals], axis=0)
    eidx_ref[...] = jnp.concatenate(idxs, axis=0)

    r_i = lax.broadcasted_iota(jnp.int32, (tt, tt), 0)
    c_i = lax.broadcasted_iota(jnp.int32, (tt, tt), 1)
    upper = jnp.where(r_i < c_i, 1.0, 0.0).astype(BF16)
    rank_full = jnp.dot(msel.astype(BF16), upper, preferred_element_type=F32) + cnt_scr[...]
    ranks = [jnp.sum(jnp.where(ei == idxs[k], rank_full, 0.0), axis=0, keepdims=True)
             for k in range(TOP_K)]
    rank_ref[...] = jnp.concatenate(ranks, axis=0).astype(jnp.int32)
    cnt_scr[...] = cnt_scr[...] + jnp.sum(msel, axis=1, keepdims=True)
    cnt_ref[...] = cnt_scr[...]


def _router(h_p, h_s, w_router, router_bias, tt):
    d = h_p.shape[1]
    n = h_p.shape[0] + h_s.shape[0]
    assert h_p.shape[0] % tt == 0 and h_s.shape[0] % tt == 0
    nq_p = h_p.shape[0] // tt
    ne = w_router.shape[1]
    wh = w_router.T.astype(BF16)
    kern = functools.partial(_router_kernel, tt=tt, nq_p=nq_p)
    return pl.pallas_call(
        kern,
        grid=(n // tt,),
        in_specs=_dual_row_specs(tt, d, nq_p) + [
            pl.BlockSpec((ne, d), lambda i: (0, 0)),
            pl.BlockSpec((ne, 1), lambda i: (0, 0)),
        ],
        out_specs=[
            pl.BlockSpec((TOP_K, tt), lambda i: (0, i)),
            pl.BlockSpec((TOP_K, tt), lambda i: (0, i)),
            pl.BlockSpec((TOP_K, tt), lambda i: (0, i)),
            pl.BlockSpec((ne, 1), lambda i: (0, 0)),
        ],
        out_shape=[
            jax.ShapeDtypeStruct((TOP_K, n), jnp.int32),
            jax.ShapeDtypeStruct((TOP_K, n), jnp.int32),
            jax.ShapeDtypeStruct((TOP_K, n), F32),
            jax.ShapeDtypeStruct((ne, 1), F32),
        ],
        scratch_shapes=[pltpu.VMEM((ne, 1), F32)],
        compiler_params=_cparams("arbitrary"),
        name="router",
    )(h_p, h_s, wh, router_bias.astype(F32).reshape(ne, 1))


def _positions_kernel(eidx_ref, rank_ref, start_ref, pos_ref):
    ne = start_ref.shape[0]
    tt = eidx_ref.shape[1]
    ei = lax.broadcasted_iota(jnp.int32, (ne, tt), 0)
    start = start_ref[...]
    rows = [jnp.sum(jnp.where(ei == eidx_ref[k:k + 1, :], start, 0.0), axis=0, keepdims=True)
            for k in range(TOP_K)]
    pos_ref[...] = jnp.concatenate(rows, axis=0).astype(jnp.int32) + rank_ref[...]


def _positions(eidx, rank, start, tt):
    n = eidx.shape[1]
    ne = start.shape[0]
    return pl.pallas_call(
        _positions_kernel,
        grid=(n // tt,),
        in_specs=[
            pl.BlockSpec((TOP_K, tt), lambda i: (0, i)),
            pl.BlockSpec((TOP_K, tt), lambda i: (0, i)),
            pl.BlockSpec((ne, 1), lambda i: (0, 0)),
        ],
        out_specs=pl.BlockSpec((TOP_K, tt), lambda i: (0, i)),
        out_shape=jax.ShapeDtypeStruct((TOP_K, n), jnp.int32),
        compiler_params=_cparams("parallel"),
        name="positions",
    )(eidx, rank, start.astype(F32).reshape(ne, 1))


def _sc_workers():
    info = plsc.get_sparse_core_info()
    return info.num_cores, info.num_subcores


def _sc_dispatch(hp_p, hp_s, pos_c, chunk):
    w = hp_p.shape[1]
    n = hp_p.shape[0] + hp_s.shape[0]
    nch_p = hp_p.shape[0] // chunk
    nc, ns = _sc_workers()
    assert hp_p.shape[0] % chunk == 0 and n % (chunk * nc * ns) == 0
    per_worker = (n // chunk) // (nc * ns)
    mesh = plsc.VectorSubcoreMesh(core_axis_name="c", subcore_axis_name="s")

    @functools.partial(
        pl.kernel, mesh=mesh,
        out_type=jax.ShapeDtypeStruct((n * TOP_K, w), hp_p.dtype),
        scratch_types=[pltpu.VMEM((TOP_K, chunk), jnp.int32),
                       pltpu.VMEM((chunk, w), hp_p.dtype),
                       pltpu.SemaphoreType.DMA],
    )
    def dispatch(hp_hbm, hs_hbm, pos_hbm, xs_hbm, idx_v, rows_v, sem):
        wid = lax.axis_index("s") * nc + lax.axis_index("c")

        @pl.loop(0, per_worker)
        def _(ci):
            c = wid * per_worker + ci

            @pl.when(c < nch_p)
            def _():
                pltpu.sync_copy(hp_hbm.at[pl.ds(pl.multiple_of(c * chunk, chunk), chunk)], rows_v)

            @pl.when(c >= nch_p)
            def _():
                pltpu.sync_copy(
                    hs_hbm.at[pl.ds(pl.multiple_of((c - nch_p) * chunk, chunk), chunk)], rows_v)

            pltpu.sync_copy(pos_hbm.at[c], idx_v)
            copies = [pltpu.async_copy(rows_v, xs_hbm.at[idx_v.at[k]], sem) for k in range(TOP_K)]
            for cp in copies:
                cp.wait()

    return dispatch(hp_p, hp_s, pos_c)


def _sc_gather(ys, pos_c, chunk):
    w = ys.shape[1]
    n = pos_c.shape[0] * chunk
    nc, ns = _sc_workers()
    assert pos_c.shape[0] % (nc * ns) == 0
    per_worker = pos_c.shape[0] // (nc * ns)
    mesh = plsc.VectorSubcoreMesh(core_axis_name="c", subcore_axis_name="s")
    nbuf = 3

    @functools.partial(
        pl.kernel, mesh=mesh,
        out_type=jax.ShapeDtypeStruct((TOP_K, n, w), ys.dtype),
        scratch_types=[pltpu.VMEM((TOP_K, chunk), jnp.int32),
                       pltpu.VMEM((nbuf, chunk, w), ys.dtype),
                       pltpu.SemaphoreType.DMA((nbuf,)),
                       pltpu.SemaphoreType.DMA((nbuf,))],
    )
    def gather(ys_hbm, pos_hbm, out_hbm, idx_v, rows_v, gsem, wsem):
        wid = lax.axis_index("s") * nc + lax.axis_index("c")

        @pl.loop(0, per_worker)
        def _(ci):
            c = wid * per_worker + ci
            off = pl.multiple_of(c * chunk, chunk)
            pltpu.sync_copy(pos_hbm.at[c], idx_v)

            def start_gather(k):
                b = k % nbuf
                return pltpu.async_copy(ys_hbm.at[idx_v.at[k]], rows_v.at[b], gsem.at[b])

            gathers = {0: start_gather(0)}
            writes = {}
            for k in range(TOP_K):
                if k + 1 < TOP_K:
                    if k + 1 - nbuf >= 0:
                        writes.pop(k + 1 - nbuf).wait()
                    gathers[k + 1] = start_gather(k + 1)
                gathers.pop(k).wait()
                b = k % nbuf
                writes[k] = pltpu.async_copy(rows_v.at[b], out_hbm.at[k, pl.ds(off, chunk)],
                                             wsem.at[b])
            for k in sorted(writes):
                writes[k].wait()

    return gather(ys, pos_c)


def _experts_kernel(fe_ref, le_ref, off_ref, nxt_ref, slot_ref, xs_ref, wg_hbm, wu_hbm, wd_hbm,
                    ys_ref, wgb, wub, wdb, acc, wgf, wuf, wdf, wsem, cur_ref, *, tm):
    t = pl.program_id(0)
    base = t * tm
    half = xs_ref.shape[1]

    @pl.when(t == 0)
    def _():
        cur_ref[0] = -1

    def weight_copies(expert, slot):
        return (pltpu.make_async_copy(wg_hbm.at[expert], wgf.at[slot], wsem.at[slot, 0]),
                pltpu.make_async_copy(wu_hbm.at[expert], wuf.at[slot], wsem.at[slot, 1]),
                pltpu.make_async_copy(wd_hbm.at[expert], wdf.at[slot], wsem.at[slot, 2]))

    def load_weights(e):
        slot = slot_ref[e]

        @pl.when(cur_ref[0] < 0)
        def _():
            for cp in weight_copies(e, slot):
                cp.start()

        for cp in weight_copies(e, slot):
            cp.wait()
        wgb[...] = wgf[slot].astype(BF16)
        wub[...] = wuf[slot].astype(BF16)
        wdb[...] = wdf[slot].astype(BF16)

        @pl.when(nxt_ref[e] != e)
        def _():
            for cp in weight_copies(nxt_ref[e], 1 - slot):
                cp.start()

        cur_ref[0] = e

    def visit(e, carry):
        lo_row, hi_row = off_ref[e], off_ref[e + 1]
        _visit_expert(e, lo_row, hi_row)
        return carry

    def ffn(rows, m, r0, masked, lo_row, hi_row):
        lo, hi = _unpack_halves(xs_ref[rows, :])
        g = (jnp.dot(lo, wgb[:half], preferred_element_type=F32)
             + jnp.dot(hi, wgb[half:], preferred_element_type=F32))
        u = (jnp.dot(lo, wub[:half], preferred_element_type=F32)
             + jnp.dot(hi, wub[half:], preferred_element_type=F32))
        act = (g * jax.nn.sigmoid(g)) * u
        if masked:
            row = r0 + lax.broadcasted_iota(jnp.int32, (m, 1), 0)
            act = jnp.where((row >= lo_row) & (row < hi_row), act, 0.0)
        return jnp.dot(act.astype(BF16), wdb[...], preferred_element_type=F32)

    def shared_block(rows, m, r0, lo_row, hi_row):
        y = ffn(rows, m, r0, True, lo_row, hi_row)
        opens = lo_row <= r0

        @pl.when(opens)
        def _():
            acc[rows, :] = y

        @pl.when(jnp.logical_not(opens))
        def _():
            acc[rows, :] = acc[rows, :] + y

        ys_ref[rows, :] = _pack_halves(acc[rows, :])

    def _visit_expert(e, lo_row, hi_row):
        @pl.when(hi_row > lo_row)
        def _():
            pl.when(cur_ref[0] != e)(functools.partial(load_weights, e))
            for b in range(tm // EXPERT_SUB):
                rows = pl.ds(b * EXPERT_SUB, EXPERT_SUB)
                r0 = base + b * EXPERT_SUB
                touched = (lo_row < r0 + EXPERT_SUB) & (hi_row > r0)
                whole = (lo_row <= r0) & (hi_row >= r0 + EXPERT_SUB)

                @pl.when(touched & whole)
                def _(rows=rows, r0=r0):
                    ys_ref[rows, :] = _pack_halves(
                        ffn(rows, EXPERT_SUB, r0, False, lo_row, hi_row))

                pl.when(touched & jnp.logical_not(whole))(
                    functools.partial(shared_block, rows, EXPERT_SUB, r0, lo_row, hi_row))

    lax.fori_loop(fe_ref[t], le_ref[t] + 1, visit, 0)


def _experts(xs, sched, w_gate, w_up, w_down, tm):
    n_rows, half = xs.shape
    ne, d, de = w_gate.shape
    assert n_rows % tm == 0 and tm % EXPERT_SUB == 0

    grid_spec = pltpu.PrefetchScalarGridSpec(
        num_scalar_prefetch=len(sched),
        grid=(n_rows // tm,),
        in_specs=[
            pl.BlockSpec((tm, half), lambda t, *_: (t, 0)),
            pl.BlockSpec(memory_space=pl.ANY),
            pl.BlockSpec(memory_space=pl.ANY),
            pl.BlockSpec(memory_space=pl.ANY),
        ],
        out_specs=pl.BlockSpec((tm, half), lambda t, *_: (t, 0)),
        scratch_shapes=[
            pltpu.VMEM((d, de), BF16),
            pltpu.VMEM((d, de), BF16),
            pltpu.VMEM((de, d), BF16),
            pltpu.VMEM((tm, d), F32),
            pltpu.VMEM((2, d, de), F32),
            pltpu.VMEM((2, d, de), F32),
            pltpu.VMEM((2, de, d), F32),
            pltpu.SemaphoreType.DMA((2, 3)),
            pltpu.SMEM((1,), jnp.int32),
        ],
    )
    return pl.pallas_call(
        functools.partial(_experts_kernel, tm=tm),
        grid_spec=grid_spec,
        out_shape=jax.ShapeDtypeStruct((n_rows, half), jnp.uint32),
        compiler_params=_cparams("arbitrary"),
        name="experts",
    )(*sched, xs, w_gate, w_up, w_down)


def _expert_schedule(counts, n_rows, tm):
    ne = counts.shape[0]
    off = jnp.concatenate([jnp.zeros((1,), jnp.int32), jnp.cumsum(counts)]).astype(jnp.int32)
    tile_lo = jnp.arange(n_rows // tm, dtype=jnp.int32) * tm
    owner = lambda row: jnp.sum((off[None, 1:] <= row[:, None]).astype(jnp.int32), axis=1)
    first_e = owner(tile_lo)
    last_e = owner(tile_lo + (tm - 1))
    ids = jnp.arange(ne, dtype=jnp.int32)
    later = (ids[None, :] > ids[:, None]) & (counts[None, :] > 0)
    next_e = jnp.min(jnp.where(later, ids[None, :], ne), axis=1)
    next_e = jnp.where(next_e == ne, ids, next_e)
    slot = (jnp.cumsum((counts > 0).astype(jnp.int32)) - 1) % 2
    i32 = lambda v: v.astype(jnp.int32)
    return (i32(first_e), i32(last_e), off, i32(next_e), i32(slot))


def _combine_kernel(h_ref, gate_ref, yk_ref, wsg_ref, wsu_ref, wsd_ref, g2_ref, b2_ref, out_ref,
                    *, tt):
    h = h_ref[...]
    hb = h.astype(BF16)
    g = jnp.dot(hb, wsg_ref[...], preferred_element_type=F32)
    u = jnp.dot(hb, wsu_ref[...], preferred_element_type=F32)
    act = (g * jax.nn.sigmoid(g)) * u
    f = jnp.dot(act.astype(BF16), wsd_ref[...], preferred_element_type=F32)
    gate = gate_ref[...]
    r_lo = jnp.zeros((tt, yk_ref.shape[-1]), F32)
    r_hi = jnp.zeros((tt, yk_ref.shape[-1]), F32)
    for k in range(TOP_K):
        p = yk_ref[k]
        gk = gate[:, k:k + 1]
        r_lo = r_lo + gk * pltpu.bitcast(p << 16, F32)
        r_hi = r_hi + gk * pltpu.bitcast(p & jnp.uint32(0xFFFF0000), F32)
    f = f + jnp.concatenate([r_lo, r_hi], axis=-1)
    out_ref[...] = _layer_norm(DEEPNORM_ALPHA * h + f, g2_ref[...], b2_ref[...])


def _combine(h, gate_t, gate_row0, yk, wsg_bf, wsu_bf, wsd_bf, ln2_g, ln2_b, tt):
    n, d = h.shape
    assert n % tt == 0 and gate_row0 % tt == 0
    q0 = gate_row0 // tt
    ds_ = wsg_bf.shape[1]
    kern = functools.partial(_combine_kernel, tt=tt)
    return pl.pallas_call(
        kern,
        grid=(n // tt,),
        in_specs=[
            pl.BlockSpec((tt, d), lambda i: (i, 0)),
            pl.BlockSpec((tt, TOP_K), lambda i: (i + q0, 0)),
            pl.BlockSpec((TOP_K, tt, yk.shape[2]), lambda i: (0, i, 0)),
            pl.BlockSpec((d, ds_), lambda i: (0, 0)),
            pl.BlockSpec((d, ds_), lambda i: (0, 0)),
            pl.BlockSpec((ds_, d), lambda i: (0, 0)),
            pl.BlockSpec((1, d), lambda i: (0, 0)),
            pl.BlockSpec((1, d), lambda i: (0, 0)),
        ],
        out_specs=pl.BlockSpec((tt, d), lambda i: (i, 0)),
        out_shape=jax.ShapeDtypeStruct((n, d), F32),
        compiler_params=_cparams("parallel"),
        name="combine",
    )(h, gate_t, yk, wsg_bf, wsu_bf, wsd_bf, ln2_g.reshape(1, d), ln2_b.reshape(1, d))


def _pick_tile(seq, bsz, rows):
    return max(SUBLANES, min(seq, rows // bsz))


def _mixer(x, meta_state, u_meta_last, p):
    bsz, seq, _ = x.shape
    tt = _pick_tile(seq, bsz, 1024)
    gb, u, su = _embed_inproj(x, p["ln_emb_g"], p["ln_emb_b"], p["w_in_bf"], tt)
    tc = _pick_tile(seq, bsz, S5_CHUNK_ROWS)
    s0 = jnp.stack([jnp.broadcast_to(meta_state, (N_SLABS, SUBLANES, 2 * SLAB_STATE)),
                    jnp.zeros((N_SLABS, SUBLANES, 2 * SLAB_STATE), F32)])
    a_b = jnp.broadcast_to(p["s5_a"][:, :, None, :], (2, N_SLABS, SUBLANES, 2 * SLAB_STATE))
    yf, yr = _s5_scan(su, s0, p["s5_wb"], p["s5_wc"], a_b, bsz=bsz, tc=tc, dirs=(False, True),
                      emit_y=True, emit_state=False)
    tt3 = _pick_tile(seq, bsz, 512)
    h1, h1p = _mixer_tail(x, p["ln_emb_g"], p["ln_emb_b"], gb, u, u_meta_last, su, yf, yr,
                          p["conv_w"], p["conv_b"], p["ssm_d"], p["w_glu_bf"], p["b_glu"],
                          p["norm_a_g"], p["norm_b_g"], p["w_out_bf"], p["ln1_g"], p["ln1_b"], tt3)
    d = h1.shape[-1]
    return h1.reshape(bsz * seq, d), h1p.reshape(bsz * seq, d // 2)


def kernel(x_prompt, x_sample, meta_tokens, ln_emb_g, ln_emb_b, w_in, conv_w, conv_b, ssm_lambda_re, ssm_lambda_im, ssm_log_step, ssm_b_re, ssm_b_im, ssm_c_re, ssm_c_im, ssm_d, w_glu, b_glu, norm_a_g, norm_b_g, w_out, ln1_g, ln1_b, w_router, router_bias, w_exp_gate, w_exp_up, w_exp_down, w_sh_gate, w_sh_up, w_sh_down, ln2_g, ln2_b):
    l = 0
    d = x_prompt.shape[-1]
    dirs = [_s5_params(ssm_lambda_re[l, k].astype(F32), ssm_lambda_im[l, k].astype(F32),
                       ssm_log_step[l, k], ssm_b_re[l, k].astype(F32), ssm_b_im[l, k].astype(F32),
                       ssm_c_re[l, k], ssm_c_im[l, k]) for k in range(2)]
    p = dict(
        ln_emb_g=ln_emb_g, ln_emb_b=ln_emb_b, w_in_bf=w_in[l].astype(BF16),
        conv_w=conv_w[l], conv_b=conv_b[l], ssm_d=ssm_d[l],
        w_glu_bf=w_glu[l].astype(BF16), b_glu=b_glu[l], norm_a_g=norm_a_g[l],
        norm_b_g=norm_b_g[l], w_out_bf=w_out[l].astype(BF16), ln1_g=ln1_g[l], ln1_b=ln1_b[l],
        s5_wb=jnp.stack([dirs[0][0], dirs[1][0]]), s5_wc=jnp.stack([dirs[0][1], dirs[1][1]]),
        s5_a=jnp.stack([dirs[0][2], dirs[1][2]]),
    )
    mb = SUBLANES
    xm = jnp.broadcast_to(meta_tokens.astype(F32)[None], (mb, N_META, d))
    _, u_m, su_m = _embed_inproj(xm, ln_emb_g, ln_emb_b, p["w_in_bf"], N_META)
    a_m = jnp.broadcast_to(p["s5_a"][:1, :, None, :], (1, N_SLABS, mb, 2 * SLAB_STATE))
    (st_m,) = _s5_scan(su_m, jnp.zeros((1, N_SLABS, mb, 2 * SLAB_STATE), F32), p["s5_wb"][:1],
                       p["s5_wc"][:1], a_m, bsz=mb, tc=N_META, dirs=(False,), emit_y=False,
                       emit_state=True)
    meta_state = st_m[0, :, :1, :]
    u_meta_last = u_m[0, N_META - 1]

    h1_p, h1p_p = _mixer(x_prompt, meta_state, u_meta_last, p)
    h1_s, h1p_s = _mixer(x_sample, meta_state, u_meta_last, p)
    n = h1_p.shape[0] + h1_s.shape[0]

    tr = min(h1_p.shape[0], h1_s.shape[0], ROUTER_TILE)
    eidx, rank, gate, cnt = _router(h1_p, h1_s, w_router[l], router_bias[l], tr)
    counts = cnt[:, 0].astype(jnp.int32)
    sched = _expert_schedule(counts, n * TOP_K, EXPERT_TILE)
    offsets = sched[2]
    pos = _positions(eidx, rank, offsets[:-1], min(n, POSITIONS_TILE))
    pos_c = pos.reshape(TOP_K, n // SC_CHUNK, SC_CHUNK).transpose(1, 0, 2)
    xs = _sc_dispatch(h1p_p, h1p_s, pos_c, SC_CHUNK)
    ys = _experts(xs, sched, w_exp_gate[l], w_exp_up[l], w_exp_down[l], EXPERT_TILE)
    n_p = h1_p.shape[0]
    nch_p = n_p // SC_CHUNK
    td = min(n_p, h1_s.shape[0], COMBINE_TILE)
    gate_t = gate.T
    shared = (w_sh_gate[l].astype(BF16), w_sh_up[l].astype(BF16), w_sh_down[l].astype(BF16),
              ln2_g[l], ln2_b[l], td)
    yk_p = _sc_gather(ys, pos_c[:nch_p], SC_CHUNK)
    yk_s = _sc_gather(ys, pos_c[nch_p:], SC_CHUNK)
    out_p = _combine(h1_p, gate_t, 0, yk_p, *shared)
    out_s = _combine(h1_s, gate_t, n_p, yk_s, *shared)
    return (out_p.reshape(x_prompt.shape), out_s.reshape(x_sample.shape))
```

```python
import functools

import jax
import jax.numpy as jnp
from jax import lax
from jax.experimental import pallas as pl
from jax.experimental.pallas import tpu as pltpu
from jax.experimental.pallas import tpu_sc as plsc

F32 = jnp.float32
BF16 = jnp.bfloat16

N_META = 16
CONV_WIDTH = 3
SSM_GROUP = 16
SSM_STATE = 64
N_EXPERTS = 256
TOP_K = 8
N_EXPERT_GROUPS = 8
TOPK_GROUPS = 4
ROUTED_SCALE = 2.5
DEPTH = 1
DEEPNORM_ALPHA = (2.0 * DEPTH) ** 0.25
LN_EPS = 1e-5
RMS_EPS = 1e-6

LANES = 128
SUBLANES = 8
N_SLABS = 4
GROUPS_PER_SLAB = LANES // SSM_GROUP
SLAB_STATE = GROUPS_PER_SLAB * SSM_STATE
S5_CHUNK_ROWS = 256
EXPERT_TILE = 2048
EXPERT_SUB = 512
ROUTER_TILE = 256
COMBINE_TILE = 512
POSITIONS_TILE = 2048
SC_CHUNK = 64
VMEM_LIMIT = 48 * 1024 * 1024


def _cparams(*sem):
    return pltpu.CompilerParams(dimension_semantics=sem, vmem_limit_bytes=VMEM_LIMIT)


def _layer_norm(x, g, b):
    mu = jnp.mean(x, axis=-1, keepdims=True)
    xc = x - mu
    var = jnp.mean(xc * xc, axis=-1, keepdims=True)
    return xc * lax.rsqrt(var + LN_EPS) * g + b


def _rms_norm(x, g):
    return x * lax.rsqrt(jnp.mean(x * x, axis=-1, keepdims=True) + RMS_EPS) * g


def _pack_halves(x):
    half = x.shape[-1] // 2
    bits = pltpu.bitcast(x.astype(BF16).astype(F32), jnp.uint32)
    return (bits[:, :half] >> 16) | (bits[:, half:] & jnp.uint32(0xFFFF0000))


def _unpack_halves(p):
    lo = pltpu.bitcast(p << 16, F32).astype(BF16)
    hi = pltpu.bitcast(p & jnp.uint32(0xFFFF0000), F32).astype(BF16)
    return lo, hi


def _embed_inproj_kernel(x_ref, g_ref, b_ref, w_ref, gb_ref, u_ref, su_ref, *, bsz, tt):
    d = x_ref.shape[-1]
    x = x_ref[...].reshape(bsz * tt, d)
    h0 = _layer_norm(x, g_ref[...], b_ref[...])
    proj = jnp.dot(h0.astype(BF16), w_ref[...], preferred_element_type=F32)
    wc = gb_ref.shape[-1]
    gb_ref[...] = proj[:, :wc].reshape(bsz, tt, wc)
    u_ref[...] = (proj[:, wc:2 * wc] * proj[:, 2 * wc:3 * wc]).reshape(bsz, tt, wc)
    s_u = proj[:, 3 * wc:]
    for b in range(bsz):
        for j in range(N_SLABS):
            val = s_u[b * tt:(b + 1) * tt, j * LANES:(j + 1) * LANES]
            if bsz == 1:
                su_ref[j] = val
            else:
                su_ref[j, pl.ds(b, tt, stride=bsz), :] = val


def _embed_inproj(x, ln_g, ln_b, w_in_bf, tt):
    bsz, seq, d = x.shape
    e = w_in_bf.shape[1]
    wc = (e - N_SLABS * LANES) // 3
    nt = seq // tt
    kern = functools.partial(_embed_inproj_kernel, bsz=bsz, tt=tt)
    return pl.pallas_call(
        kern,
        grid=(nt,),
        in_specs=[
            pl.BlockSpec((bsz, tt, d), lambda i: (0, i, 0)),
            pl.BlockSpec((1, d), lambda i: (0, 0)),
            pl.BlockSpec((1, d), lambda i: (0, 0)),
            pl.BlockSpec((d, e), lambda i: (0, 0)),
        ],
        out_specs=[
            pl.BlockSpec((bsz, tt, wc), lambda i: (0, i, 0)),
            pl.BlockSpec((bsz, tt, wc), lambda i: (0, i, 0)),
            pl.BlockSpec((N_SLABS, tt * bsz, LANES), lambda i: (0, i, 0)),
        ],
        out_shape=[
            jax.ShapeDtypeStruct((bsz, seq, wc), F32),
            jax.ShapeDtypeStruct((bsz, seq, wc), F32),
            jax.ShapeDtypeStruct((N_SLABS, seq * bsz, LANES), F32),
        ],
        compiler_params=_cparams("parallel"),
        name="embed_inproj",
    )(x, ln_g.reshape(1, d), ln_b.reshape(1, d), w_in_bf)


def _s5_kernel(*refs, bsz, tc, dirs, emit_y, emit_state):
    nd = len(dirs)
    it = iter(refs)
    u_refs = [next(it) for _ in range(nd)]
    s0_ref, wb_ref, wc_ref, a_ref = next(it), next(it), next(it), next(it)
    y_refs = [next(it) for _ in range(nd)] if emit_y else []
    sf_ref = next(it) if emit_state else None
    bu_ref, st_ref = next(it), next(it)
    half = SLAB_STATE

    @pl.when(pl.program_id(0) == 0)
    def _():
        st_ref[...] = s0_ref[...]

    def project_in(j):
        for k in range(nd):
            bu_ref[k, j] = jnp.dot(u_refs[k][j].astype(BF16), wb_ref[k, j],
                                   preferred_element_type=F32)

    def project_out(j):
        for k in range(nd):
            y_refs[k][j] = jnp.dot(bu_ref[k, j].astype(BF16), wc_ref[k, j],
                                   preferred_element_type=F32)

    per_tile = SUBLANES // bsz
    n_tiles = tc // per_tile

    def cmul_add(a, s, x):
        return (a[0] * s[0] - a[1] * s[1] + x[0], a[0] * s[1] + a[1] * s[0] + x[1])

    def load(k, j, ti):
        rows = pl.ds(ti * SUBLANES, SUBLANES)
        return bu_ref[k, j, rows, :half], bu_ref[k, j, rows, half:]

    def store(k, j, ti, v):
        rows = pl.ds(ti * SUBLANES, SUBLANES)
        bu_ref[k, j, rows, :half] = v[0]
        bu_ref[k, j, rows, half:] = v[1]

    if per_tile == 1:
        def recur(j):
            for k in range(nd):
                a = (a_ref[k, j, :, :half], a_ref[k, j, :, half:])
                s = (st_ref[k, j, :, :half], st_ref[k, j, :, half:])
                for i in range(n_tiles):
                    ti = (n_tiles - 1 - i) if dirs[k] else i
                    s = cmul_add(a, s, load(k, j, ti))
                    store(k, j, ti, s)
                st_ref[k, j, :, :half] = s[0]
                st_ref[k, j, :, half:] = s[1]
    else:
        assert per_tile == 2 and tuple(dirs) == (False, True)
        low = lax.broadcasted_iota(jnp.int32, (SUBLANES, half), 0) < bsz

        def pick(p, q):
            return (jnp.where(low, p[0], q[0]), jnp.where(low, p[1], q[1]))

        def swap(v):
            return (pltpu.roll(v[0], bsz, axis=0), pltpu.roll(v[1], bsz, axis=0))

        def recur(j):
            af = (a_ref[0, j, :, :half], a_ref[0, j, :, half:])
            ar = (a_ref[1, j, :, :half], a_ref[1, j, :, half:])
            a_fr, a_rf = pick(af, ar), pick(ar, af)
            s = pick((st_ref[0, j, :, :half], st_ref[0, j, :, half:]),
                     (st_ref[1, j, :, :half], st_ref[1, j, :, half:]))
            for i in range(n_tiles):
                tf, tr = i, n_tiles - 1 - i
                xf, xr = load(0, j, tf), load(1, j, tr)
                s1 = cmul_add(a_fr, s, pick(xf, xr))
                s2 = cmul_add(a_rf, swap(s1), pick(xr, xf))
                store(0, j, tf, pick(s1, s2))
                store(1, j, tr, pick(s2, s1))
                s = swap(s2)
            for k in range(nd):
                st_ref[k, j, :, :half] = s[0]
                st_ref[k, j, :, half:] = s[1]

    project_in(0)
    for j in range(N_SLABS):
        if j + 1 < N_SLABS:
            project_in(j + 1)
        recur(j)
        if emit_y:
            project_out(j)
    if emit_state:
        sf_ref[...] = st_ref[...]


def _s5_scan(su, s0, wb, wc, a_b, *, bsz, tc, dirs, emit_y, emit_state):
    rows = su.shape[1]
    seq = rows // bsz
    assert SUBLANES % bsz == 0 and seq % tc == 0 and tc % (SUBLANES // bsz) == 0
    nc = seq // tc
    r = tc * bsz
    nd = len(dirs)
    sw = 2 * SLAB_STATE

    def u_map(rev):
        return (lambda c: (0, nc - 1 - c, 0)) if rev else (lambda c: (0, c, 0))

    in_specs = [pl.BlockSpec((N_SLABS, r, LANES), u_map(rev)) for rev in dirs]
    in_specs += [
        pl.BlockSpec((nd, N_SLABS, SUBLANES, sw), lambda c: (0, 0, 0, 0)),
        pl.BlockSpec((nd, N_SLABS, LANES, sw), lambda c: (0, 0, 0, 0)),
        pl.BlockSpec((nd, N_SLABS, sw, LANES), lambda c: (0, 0, 0, 0)),
        pl.BlockSpec((nd, N_SLABS, SUBLANES, sw), lambda c: (0, 0, 0, 0)),
    ]
    out_specs, out_shape = [], []
    if emit_y:
        for rev in dirs:
            out_specs.append(pl.BlockSpec((N_SLABS, r, LANES), u_map(rev)))
            out_shape.append(jax.ShapeDtypeStruct((N_SLABS, rows, LANES), F32))
    if emit_state:
        out_specs.append(pl.BlockSpec((nd, N_SLABS, SUBLANES, sw), lambda c: (0, 0, 0, 0)))
        out_shape.append(jax.ShapeDtypeStruct((nd, N_SLABS, SUBLANES, sw), F32))
    kern = functools.partial(_s5_kernel, bsz=bsz, tc=tc, dirs=dirs, emit_y=emit_y,
                             emit_state=emit_state)
    return pl.pallas_call(
        kern,
        grid=(nc,),
        in_specs=in_specs,
        out_specs=out_specs,
        out_shape=out_shape,
        scratch_shapes=[
            pltpu.VMEM((nd, N_SLABS, r, sw), F32),
            pltpu.VMEM((nd, N_SLABS, SUBLANES, sw), F32),
        ],
        compiler_params=_cparams("arbitrary"),
        name="s5_scan",
    )(*([su] * nd), s0, wb, wc, a_b)


def _s5_params(lam_re, lam_im, log_step, b_re, b_im, c_re, c_im):
    g, p = lam_re.shape
    h = b_re.shape[-1]
    dt = jnp.exp(log_step.astype(F32))[:, None]
    mag = jnp.exp(lam_re * dt)
    ab_re = mag * jnp.cos(lam_im * dt)
    ab_im = mag * jnp.sin(lam_im * dt)
    den = lam_re * lam_re + lam_im * lam_im
    nr, ni = ab_re - 1.0, ab_im
    f_re = (nr * lam_re + ni * lam_im) / den
    f_im = (ni * lam_re - nr * lam_im) / den
    bb_re = f_re[..., None] * b_re - f_im[..., None] * b_im
    bb_im = f_re[..., None] * b_im + f_im[..., None] * b_re
    ns, gl = N_SLABS, GROUPS_PER_SLAB
    eye = jnp.eye(gl, dtype=F32)

    def in_block(bb):
        bb = bb.reshape(ns, gl, p, h)
        return jnp.einsum('sgph,gk->sghkp', bb, eye).reshape(ns, gl * h, gl * p)

    def out_block(cc):
        cc = cc.reshape(ns, gl, h, p)
        return jnp.einsum('sghp,gk->sgpkh', cc, eye).reshape(ns, gl * p, gl * h)

    wb = jnp.concatenate([in_block(bb_re), in_block(bb_im)], axis=-1)
    wc = jnp.concatenate([out_block(c_re.astype(F32)), -out_block(c_im.astype(F32))], axis=1)
    a = jnp.concatenate([ab_re.reshape(ns, gl * p), ab_im.reshape(ns, gl * p)], axis=-1)
    return wb.astype(BF16), wc.astype(BF16), a


def _mixer_tail_kernel(x_ref, ge_ref, be_ref, gb_ref, u_ref, up_ref, un_ref, um_ref, su_ref,
                       yf_ref, yr_ref, cw_ref, cb_ref, sd_ref, wg_ref, bg_ref, na_ref, nb_ref,
                       wo_ref, g1_ref, b1_ref, h1_ref, h1p_ref, *, bsz, tt):
    i = pl.program_id(0)
    nt = pl.num_programs(0)
    d = x_ref.shape[-1]
    wcv = gb_ref.shape[-1]
    row_id = lax.broadcasted_iota(jnp.int32, (tt, wcv), 0)
    ya, ys = [], []
    for b in range(bsz):
        u = u_ref[b]
        prev_edge = jnp.where(i == 0, um_ref[...], up_ref[b, SUBLANES - 1:SUBLANES, :])
        next_edge = jnp.where(i == nt - 1, jnp.zeros((1, wcv), F32), un_ref[b, 0:1, :])
        u_prev = jnp.where(row_id == 0, prev_edge, pltpu.roll(u, 1, axis=0))
        u_next = jnp.where(row_id == tt - 1, next_edge, pltpu.roll(u, tt - 1, axis=0))
        conv = u_prev * cw_ref[0:1, :] + u * cw_ref[1:2, :] + u_next * cw_ref[2:3, :] + cb_ref[...]
        ya.append(gb_ref[b] * conv)

        def slab(ref, b=b):
            parts = []
            for j in range(N_SLABS):
                if bsz == 1:
                    parts.append(ref[j])
                else:
                    parts.append(ref[j, pl.ds(b, tt, stride=bsz), :])
            return jnp.concatenate(parts, axis=-1)

        ys.append(slab(yf_ref) + slab(yr_ref) + sd_ref[...] * slab(su_ref))
    y_a = jnp.concatenate(ya, axis=0)
    y_s = jnp.concatenate(ys, axis=0)
    z = jax.nn.gelu(y_s)
    glu = jnp.dot(z.astype(BF16), wg_ref[...], preferred_element_type=F32) + bg_ref[...]
    y_b = z * jax.nn.sigmoid(glu)
    merged = jnp.concatenate([_rms_norm(y_a, na_ref[...]), _rms_norm(y_b, nb_ref[...])], axis=-1)
    m = jnp.dot(merged.astype(BF16), wo_ref[...], preferred_element_type=F32)
    h0 = _layer_norm(x_ref[...].reshape(bsz * tt, d), ge_ref[...], be_ref[...])
    h1 = _layer_norm(DEEPNORM_ALPHA * h0 + m, g1_ref[...], b1_ref[...])
    h1_ref[...] = h1.reshape(bsz, tt, d)
    h1p_ref[...] = _pack_halves(h1).reshape(bsz, tt, d // 2)


def _mixer_tail(x, ln_emb_g, ln_emb_b, gb, u, u_meta_last, su, yf, yr, conv_w, conv_b, ssm_d,
                w_glu_bf, b_glu, norm_a_g, norm_b_g, w_out_bf, ln1_g, ln1_b, tt):
    bsz, seq, d = x.shape
    wcv = gb.shape[-1]
    ws = N_SLABS * LANES
    nt = seq // tt
    tb = tt // SUBLANES
    nb8 = seq // SUBLANES
    kern = functools.partial(_mixer_tail_kernel, bsz=bsz, tt=tt)
    row = lambda n: pl.BlockSpec((1, n), lambda i: (0, 0))
    slab_spec = pl.BlockSpec((N_SLABS, tt * bsz, LANES), lambda i: (0, i, 0))
    return pl.pallas_call(
        kern,
        grid=(nt,),
        in_specs=[
            pl.BlockSpec((bsz, tt, d), lambda i: (0, i, 0)),
            row(d), row(d),
            pl.BlockSpec((bsz, tt, wcv), lambda i: (0, i, 0)),
            pl.BlockSpec((bsz, tt, wcv), lambda i: (0, i, 0)),
            pl.BlockSpec((bsz, SUBLANES, wcv), lambda i: (0, jnp.maximum(i * tb - 1, 0), 0)),
            pl.BlockSpec((bsz, SUBLANES, wcv), lambda i: (0, jnp.minimum((i + 1) * tb, nb8 - 1), 0)),
            row(wcv),
            slab_spec, slab_spec, slab_spec,
            pl.BlockSpec((CONV_WIDTH, wcv), lambda i: (0, 0)),
            row(wcv), row(ws),
            pl.BlockSpec((ws, ws), lambda i: (0, 0)),
            row(ws), row(wcv), row(ws),
            pl.BlockSpec((wcv + ws, d), lambda i: (0, 0)),
            row(d), row(d),
        ],
        out_specs=[
            pl.BlockSpec((bsz, tt, d), lambda i: (0, i, 0)),
            pl.BlockSpec((bsz, tt, d // 2), lambda i: (0, i, 0)),
        ],
        out_shape=[
            jax.ShapeDtypeStruct((bsz, seq, d), F32),
            jax.ShapeDtypeStruct((bsz, seq, d // 2), jnp.uint32),
        ],
        compiler_params=_cparams("parallel"),
        name="mixer_tail",
    )(x, ln_emb_g.reshape(1, d), ln_emb_b.reshape(1, d), gb, u, u, u,
      u_meta_last.reshape(1, wcv), su, yf, yr, conv_w, conv_b.reshape(1, wcv),
      ssm_d.reshape(1, ws), w_glu_bf, b_glu.reshape(1, ws), norm_a_g.reshape(1, wcv),
      norm_b_g.reshape(1, ws), w_out_bf, ln1_g.reshape(1, d), ln1_b.reshape(1, d))


def _dual_row_specs(rows, width, nq_p):
    return [pl.BlockSpec((rows, width), lambda q, *_: (jnp.minimum(q, nq_p - 1), 0)),
            pl.BlockSpec((rows, width), lambda q, *_: (jnp.maximum(q - nq_p, 0), 0))]


def _router_kernel(hp_ref, hs_ref, wh_ref, bias_ref, eidx_ref, rank_ref, gate_ref, cnt_ref,
                   cnt_scr, *, tt, nq_p):
    ne = wh_ref.shape[0]
    epg = ne // N_EXPERT_GROUPS
    neg = jnp.float32(-jnp.inf)

    @pl.when(pl.program_id(0) == 0)
    def _():
        cnt_scr[...] = jnp.zeros_like(cnt_scr)

    h = jnp.where(pl.program_id(0) < nq_p, hp_ref[...], hs_ref[...])
    dn = (((1,), (1,)), ((), ()))
    logits = lax.dot_general(wh_ref[...], h.astype(BF16), dn,
                             preferred_element_type=F32)
    scores = jax.nn.sigmoid(logits)
    sel = scores + bias_ref[...]

    gi = lax.broadcasted_iota(jnp.int32, (epg, tt), 0)
    gs = []
    for g in range(N_EXPERT_GROUPS):
        x = sel[g * epg:(g + 1) * epg, :]
        m1 = jnp.max(x, axis=0, keepdims=True)
        i1 = jnp.min(jnp.where(x == m1, gi, epg), axis=0, keepdims=True)
        m2 = jnp.max(jnp.where(gi == i1, neg, x), axis=0, keepdims=True)
        gs.append(m1 + m2)
    chosen = [jnp.zeros((1, tt), F32) for _ in range(N_EXPERT_GROUPS)]
    for _ in range(TOPK_GROUPS):
        m = gs[0]
        for g in range(1, N_EXPERT_GROUPS):
            m = jnp.maximum(m, gs[g])
        found = jnp.zeros((1, tt), F32)
        for g in range(N_EXPERT_GROUPS):
            hit = jnp.where((gs[g] == m) & (found == 0.0), 1.0, 0.0)
            chosen[g] = chosen[g] + hit
            found = found + hit
            gs[g] = jnp.where(hit > 0.0, neg, gs[g])
    selm = jnp.concatenate(
        [jnp.where(chosen[g] > 0.0, sel[g * epg:(g + 1) * epg, :], neg)
         for g in range(N_EXPERT_GROUPS)], axis=0)

    ei = lax.broadcasted_iota(jnp.int32, (ne, tt), 0)
    msel = jnp.zeros((ne, tt), F32)
    idxs, gvals = [], []
    for _ in range(TOP_K):
        m = jnp.max(selm, axis=0, keepdims=True)
        idx = jnp.min(jnp.where(selm == m, ei, ne), axis=0, keepdims=True)
        hit = ei == idx
        gvals.append(jnp.sum(jnp.where(hit, scores, 0.0), axis=0, keepdims=True))
        selm = jnp.where(hit, neg, selm)
        msel = jnp.where(hit, 1.0, msel)
        idxs.append(idx)
    gsum = gvals[0]
    for k in range(1, TOP_K):
        gsum = gsum + gvals[k]
    gate_ref[...] = jnp.concatenate([gv / gsum * ROUTED_SCALE for gv in gvals], axis=0)
    eidx_ref[...] = jnp.concatenate(idxs, axis=0)

    r_i = lax.broadcasted_iota(jnp.int32, (tt, tt), 0)
    c_i = lax.broadcasted_iota(jnp.int32, (tt, tt), 1)
    upper = jnp.where(r_i < c_i, 1.0, 0.0).astype(BF16)
    rank_full = jnp.dot(msel.astype(BF16), upper, preferred_element_type=F32) + cnt_scr[...]
    ranks = [jnp.sum(jnp.where(ei == idxs[k], rank_full, 0.0), axis=0, keepdims=True)
             for k in range(TOP_K)]
    rank_ref[...] = jnp.concatenate(ranks, axis=0).astype(jnp.int32)
    cnt_scr[...] = cnt_scr[...] + jnp.sum(msel, axis=1, keepdims=True)
    cnt_ref[...] = cnt_scr[...]


def _router(h_p, h_s, w_router, router_bias, tt):
    d = h_p.shape[1]
    n = h_p.shape[0] + h_s.shape[0]
    assert h_p.shape[0] % tt == 0 and h_s.shape[0] % tt == 0
    nq_p = h_p.shape[0] // tt
    ne = w_router.shape[1]
    wh = w_router.T.astype(BF16)
    kern = functools.partial(_router_kernel, tt=tt, nq_p=nq_p)
    return pl.pallas_call(
        kern,
        grid=(n // tt,),
        in_specs=_dual_row_specs(tt, d, nq_p) + [
            pl.BlockSpec((ne, d), lambda i: (0, 0)),
            pl.BlockSpec((ne, 1), lambda i: (0, 0)),
        ],
        out_specs=[
            pl.BlockSpec((TOP_K, tt), lambda i: (0, i)),
            pl.BlockSpec((TOP_K, tt), lambda i: (0, i)),
            pl.BlockSpec((TOP_K, tt), lambda i: (0, i)),
            pl.BlockSpec((ne, 1), lambda i: (0, 0)),
        ],
        out_shape=[
            jax.ShapeDtypeStruct((TOP_K, n), jnp.int32),
            jax.ShapeDtypeStruct((TOP_K, n), jnp.int32),
            jax.ShapeDtypeStruct((TOP_K, n), F32),
            jax.ShapeDtypeStruct((ne, 1), F32),
        ],
        scratch_shapes=[pltpu.VMEM((ne, 1), F32)],
        compiler_params=_cparams("arbitrary"),
        name="router",
    )(h_p, h_s, wh, router_bias.astype(F32).reshape(ne, 1))


def _positions_kernel(eidx_ref, rank_ref, start_ref, pos_ref):
    ne = start_ref.shape[0]
    tt = eidx_ref.shape[1]
    ei = lax.broadcasted_iota(jnp.int32, (ne, tt), 0)
    start = start_ref[...]
    rows = [jnp.sum(jnp.where(ei == eidx_ref[k:k + 1, :], start, 0.0), axis=0, keepdims=True)
            for k in range(TOP_K)]
    pos_ref[...] = jnp.concatenate(rows, axis=0).astype(jnp.int32) + rank_ref[...]


def _positions(eidx, rank, start, tt):
    n = eidx.shape[1]
    ne = start.shape[0]
    return pl.pallas_call(
        _positions_kernel,
        grid=(n // tt,),
        in_specs=[
            pl.BlockSpec((TOP_K, tt), lambda i: (0, i)),
            pl.BlockSpec((TOP_K, tt), lambda i: (0, i)),
            pl.BlockSpec((ne, 1), lambda i: (0, 0)),
        ],
        out_specs=pl.BlockSpec((TOP_K, tt), lambda i: (0, i)),
        out_shape=jax.ShapeDtypeStruct((TOP_K, n), jnp.int32),
        compiler_params=_cparams("parallel"),
        name="positions",
    )(eidx, rank, start.astype(F32).reshape(ne, 1))


def _sc_workers():
    info = plsc.get_sparse_core_info()
    return info.num_cores, info.num_subcores


def _sc_dispatch(hp_p, hp_s, pos_c, chunk):
    w = hp_p.shape[1]
    n = hp_p.shape[0] + hp_s.shape[0]
    nch_p = hp_p.shape[0] // chunk
    nc, ns = _sc_workers()
    assert hp_p.shape[0] % chunk == 0 and n % (chunk * nc * ns) == 0
    per_worker = (n // chunk) // (nc * ns)
    mesh = plsc.VectorSubcoreMesh(core_axis_name="c", subcore_axis_name="s")

    assert per_worker % 2 == 0

    @functools.partial(
        pl.kernel, mesh=mesh,
        out_type=jax.ShapeDtypeStruct((n * TOP_K, w), hp_p.dtype),
        scratch_types=[pltpu.VMEM((2, TOP_K, chunk), jnp.int32),
                       pltpu.VMEM((2, chunk, w), hp_p.dtype),
                       pltpu.SemaphoreType.DMA((2,)),
                       pltpu.SemaphoreType.DMA((2,)),
                       pltpu.SemaphoreType.DMA],
    )
    def dispatch(hp_hbm, hs_hbm, pos_hbm, xs_hbm, idx_v, rows_v, rsem, isem, ssem):
        wid = lax.axis_index("s") * nc + lax.axis_index("c")
        first = wid * per_worker

        def loads(c, b):
            cp_ = jnp.minimum(c, nch_p - 1)
            cs_ = jnp.maximum(c - nch_p, 0)
            src_p = hp_hbm.at[pl.ds(pl.multiple_of(cp_ * chunk, chunk), chunk)]
            src_s = hs_hbm.at[pl.ds(pl.multiple_of(cs_ * chunk, chunk), chunk)]
            return (pltpu.make_async_copy(src_p, rows_v.at[b], rsem.at[b]),
                    pltpu.make_async_copy(src_s, rows_v.at[b], rsem.at[b]),
                    pltpu.make_async_copy(pos_hbm.at[c], idx_v.at[b], isem.at[b]))

        def start_loads(c, b):
            from_p, from_s, idx = loads(c, b)
            pl.when(c < nch_p)(from_p.start)
            pl.when(c >= nch_p)(from_s.start)
            idx.start()

        def wait_loads(c, b):
            from_p, from_s, idx = loads(c, b)
            pl.when(c < nch_p)(from_p.wait)
            pl.when(c >= nch_p)(from_s.wait)
            idx.wait()

        start_loads(first, 0)

        @pl.loop(0, per_worker, step=2)
        def _(g):
            for b in range(2):
                c = first + g + b
                wait_loads(c, b)
                pl.when(g + b + 1 < per_worker)(functools.partial(start_loads, c + 1, 1 - b))
                copies = [pltpu.async_copy(rows_v.at[b], xs_hbm.at[idx_v.at[b, k]], ssem)
                          for k in range(TOP_K)]
                for cp in copies:
                    cp.wait()

    return dispatch(hp_p, hp_s, pos_c)


def _sc_gather(ys, pos_c, chunk):
    w = ys.shape[1]
    n = pos_c.shape[0] * chunk
    nc, ns = _sc_workers()
    assert pos_c.shape[0] % (nc * ns) == 0
    per_worker = pos_c.shape[0] // (nc * ns)
    mesh = plsc.VectorSubcoreMesh(core_axis_name="c", subcore_axis_name="s")
    nbuf = 3

    @functools.partial(
        pl.kernel, mesh=mesh,
        out_type=jax.ShapeDtypeStruct((TOP_K, n, w), ys.dtype),
        scratch_types=[pltpu.VMEM((TOP_K, chunk), jnp.int32),
                       pltpu.VMEM((nbuf, chunk, w), ys.dtype),
                       pltpu.SemaphoreType.DMA((nbuf,)),
                       pltpu.SemaphoreType.DMA((nbuf,))],
    )
    def gather(ys_hbm, pos_hbm, out_hbm, idx_v, rows_v, gsem, wsem):
        wid = lax.axis_index("s") * nc + lax.axis_index("c")

        @pl.loop(0, per_worker)
        def _(ci):
            c = wid * per_worker + ci
            off = pl.multiple_of(c * chunk, chunk)
            pltpu.sync_copy(pos_hbm.at[c], idx_v)

            def start_gather(k):
                b = k % nbuf
                return pltpu.async_copy(ys_hbm.at[idx_v.at[k]], rows_v.at[b], gsem.at[b])

            gathers = {0: start_gather(0)}
            writes = {}
            for k in range(TOP_K):
                if k + 1 < TOP_K:
                    if k + 1 - nbuf >= 0:
                        writes.pop(k + 1 - nbuf).wait()
                    gathers[k + 1] = start_gather(k + 1)
                gathers.pop(k).wait()
                b = k % nbuf
                writes[k] = pltpu.async_copy(rows_v.at[b], out_hbm.at[k, pl.ds(off, chunk)],
                                             wsem.at[b])
            for k in sorted(writes):
                writes[k].wait()

    return gather(ys, pos_c)


def _experts_kernel(fe_ref, le_ref, off_ref, nxt_ref, slot_ref, xs_ref, wg_hbm, wu_hbm, wd_hbm,
                    ys_ref, wgb, wub, wdb, acc, wgf, wuf, wdf, wsem, cur_ref, *, tm):
    t = pl.program_id(0)
    base = t * tm
    half = xs_ref.shape[1]

    @pl.when(t == 0)
    def _():
        cur_ref[0] = -1

    def weight_copies(expert, slot):
        return (pltpu.make_async_copy(wg_hbm.at[expert], wgf.at[slot], wsem.at[slot, 0]),
                pltpu.make_async_copy(wu_hbm.at[expert], wuf.at[slot], wsem.at[slot, 1]),
                pltpu.make_async_copy(wd_hbm.at[expert], wdf.at[slot], wsem.at[slot, 2]))

    def load_weights(e):
        slot = slot_ref[e]

        @pl.when(cur_ref[0] < 0)
        def _():
            for cp in weight_copies(e, slot):
                cp.start()

        for cp in weight_copies(e, slot):
            cp.wait()
        wgb[...] = wgf[slot].astype(BF16)
        wub[...] = wuf[slot].astype(BF16)
        wdb[...] = wdf[slot].astype(BF16)

        @pl.when(nxt_ref[e] != e)
        def _():
            for cp in weight_copies(nxt_ref[e], 1 - slot):
                cp.start()

        cur_ref[0] = e

    def visit(e, carry):
        lo_row, hi_row = off_ref[e], off_ref[e + 1]
        _visit_expert(e, lo_row, hi_row)
        return carry

    def ffn(rows, m, r0, masked, lo_row, hi_row):
        lo, hi = _unpack_halves(xs_ref[rows, :])
        g = (jnp.dot(lo, wgb[:half], preferred_element_type=F32)
             + jnp.dot(hi, wgb[half:], preferred_element_type=F32))
        u = (jnp.dot(lo, wub[:half], preferred_element_type=F32)
             + jnp.dot(hi, wub[half:], preferred_element_type=F32))
        act = (g * jax.nn.sigmoid(g)) * u
        if masked:
            row = r0 + lax.broadcasted_iota(jnp.int32, (m, 1), 0)
            act = jnp.where((row >= lo_row) & (row < hi_row), act, 0.0)
        return jnp.dot(act.astype(BF16), wdb[...], preferred_element_type=F32)

    def shared_block(rows, m, r0, lo_row, hi_row):
        y = ffn(rows, m, r0, True, lo_row, hi_row)
        opens = lo_row <= r0

        @pl.when(opens)
        def _():
            acc[rows, :] = y

        @pl.when(jnp.logical_not(opens))
        def _():
            acc[rows, :] = acc[rows, :] + y

        ys_ref[rows, :] = _pack_halves(acc[rows, :])

    def _visit_expert(e, lo_row, hi_row):
        @pl.when(hi_row > lo_row)
        def _():
            pl.when(cur_ref[0] != e)(functools.partial(load_weights, e))
            for b in range(tm // EXPERT_SUB):
                rows = pl.ds(b * EXPERT_SUB, EXPERT_SUB)
                r0 = base + b * EXPERT_SUB
                touched = (lo_row < r0 + EXPERT_SUB) & (hi_row > r0)
                whole = (lo_row <= r0) & (hi_row >= r0 + EXPERT_SUB)

                @pl.when(touched & whole)
                def _(rows=rows, r0=r0):
                    ys_ref[rows, :] = _pack_halves(
                        ffn(rows, EXPERT_SUB, r0, False, lo_row, hi_row))

                pl.when(touched & jnp.logical_not(whole))(
                    functools.partial(shared_block, rows, EXPERT_SUB, r0, lo_row, hi_row))

    lax.fori_loop(fe_ref[t], le_ref[t] + 1, visit, 0)


def _experts(xs, sched, w_gate, w_up, w_down, tm):
    n_rows, half = xs.shape
    ne, d, de = w_gate.shape
    assert n_rows % tm == 0 and tm % EXPERT_SUB == 0

    grid_spec = pltpu.PrefetchScalarGridSpec(
        num_scalar_prefetch=len(sched),
        grid=(n_rows // tm,),
        in_specs=[
            pl.BlockSpec((tm, half), lambda t, *_: (t, 0)),
            pl.BlockSpec(memory_space=pl.ANY),
            pl.BlockSpec(memory_space=pl.ANY),
            pl.BlockSpec(memory_space=pl.ANY),
        ],
        out_specs=pl.BlockSpec((tm, half), lambda t, *_: (t, 0)),
        scratch_shapes=[
            pltpu.VMEM((d, de), BF16),
            pltpu.VMEM((d, de), BF16),
            pltpu.VMEM((de, d), BF16),
            pltpu.VMEM((tm, d), F32),
            pltpu.VMEM((2, d, de), F32),
            pltpu.VMEM((2, d, de), F32),
            pltpu.VMEM((2, de, d), F32),
            pltpu.SemaphoreType.DMA((2, 3)),
            pltpu.SMEM((1,), jnp.int32),
        ],
    )
    return pl.pallas_call(
        functools.partial(_experts_kernel, tm=tm),
        grid_spec=grid_spec,
        out_shape=jax.ShapeDtypeStruct((n_rows, half), jnp.uint32),
        compiler_params=_cparams("arbitrary"),
        name="experts",
    )(*sched, xs, w_gate, w_up, w_down)


def _expert_schedule(counts, n_rows, tm):
    ne = counts.shape[0]
    off = jnp.concatenate([jnp.zeros((1,), jnp.int32), jnp.cumsum(counts)]).astype(jnp.int32)
    tile_lo = jnp.arange(n_rows // tm, dtype=jnp.int32) * tm
    owner = lambda row: jnp.sum((off[None, 1:] <= row[:, None]).astype(jnp.int32), axis=1)
    first_e = owner(tile_lo)
    last_e = owner(tile_lo + (tm - 1))
    ids = jnp.arange(ne, dtype=jnp.int32)
    later = (ids[None, :] > ids[:, None]) & (counts[None, :] > 0)
    next_e = jnp.min(jnp.where(later, ids[None, :], ne), axis=1)
    next_e = jnp.where(next_e == ne, ids, next_e)
    slot = (jnp.cumsum((counts > 0).astype(jnp.int32)) - 1) % 2
    i32 = lambda v: v.astype(jnp.int32)
    return (i32(first_e), i32(last_e), off, i32(next_e), i32(slot))


def _combine_kernel(h_ref, gate_ref, yk_ref, wsg_ref, wsu_ref, wsd_ref, g2_ref, b2_ref, out_ref,
                    *, tt):
    h = h_ref[...]
    hb = h.astype(BF16)
    g = jnp.dot(hb, wsg_ref[...], preferred_element_type=F32)
    u = jnp.dot(hb, wsu_ref[...], preferred_element_type=F32)
    act = (g * jax.nn.sigmoid(g)) * u
    f = jnp.dot(act.astype(BF16), wsd_ref[...], preferred_element_type=F32)
    gate = gate_ref[...]
    r_lo = jnp.zeros((tt, yk_ref.shape[-1]), F32)
    r_hi = jnp.zeros((tt, yk_ref.shape[-1]), F32)
    for k in range(TOP_K):
        p = yk_ref[k]
        gk = gate[:, k:k + 1]
        r_lo = r_lo + gk * pltpu.bitcast(p << 16, F32)
        r_hi = r_hi + gk * pltpu.bitcast(p & jnp.uint32(0xFFFF0000), F32)
    f = f + jnp.concatenate([r_lo, r_hi], axis=-1)
    out_ref[...] = _layer_norm(DEEPNORM_ALPHA * h + f, g2_ref[...], b2_ref[...])


def _combine(h, gate_t, gate_row0, yk, wsg_bf, wsu_bf, wsd_bf, ln2_g, ln2_b, tt):
    n, d = h.shape
    assert n % tt == 0 and gate_row0 % tt == 0
    q0 = gate_row0 // tt
    ds_ = wsg_bf.shape[1]
    kern = functools.partial(_combine_kernel, tt=tt)
    return pl.pallas_call(
        kern,
        grid=(n // tt,),
        in_specs=[
            pl.BlockSpec((tt, d), lambda i: (i, 0)),
            pl.BlockSpec((tt, TOP_K), lambda i: (i + q0, 0)),
            pl.BlockSpec((TOP_K, tt, yk.shape[2]), lambda i: (0, i, 0)),
            pl.BlockSpec((d, ds_), lambda i: (0, 0)),
            pl.BlockSpec((d, ds_), lambda i: (0, 0)),
            pl.BlockSpec((ds_, d), lambda i: (0, 0)),
            pl.BlockSpec((1, d), lambda i: (0, 0)),
            pl.BlockSpec((1, d), lambda i: (0, 0)),
        ],
        out_specs=pl.BlockSpec((tt, d), lambda i: (i, 0)),
        out_shape=jax.ShapeDtypeStruct((n, d), F32),
        compiler_params=_cparams("parallel"),
        name="combine",
    )(h, gate_t, yk, wsg_bf, wsu_bf, wsd_bf, ln2_g.reshape(1, d), ln2_b.reshape(1, d))


def _pick_tile(seq, bsz, rows):
    return max(SUBLANES, min(seq, rows // bsz))


def _mixer(x, meta_state, u_meta_last, p):
    bsz, seq, _ = x.shape
    tt = _pick_tile(seq, bsz, 1024)
    gb, u, su = _embed_inproj(x, p["ln_emb_g"], p["ln_emb_b"], p["w_in_bf"], tt)
    tc = _pick_tile(seq, bsz, S5_CHUNK_ROWS)
    s0 = jnp.stack([jnp.broadcast_to(meta_state, (N_SLABS, SUBLANES, 2 * SLAB_STATE)),
                    jnp.zeros((N_SLABS, SUBLANES, 2 * SLAB_STATE), F32)])
    a_b = jnp.broadcast_to(p["s5_a"][:, :, None, :], (2, N_SLABS, SUBLANES, 2 * SLAB_STATE))
    yf, yr = _s5_scan(su, s0, p["s5_wb"], p["s5_wc"], a_b, bsz=bsz, tc=tc, dirs=(False, True),
                      emit_y=True, emit_state=False)
    tt3 = _pick_tile(seq, bsz, 512)
    h1, h1p = _mixer_tail(x, p["ln_emb_g"], p["ln_emb_b"], gb, u, u_meta_last, su, yf, yr,
                          p["conv_w"], p["conv_b"], p["ssm_d"], p["w_glu_bf"], p["b_glu"],
                          p["norm_a_g"], p["norm_b_g"], p["w_out_bf"], p["ln1_g"], p["ln1_b"], tt3)
    d = h1.shape[-1]
    return h1.reshape(bsz * seq, d), h1p.reshape(bsz * seq, d // 2)


def kernel(x_prompt, x_sample, meta_tokens, ln_emb_g, ln_emb_b, w_in, conv_w, conv_b, ssm_lambda_re, ssm_lambda_im, ssm_log_step, ssm_b_re, ssm_b_im, ssm_c_re, ssm_c_im, ssm_d, w_glu, b_glu, norm_a_g, norm_b_g, w_out, ln1_g, ln1_b, w_router, router_bias, w_exp_gate, w_exp_up, w_exp_down, w_sh_gate, w_sh_up, w_sh_down, ln2_g, ln2_b):
    l = 0
    d = x_prompt.shape[-1]
    dirs = [_s5_params(ssm_lambda_re[l, k].astype(F32), ssm_lambda_im[l, k].astype(F32),
                       ssm_log_step[l, k], ssm_b_re[l, k].astype(F32), ssm_b_im[l, k].astype(F32),
                       ssm_c_re[l, k], ssm_c_im[l, k]) for k in range(2)]
    p = dict(
        ln_emb_g=ln_emb_g, ln_emb_b=ln_emb_b, w_in_bf=w_in[l].astype(BF16),
        conv_w=conv_w[l], conv_b=conv_b[l], ssm_d=ssm_d[l],
        w_glu_bf=w_glu[l].astype(BF16), b_glu=b_glu[l], norm_a_g=norm_a_g[l],
        norm_b_g=norm_b_g[l], w_out_bf=w_out[l].astype(BF16), ln1_g=ln1_g[l], ln1_b=ln1_b[l],
        s5_wb=jnp.stack([dirs[0][0], dirs[1][0]]), s5_wc=jnp.stack([dirs[0][1], dirs[1][1]]),
        s5_a=jnp.stack([dirs[0][2], dirs[1][2]]),
    )
    mb = SUBLANES
    xm = jnp.broadcast_to(meta_tokens.astype(F32)[None], (mb, N_META, d))
    _, u_m, su_m = _embed_inproj(xm, ln_emb_g, ln_emb_b, p["w_in_bf"], N_META)
    a_m = jnp.broadcast_to(p["s5_a"][:1, :, None, :], (1, N_SLABS, mb, 2 * SLAB_STATE))
    (st_m,) = _s5_scan(su_m, jnp.zeros((1, N_SLABS, mb, 2 * SLAB_STATE), F32), p["s5_wb"][:1],
                       p["s5_wc"][:1], a_m, bsz=mb, tc=N_META, dirs=(False,), emit_y=False,
                       emit_state=True)
    meta_state = st_m[0, :, :1, :]
    u_meta_last = u_m[0, N_META - 1]

    h1_p, h1p_p = _mixer(x_prompt, meta_state, u_meta_last, p)
    h1_s, h1p_s = _mixer(x_sample, meta_state, u_meta_last, p)
    n = h1_p.shape[0] + h1_s.shape[0]

    tr = min(h1_p.shape[0], h1_s.shape[0], ROUTER_TILE)
    eidx, rank, gate, cnt = _router(h1_p, h1_s, w_router[l], router_bias[l], tr)
    counts = cnt[:, 0].astype(jnp.int32)
    sched = _expert_schedule(counts, n * TOP_K, EXPERT_TILE)
    offsets = sched[2]
    pos = _positions(eidx, rank, offsets[:-1], min(n, POSITIONS_TILE))
    pos_c = pos.reshape(TOP_K, n // SC_CHUNK, SC_CHUNK).transpose(1, 0, 2)
    xs = _sc_dispatch(h1p_p, h1p_s, pos_c, SC_CHUNK)
    ys = _experts(xs, sched, w_exp_gate[l], w_exp_up[l], w_exp_down[l], EXPERT_TILE)
    n_p = h1_p.shape[0]
    nch_p = n_p // SC_CHUNK
    td = min(n_p, h1_s.shape[0], COMBINE_TILE)
    gate_t = gate.T
    shared = (w_sh_gate[l].astype(BF16), w_sh_up[l].astype(BF16), w_sh_down[l].astype(BF16),
              ln2_g[l], ln2_b[l], td)
    yk_p = _sc_gather(ys, pos_c[:nch_p], SC_CHUNK)
    yk_s = _sc_gather(ys, pos_c[nch_p:], SC_CHUNK)
    out_p = _combine(h1_p, gate_t, 0, yk_p, *shared)
    out_s = _combine(h1_s, gate_t, n_p, yk_s, *shared)
    return (out_p.reshape(x_prompt.shape), out_s.reshape(x_sample.shape))
```

```python
import functools

import jax
import jax.numpy as jnp
from jax import lax
from jax.experimental import pallas as pl
from jax.experimental.pallas import tpu as pltpu
from jax.experimental.pallas import tpu_sc as plsc

F32 = jnp.float32
BF16 = jnp.bfloat16

N_META = 16
CONV_WIDTH = 3
SSM_GROUP = 16
SSM_STATE = 64
N_EXPERTS = 256
TOP_K = 8
N_EXPERT_GROUPS = 8
TOPK_GROUPS = 4
ROUTED_SCALE = 2.5
DEPTH = 1
DEEPNORM_ALPHA = (2.0 * DEPTH) ** 0.25
LN_EPS = 1e-5
RMS_EPS = 1e-6

LANES = 128
SUBLANES = 8
N_SLABS = 4
GROUPS_PER_SLAB = LANES // SSM_GROUP
SLAB_STATE = GROUPS_PER_SLAB * SSM_STATE
S5_CHUNK_ROWS = 256
EXPERT_TILE = 2048
EXPERT_SUB = 512
ROUTER_TILE = 512
COMBINE_TILE = 512
POSITIONS_TILE = 2048
SC_CHUNK = 64
VMEM_LIMIT = 48 * 1024 * 1024


def _cparams(*sem):
    return pltpu.CompilerParams(dimension_semantics=sem, vmem_limit_bytes=VMEM_LIMIT)


def _layer_norm(x, g, b):
    mu = jnp.mean(x, axis=-1, keepdims=True)
    xc = x - mu
    var = jnp.mean(xc * xc, axis=-1, keepdims=True)
    return xc * lax.rsqrt(var + LN_EPS) * g + b


def _rms_norm(x, g):
    return x * lax.rsqrt(jnp.mean(x * x, axis=-1, keepdims=True) + RMS_EPS) * g


def _pack_halves(x):
    half = x.shape[-1] // 2
    bits = pltpu.bitcast(x.astype(BF16).astype(F32), jnp.uint32)
    return (bits[:, :half] >> 16) | (bits[:, half:] & jnp.uint32(0xFFFF0000))


def _unpack_halves(p):
    lo = pltpu.bitcast(p << 16, F32).astype(BF16)
    hi = pltpu.bitcast(p & jnp.uint32(0xFFFF0000), F32).astype(BF16)
    return lo, hi


def _embed_inproj_kernel(x_ref, g_ref, b_ref, w_ref, gb_ref, u_ref, su_ref, *, bsz, tt):
    d = x_ref.shape[-1]
    x = x_ref[...].reshape(bsz * tt, d)
    h0 = _layer_norm(x, g_ref[...], b_ref[...])
    proj = jnp.dot(h0.astype(BF16), w_ref[...], preferred_element_type=F32)
    wc = gb_ref.shape[-1]
    gb_ref[...] = proj[:, :wc].reshape(bsz, tt, wc)
    u_ref[...] = (proj[:, wc:2 * wc] * proj[:, 2 * wc:3 * wc]).reshape(bsz, tt, wc)
    s_u = proj[:, 3 * wc:]
    for b in range(bsz):
        for j in range(N_SLABS):
            val = s_u[b * tt:(b + 1) * tt, j * LANES:(j + 1) * LANES]
            if bsz == 1:
                su_ref[j] = val
            else:
                su_ref[j, pl.ds(b, tt, stride=bsz), :] = val


def _embed_inproj(x, ln_g, ln_b, w_in_bf, tt):
    bsz, seq, d = x.shape
    e = w_in_bf.shape[1]
    wc = (e - N_SLABS * LANES) // 3
    nt = seq // tt
    kern = functools.partial(_embed_inproj_kernel, bsz=bsz, tt=tt)
    return pl.pallas_call(
        kern,
        grid=(nt,),
        in_specs=[
            pl.BlockSpec((bsz, tt, d), lambda i: (0, i, 0)),
            pl.BlockSpec((1, d), lambda i: (0, 0)),
            pl.BlockSpec((1, d), lambda i: (0, 0)),
            pl.BlockSpec((d, e), lambda i: (0, 0)),
        ],
        out_specs=[
            pl.BlockSpec((bsz, tt, wc), lambda i: (0, i, 0)),
            pl.BlockSpec((bsz, tt, wc), lambda i: (0, i, 0)),
            pl.BlockSpec((N_SLABS, tt * bsz, LANES), lambda i: (0, i, 0)),
        ],
        out_shape=[
            jax.ShapeDtypeStruct((bsz, seq, wc), F32),
            jax.ShapeDtypeStruct((bsz, seq, wc), F32),
            jax.ShapeDtypeStruct((N_SLABS, seq * bsz, LANES), F32),
        ],
        compiler_params=_cparams("parallel"),
        name="embed_inproj",
    )(x, ln_g.reshape(1, d), ln_b.reshape(1, d), w_in_bf)


def _s5_kernel(*refs, bsz, tc, dirs, emit_y, emit_state):
    nd = len(dirs)
    it = iter(refs)
    u_refs = [next(it) for _ in range(nd)]
    s0_ref, wb_ref, wc_ref, a_ref = next(it), next(it), next(it), next(it)
    y_refs = [next(it) for _ in range(nd)] if emit_y else []
    sf_ref = next(it) if emit_state else None
    bu_ref, st_ref = next(it), next(it)
    half = SLAB_STATE

    @pl.when(pl.program_id(0) == 0)
    def _():
        st_ref[...] = s0_ref[...]

    def project_in(j):
        for k in range(nd):
            bu_ref[k, j] = jnp.dot(u_refs[k][j].astype(BF16), wb_ref[k, j],
                                   preferred_element_type=F32)

    def project_out(j):
        for k in range(nd):
            y_refs[k][j] = jnp.dot(bu_ref[k, j].astype(BF16), wc_ref[k, j],
                                   preferred_element_type=F32)

    per_tile = SUBLANES // bsz
    n_tiles = tc // per_tile

    def cmul_add(a, s, x):
        return (a[0] * s[0] - a[1] * s[1] + x[0], a[0] * s[1] + a[1] * s[0] + x[1])

    def load(k, j, ti):
        rows = pl.ds(ti * SUBLANES, SUBLANES)
        return bu_ref[k, j, rows, :half], bu_ref[k, j, rows, half:]

    def store(k, j, ti, v):
        rows = pl.ds(ti * SUBLANES, SUBLANES)
        bu_ref[k, j, rows, :half] = v[0]
        bu_ref[k, j, rows, half:] = v[1]

    if per_tile == 1:
        def recur(j):
            for k in range(nd):
                a = (a_ref[k, j, :, :half], a_ref[k, j, :, half:])
                s = (st_ref[k, j, :, :half], st_ref[k, j, :, half:])
                for i in range(n_tiles):
                    ti = (n_tiles - 1 - i) if dirs[k] else i
                    s = cmul_add(a, s, load(k, j, ti))
                    store(k, j, ti, s)
                st_ref[k, j, :, :half] = s[0]
                st_ref[k, j, :, half:] = s[1]
    else:
        assert per_tile == 2 and tuple(dirs) == (False, True)
        low = lax.broadcasted_iota(jnp.int32, (SUBLANES, half), 0) < bsz

        def pick(p, q):
            return (jnp.where(low, p[0], q[0]), jnp.where(low, p[1], q[1]))

        def swap(v):
            return (pltpu.roll(v[0], bsz, axis=0), pltpu.roll(v[1], bsz, axis=0))

        def recur(j):
            af = (a_ref[0, j, :, :half], a_ref[0, j, :, half:])
            ar = (a_ref[1, j, :, :half], a_ref[1, j, :, half:])
            a_fr, a_rf = pick(af, ar), pick(ar, af)
            s = pick((st_ref[0, j, :, :half], st_ref[0, j, :, half:]),
                     (st_ref[1, j, :, :half], st_ref[1, j, :, half:]))
            for i in range(n_tiles):
                tf, tr = i, n_tiles - 1 - i
                xf, xr = load(0, j, tf), load(1, j, tr)
                s1 = cmul_add(a_fr, s, pick(xf, xr))
                s2 = cmul_add(a_rf, swap(s1), pick(xr, xf))
                store(0, j, tf, pick(s1, s2))
                store(1, j, tr, pick(s2, s1))
                s = swap(s2)
            for k in range(nd):
                st_ref[k, j, :, :half] = s[0]
                st_ref[k, j, :, half:] = s[1]

    project_in(0)
    for j in range(N_SLABS):
        if j + 1 < N_SLABS:
            project_in(j + 1)
        recur(j)
        if emit_y:
            project_out(j)
    if emit_state:
        sf_ref[...] = st_ref[...]


def _s5_scan(su, s0, wb, wc, a_b, *, bsz, tc, dirs, emit_y, emit_state):
    rows = su.shape[1]
    seq = rows // bsz
    assert SUBLANES % bsz == 0 and seq % tc == 0 and tc % (SUBLANES // bsz) == 0
    nc = seq // tc
    r = tc * bsz
    nd = len(dirs)
    sw = 2 * SLAB_STATE

    def u_map(rev):
        return (lambda c: (0, nc - 1 - c, 0)) if rev else (lambda c: (0, c, 0))

    in_specs = [pl.BlockSpec((N_SLABS, r, LANES), u_map(rev)) for rev in dirs]
    in_specs += [
        pl.BlockSpec((nd, N_SLABS, SUBLANES, sw), lambda c: (0, 0, 0, 0)),
        pl.BlockSpec((nd, N_SLABS, LANES, sw), lambda c: (0, 0, 0, 0)),
        pl.BlockSpec((nd, N_SLABS, sw, LANES), lambda c: (0, 0, 0, 0)),
        pl.BlockSpec((nd, N_SLABS, SUBLANES, sw), lambda c: (0, 0, 0, 0)),
    ]
    out_specs, out_shape = [], []
    if emit_y:
        for rev in dirs:
            out_specs.append(pl.BlockSpec((N_SLABS, r, LANES), u_map(rev)))
            out_shape.append(jax.ShapeDtypeStruct((N_SLABS, rows, LANES), F32))
    if emit_state:
        out_specs.append(pl.BlockSpec((nd, N_SLABS, SUBLANES, sw), lambda c: (0, 0, 0, 0)))
        out_shape.append(jax.ShapeDtypeStruct((nd, N_SLABS, SUBLANES, sw), F32))
    kern = functools.partial(_s5_kernel, bsz=bsz, tc=tc, dirs=dirs, emit_y=emit_y,
                             emit_state=emit_state)
    return pl.pallas_call(
        kern,
        grid=(nc,),
        in_specs=in_specs,
        out_specs=out_specs,
        out_shape=out_shape,
        scratch_shapes=[
            pltpu.VMEM((nd, N_SLABS, r, sw), F32),
            pltpu.VMEM((nd, N_SLABS, SUBLANES, sw), F32),
        ],
        compiler_params=_cparams("arbitrary"),
        name="s5_scan",
    )(*([su] * nd), s0, wb, wc, a_b)


def _s5_params(lam_re, lam_im, log_step, b_re, b_im, c_re, c_im):
    g, p = lam_re.shape
    h = b_re.shape[-1]
    dt = jnp.exp(log_step.astype(F32))[:, None]
    mag = jnp.exp(lam_re * dt)
    ab_re = mag * jnp.cos(lam_im * dt)
    ab_im = mag * jnp.sin(lam_im * dt)
    den = lam_re * lam_re + lam_im * lam_im
    nr, ni = ab_re - 1.0, ab_im
    f_re = (nr * lam_re + ni * lam_im) / den
    f_im = (ni * lam_re - nr * lam_im) / den
    bb_re = f_re[..., None] * b_re - f_im[..., None] * b_im
    bb_im = f_re[..., None] * b_im + f_im[..., None] * b_re
    ns, gl = N_SLABS, GROUPS_PER_SLAB
    eye = jnp.eye(gl, dtype=F32)

    def in_block(bb):
        bb = bb.reshape(ns, gl, p, h)
        return jnp.einsum('sgph,gk->sghkp', bb, eye).reshape(ns, gl * h, gl * p)

    def out_block(cc):
        cc = cc.reshape(ns, gl, h, p)
        return jnp.einsum('sghp,gk->sgpkh', cc, eye).reshape(ns, gl * p, gl * h)

    wb = jnp.concatenate([in_block(bb_re), in_block(bb_im)], axis=-1)
    wc = jnp.concatenate([out_block(c_re.astype(F32)), -out_block(c_im.astype(F32))], axis=1)
    a = jnp.concatenate([ab_re.reshape(ns, gl * p), ab_im.reshape(ns, gl * p)], axis=-1)
    return wb.astype(BF16), wc.astype(BF16), a


def _mixer_tail_kernel(x_ref, ge_ref, be_ref, gb_ref, u_ref, up_ref, un_ref, um_ref, su_ref,
                       yf_ref, yr_ref, cw_ref, cb_ref, sd_ref, wg_ref, bg_ref, na_ref, nb_ref,
                       wo_ref, g1_ref, b1_ref, h1_ref, h1p_ref, *, bsz, tt):
    i = pl.program_id(0)
    nt = pl.num_programs(0)
    d = x_ref.shape[-1]
    wcv = gb_ref.shape[-1]
    row_id = lax.broadcasted_iota(jnp.int32, (tt, wcv), 0)
    ya, ys = [], []
    for b in range(bsz):
        u = u_ref[b]
        prev_edge = jnp.where(i == 0, um_ref[...], up_ref[b, SUBLANES - 1:SUBLANES, :])
        next_edge = jnp.where(i == nt - 1, jnp.zeros((1, wcv), F32), un_ref[b, 0:1, :])
        u_prev = jnp.where(row_id == 0, prev_edge, pltpu.roll(u, 1, axis=0))
        u_next = jnp.where(row_id == tt - 1, next_edge, pltpu.roll(u, tt - 1, axis=0))
        conv = u_prev * cw_ref[0:1, :] + u * cw_ref[1:2, :] + u_next * cw_ref[2:3, :] + cb_ref[...]
        ya.append(gb_ref[b] * conv)

        def slab(ref, b=b):
            parts = []
            for j in range(N_SLABS):
                if bsz == 1:
                    parts.append(ref[j])
                else:
                    parts.append(ref[j, pl.ds(b, tt, stride=bsz), :])
            return jnp.concatenate(parts, axis=-1)

        ys.append(slab(yf_ref) + slab(yr_ref) + sd_ref[...] * slab(su_ref))
    y_a = jnp.concatenate(ya, axis=0)
    y_s = jnp.concatenate(ys, axis=0)
    z = jax.nn.gelu(y_s)
    glu = jnp.dot(z.astype(BF16), wg_ref[...], preferred_element_type=F32) + bg_ref[...]
    y_b = z * jax.nn.sigmoid(glu)
    merged = jnp.concatenate([_rms_norm(y_a, na_ref[...]), _rms_norm(y_b, nb_ref[...])], axis=-1)
    m = jnp.dot(merged.astype(BF16), wo_ref[...], preferred_element_type=F32)
    h0 = _layer_norm(x_ref[...].reshape(bsz * tt, d), ge_ref[...], be_ref[...])
    h1 = _layer_norm(DEEPNORM_ALPHA * h0 + m, g1_ref[...], b1_ref[...])
    h1_ref[...] = h1.reshape(bsz, tt, d)
    h1p_ref[...] = _pack_halves(h1).reshape(bsz, tt, d // 2)


def _mixer_tail(x, ln_emb_g, ln_emb_b, gb, u, u_meta_last, su, yf, yr, conv_w, conv_b, ssm_d,
                w_glu_bf, b_glu, norm_a_g, norm_b_g, w_out_bf, ln1_g, ln1_b, tt):
    bsz, seq, d = x.shape
    wcv = gb.shape[-1]
    ws = N_SLABS * LANES
    nt = seq // tt
    tb = tt // SUBLANES
    nb8 = seq // SUBLANES
    kern = functools.partial(_mixer_tail_kernel, bsz=bsz, tt=tt)
    row = lambda n: pl.BlockSpec((1, n), lambda i: (0, 0))
    slab_spec = pl.BlockSpec((N_SLABS, tt * bsz, LANES), lambda i: (0, i, 0))
    return pl.pallas_call(
        kern,
        grid=(nt,),
        in_specs=[
            pl.BlockSpec((bsz, tt, d), lambda i: (0, i, 0)),
            row(d), row(d),
            pl.BlockSpec((bsz, tt, wcv), lambda i: (0, i, 0)),
            pl.BlockSpec((bsz, tt, wcv), lambda i: (0, i, 0)),
            pl.BlockSpec((bsz, SUBLANES, wcv), lambda i: (0, jnp.maximum(i * tb - 1, 0), 0)),
            pl.BlockSpec((bsz, SUBLANES, wcv), lambda i: (0, jnp.minimum((i + 1) * tb, nb8 - 1), 0)),
            row(wcv),
            slab_spec, slab_spec, slab_spec,
            pl.BlockSpec((CONV_WIDTH, wcv), lambda i: (0, 0)),
            row(wcv), row(ws),
            pl.BlockSpec((ws, ws), lambda i: (0, 0)),
            row(ws), row(wcv), row(ws),
            pl.BlockSpec((wcv + ws, d), lambda i: (0, 0)),
            row(d), row(d),
        ],
        out_specs=[
            pl.BlockSpec((bsz, tt, d), lambda i: (0, i, 0)),
            pl.BlockSpec((bsz, tt, d // 2), lambda i: (0, i, 0)),
        ],
        out_shape=[
            jax.ShapeDtypeStruct((bsz, seq, d), F32),
            jax.ShapeDtypeStruct((bsz, seq, d // 2), jnp.uint32),
        ],
        compiler_params=_cparams("parallel"),
        name="mixer_tail",
    )(x, ln_emb_g.reshape(1, d), ln_emb_b.reshape(1, d), gb, u, u, u,
      u_meta_last.reshape(1, wcv), su, yf, yr, conv_w, conv_b.reshape(1, wcv),
      ssm_d.reshape(1, ws), w_glu_bf, b_glu.reshape(1, ws), norm_a_g.reshape(1, wcv),
      norm_b_g.reshape(1, ws), w_out_bf, ln1_g.reshape(1, d), ln1_b.reshape(1, d))


def _dual_row_specs(rows, width, nq_p):
    return [pl.BlockSpec((rows, width), lambda q, *_: (jnp.minimum(q, nq_p - 1), 0)),
            pl.BlockSpec((rows, width), lambda q, *_: (jnp.maximum(q - nq_p, 0), 0))]


def _router_kernel(hp_ref, hs_ref, wh_ref, bias_ref, eidx_ref, rank_ref, gate_ref, cnt_ref,
                   cnt_scr, *, tt, nq_p):
    ne = wh_ref.shape[0]
    epg = ne // N_EXPERT_GROUPS
    neg = jnp.float32(-jnp.inf)

    @pl.when(pl.program_id(0) == 0)
    def _():
        cnt_scr[...] = jnp.zeros_like(cnt_scr)

    h = jnp.where(pl.program_id(0) < nq_p, hp_ref[...], hs_ref[...])
    dn = (((1,), (1,)), ((), ()))
    logits = lax.dot_general(wh_ref[...], h.astype(BF16), dn,
                             preferred_element_type=F32)
    scores = jax.nn.sigmoid(logits)
    sel = scores + bias_ref[...]

    gi = lax.broadcasted_iota(jnp.int32, (epg, tt), 0)
    gs = []
    for g in range(N_EXPERT_GROUPS):
        x = sel[g * epg:(g + 1) * epg, :]
        m1 = jnp.max(x, axis=0, keepdims=True)
        i1 = jnp.min(jnp.where(x == m1, gi, epg), axis=0, keepdims=True)
        m2 = jnp.max(jnp.where(gi == i1, neg, x), axis=0, keepdims=True)
        gs.append(m1 + m2)
    chosen = [jnp.zeros((1, tt), F32) for _ in range(N_EXPERT_GROUPS)]
    for _ in range(TOPK_GROUPS):
        m = gs[0]
        for g in range(1, N_EXPERT_GROUPS):
            m = jnp.maximum(m, gs[g])
        found = jnp.zeros((1, tt), F32)
        for g in range(N_EXPERT_GROUPS):
            hit = jnp.where((gs[g] == m) & (found == 0.0), 1.0, 0.0)
            chosen[g] = chosen[g] + hit
            found = found + hit
            gs[g] = jnp.where(hit > 0.0, neg, gs[g])
    selm = jnp.concatenate(
        [jnp.where(chosen[g] > 0.0, sel[g * epg:(g + 1) * epg, :], neg)
         for g in range(N_EXPERT_GROUPS)], axis=0)

    ei = lax.broadcasted_iota(jnp.int32, (ne, tt), 0)
    msel = jnp.zeros((ne, tt), F32)
    idxs, gvals = [], []
    for _ in range(TOP_K):
        m = jnp.max(selm, axis=0, keepdims=True)
        idx = jnp.min(jnp.where(selm == m, ei, ne), axis=0, keepdims=True)
        hit = ei == idx
        gvals.append(jnp.sum(jnp.where(hit, scores, 0.0), axis=0, keepdims=True))
        selm = jnp.where(hit, neg, selm)
        msel = jnp.where(hit, 1.0, msel)
        idxs.append(idx)
    gsum = gvals[0]
    for k in range(1, TOP_K):
        gsum = gsum + gvals[k]
    gate_ref[...] = jnp.concatenate([gv / gsum * ROUTED_SCALE for gv in gvals], axis=0)
    eidx_ref[...] = jnp.concatenate(idxs, axis=0)

    r_i = lax.broadcasted_iota(jnp.int32, (tt, tt), 0)
    c_i = lax.broadcasted_iota(jnp.int32, (tt, tt), 1)
    upper = jnp.where(r_i < c_i, 1.0, 0.0).astype(BF16)
    rank_full = jnp.dot(msel.astype(BF16), upper, preferred_element_type=F32) + cnt_scr[...]
    ranks = [jnp.sum(jnp.where(ei == idxs[k], rank_full, 0.0), axis=0, keepdims=True)
             for k in range(TOP_K)]
    rank_ref[...] = jnp.concatenate(ranks, axis=0).astype(jnp.int32)
    cnt_scr[...] = cnt_scr[...] + jnp.sum(msel, axis=1, keepdims=True)
    cnt_ref[...] = cnt_scr[...]


def _router(h_p, h_s, w_router, router_bias, tt):
    d = h_p.shape[1]
    n = h_p.shape[0] + h_s.shape[0]
    assert h_p.shape[0] % tt == 0 and h_s.shape[0] % tt == 0
    nq_p = h_p.shape[0] // tt
    ne = w_router.shape[1]
    wh = w_router.T.astype(BF16)
    kern = functools.partial(_router_kernel, tt=tt, nq_p=nq_p)
    return pl.pallas_call(
        kern,
        grid=(n // tt,),
        in_specs=_dual_row_specs(tt, d, nq_p) + [
            pl.BlockSpec((ne, d), lambda i: (0, 0)),
            pl.BlockSpec((ne, 1), lambda i: (0, 0)),
        ],
        out_specs=[
            pl.BlockSpec((TOP_K, tt), lambda i: (0, i)),
            pl.BlockSpec((TOP_K, tt), lambda i: (0, i)),
            pl.BlockSpec((TOP_K, tt), lambda i: (0, i)),
            pl.BlockSpec((ne, 1), lambda i: (0, 0)),
        ],
        out_shape=[
            jax.ShapeDtypeStruct((TOP_K, n), jnp.int32),
            jax.ShapeDtypeStruct((TOP_K, n), jnp.int32),
            jax.ShapeDtypeStruct((TOP_K, n), F32),
            jax.ShapeDtypeStruct((ne, 1), F32),
        ],
        scratch_shapes=[pltpu.VMEM((ne, 1), F32)],
        compiler_params=_cparams("arbitrary"),
        name="router",
    )(h_p, h_s, wh, router_bias.astype(F32).reshape(ne, 1))


def _positions_kernel(eidx_ref, rank_ref, start_ref, pos_ref):
    ne = start_ref.shape[0]
    tt = eidx_ref.shape[1]
    ei = lax.broadcasted_iota(jnp.int32, (ne, tt), 0)
    start = start_ref[...]
    rows = [jnp.sum(jnp.where(ei == eidx_ref[k:k + 1, :], start, 0.0), axis=0, keepdims=True)
            for k in range(TOP_K)]
    pos_ref[...] = jnp.concatenate(rows, axis=0).astype(jnp.int32) + rank_ref[...]


def _positions(eidx, rank, start, tt):
    n = eidx.shape[1]
    ne = start.shape[0]
    return pl.pallas_call(
        _positions_kernel,
        grid=(n // tt,),
        in_specs=[
            pl.BlockSpec((TOP_K, tt), lambda i: (0, i)),
            pl.BlockSpec((TOP_K, tt), lambda i: (0, i)),
            pl.BlockSpec((ne, 1), lambda i: (0, 0)),
        ],
        out_specs=pl.BlockSpec((TOP_K, tt), lambda i: (0, i)),
        out_shape=jax.ShapeDtypeStruct((TOP_K, n), jnp.int32),
        compiler_params=_cparams("parallel"),
        name="positions",
    )(eidx, rank, start.astype(F32).reshape(ne, 1))


def _sc_workers():
    info = plsc.get_sparse_core_info()
    return info.num_cores, info.num_subcores


def _sc_dispatch(hp_p, hp_s, pos_c, chunk):
    w = hp_p.shape[1]
    n = hp_p.shape[0] + hp_s.shape[0]
    nch_p = hp_p.shape[0] // chunk
    nc, ns = _sc_workers()
    assert hp_p.shape[0] % chunk == 0 and n % (chunk * nc * ns) == 0
    per_worker = (n // chunk) // (nc * ns)
    mesh = plsc.VectorSubcoreMesh(core_axis_name="c", subcore_axis_name="s")

    assert per_worker % 2 == 0

    @functools.partial(
        pl.kernel, mesh=mesh,
        out_type=jax.ShapeDtypeStruct((n * TOP_K, w), hp_p.dtype),
        scratch_types=[pltpu.VMEM((2, TOP_K, chunk), jnp.int32),
                       pltpu.VMEM((2, chunk, w), hp_p.dtype),
                       pltpu.SemaphoreType.DMA((2,)),
                       pltpu.SemaphoreType.DMA((2,)),
                       pltpu.SemaphoreType.DMA],
    )
    def dispatch(hp_hbm, hs_hbm, pos_hbm, xs_hbm, idx_v, rows_v, rsem, isem, ssem):
        wid = lax.axis_index("s") * nc + lax.axis_index("c")
        first = wid * per_worker

        def loads(c, b):
            cp_ = jnp.minimum(c, nch_p - 1)
            cs_ = jnp.maximum(c - nch_p, 0)
            src_p = hp_hbm.at[pl.ds(pl.multiple_of(cp_ * chunk, chunk), chunk)]
            src_s = hs_hbm.at[pl.ds(pl.multiple_of(cs_ * chunk, chunk), chunk)]
            return (pltpu.make_async_copy(src_p, rows_v.at[b], rsem.at[b]),
                    pltpu.make_async_copy(src_s, rows_v.at[b], rsem.at[b]),
                    pltpu.make_async_copy(pos_hbm.at[c], idx_v.at[b], isem.at[b]))

        def start_loads(c, b):
            from_p, from_s, idx = loads(c, b)
            pl.when(c < nch_p)(from_p.start)
            pl.when(c >= nch_p)(from_s.start)
            idx.start()

        def wait_loads(c, b):
            from_p, from_s, idx = loads(c, b)
            pl.when(c < nch_p)(from_p.wait)
            pl.when(c >= nch_p)(from_s.wait)
            idx.wait()

        start_loads(first, 0)

        @pl.loop(0, per_worker, step=2)
        def _(g):
            for b in range(2):
                c = first + g + b
                wait_loads(c, b)
                pl.when(g + b + 1 < per_worker)(functools.partial(start_loads, c + 1, 1 - b))
                copies = [pltpu.async_copy(rows_v.at[b], xs_hbm.at[idx_v.at[b, k]], ssem)
                          for k in range(TOP_K)]
                for cp in copies:
                    cp.wait()

    return dispatch(hp_p, hp_s, pos_c)


def _sc_gather(ys, pos_c, chunk):
    w = ys.shape[1]
    n = pos_c.shape[0] * chunk
    nc, ns = _sc_workers()
    assert pos_c.shape[0] % (nc * ns) == 0
    per_worker = pos_c.shape[0] // (nc * ns)
    mesh = plsc.VectorSubcoreMesh(core_axis_name="c", subcore_axis_name="s")
    nbuf = 3

    @functools.partial(
        pl.kernel, mesh=mesh,
        out_type=jax.ShapeDtypeStruct((TOP_K, n, w), ys.dtype),
        scratch_types=[pltpu.VMEM((TOP_K, chunk), jnp.int32),
                       pltpu.VMEM((nbuf, chunk, w), ys.dtype),
                       pltpu.SemaphoreType.DMA((nbuf,)),
                       pltpu.SemaphoreType.DMA((nbuf,))],
    )
    def gather(ys_hbm, pos_hbm, out_hbm, idx_v, rows_v, gsem, wsem):
        wid = lax.axis_index("s") * nc + lax.axis_index("c")

        @pl.loop(0, per_worker)
        def _(ci):
            c = wid * per_worker + ci
            off = pl.multiple_of(c * chunk, chunk)
            pltpu.sync_copy(pos_hbm.at[c], idx_v)

            def start_gather(k):
                b = k % nbuf
                return pltpu.async_copy(ys_hbm.at[idx_v.at[k]], rows_v.at[b], gsem.at[b])

            gathers = {0: start_gather(0)}
            writes = {}
            for k in range(TOP_K):
                if k + 1 < TOP_K:
                    if k + 1 - nbuf >= 0:
                        writes.pop(k + 1 - nbuf).wait()
                    gathers[k + 1] = start_gather(k + 1)
                gathers.pop(k).wait()
                b = k % nbuf
                writes[k] = pltpu.async_copy(rows_v.at[b], out_hbm.at[k, pl.ds(off, chunk)],
                                             wsem.at[b])
            for k in sorted(writes):
                writes[k].wait()

    return gather(ys, pos_c)


def _experts_kernel(fe_ref, le_ref, off_ref, nxt_ref, slot_ref, xs_ref, wg_hbm, wu_hbm, wd_hbm,
                    ys_ref, wgb, wub, wdb, acc, wgf, wuf, wdf, wsem, cur_ref, *, tm):
    t = pl.program_id(0)
    base = t * tm
    half = xs_ref.shape[1]

    @pl.when(t == 0)
    def _():
        cur_ref[0] = -1

    def weight_copies(expert, slot):
        return (pltpu.make_async_copy(wg_hbm.at[expert], wgf.at[slot], wsem.at[slot, 0]),
                pltpu.make_async_copy(wu_hbm.at[expert], wuf.at[slot], wsem.at[slot, 1]),
                pltpu.make_async_copy(wd_hbm.at[expert], wdf.at[slot], wsem.at[slot, 2]))

    def load_weights(e):
        slot = slot_ref[e]

        @pl.when(cur_ref[0] < 0)
        def _():
            for cp in weight_copies(e, slot):
                cp.start()

        for cp in weight_copies(e, slot):
            cp.wait()
        wgb[...] = wgf[slot].astype(BF16)
        wub[...] = wuf[slot].astype(BF16)
        wdb[...] = wdf[slot].astype(BF16)

        @pl.when(nxt_ref[e] != e)
        def _():
            for cp in weight_copies(nxt_ref[e], 1 - slot):
                cp.start()

        cur_ref[0] = e

    def visit(e, carry):
        lo_row, hi_row = off_ref[e], off_ref[e + 1]
        _visit_expert(e, lo_row, hi_row)
        return carry

    def ffn(rows, m, r0, masked, lo_row, hi_row):
        lo, hi = _unpack_halves(xs_ref[rows, :])
        g = (jnp.dot(lo, wgb[:half], preferred_element_type=F32)
             + jnp.dot(hi, wgb[half:], preferred_element_type=F32))
        u = (jnp.dot(lo, wub[:half], preferred_element_type=F32)
             + jnp.dot(hi, wub[half:], preferred_element_type=F32))
        act = (g * jax.nn.sigmoid(g)) * u
        if masked:
            row = r0 + lax.broadcasted_iota(jnp.int32, (m, 1), 0)
            act = jnp.where((row >= lo_row) & (row < hi_row), act, 0.0)
        return jnp.dot(act.astype(BF16), wdb[...], preferred_element_type=F32)

    def shared_block(rows, m, r0, lo_row, hi_row):
        y = ffn(rows, m, r0, True, lo_row, hi_row)
        opens = lo_row <= r0

        @pl.when(opens)
        def _():
            acc[rows, :] = y

        @pl.when(jnp.logical_not(opens))
        def _():
            acc[rows, :] = acc[rows, :] + y

        ys_ref[rows, :] = _pack_halves(acc[rows, :])

    def _visit_expert(e, lo_row, hi_row):
        @pl.when(hi_row > lo_row)
        def _():
            pl.when(cur_ref[0] != e)(functools.partial(load_weights, e))
            for b in range(tm // EXPERT_SUB):
                rows = pl.ds(b * EXPERT_SUB, EXPERT_SUB)
                r0 = base + b * EXPERT_SUB
                touched = (lo_row < r0 + EXPERT_SUB) & (hi_row > r0)
                whole = (lo_row <= r0) & (hi_row >= r0 + EXPERT_SUB)

                @pl.when(touched & whole)
                def _(rows=rows, r0=r0):
                    ys_ref[rows, :] = _pack_halves(
                        ffn(rows, EXPERT_SUB, r0, False, lo_row, hi_row))

                pl.when(touched & jnp.logical_not(whole))(
                    functools.partial(shared_block, rows, EXPERT_SUB, r0, lo_row, hi_row))

    lax.fori_loop(fe_ref[t], le_ref[t] + 1, visit, 0)


def _experts(xs, sched, w_gate, w_up, w_down, tm):
    n_rows, half = xs.shape
    ne, d, de = w_gate.shape
    assert n_rows % tm == 0 and tm % EXPERT_SUB == 0

    grid_spec = pltpu.PrefetchScalarGridSpec(
        num_scalar_prefetch=len(sched),
        grid=(n_rows // tm,),
        in_specs=[
            pl.BlockSpec((tm, half), lambda t, *_: (t, 0)),
            pl.BlockSpec(memory_space=pl.ANY),
            pl.BlockSpec(memory_space=pl.ANY),
            pl.BlockSpec(memory_space=pl.ANY),
        ],
        out_specs=pl.BlockSpec((tm, half), lambda t, *_: (t, 0)),
        scratch_shapes=[
            pltpu.VMEM((d, de), BF16),
            pltpu.VMEM((d, de), BF16),
            pltpu.VMEM((de, d), BF16),
            pltpu.VMEM((tm, d), F32),
            pltpu.VMEM((2, d, de), F32),
            pltpu.VMEM((2, d, de), F32),
            pltpu.VMEM((2, de, d), F32),
            pltpu.SemaphoreType.DMA((2, 3)),
            pltpu.SMEM((1,), jnp.int32),
        ],
    )
    return pl.pallas_call(
        functools.partial(_experts_kernel, tm=tm),
        grid_spec=grid_spec,
        out_shape=jax.ShapeDtypeStruct((n_rows, half), jnp.uint32),
        compiler_params=_cparams("arbitrary"),
        name="experts",
    )(*sched, xs, w_gate, w_up, w_down)


def _expert_schedule(counts, n_rows, tm):
    ne = counts.shape[0]
    off = jnp.concatenate([jnp.zeros((1,), jnp.int32), jnp.cumsum(counts)]).astype(jnp.int32)
    tile_lo = jnp.arange(n_rows // tm, dtype=jnp.int32) * tm
    owner = lambda row: jnp.sum((off[None, 1:] <= row[:, None]).astype(jnp.int32), axis=1)
    first_e = owner(tile_lo)
    last_e = owner(tile_lo + (tm - 1))
    ids = jnp.arange(ne, dtype=jnp.int32)
    later = (ids[None, :] > ids[:, None]) & (counts[None, :] > 0)
    next_e = jnp.min(jnp.where(later, ids[None, :], ne), axis=1)
    next_e = jnp.where(next_e == ne, ids, next_e)
    slot = (jnp.cumsum((counts > 0).astype(jnp.int32)) - 1) % 2
    i32 = lambda v: v.astype(jnp.int32)
    return (i32(first_e), i32(last_e), off, i32(next_e), i32(slot))


def _combine_kernel(h_ref, gate_ref, yk_ref, wsg_ref, wsu_ref, wsd_ref, g2_ref, b2_ref, out_ref,
                    *, tt):
    h = h_ref[...]
    hb = h.astype(BF16)
    g = jnp.dot(hb, wsg_ref[...], preferred_element_type=F32)
    u = jnp.dot(hb, wsu_ref[...], preferred_element_type=F32)
    act = (g * jax.nn.sigmoid(g)) * u
    f = jnp.dot(act.astype(BF16), wsd_ref[...], preferred_element_type=F32)
    gate = gate_ref[...]
    r_lo = jnp.zeros((tt, yk_ref.shape[-1]), F32)
    r_hi = jnp.zeros((tt, yk_ref.shape[-1]), F32)
    for k in range(TOP_K):
        p = yk_ref[k]
        gk = gate[:, k:k + 1]
        r_lo = r_lo + gk * pltpu.bitcast(p << 16, F32)
        r_hi = r_hi + gk * pltpu.bitcast(p & jnp.uint32(0xFFFF0000), F32)
    f = f + jnp.concatenate([r_lo, r_hi], axis=-1)
    out_ref[...] = _layer_norm(DEEPNORM_ALPHA * h + f, g2_ref[...], b2_ref[...])


def _combine(h, gate_t, gate_row0, yk, wsg_bf, wsu_bf, wsd_bf, ln2_g, ln2_b, tt):
    n, d = h.shape
    assert n % tt == 0 and gate_row0 % tt == 0
    q0 = gate_row0 // tt
    ds_ = wsg_bf.shape[1]
    kern = functools.partial(_combine_kernel, tt=tt)
    return pl.pallas_call(
        kern,
        grid=(n // tt,),
        in_specs=[
            pl.BlockSpec((tt, d), lambda i: (i, 0)),
            pl.BlockSpec((tt, TOP_K), lambda i: (i + q0, 0)),
            pl.BlockSpec((TOP_K, tt, yk.shape[2]), lambda i: (0, i, 0)),
            pl.BlockSpec((d, ds_), lambda i: (0, 0)),
            pl.BlockSpec((d, ds_), lambda i: (0, 0)),
            pl.BlockSpec((ds_, d), lambda i: (0, 0)),
            pl.BlockSpec((1, d), lambda i: (0, 0)),
            pl.BlockSpec((1, d), lambda i: (0, 0)),
        ],
        out_specs=pl.BlockSpec((tt, d), lambda i: (i, 0)),
        out_shape=jax.ShapeDtypeStruct((n, d), F32),
        compiler_params=_cparams("parallel"),
        name="combine",
    )(h, gate_t, yk, wsg_bf, wsu_bf, wsd_bf, ln2_g.reshape(1, d), ln2_b.reshape(1, d))


def _pick_tile(seq, bsz, rows):
    return max(SUBLANES, min(seq, rows // bsz))


def _mixer(x, meta_state, u_meta_last, p):
    bsz, seq, _ = x.shape
    tt = _pick_tile(seq, bsz, 1024)
    gb, u, su = _embed_inproj(x, p["ln_emb_g"], p["ln_emb_b"], p["w_in_bf"], tt)
    tc = _pick_tile(seq, bsz, S5_CHUNK_ROWS)
    s0 = jnp.stack([jnp.broadcast_to(meta_state, (N_SLABS, SUBLANES, 2 * SLAB_STATE)),
                    jnp.zeros((N_SLABS, SUBLANES, 2 * SLAB_STATE), F32)])
    a_b = jnp.broadcast_to(p["s5_a"][:, :, None, :], (2, N_SLABS, SUBLANES, 2 * SLAB_STATE))
    yf, yr = _s5_scan(su, s0, p["s5_wb"], p["s5_wc"], a_b, bsz=bsz, tc=tc, dirs=(False, True),
                      emit_y=True, emit_state=False)
    tt3 = _pick_tile(seq, bsz, 512)
    h1, h1p = _mixer_tail(x, p["ln_emb_g"], p["ln_emb_b"], gb, u, u_meta_last, su, yf, yr,
                          p["conv_w"], p["conv_b"], p["ssm_d"], p["w_glu_bf"], p["b_glu"],
                          p["norm_a_g"], p["norm_b_g"], p["w_out_bf"], p["ln1_g"], p["ln1_b"], tt3)
    d = h1.shape[-1]
    return h1.reshape(bsz * seq, d), h1p.reshape(bsz * seq, d // 2)


def kernel(x_prompt, x_sample, meta_tokens, ln_emb_g, ln_emb_b, w_in, conv_w, conv_b, ssm_lambda_re, ssm_lambda_im, ssm_log_step, ssm_b_re, ssm_b_im, ssm_c_re, ssm_c_im, ssm_d, w_glu, b_glu, norm_a_g, norm_b_g, w_out, ln1_g, ln1_b, w_router, router_bias, w_exp_gate, w_exp_up, w_exp_down, w_sh_gate, w_sh_up, w_sh_down, ln2_g, ln2_b):
    l = 0
    d = x_prompt.shape[-1]
    dirs = [_s5_params(ssm_lambda_re[l, k].astype(F32), ssm_lambda_im[l, k].astype(F32),
                       ssm_log_step[l, k], ssm_b_re[l, k].astype(F32), ssm_b_im[l, k].astype(F32),
                       ssm_c_re[l, k], ssm_c_im[l, k]) for k in range(2)]
    p = dict(
        ln_emb_g=ln_emb_g, ln_emb_b=ln_emb_b, w_in_bf=w_in[l].astype(BF16),
        conv_w=conv_w[l], conv_b=conv_b[l], ssm_d=ssm_d[l],
        w_glu_bf=w_glu[l].astype(BF16), b_glu=b_glu[l], norm_a_g=norm_a_g[l],
        norm_b_g=norm_b_g[l], w_out_bf=w_out[l].astype(BF16), ln1_g=ln1_g[l], ln1_b=ln1_b[l],
        s5_wb=jnp.stack([dirs[0][0], dirs[1][0]]), s5_wc=jnp.stack([dirs[0][1], dirs[1][1]]),
        s5_a=jnp.stack([dirs[0][2], dirs[1][2]]),
    )
    mb = SUBLANES
    xm = jnp.broadcast_to(meta_tokens.astype(F32)[None], (mb, N_META, d))
    _, u_m, su_m = _embed_inproj(xm, ln_emb_g, ln_emb_b, p["w_in_bf"], N_META)
    a_m = jnp.broadcast_to(p["s5_a"][:1, :, None, :], (1, N_SLABS, mb, 2 * SLAB_STATE))
    (st_m,) = _s5_scan(su_m, jnp.zeros((1, N_SLABS, mb, 2 * SLAB_STATE), F32), p["s5_wb"][:1],
                       p["s5_wc"][:1], a_m, bsz=mb, tc=N_META, dirs=(False,), emit_y=False,
                       emit_state=True)
    meta_state = st_m[0, :, :1, :]
    u_meta_last = u_m[0, N_META - 1]

    h1_p, h1p_p = _mixer(x_prompt, meta_state, u_meta_last, p)
    h1_s, h1p_s = _mixer(x_sample, meta_state, u_meta_last, p)
    n = h1_p.shape[0] + h1_s.shape[0]

    tr = min(h1_p.shape[0], h1_s.shape[0], ROUTER_TILE)
    eidx, rank, gate, cnt = _router(h1_p, h1_s, w_router[l], router_bias[l], tr)
    counts = cnt[:, 0].astype(jnp.int32)
    sched = _expert_schedule(counts, n * TOP_K, EXPERT_TILE)
    offsets = sched[2]
    pos = _positions(eidx, rank, offsets[:-1], min(n, POSITIONS_TILE))
    pos_c = pos.reshape(TOP_K, n // SC_CHUNK, SC_CHUNK).transpose(1, 0, 2)
    xs = _sc_dispatch(h1p_p, h1p_s, pos_c, SC_CHUNK)
    ys = _experts(xs, sched, w_exp_gate[l], w_exp_up[l], w_exp_down[l], EXPERT_TILE)
    n_p = h1_p.shape[0]
    nch_p = n_p // SC_CHUNK
    td = min(n_p, h1_s.shape[0], COMBINE_TILE)
    gate_t = gate.T
    shared = (w_sh_gate[l].astype(BF16), w_sh_up[l].astype(BF16), w_sh_down[l].astype(BF16),
              ln2_g[l], ln2_b[l], td)
    yk_p = _sc_gather(ys, pos_c[:nch_p], SC_CHUNK)
    yk_s = _sc_gather(ys, pos_c[nch_p:], SC_CHUNK)
    out_p = _combine(h1_p, gate_t, 0, yk_p, *shared)
    out_s = _combine(h1_s, gate_t, n_p, yk_s, *shared)
    return (out_p.reshape(x_prompt.shape), out_s.reshape(x_sample.shape))
```

```python
import functools

import jax
import jax.numpy as jnp
from jax import lax
from jax.experimental import pallas as pl
from jax.experimental.pallas import tpu as pltpu
from jax.experimental.pallas import tpu_sc as plsc

F32 = jnp.float32
BF16 = jnp.bfloat16

N_META = 16
CONV_WIDTH = 3
SSM_GROUP = 16
SSM_STATE = 64
N_EXPERTS = 256
TOP_K = 8
N_EXPERT_GROUPS = 8
TOPK_GROUPS = 4
ROUTED_SCALE = 2.5
DEPTH = 1
DEEPNORM_ALPHA = (2.0 * DEPTH) ** 0.25
LN_EPS = 1e-5
RMS_EPS = 1e-6

LANES = 128
SUBLANES = 8
N_SLABS = 4
GROUPS_PER_SLAB = LANES // SSM_GROUP
SLAB_STATE = GROUPS_PER_SLAB * SSM_STATE
S5_CHUNK_ROWS = 256
EXPERT_TILE = 2048
EXPERT_SUB = 512
ROUTER_TILE = 512
COMBINE_TILE = 512
POSITIONS_TILE = 2048
SC_CHUNK = 64
VMEM_LIMIT = 48 * 1024 * 1024


def _cparams(*sem):
    return pltpu.CompilerParams(dimension_semantics=sem, vmem_limit_bytes=VMEM_LIMIT)


def _layer_norm(x, g, b):
    mu = jnp.mean(x, axis=-1, keepdims=True)
    xc = x - mu
    var = jnp.mean(xc * xc, axis=-1, keepdims=True)
    return xc * lax.rsqrt(var + LN_EPS) * g + b


def _rms_norm(x, g):
    return x * lax.rsqrt(jnp.mean(x * x, axis=-1, keepdims=True) + RMS_EPS) * g


def _pack_halves(x):
    half = x.shape[-1] // 2
    bits = pltpu.bitcast(x.astype(BF16).astype(F32), jnp.uint32)
    return (bits[:, :half] >> 16) | (bits[:, half:] & jnp.uint32(0xFFFF0000))


def _unpack_halves(p):
    lo = pltpu.bitcast(p << 16, F32).astype(BF16)
    hi = pltpu.bitcast(p & jnp.uint32(0xFFFF0000), F32).astype(BF16)
    return lo, hi


def _embed_inproj_kernel(x_ref, g_ref, b_ref, w_ref, gb_ref, u_ref, su_ref, *, bsz, tt):
    d = x_ref.shape[-1]
    x = x_ref[...].reshape(bsz * tt, d)
    h0 = _layer_norm(x, g_ref[...], b_ref[...])
    proj = jnp.dot(h0.astype(BF16), w_ref[...], preferred_element_type=F32)
    wc = gb_ref.shape[-1]
    gb_ref[...] = proj[:, :wc].reshape(bsz, tt, wc)
    u_ref[...] = (proj[:, wc:2 * wc] * proj[:, 2 * wc:3 * wc]).reshape(bsz, tt, wc)
    s_u = proj[:, 3 * wc:]
    for b in range(bsz):
        for j in range(N_SLABS):
            val = s_u[b * tt:(b + 1) * tt, j * LANES:(j + 1) * LANES]
            if bsz == 1:
                su_ref[j] = val
            else:
                su_ref[j, pl.ds(b, tt, stride=bsz), :] = val


def _embed_inproj(x, ln_g, ln_b, w_in_bf, tt):
    bsz, seq, d = x.shape
    e = w_in_bf.shape[1]
    wc = (e - N_SLABS * LANES) // 3
    nt = seq // tt
    kern = functools.partial(_embed_inproj_kernel, bsz=bsz, tt=tt)
    return pl.pallas_call(
        kern,
        grid=(nt,),
        in_specs=[
            pl.BlockSpec((bsz, tt, d), lambda i: (0, i, 0)),
            pl.BlockSpec((1, d), lambda i: (0, 0)),
            pl.BlockSpec((1, d), lambda i: (0, 0)),
            pl.BlockSpec((d, e), lambda i: (0, 0)),
        ],
        out_specs=[
            pl.BlockSpec((bsz, tt, wc), lambda i: (0, i, 0)),
            pl.BlockSpec((bsz, tt, wc), lambda i: (0, i, 0)),
            pl.BlockSpec((N_SLABS, tt * bsz, LANES), lambda i: (0, i, 0)),
        ],
        out_shape=[
            jax.ShapeDtypeStruct((bsz, seq, wc), F32),
            jax.ShapeDtypeStruct((bsz, seq, wc), F32),
            jax.ShapeDtypeStruct((N_SLABS, seq * bsz, LANES), F32),
        ],
        compiler_params=_cparams("parallel"),
        name="embed_inproj",
    )(x, ln_g.reshape(1, d), ln_b.reshape(1, d), w_in_bf)


def _s5_kernel(*refs, bsz, tc, dirs, emit_y, emit_state):
    nd = len(dirs)
    it = iter(refs)
    u_refs = [next(it) for _ in range(nd)]
    s0_ref, wb_ref, wc_ref, a_ref = next(it), next(it), next(it), next(it)
    y_refs = [next(it) for _ in range(nd)] if emit_y else []
    sf_ref = next(it) if emit_state else None
    bu_ref, st_ref = next(it), next(it)
    half = SLAB_STATE

    @pl.when(pl.program_id(0) == 0)
    def _():
        st_ref[...] = s0_ref[...]

    def project_in(j):
        for k in range(nd):
            bu_ref[k, j] = jnp.dot(u_refs[k][j].astype(BF16), wb_ref[k, j],
                                   preferred_element_type=F32)

    def project_out(j):
        for k in range(nd):
            y_refs[k][j] = jnp.dot(bu_ref[k, j].astype(BF16), wc_ref[k, j],
                                   preferred_element_type=F32)

    per_tile = SUBLANES // bsz
    n_tiles = tc // per_tile

    def cmul_add(a, s, x):
        return (a[0] * s[0] - a[1] * s[1] + x[0], a[0] * s[1] + a[1] * s[0] + x[1])

    def load(k, j, ti):
        rows = pl.ds(ti * SUBLANES, SUBLANES)
        return bu_ref[k, j, rows, :half], bu_ref[k, j, rows, half:]

    def store(k, j, ti, v):
        rows = pl.ds(ti * SUBLANES, SUBLANES)
        bu_ref[k, j, rows, :half] = v[0]
        bu_ref[k, j, rows, half:] = v[1]

    if per_tile == 1:
        def recur(j):
            for k in range(nd):
                a = (a_ref[k, j, :, :half], a_ref[k, j, :, half:])
                s = (st_ref[k, j, :, :half], st_ref[k, j, :, half:])
                for i in range(n_tiles):
                    ti = (n_tiles - 1 - i) if dirs[k] else i
                    s = cmul_add(a, s, load(k, j, ti))
                    store(k, j, ti, s)
                st_ref[k, j, :, :half] = s[0]
                st_ref[k, j, :, half:] = s[1]
    else:
        assert per_tile == 2 and tuple(dirs) == (False, True)
        low = lax.broadcasted_iota(jnp.int32, (SUBLANES, half), 0) < bsz

        def pick(p, q):
            return (jnp.where(low, p[0], q[0]), jnp.where(low, p[1], q[1]))

        def swap(v):
            return (pltpu.roll(v[0], bsz, axis=0), pltpu.roll(v[1], bsz, axis=0))

        def recur(j):
            af = (a_ref[0, j, :, :half], a_ref[0, j, :, half:])
            ar = (a_ref[1, j, :, :half], a_ref[1, j, :, half:])
            a_fr, a_rf = pick(af, ar), pick(ar, af)
            s = pick((st_ref[0, j, :, :half], st_ref[0, j, :, half:]),
                     (st_ref[1, j, :, :half], st_ref[1, j, :, half:]))
            for i in range(n_tiles):
                tf, tr = i, n_tiles - 1 - i
                xf, xr = load(0, j, tf), load(1, j, tr)
                s1 = cmul_add(a_fr, s, pick(xf, xr))
                s2 = cmul_add(a_rf, swap(s1), pick(xr, xf))
                store(0, j, tf, pick(s1, s2))
                store(1, j, tr, pick(s2, s1))
                s = swap(s2)
            for k in range(nd):
                st_ref[k, j, :, :half] = s[0]
                st_ref[k, j, :, half:] = s[1]

    project_in(0)
    for j in range(N_SLABS):
        if j + 1 < N_SLABS:
            project_in(j + 1)
        recur(j)
        if emit_y:
            project_out(j)
    if emit_state:
        sf_ref[...] = st_ref[...]


def _s5_scan(su, s0, wb, wc, a_b, *, bsz, tc, dirs, emit_y, emit_state):
    rows = su.shape[1]
    seq = rows // bsz
    assert SUBLANES % bsz == 0 and seq % tc == 0 and tc % (SUBLANES // bsz) == 0
    nc = seq // tc
    r = tc * bsz
    nd = len(dirs)
    sw = 2 * SLAB_STATE

    def u_map(rev):
        return (lambda c: (0, nc - 1 - c, 0)) if rev else (lambda c: (0, c, 0))

    in_specs = [pl.BlockSpec((N_SLABS, r, LANES), u_map(rev)) for rev in dirs]
    in_specs += [
        pl.BlockSpec((nd, N_SLABS, SUBLANES, sw), lambda c: (0, 0, 0, 0)),
        pl.BlockSpec((nd, N_SLABS, LANES, sw), lambda c: (0, 0, 0, 0)),
        pl.BlockSpec((nd, N_SLABS, sw, LANES), lambda c: (0, 0, 0, 0)),
        pl.BlockSpec((nd, N_SLABS, SUBLANES, sw), lambda c: (0, 0, 0, 0)),
    ]
    out_specs, out_shape = [], []
    if emit_y:
        for rev in dirs:
            out_specs.append(pl.BlockSpec((N_SLABS, r, LANES), u_map(rev)))
            out_shape.append(jax.ShapeDtypeStruct((N_SLABS, rows, LANES), F32))
    if emit_state:
        out_specs.append(pl.BlockSpec((nd, N_SLABS, SUBLANES, sw), lambda c: (0, 0, 0, 0)))
        out_shape.append(jax.ShapeDtypeStruct((nd, N_SLABS, SUBLANES, sw), F32))
    kern = functools.partial(_s5_kernel, bsz=bsz, tc=tc, dirs=dirs, emit_y=emit_y,
                             emit_state=emit_state)
    return pl.pallas_call(
        kern,
        grid=(nc,),
        in_specs=in_specs,
        out_specs=out_specs,
        out_shape=out_shape,
        scratch_shapes=[
            pltpu.VMEM((nd, N_SLABS, r, sw), F32),
            pltpu.VMEM((nd, N_SLABS, SUBLANES, sw), F32),
        ],
        compiler_params=_cparams("arbitrary"),
        name="s5_scan",
    )(*([su] * nd), s0, wb, wc, a_b)


def _s5_params(lam_re, lam_im, log_step, b_re, b_im, c_re, c_im):
    g, p = lam_re.shape
    h = b_re.shape[-1]
    dt = jnp.exp(log_step.astype(F32))[:, None]
    mag = jnp.exp(lam_re * dt)
    ab_re = mag * jnp.cos(lam_im * dt)
    ab_im = mag * jnp.sin(lam_im * dt)
    den = lam_re * lam_re + lam_im * lam_im
    nr, ni = ab_re - 1.0, ab_im
    f_re = (nr * lam_re + ni * lam_im) / den
    f_im = (ni * lam_re - nr * lam_im) / den
    bb_re = f_re[..., None] * b_re - f_im[..., None] * b_im
    bb_im = f_re[..., None] * b_im + f_im[..., None] * b_re
    ns, gl = N_SLABS, GROUPS_PER_SLAB
    eye = jnp.eye(gl, dtype=F32)

    def in_block(bb):
        bb = bb.reshape(ns, gl, p, h)
        return jnp.einsum('sgph,gk->sghkp', bb, eye).reshape(ns, gl * h, gl * p)

    def out_block(cc):
        cc = cc.reshape(ns, gl, h, p)
        return jnp.einsum('sghp,gk->sgpkh', cc, eye).reshape(ns, gl * p, gl * h)

    wb = jnp.concatenate([in_block(bb_re), in_block(bb_im)], axis=-1)
    wc = jnp.concatenate([out_block(c_re.astype(F32)), -out_block(c_im.astype(F32))], axis=1)
    a = jnp.concatenate([ab_re.reshape(ns, gl * p), ab_im.reshape(ns, gl * p)], axis=-1)
    return wb.astype(BF16), wc.astype(BF16), a


def _mixer_tail_kernel(x_ref, ge_ref, be_ref, gb_ref, u_ref, up_ref, un_ref, um_ref, su_ref,
                       yf_ref, yr_ref, cw_ref, cb_ref, sd_ref, wg_ref, bg_ref, na_ref, nb_ref,
                       wo_ref, g1_ref, b1_ref, h1_ref, h1p_ref, *, bsz, tt):
    i = pl.program_id(0)
    nt = pl.num_programs(0)
    d = x_ref.shape[-1]
    wcv = gb_ref.shape[-1]
    row_id = lax.broadcasted_iota(jnp.int32, (tt, wcv), 0)
    ya, ys = [], []
    for b in range(bsz):
        u = u_ref[b]
        prev_edge = jnp.where(i == 0, um_ref[...], up_ref[b, SUBLANES - 1:SUBLANES, :])
        next_edge = jnp.where(i == nt - 1, jnp.zeros((1, wcv), F32), un_ref[b, 0:1, :])
        u_prev = jnp.where(row_id == 0, prev_edge, pltpu.roll(u, 1, axis=0))
        u_next = jnp.where(row_id == tt - 1, next_edge, pltpu.roll(u, tt - 1, axis=0))
        conv = u_prev * cw_ref[0:1, :] + u * cw_ref[1:2, :] + u_next * cw_ref[2:3, :] + cb_ref[...]
        ya.append(gb_ref[b] * conv)

        def slab(ref, b=b):
            parts = []
            for j in range(N_SLABS):
                if bsz == 1:
                    parts.append(ref[j])
                else:
                    parts.append(ref[j, pl.ds(b, tt, stride=bsz), :])
            return jnp.concatenate(parts, axis=-1)

        ys.append(slab(yf_ref) + slab(yr_ref) + sd_ref[...] * slab(su_ref))
    y_a = jnp.concatenate(ya, axis=0)
    y_s = jnp.concatenate(ys, axis=0)
    z = jax.nn.gelu(y_s)
    glu = jnp.dot(z.astype(BF16), wg_ref[...], preferred_element_type=F32) + bg_ref[...]
    y_b = z * jax.nn.sigmoid(glu)
    merged = jnp.concatenate([_rms_norm(y_a, na_ref[...]), _rms_norm(y_b, nb_ref[...])], axis=-1)
    m = jnp.dot(merged.astype(BF16), wo_ref[...], preferred_element_type=F32)
    h0 = _layer_norm(x_ref[...].reshape(bsz * tt, d), ge_ref[...], be_ref[...])
    h1 = _layer_norm(DEEPNORM_ALPHA * h0 + m, g1_ref[...], b1_ref[...])
    h1_ref[...] = h1.reshape(bsz, tt, d)
    h1p_ref[...] = _pack_halves(h1).reshape(bsz, tt, d // 2)


def _mixer_tail(x, ln_emb_g, ln_emb_b, gb, u, u_meta_last, su, yf, yr, conv_w, conv_b, ssm_d,
                w_glu_bf, b_glu, norm_a_g, norm_b_g, w_out_bf, ln1_g, ln1_b, tt):
    bsz, seq, d = x.shape
    wcv = gb.shape[-1]
    ws = N_SLABS * LANES
    nt = seq // tt
    tb = tt // SUBLANES
    nb8 = seq // SUBLANES
    kern = functools.partial(_mixer_tail_kernel, bsz=bsz, tt=tt)
    row = lambda n: pl.BlockSpec((1, n), lambda i: (0, 0))
    slab_spec = pl.BlockSpec((N_SLABS, tt * bsz, LANES), lambda i: (0, i, 0))
    return pl.pallas_call(
        kern,
        grid=(nt,),
        in_specs=[
            pl.BlockSpec((bsz, tt, d), lambda i: (0, i, 0)),
            row(d), row(d),
            pl.BlockSpec((bsz, tt, wcv), lambda i: (0, i, 0)),
            pl.BlockSpec((bsz, tt, wcv), lambda i: (0, i, 0)),
            pl.BlockSpec((bsz, SUBLANES, wcv), lambda i: (0, jnp.maximum(i * tb - 1, 0), 0)),
            pl.BlockSpec((bsz, SUBLANES, wcv), lambda i: (0, jnp.minimum((i + 1) * tb, nb8 - 1), 0)),
            row(wcv),
            slab_spec, slab_spec, slab_spec,
            pl.BlockSpec((CONV_WIDTH, wcv), lambda i: (0, 0)),
            row(wcv), row(ws),
            pl.BlockSpec((ws, ws), lambda i: (0, 0)),
            row(ws), row(wcv), row(ws),
            pl.BlockSpec((wcv + ws, d), lambda i: (0, 0)),
            row(d), row(d),
        ],
        out_specs=[
            pl.BlockSpec((bsz, tt, d), lambda i: (0, i, 0)),
            pl.BlockSpec((bsz, tt, d // 2), lambda i: (0, i, 0)),
        ],
        out_shape=[
            jax.ShapeDtypeStruct((bsz, seq, d), F32),
            jax.ShapeDtypeStruct((bsz, seq, d // 2), jnp.uint32),
        ],
        compiler_params=_cparams("parallel"),
        name="mixer_tail",
    )(x, ln_emb_g.reshape(1, d), ln_emb_b.reshape(1, d), gb, u, u, u,
      u_meta_last.reshape(1, wcv), su, yf, yr, conv_w, conv_b.reshape(1, wcv),
      ssm_d.reshape(1, ws), w_glu_bf, b_glu.reshape(1, ws), norm_a_g.reshape(1, wcv),
      norm_b_g.reshape(1, ws), w_out_bf, ln1_g.reshape(1, d), ln1_b.reshape(1, d))


def _dual_row_specs(rows, width, nq_p):
    return [pl.BlockSpec((rows, width), lambda q, *_: (jnp.minimum(q, nq_p - 1), 0)),
            pl.BlockSpec((rows, width), lambda q, *_: (jnp.maximum(q - nq_p, 0), 0))]


def _router_kernel(hp_ref, hs_ref, wh_ref, bias_ref, eidx_ref, rank_ref, gate_ref, cnt_ref,
                   cnt_scr, *, tt, nq_p):
    ne = wh_ref.shape[0]
    epg = ne // N_EXPERT_GROUPS
    neg = jnp.float32(-jnp.inf)

    @pl.when(pl.program_id(0) == 0)
    def _():
        cnt_scr[...] = jnp.zeros_like(cnt_scr)

    h = jnp.where(pl.program_id(0) < nq_p, hp_ref[...], hs_ref[...])
    dn = (((1,), (1,)), ((), ()))
    logits = lax.dot_general(wh_ref[...], h.astype(BF16), dn,
                             preferred_element_type=F32)
    scores = jax.nn.sigmoid(logits)
    sel = scores + bias_ref[...]

    gi = lax.broadcasted_iota(jnp.int32, (epg, tt), 0)
    gs = []
    for g in range(N_EXPERT_GROUPS):
        x = sel[g * epg:(g + 1) * epg, :]
        m1 = jnp.max(x, axis=0, keepdims=True)
        i1 = jnp.min(jnp.where(x == m1, gi, epg), axis=0, keepdims=True)
        m2 = jnp.max(jnp.where(gi == i1, neg, x), axis=0, keepdims=True)
        gs.append(m1 + m2)
    chosen = [jnp.zeros((1, tt), F32) for _ in range(N_EXPERT_GROUPS)]
    for _ in range(TOPK_GROUPS):
        m = gs[0]
        for g in range(1, N_EXPERT_GROUPS):
            m = jnp.maximum(m, gs[g])
        found = jnp.zeros((1, tt), F32)
        for g in range(N_EXPERT_GROUPS):
            hit = jnp.where((gs[g] == m) & (found == 0.0), 1.0, 0.0)
            chosen[g] = chosen[g] + hit
            found = found + hit
            gs[g] = jnp.where(hit > 0.0, neg, gs[g])
    selm = jnp.concatenate(
        [jnp.where(chosen[g] > 0.0, sel[g * epg:(g + 1) * epg, :], neg)
         for g in range(N_EXPERT_GROUPS)], axis=0)

    ei = lax.broadcasted_iota(jnp.int32, (ne, tt), 0)
    msel = jnp.zeros((ne, tt), F32)
    idxs, gvals = [], []
    for _ in range(TOP_K):
        m = jnp.max(selm, axis=0, keepdims=True)
        idx = jnp.min(jnp.where(selm == m, ei, ne), axis=0, keepdims=True)
        hit = ei == idx
        gvals.append(jnp.sum(jnp.where(hit, scores, 0.0), axis=0, keepdims=True))
        selm = jnp.where(hit, neg, selm)
        msel = jnp.where(hit, 1.0, msel)
        idxs.append(idx)
    gsum = gvals[0]
    for k in range(1, TOP_K):
        gsum = gsum + gvals[k]
    gate_ref[...] = jnp.concatenate([gv / gsum * ROUTED_SCALE for gv in gvals], axis=0)
    eidx_ref[...] = jnp.concatenate(idxs, axis=0)

    r_i = lax.broadcasted_iota(jnp.int32, (tt, tt), 0)
    c_i = lax.broadcasted_iota(jnp.int32, (tt, tt), 1)
    upper = jnp.where(r_i < c_i, 1.0, 0.0).astype(BF16)
    rank_full = jnp.dot(msel.astype(BF16), upper, preferred_element_type=F32) + cnt_scr[...]
    ranks = [jnp.sum(jnp.where(ei == idxs[k], rank_full, 0.0), axis=0, keepdims=True)
             for k in range(TOP_K)]
    rank_ref[...] = jnp.concatenate(ranks, axis=0).astype(jnp.int32)
    cnt_scr[...] = cnt_scr[...] + jnp.sum(msel, axis=1, keepdims=True)
    cnt_ref[...] = cnt_scr[...]


def _router(h_p, h_s, w_router, router_bias, tt):
    d = h_p.shape[1]
    n = h_p.shape[0] + h_s.shape[0]
    assert h_p.shape[0] % tt == 0 and h_s.shape[0] % tt == 0
    nq_p = h_p.shape[0] // tt
    ne = w_router.shape[1]
    wh = w_router.T.astype(BF16)
    kern = functools.partial(_router_kernel, tt=tt, nq_p=nq_p)
    return pl.pallas_call(
        kern,
        grid=(n // tt,),
        in_specs=_dual_row_specs(tt, d, nq_p) + [
            pl.BlockSpec((ne, d), lambda i: (0, 0)),
            pl.BlockSpec((ne, 1), lambda i: (0, 0)),
        ],
        out_specs=[
            pl.BlockSpec((TOP_K, tt), lambda i: (0, i)),
            pl.BlockSpec((TOP_K, tt), lambda i: (0, i)),
            pl.BlockSpec((TOP_K, tt), lambda i: (0, i)),
            pl.BlockSpec((ne, 1), lambda i: (0, 0)),
        ],
        out_shape=[
            jax.ShapeDtypeStruct((TOP_K, n), jnp.int32),
            jax.ShapeDtypeStruct((TOP_K, n), jnp.int32),
            jax.ShapeDtypeStruct((TOP_K, n), F32),
            jax.ShapeDtypeStruct((ne, 1), F32),
        ],
        scratch_shapes=[pltpu.VMEM((ne, 1), F32)],
        compiler_params=_cparams("arbitrary"),
        name="router",
    )(h_p, h_s, wh, router_bias.astype(F32).reshape(ne, 1))


def _positions_kernel(eidx_ref, rank_ref, start_ref, pos_ref):
    ne = start_ref.shape[0]
    tt = eidx_ref.shape[1]
    ei = lax.broadcasted_iota(jnp.int32, (ne, tt), 0)
    start = start_ref[...]
    rows = [jnp.sum(jnp.where(ei == eidx_ref[k:k + 1, :], start, 0.0), axis=0, keepdims=True)
            for k in range(TOP_K)]
    pos = jnp.concatenate(rows, axis=0).astype(jnp.int32) + rank_ref[...]
    for c in range(tt // SC_CHUNK):
        pos_ref[c] = pos[:, c * SC_CHUNK:(c + 1) * SC_CHUNK]


def _positions(eidx, rank, start, tt):
    n = eidx.shape[1]
    ne = start.shape[0]
    assert n % tt == 0 and tt % SC_CHUNK == 0
    return pl.pallas_call(
        _positions_kernel,
        grid=(n // tt,),
        in_specs=[
            pl.BlockSpec((TOP_K, tt), lambda i: (0, i)),
            pl.BlockSpec((TOP_K, tt), lambda i: (0, i)),
            pl.BlockSpec((ne, 1), lambda i: (0, 0)),
        ],
        out_specs=pl.BlockSpec((tt // SC_CHUNK, TOP_K, SC_CHUNK), lambda i: (i, 0, 0)),
        out_shape=jax.ShapeDtypeStruct((n // SC_CHUNK, TOP_K, SC_CHUNK), jnp.int32),
        compiler_params=_cparams("parallel"),
        name="positions",
    )(eidx, rank, start.astype(F32).reshape(ne, 1))


def _sc_workers():
    info = plsc.get_sparse_core_info()
    return info.num_cores, info.num_subcores


def _sc_dispatch(hp_p, hp_s, pos_c, chunk):
    w = hp_p.shape[1]
    n = hp_p.shape[0] + hp_s.shape[0]
    nch_p = hp_p.shape[0] // chunk
    nc, ns = _sc_workers()
    assert hp_p.shape[0] % chunk == 0 and n % (chunk * nc * ns) == 0
    per_worker = (n // chunk) // (nc * ns)
    mesh = plsc.VectorSubcoreMesh(core_axis_name="c", subcore_axis_name="s")

    assert per_worker % 2 == 0

    @functools.partial(
        pl.kernel, mesh=mesh,
        out_type=jax.ShapeDtypeStruct((n * TOP_K, w), hp_p.dtype),
        scratch_types=[pltpu.VMEM((2, TOP_K, chunk), jnp.int32),
                       pltpu.VMEM((2, chunk, w), hp_p.dtype),
                       pltpu.SemaphoreType.DMA((2,)),
                       pltpu.SemaphoreType.DMA((2,)),
                       pltpu.SemaphoreType.DMA],
    )
    def dispatch(hp_hbm, hs_hbm, pos_hbm, xs_hbm, idx_v, rows_v, rsem, isem, ssem):
        wid = lax.axis_index("s") * nc + lax.axis_index("c")
        first = wid * per_worker

        def loads(c, b):
            cp_ = jnp.minimum(c, nch_p - 1)
            cs_ = jnp.maximum(c - nch_p, 0)
            src_p = hp_hbm.at[pl.ds(pl.multiple_of(cp_ * chunk, chunk), chunk)]
            src_s = hs_hbm.at[pl.ds(pl.multiple_of(cs_ * chunk, chunk), chunk)]
            return (pltpu.make_async_copy(src_p, rows_v.at[b], rsem.at[b]),
                    pltpu.make_async_copy(src_s, rows_v.at[b], rsem.at[b]),
                    pltpu.make_async_copy(pos_hbm.at[c], idx_v.at[b], isem.at[b]))

        def start_loads(c, b):
            from_p, from_s, idx = loads(c, b)
            pl.when(c < nch_p)(from_p.start)
            pl.when(c >= nch_p)(from_s.start)
            idx.start()

        def wait_loads(c, b):
            from_p, from_s, idx = loads(c, b)
            pl.when(c < nch_p)(from_p.wait)
            pl.when(c >= nch_p)(from_s.wait)
            idx.wait()

        start_loads(first, 0)

        @pl.loop(0, per_worker, step=2)
        def _(g):
            for b in range(2):
                c = first + g + b
                wait_loads(c, b)
                pl.when(g + b + 1 < per_worker)(functools.partial(start_loads, c + 1, 1 - b))
                copies = [pltpu.async_copy(rows_v.at[b], xs_hbm.at[idx_v.at[b, k]], ssem)
                          for k in range(TOP_K)]
                for cp in copies:
                    cp.wait()

    return dispatch(hp_p, hp_s, pos_c)


def _sc_gather(ys, pos_c, chunk):
    w = ys.shape[1]
    n = pos_c.shape[0] * chunk
    nc, ns = _sc_workers()
    assert pos_c.shape[0] % (nc * ns) == 0
    per_worker = pos_c.shape[0] // (nc * ns)
    mesh = plsc.VectorSubcoreMesh(core_axis_name="c", subcore_axis_name="s")
    nbuf = 3

    @functools.partial(
        pl.kernel, mesh=mesh,
        out_type=jax.ShapeDtypeStruct((TOP_K, n, w), ys.dtype),
        scratch_types=[pltpu.VMEM((TOP_K, chunk), jnp.int32),
                       pltpu.VMEM((nbuf, chunk, w), ys.dtype),
                       pltpu.SemaphoreType.DMA((nbuf,)),
                       pltpu.SemaphoreType.DMA((nbuf,))],
    )
    def gather(ys_hbm, pos_hbm, out_hbm, idx_v, rows_v, gsem, wsem):
        wid = lax.axis_index("s") * nc + lax.axis_index("c")

        @pl.loop(0, per_worker)
        def _(ci):
            c = wid * per_worker + ci
            off = pl.multiple_of(c * chunk, chunk)
            pltpu.sync_copy(pos_hbm.at[c], idx_v)

            def start_gather(k):
                b = k % nbuf
                return pltpu.async_copy(ys_hbm.at[idx_v.at[k]], rows_v.at[b], gsem.at[b])

            gathers = {0: start_gather(0)}
            writes = {}
            for k in range(TOP_K):
                if k + 1 < TOP_K:
                    if k + 1 - nbuf >= 0:
                        writes.pop(k + 1 - nbuf).wait()
                    gathers[k + 1] = start_gather(k + 1)
                gathers.pop(k).wait()
                b = k % nbuf
                writes[k] = pltpu.async_copy(rows_v.at[b], out_hbm.at[k, pl.ds(off, chunk)],
                                             wsem.at[b])
            for k in sorted(writes):
                writes[k].wait()

    return gather(ys, pos_c)


def _experts_kernel(fe_ref, le_ref, off_ref, nxt_ref, slot_ref, xs_ref, wg_hbm, wu_hbm, wd_hbm,
                    ys_ref, wgb, wub, wdb, acc, wgf, wuf, wdf, wsem, cur_ref, *, tm):
    t = pl.program_id(0)
    base = t * tm
    half = xs_ref.shape[1]

    @pl.when(t == 0)
    def _():
        cur_ref[0] = -1

    def weight_copies(expert, slot):
        return (pltpu.make_async_copy(wg_hbm.at[expert], wgf.at[slot], wsem.at[slot, 0]),
                pltpu.make_async_copy(wu_hbm.at[expert], wuf.at[slot], wsem.at[slot, 1]),
                pltpu.make_async_copy(wd_hbm.at[expert], wdf.at[slot], wsem.at[slot, 2]))

    def load_weights(e):
        slot = slot_ref[e]

        @pl.when(cur_ref[0] < 0)
        def _():
            for cp in weight_copies(e, slot):
                cp.start()

        for cp in weight_copies(e, slot):
            cp.wait()
        wgb[...] = wgf[slot].astype(BF16)
        wub[...] = wuf[slot].astype(BF16)
        wdb[...] = wdf[slot].astype(BF16)

        @pl.when(nxt_ref[e] != e)
        def _():
            for cp in weight_copies(nxt_ref[e], 1 - slot):
                cp.start()

        cur_ref[0] = e

    def visit(e, carry):
        lo_row, hi_row = off_ref[e], off_ref[e + 1]
        _visit_expert(e, lo_row, hi_row)
        return carry

    def ffn(rows, m, r0, masked, lo_row, hi_row):
        lo, hi = _unpack_halves(xs_ref[rows, :])
        g = (jnp.dot(lo, wgb[:half], preferred_element_type=F32)
             + jnp.dot(hi, wgb[half:], preferred_element_type=F32))
        u = (jnp.dot(lo, wub[:half], preferred_element_type=F32)
             + jnp.dot(hi, wub[half:], preferred_element_type=F32))
        act = (g * jax.nn.sigmoid(g)) * u
        if masked:
            row = r0 + lax.broadcasted_iota(jnp.int32, (m, 1), 0)
            act = jnp.where((row >= lo_row) & (row < hi_row), act, 0.0)
        return jnp.dot(act.astype(BF16), wdb[...], preferred_element_type=F32)

    def shared_block(rows, m, r0, lo_row, hi_row):
        y = ffn(rows, m, r0, True, lo_row, hi_row)
        opens = lo_row <= r0

        @pl.when(opens)
        def _():
            acc[rows, :] = y

        @pl.when(jnp.logical_not(opens))
        def _():
            acc[rows, :] = acc[rows, :] + y

        ys_ref[rows, :] = _pack_halves(acc[rows, :])

    def _visit_expert(e, lo_row, hi_row):
        @pl.when(hi_row > lo_row)
        def _():
            pl.when(cur_ref[0] != e)(functools.partial(load_weights, e))
            for b in range(tm // EXPERT_SUB):
                rows = pl.ds(b * EXPERT_SUB, EXPERT_SUB)
                r0 = base + b * EXPERT_SUB
                touched = (lo_row < r0 + EXPERT_SUB) & (hi_row > r0)
                whole = (lo_row <= r0) & (hi_row >= r0 + EXPERT_SUB)

                @pl.when(touched & whole)
                def _(rows=rows, r0=r0):
                    ys_ref[rows, :] = _pack_halves(
                        ffn(rows, EXPERT_SUB, r0, False, lo_row, hi_row))

                pl.when(touched & jnp.logical_not(whole))(
                    functools.partial(shared_block, rows, EXPERT_SUB, r0, lo_row, hi_row))

    lax.fori_loop(fe_ref[t], le_ref[t] + 1, visit, 0)


def _experts(xs, sched, w_gate, w_up, w_down, tm):
    n_rows, half = xs.shape
    ne, d, de = w_gate.shape
    assert n_rows % tm == 0 and tm % EXPERT_SUB == 0

    grid_spec = pltpu.PrefetchScalarGridSpec(
        num_scalar_prefetch=len(sched),
        grid=(n_rows // tm,),
        in_specs=[
            pl.BlockSpec((tm, half), lambda t, *_: (t, 0)),
            pl.BlockSpec(memory_space=pl.ANY),
            pl.BlockSpec(memory_space=pl.ANY),
            pl.BlockSpec(memory_space=pl.ANY),
        ],
        out_specs=pl.BlockSpec((tm, half), lambda t, *_: (t, 0)),
        scratch_shapes=[
            pltpu.VMEM((d, de), BF16),
            pltpu.VMEM((d, de), BF16),
            pltpu.VMEM((de, d), BF16),
            pltpu.VMEM((tm, d), F32),
            pltpu.VMEM((2, d, de), F32),
            pltpu.VMEM((2, d, de), F32),
            pltpu.VMEM((2, de, d), F32),
            pltpu.SemaphoreType.DMA((2, 3)),
            pltpu.SMEM((1,), jnp.int32),
        ],
    )
    return pl.pallas_call(
        functools.partial(_experts_kernel, tm=tm),
        grid_spec=grid_spec,
        out_shape=jax.ShapeDtypeStruct((n_rows, half), jnp.uint32),
        compiler_params=_cparams("arbitrary"),
        name="experts",
    )(*sched, xs, w_gate, w_up, w_down)


def _expert_schedule(counts, n_rows, tm):
    ne = counts.shape[0]
    off = jnp.concatenate([jnp.zeros((1,), jnp.int32), jnp.cumsum(counts)]).astype(jnp.int32)
    tile_lo = jnp.arange(n_rows // tm, dtype=jnp.int32) * tm
    owner = lambda row: jnp.sum((off[None, 1:] <= row[:, None]).astype(jnp.int32), axis=1)
    first_e = owner(tile_lo)
    last_e = owner(tile_lo + (tm - 1))
    ids = jnp.arange(ne, dtype=jnp.int32)
    later = (ids[None, :] > ids[:, None]) & (counts[None, :] > 0)
    next_e = jnp.min(jnp.where(later, ids[None, :], ne), axis=1)
    next_e = jnp.where(next_e == ne, ids, next_e)
    slot = (jnp.cumsum((counts > 0).astype(jnp.int32)) - 1) % 2
    i32 = lambda v: v.astype(jnp.int32)
    return (i32(first_e), i32(last_e), off, i32(next_e), i32(slot))


def _combine_kernel(h_ref, gate_ref, yk_ref, wsg_ref, wsu_ref, wsd_ref, g2_ref, b2_ref, out_ref,
                    *, tt):
    h = h_ref[...]
    hb = h.astype(BF16)
    g = jnp.dot(hb, wsg_ref[...], preferred_element_type=F32)
    u = jnp.dot(hb, wsu_ref[...], preferred_element_type=F32)
    act = (g * jax.nn.sigmoid(g)) * u
    f = jnp.dot(act.astype(BF16), wsd_ref[...], preferred_element_type=F32)
    gate = gate_ref[...]
    r_lo = jnp.zeros((tt, yk_ref.shape[-1]), F32)
    r_hi = jnp.zeros((tt, yk_ref.shape[-1]), F32)
    for k in range(TOP_K):
        p = yk_ref[k]
        gk = gate[:, k:k + 1]
        r_lo = r_lo + gk * pltpu.bitcast(p << 16, F32)
        r_hi = r_hi + gk * pltpu.bitcast(p & jnp.uint32(0xFFFF0000), F32)
    f = f + jnp.concatenate([r_lo, r_hi], axis=-1)
    out_ref[...] = _layer_norm(DEEPNORM_ALPHA * h + f, g2_ref[...], b2_ref[...])


def _combine(h, gate_t, gate_row0, yk, wsg_bf, wsu_bf, wsd_bf, ln2_g, ln2_b, tt):
    n, d = h.shape
    assert n % tt == 0 and gate_row0 % tt == 0
    q0 = gate_row0 // tt
    ds_ = wsg_bf.shape[1]
    kern = functools.partial(_combine_kernel, tt=tt)
    return pl.pallas_call(
        kern,
        grid=(n // tt,),
        in_specs=[
            pl.BlockSpec((tt, d), lambda i: (i, 0)),
            pl.BlockSpec((tt, TOP_K), lambda i: (i + q0, 0)),
            pl.BlockSpec((TOP_K, tt, yk.shape[2]), lambda i: (0, i, 0)),
            pl.BlockSpec((d, ds_), lambda i: (0, 0)),
            pl.BlockSpec((d, ds_), lambda i: (0, 0)),
            pl.BlockSpec((ds_, d), lambda i: (0, 0)),
            pl.BlockSpec((1, d), lambda i: (0, 0)),
            pl.BlockSpec((1, d), lambda i: (0, 0)),
        ],
        out_specs=pl.BlockSpec((tt, d), lambda i: (i, 0)),
        out_shape=jax.ShapeDtypeStruct((n, d), F32),
        compiler_params=_cparams("parallel"),
        name="combine",
    )(h, gate_t, yk, wsg_bf, wsu_bf, wsd_bf, ln2_g.reshape(1, d), ln2_b.reshape(1, d))


def _pick_tile(seq, bsz, rows):
    return max(SUBLANES, min(seq, rows // bsz))


def _mixer(x, meta_state, u_meta_last, p):
    bsz, seq, _ = x.shape
    tt = _pick_tile(seq, bsz, 1024)
    gb, u, su = _embed_inproj(x, p["ln_emb_g"], p["ln_emb_b"], p["w_in_bf"], tt)
    tc = _pick_tile(seq, bsz, S5_CHUNK_ROWS * (SUBLANES // bsz))
    s0 = jnp.stack([jnp.broadcast_to(meta_state, (N_SLABS, SUBLANES, 2 * SLAB_STATE)),
                    jnp.zeros((N_SLABS, SUBLANES, 2 * SLAB_STATE), F32)])
    a_b = jnp.broadcast_to(p["s5_a"][:, :, None, :], (2, N_SLABS, SUBLANES, 2 * SLAB_STATE))
    yf, yr = _s5_scan(su, s0, p["s5_wb"], p["s5_wc"], a_b, bsz=bsz, tc=tc, dirs=(False, True),
                      emit_y=True, emit_state=False)
    tt3 = _pick_tile(seq, bsz, 512)
    h1, h1p = _mixer_tail(x, p["ln_emb_g"], p["ln_emb_b"], gb, u, u_meta_last, su, yf, yr,
                          p["conv_w"], p["conv_b"], p["ssm_d"], p["w_glu_bf"], p["b_glu"],
                          p["norm_a_g"], p["norm_b_g"], p["w_out_bf"], p["ln1_g"], p["ln1_b"], tt3)
    d = h1.shape[-1]
    return h1.reshape(bsz * seq, d), h1p.reshape(bsz * seq, d // 2)


def kernel(x_prompt, x_sample, meta_tokens, ln_emb_g, ln_emb_b, w_in, conv_w, conv_b, ssm_lambda_re, ssm_lambda_im, ssm_log_step, ssm_b_re, ssm_b_im, ssm_c_re, ssm_c_im, ssm_d, w_glu, b_glu, norm_a_g, norm_b_g, w_out, ln1_g, ln1_b, w_router, router_bias, w_exp_gate, w_exp_up, w_exp_down, w_sh_gate, w_sh_up, w_sh_down, ln2_g, ln2_b):
    l = 0
    d = x_prompt.shape[-1]
    dirs = [_s5_params(ssm_lambda_re[l, k].astype(F32), ssm_lambda_im[l, k].astype(F32),
                       ssm_log_step[l, k], ssm_b_re[l, k].astype(F32), ssm_b_im[l, k].astype(F32),
                       ssm_c_re[l, k], ssm_c_im[l, k]) for k in range(2)]
    p = dict(
        ln_emb_g=ln_emb_g, ln_emb_b=ln_emb_b, w_in_bf=w_in[l].astype(BF16),
        conv_w=conv_w[l], conv_b=conv_b[l], ssm_d=ssm_d[l],
        w_glu_bf=w_glu[l].astype(BF16), b_glu=b_glu[l], norm_a_g=norm_a_g[l],
        norm_b_g=norm_b_g[l], w_out_bf=w_out[l].astype(BF16), ln1_g=ln1_g[l], ln1_b=ln1_b[l],
        s5_wb=jnp.stack([dirs[0][0], dirs[1][0]]), s5_wc=jnp.stack([dirs[0][1], dirs[1][1]]),
        s5_a=jnp.stack([dirs[0][2], dirs[1][2]]),
    )
    mb = SUBLANES
    xm = jnp.broadcast_to(meta_tokens.astype(F32)[None], (mb, N_META, d))
    _, u_m, su_m = _embed_inproj(xm, ln_emb_g, ln_emb_b, p["w_in_bf"], N_META)
    a_m = jnp.broadcast_to(p["s5_a"][:1, :, None, :], (1, N_SLABS, mb, 2 * SLAB_STATE))
    (st_m,) = _s5_scan(su_m, jnp.zeros((1, N_SLABS, mb, 2 * SLAB_STATE), F32), p["s5_wb"][:1],
                       p["s5_wc"][:1], a_m, bsz=mb, tc=N_META, dirs=(False,), emit_y=False,
                       emit_state=True)
    meta_state = st_m[0, :, :1, :]
    u_meta_last = u_m[0, N_META - 1]

    h1_p, h1p_p = _mixer(x_prompt, meta_state, u_meta_last, p)
    h1_s, h1p_s = _mixer(x_sample, meta_state, u_meta_last, p)
    n = h1_p.shape[0] + h1_s.shape[0]

    tr = min(h1_p.shape[0], h1_s.shape[0], ROUTER_TILE)
    eidx, rank, gate, cnt = _router(h1_p, h1_s, w_router[l], router_bias[l], tr)
    counts = cnt[:, 0].astype(jnp.int32)
    sched = _expert_schedule(counts, n * TOP_K, EXPERT_TILE)
    offsets = sched[2]
    pos_c = _positions(eidx, rank, offsets[:-1], min(n, POSITIONS_TILE))
    xs = _sc_dispatch(h1p_p, h1p_s, pos_c, SC_CHUNK)
    ys = _experts(xs, sched, w_exp_gate[l], w_exp_up[l], w_exp_down[l], EXPERT_TILE)
    n_p = h1_p.shape[0]
    nch_p = n_p // SC_CHUNK
    td = min(n_p, h1_s.shape[0], COMBINE_TILE)
    gate_t = gate.T
    shared = (w_sh_gate[l].astype(BF16), w_sh_up[l].astype(BF16), w_sh_down[l].astype(BF16),
              ln2_g[l], ln2_b[l], td)
    yk_p = _sc_gather(ys, pos_c[:nch_p], SC_CHUNK)
    yk_s = _sc_gather(ys, pos_c[nch_p:], SC_CHUNK)
    out_p = _combine(h1_p, gate_t, 0, yk_p, *shared)
    out_s = _combine(h1_s, gate_t, n_p, yk_s, *shared)
    return (out_p.reshape(x_prompt.shape), out_s.reshape(x_sample.shape))
```
